```python
import math
import jax, jax.numpy as jnp
from jax import lax
import numpy as np

D_MODEL = 2048
BATCH = 8
SEQ = 4096
DEPTH = 4

N_MEM = 256
RMS_EPS = 1e-5
CONV_WIDTH = 3
A_WIDTH = D_MODEL // 2
S5_WIDTH = D_MODEL // 2
S5_GROUP = 16
S5_GROUPS = S5_WIDTH // S5_GROUP
S5_STATE = 64
EVEN_IN = 3 * A_WIDTH + S5_WIDTH
HEAD_DIM = 64
N_Q_HEADS = D_MODEL // HEAD_DIM
Q_PER_KV = 8
N_KV_HEADS = N_Q_HEADS // Q_PER_KV
WINDOW = 128
BLOCK = 128
ODD_IN = (N_Q_HEADS + 2 * N_KV_HEADS) * HEAD_DIM
N_BUCKETS = 32
MAX_DISTANCE = 128
X_HEADS = 4
X_HEAD_DIM = D_MODEL // X_HEADS
D_FF = 5632
NEG_INF = -1e30
N_EVEN = (DEPTH + 1) // 2
N_ODD = DEPTH // 2

kernel_name = "hybrid_shortconv_s5_swa_sink_trunk"


def rms_norm(x, g):
    xf = x.astype(jnp.float32)
    y = xf * lax.rsqrt(jnp.mean(xf * xf, axis=-1, keepdims=True) + RMS_EPS)
    return (y * g.astype(jnp.float32)).astype(x.dtype)


def causal_dwconv(u, w):
    L = u.shape[1]
    up = jnp.pad(u, ((0, 0), (CONV_WIDTH - 1, 0), (0, 0)))
    y = up[:, 0:L] * w[0]
    for k in range(1, CONV_WIDTH):
        y = y + up[:, k:k + L] * w[k]
    return y


def t5_causal_bucket(rel):
    max_exact = N_BUCKETS // 2
    n = jnp.maximum(rel, 0)
    nf = jnp.maximum(n, max_exact).astype(jnp.float32)
    large = max_exact + (jnp.log(nf / max_exact) / math.log(MAX_DISTANCE / max_exact)
                         * (N_BUCKETS - max_exact)).astype(jnp.int32)
    large = jnp.minimum(large, N_BUCKETS - 1)
    return jnp.where(n < max_exact, n, large)


def s5_branch(u, a_re, a_im, log_dt, b_re, b_im, c_re, c_im, d, glu_w):
    f32 = jnp.float32
    Bsz, L, _ = u.shape
    ug = u.astype(f32).reshape(Bsz, L, S5_GROUPS, S5_GROUP)
    lam = lax.complex(a_re.astype(f32), a_im.astype(f32))
    dt = jnp.exp(log_dt.astype(f32))[:, None]
    a_bar = jnp.exp(lam * dt)
    b = lax.complex(b_re.astype(f32), b_im.astype(f32))
    b_bar = ((a_bar - 1.0) / lam)[..., None] * b
    bu = jnp.einsum('gph,blgh->blgp', b_bar, ug.astype(jnp.complex64))
    a_elems = jnp.broadcast_to(a_bar, (1, L) + a_bar.shape)

    def combine(e1, e2):
        a1, s1 = e1
        a2, s2 = e2
        return a1 * a2, a2 * s1 + s2

    _, states = lax.associative_scan(combine, (a_elems, bu), axis=1)
    c = lax.complex(c_re.astype(f32), c_im.astype(f32))
    y = jnp.real(jnp.einsum('ghp,blgp->blgh', c, states)) \
        + d.astype(f32).reshape(S5_GROUPS, S5_GROUP) * ug
    yg = jax.nn.gelu(y)
    gate = jnp.einsum('blgh,gho->blgo', yg, glu_w.astype(f32))
    out = yg * jax.nn.sigmoid(gate)
    return out.reshape(Bsz, L, S5_WIDTH).astype(u.dtype)


def conv_ssm_mixer(h, w_in, conv_w, a_re, a_im, log_dt, b_re, b_im, c_re, c_im, d, glu_w, w_out):
    z = h @ w_in
    gate_b, gate_c, xa, u = jnp.split(z, [A_WIDTH, 2 * A_WIDTH, 3 * A_WIDTH], axis=-1)
    ya = gate_b * causal_dwconv(gate_c * xa, conv_w)
    ys = s5_branch(u, a_re, a_im, log_dt, b_re, b_im, c_re, c_im, d, glu_w)
    return jnp.concatenate([ya, ys], axis=-1) @ w_out


def swa_sink_attention(h, w_qkv, b_qkv, sinks, rel_bias, w_out):
    Bsz, L, _ = h.shape
    nblk = L // BLOCK
    z = h @ w_qkv + b_qkv
    q, k, v = jnp.split(z, [N_Q_HEADS * HEAD_DIM, (N_Q_HEADS + N_KV_HEADS) * HEAD_DIM], axis=-1)
    q = q.reshape(Bsz, nblk, BLOCK, N_KV_HEADS, Q_PER_KV, HEAD_DIM)
    k = k.reshape(Bsz, nblk, BLOCK, N_KV_HEADS, HEAD_DIM)
    v = v.reshape(Bsz, nblk, BLOCK, N_KV_HEADS, HEAD_DIM)

    def with_prev(t):
        prev = jnp.pad(t, ((0, 0), (1, 0), (0, 0), (0, 0), (0, 0)))[:, :-1]
        return jnp.concatenate([prev, t], axis=2)

    kb, vb = with_prev(k), with_prev(v)
    s = jnp.einsum('bnqkgd,bnskd->bnkgqs', q, kb).astype(jnp.float32) * (HEAD_DIM ** -0.5)
    qi = jnp.arange(BLOCK, dtype=jnp.int32)[:, None]
    kj = jnp.arange(2 * BLOCK, dtype=jnp.int32)[None, :]
    rel = qi + BLOCK - kj
    bias = rel_bias.astype(jnp.float32)[t5_causal_bucket(rel)]
    bias = jnp.transpose(bias, (2, 0, 1)).reshape(N_KV_HEADS, Q_PER_KV, BLOCK, 2 * BLOCK)
    blk = jnp.arange(nblk, dtype=jnp.int32)[:, None, None]
    valid = (rel >= 0)[None] & (rel < WINDOW)[None] & (blk * BLOCK + kj[None] - BLOCK >= 0)
    s = jnp.where(valid[None, :, None, None], s + bias[None, None], NEG_INF)
    sink = jnp.broadcast_to(sinks.astype(jnp.float32).reshape(N_KV_HEADS, Q_PER_KV)[None, None, :, :, None, None],
                            s.shape[:-1] + (1,))
    p = jax.nn.softmax(jnp.concatenate([s, sink], axis=-1), axis=-1)[..., :-1]
    o = jnp.einsum('bnkgqs,bnskd->bnqkgd', p.astype(vb.dtype), vb)
    return o.reshape(Bsz, L, N_Q_HEADS * HEAD_DIM) @ w_out


def memory_cross_attention(h, mem_n, w_q, w_kv, w_o):
    Bsz, L, _ = h.shape
    q = (h @ w_q).reshape(Bsz, L, X_HEADS, X_HEAD_DIM)
    k, v = jnp.split(mem_n @ w_kv, 2, axis=-1)
    k = k.reshape(Bsz, -1, X_HEADS, X_HEAD_DIM)
    v = v.reshape(Bsz, -1, X_HEADS, X_HEAD_DIM)
    s = jnp.einsum('blhd,bmhd->bhlm', q, k).astype(jnp.float32) * (X_HEAD_DIM ** -0.5)
    p = jax.nn.softmax(s, axis=-1).astype(v.dtype)
    o = jnp.einsum('bhlm,bmhd->blhd', p, v).reshape(Bsz, L, D_MODEL)
    return o @ w_o


def conv_gated_mlp(h, w_gate, w_up, conv_w, conv_b, w_down):
    g = causal_dwconv(h @ w_gate, conv_w) + conv_b
    return (jax.nn.silu(g) * (h @ w_up)) @ w_down


def _fwd_setup_inputs(seed: int = 0) -> dict:
    key = jax.random.key(seed)
    ks = jax.random.split(key, 40)
    f32 = jnp.float32
    D = D_MODEL

    def nrm(k, shape, fan_in):
        return jax.random.normal(k, shape, f32) * (fan_in ** -0.5)

    def gain(k, shape):
        return 1.0 + 0.02 * jax.random.normal(k, shape, f32)

    a_re = -0.5 * jnp.exp(0.05 * jax.random.normal(ks[8], (N_EVEN, S5_GROUPS, S5_STATE), f32))
    a_im = math.pi * jnp.arange(S5_STATE, dtype=f32) + 0.01 * jax.random.normal(ks[9], (N_EVEN, S5_GROUPS, S5_STATE), f32)
    log_dt = jax.random.uniform(ks[10], (N_EVEN, S5_GROUPS), f32, math.log(1e-3), math.log(1e-1))
    return {
        "x": jax.random.normal(ks[0], (BATCH, SEQ, D), f32),
        "mem": jax.random.normal(ks[1], (BATCH, N_MEM, D), f32),
        "norm_mix": gain(ks[2], (DEPTH, D)),
        "norm_xattn": gain(ks[3], (DEPTH, D)),
        "norm_ffn": gain(ks[4], (DEPTH, D)),
        "norm_final": gain(ks[5], (D,)),
        "norm_mem": gain(ks[6], (D,)),
        "rel_bias": 0.5 * jax.random.normal(ks[7], (N_BUCKETS, N_Q_HEADS), f32),
        "ev_w_in": nrm(ks[11], (N_EVEN, D, EVEN_IN), D),
        "ev_conv_w": nrm(ks[12], (N_EVEN, CONV_WIDTH, A_WIDTH), CONV_WIDTH),
        "s5_a_re": a_re,
        "s5_a_im": a_im,
        "s5_log_dt": log_dt,
        "s5_b_re": nrm(ks[13], (N_EVEN, S5_GROUPS, S5_STATE, S5_GROUP), 2 * S5_GROUP),
        "s5_b_im": nrm(ks[14], (N_EVEN, S5_GROUPS, S5_STATE, S5_GROUP), 2 * S5_GROUP),
        "s5_c_re": nrm(ks[15], (N_EVEN, S5_GROUPS, S5_GROUP, S5_STATE), 2 * S5_STATE),
        "s5_c_im": nrm(ks[16], (N_EVEN, S5_GROUPS, S5_GROUP, S5_STATE), 2 * S5_STATE),
        "s5_d": jax.random.normal(ks[17], (N_EVEN, S5_WIDTH), f32),
        "s5_glu_w": nrm(ks[18], (N_EVEN, S5_GROUPS, S5_GROUP, S5_GROUP), S5_GROUP),
        "ev_w_out": nrm(ks[19], (N_EVEN, D, D), D),
        "od_w_qkv": nrm(ks[20], (N_ODD, D, ODD_IN), D),
        "od_b_qkv": 0.02 * jax.random.normal(ks[21], (N_ODD, ODD_IN), f32),
        "od_sinks": jax.random.normal(ks[22], (N_ODD, N_Q_HEADS), f32),
        "od_w_out": nrm(ks[23], (N_ODD, N_Q_HEADS * HEAD_DIM, D), D),
        "xa_w_q": nrm(ks[24], (DEPTH, D, D), D),
        "xa_w_kv": nrm(ks[25], (DEPTH, D, 2 * D), D),
        "xa_w_o": nrm(ks[26], (DEPTH, D, D), D),
        "ff_w_gate": nrm(ks[27], (DEPTH, D, D_FF), D),
        "ff_w_up": nrm(ks[28], (DEPTH, D, D_FF), D),
        "ff_conv_w": nrm(ks[29], (DEPTH, CONV_WIDTH, D_FF), CONV_WIDTH),
        "ff_conv_b": 0.02 * jax.random.normal(ks[30], (DEPTH, D_FF), f32),
        "ff_w_down": nrm(ks[31], (DEPTH, D_FF, D), D_FF),
    }


def _fwd_reference(x, mem, norm_mix, norm_xattn, norm_ffn, norm_final, norm_mem, rel_bias,
              ev_w_in, ev_conv_w, s5_a_re, s5_a_im, s5_log_dt, s5_b_re, s5_b_im,
              s5_c_re, s5_c_im, s5_d, s5_glu_w, ev_w_out,
              od_w_qkv, od_b_qkv, od_sinks, od_w_out,
              xa_w_q, xa_w_kv, xa_w_o,
              ff_w_gate, ff_w_up, ff_conv_w, ff_conv_b, ff_w_down):
    mem_n = rms_norm(mem, norm_mem)
    h = x
    for l in range(DEPTH):
        i = l // 2
        hn = rms_norm(h, norm_mix[l])
        if l % 2 == 0:
            h = h + conv_ssm_mixer(hn, ev_w_in[i], ev_conv_w[i], s5_a_re[i], s5_a_im[i],
                                   s5_log_dt[i], s5_b_re[i], s5_b_im[i], s5_c_re[i],
                                   s5_c_im[i], s5_d[i], s5_glu_w[i], ev_w_out[i])
        else:
            h = h + swa_sink_attention(hn, od_w_qkv[i], od_b_qkv[i], od_sinks[i], rel_bias, od_w_out[i])
        h = h + memory_cross_attention(rms_norm(h, norm_xattn[l]), mem_n, xa_w_q[l], xa_w_kv[l], xa_w_o[l])
        h = h + conv_gated_mlp(rms_norm(h, norm_ffn[l]), ff_w_gate[l], ff_w_up[l],
                               ff_conv_w[l], ff_conv_b[l], ff_w_down[l])
    return rms_norm(h, norm_final)


import jax as _jax
import jax.numpy as _jnp

TWIN_FORMAT = 'train_step'
FWD_PARAMS = ['x', 'mem', 'norm_mix', 'norm_xattn', 'norm_ffn', 'norm_final', 'norm_mem', 'rel_bias', 'ev_w_in', 'ev_conv_w', 's5_a_re', 's5_a_im', 's5_log_dt', 's5_b_re', 's5_b_im', 's5_c_re', 's5_c_im', 's5_d', 's5_glu_w', 'ev_w_out', 'od_w_qkv', 'od_b_qkv', 'od_sinks', 'od_w_out', 'xa_w_q', 'xa_w_kv', 'xa_w_o', 'ff_w_gate', 'ff_w_up', 'ff_conv_w', 'ff_conv_b', 'ff_w_down']
TWIN_WEIGHTS = ['norm_mix', 'norm_xattn', 'norm_ffn', 'norm_final', 'norm_mem', 'rel_bias', 'ev_w_in', 'ev_conv_w', 's5_a_re', 's5_a_im', 's5_log_dt', 's5_b_re', 's5_b_im', 's5_c_re', 's5_c_im', 's5_d', 's5_glu_w', 'ev_w_out', 'od_w_qkv', 'od_b_qkv', 'od_sinks', 'od_w_out', 'xa_w_q', 'xa_w_kv', 'xa_w_o', 'ff_w_gate', 'ff_w_up', 'ff_conv_w', 'ff_conv_b', 'ff_w_down']
TWIN_DIFF_INPUT = 'x'
TWIN_INPUTS = ['x', 'mem', 'norm_mix', 'norm_xattn', 'norm_ffn', 'norm_final', 'norm_mem', 'rel_bias', 'ev_w_in', 'ev_conv_w', 's5_a_re', 's5_a_im', 's5_log_dt', 's5_b_re', 's5_b_im', 's5_c_re', 's5_c_im', 's5_d', 's5_glu_w', 'ev_w_out', 'od_w_qkv', 'od_b_qkv', 'od_sinks', 'od_w_out', 'xa_w_q', 'xa_w_kv', 'xa_w_o', 'ff_w_gate', 'ff_w_up', 'ff_conv_w', 'ff_conv_b', 'ff_w_down', 'loss_target', 'm_norm_mix', 'm_norm_xattn', 'm_norm_ffn', 'm_norm_final', 'm_norm_mem', 'm_rel_bias', 'm_ev_w_in', 'm_ev_conv_w', 'm_s5_a_re', 'm_s5_a_im', 'm_s5_log_dt', 'm_s5_b_re', 'm_s5_b_im', 'm_s5_c_re', 'm_s5_c_im', 'm_s5_d', 'm_s5_glu_w', 'm_ev_w_out', 'm_od_w_qkv', 'm_od_b_qkv', 'm_od_sinks', 'm_od_w_out', 'm_xa_w_q', 'm_xa_w_kv', 'm_xa_w_o', 'm_ff_w_gate', 'm_ff_w_up', 'm_ff_conv_w', 'm_ff_conv_b', 'm_ff_w_down', 'v_norm_mix', 'v_norm_xattn', 'v_norm_ffn', 'v_norm_final', 'v_norm_mem', 'v_rel_bias', 'v_ev_w_in', 'v_ev_conv_w', 'v_s5_a_re', 'v_s5_a_im', 'v_s5_log_dt', 'v_s5_b_re', 'v_s5_b_im', 'v_s5_c_re', 'v_s5_c_im', 'v_s5_d', 'v_s5_glu_w', 'v_ev_w_out', 'v_od_w_qkv', 'v_od_b_qkv', 'v_od_sinks', 'v_od_w_out', 'v_xa_w_q', 'v_xa_w_kv', 'v_xa_w_o', 'v_ff_w_gate', 'v_ff_w_up', 'v_ff_conv_w', 'v_ff_conv_b', 'v_ff_w_down']
TWIN_OUTPUTS = ['loss', 'grad_x', 'grad_norm_mix', 'grad_norm_xattn', 'grad_norm_ffn', 'grad_norm_final', 'grad_norm_mem', 'grad_rel_bias', 'grad_ev_w_in', 'grad_ev_conv_w', 'grad_s5_a_re', 'grad_s5_a_im', 'grad_s5_log_dt', 'grad_s5_b_re', 'grad_s5_b_im', 'grad_s5_c_re', 'grad_s5_c_im', 'grad_s5_d', 'grad_s5_glu_w', 'grad_ev_w_out', 'grad_od_w_qkv', 'grad_od_b_qkv', 'grad_od_sinks', 'grad_od_w_out', 'grad_xa_w_q', 'grad_xa_w_kv', 'grad_xa_w_o', 'grad_ff_w_gate', 'grad_ff_w_up', 'grad_ff_conv_w', 'grad_ff_conv_b', 'grad_ff_w_down', 'delta_norm_mix', 'delta_norm_xattn', 'delta_norm_ffn', 'delta_norm_final', 'delta_norm_mem', 'delta_rel_bias', 'delta_ev_w_in', 'delta_ev_conv_w', 'delta_s5_a_re', 'delta_s5_a_im', 'delta_s5_log_dt', 'delta_s5_b_re', 'delta_s5_b_im', 'delta_s5_c_re', 'delta_s5_c_im', 'delta_s5_d', 'delta_s5_glu_w', 'delta_ev_w_out', 'delta_od_w_qkv', 'delta_od_b_qkv', 'delta_od_sinks', 'delta_od_w_out', 'delta_xa_w_q', 'delta_xa_w_kv', 'delta_xa_w_o', 'delta_ff_w_gate', 'delta_ff_w_up', 'delta_ff_conv_w', 'delta_ff_conv_b', 'delta_ff_w_down', 'new_m_norm_mix', 'new_m_norm_xattn', 'new_m_norm_ffn', 'new_m_norm_final', 'new_m_norm_mem', 'new_m_rel_bias', 'new_m_ev_w_in', 'new_m_ev_conv_w', 'new_m_s5_a_re', 'new_m_s5_a_im', 'new_m_s5_log_dt', 'new_m_s5_b_re', 'new_m_s5_b_im', 'new_m_s5_c_re', 'new_m_s5_c_im', 'new_m_s5_d', 'new_m_s5_glu_w', 'new_m_ev_w_out', 'new_m_od_w_qkv', 'new_m_od_b_qkv', 'new_m_od_sinks', 'new_m_od_w_out', 'new_m_xa_w_q', 'new_m_xa_w_kv', 'new_m_xa_w_o', 'new_m_ff_w_gate', 'new_m_ff_w_up', 'new_m_ff_conv_w', 'new_m_ff_conv_b', 'new_m_ff_w_down', 'new_v_norm_mix', 'new_v_norm_xattn', 'new_v_norm_ffn', 'new_v_norm_final', 'new_v_norm_mem', 'new_v_rel_bias', 'new_v_ev_w_in', 'new_v_ev_conv_w', 'new_v_s5_a_re', 'new_v_s5_a_im', 'new_v_s5_log_dt', 'new_v_s5_b_re', 'new_v_s5_b_im', 'new_v_s5_c_re', 'new_v_s5_c_im', 'new_v_s5_d', 'new_v_s5_glu_w', 'new_v_ev_w_out', 'new_v_od_w_qkv', 'new_v_od_b_qkv', 'new_v_od_sinks', 'new_v_od_w_out', 'new_v_xa_w_q', 'new_v_xa_w_kv', 'new_v_xa_w_o', 'new_v_ff_w_gate', 'new_v_ff_w_up', 'new_v_ff_conv_w', 'new_v_ff_conv_b', 'new_v_ff_w_down']
TWIN_LEAF_KINDS = {'loss': 'loss', 'grad_x': 'grad_x', 'grad_norm_mix': 'grad_w', 'grad_norm_xattn': 'grad_w', 'grad_norm_ffn': 'grad_w', 'grad_norm_final': 'grad_w', 'grad_norm_mem': 'grad_w', 'grad_rel_bias': 'grad_w', 'grad_ev_w_in': 'grad_w', 'grad_ev_conv_w': 'grad_w', 'grad_s5_a_re': 'grad_w', 'grad_s5_a_im': 'grad_w', 'grad_s5_log_dt': 'grad_w', 'grad_s5_b_re': 'grad_w', 'grad_s5_b_im': 'grad_w', 'grad_s5_c_re': 'grad_w', 'grad_s5_c_im': 'grad_w', 'grad_s5_d': 'grad_w', 'grad_s5_glu_w': 'grad_w', 'grad_ev_w_out': 'grad_w', 'grad_od_w_qkv': 'grad_w', 'grad_od_b_qkv': 'grad_w', 'grad_od_sinks': 'grad_w', 'grad_od_w_out': 'grad_w', 'grad_xa_w_q': 'grad_w', 'grad_xa_w_kv': 'grad_w', 'grad_xa_w_o': 'grad_w', 'grad_ff_w_gate': 'grad_w', 'grad_ff_w_up': 'grad_w', 'grad_ff_conv_w': 'grad_w', 'grad_ff_conv_b': 'grad_w', 'grad_ff_w_down': 'grad_w', 'delta_norm_mix': 'delta_w', 'delta_norm_xattn': 'delta_w', 'delta_norm_ffn': 'delta_w', 'delta_norm_final': 'delta_w', 'delta_norm_mem': 'delta_w', 'delta_rel_bias': 'delta_w', 'delta_ev_w_in': 'delta_w', 'delta_ev_conv_w': 'delta_w', 'delta_s5_a_re': 'delta_w', 'delta_s5_a_im': 'delta_w', 'delta_s5_log_dt': 'delta_w', 'delta_s5_b_re': 'delta_w', 'delta_s5_b_im': 'delta_w', 'delta_s5_c_re': 'delta_w', 'delta_s5_c_im': 'delta_w', 'delta_s5_d': 'delta_w', 'delta_s5_glu_w': 'delta_w', 'delta_ev_w_out': 'delta_w', 'delta_od_w_qkv': 'delta_w', 'delta_od_b_qkv': 'delta_w', 'delta_od_sinks': 'delta_w', 'delta_od_w_out': 'delta_w', 'delta_xa_w_q': 'delta_w', 'delta_xa_w_kv': 'delta_w', 'delta_xa_w_o': 'delta_w', 'delta_ff_w_gate': 'delta_w', 'delta_ff_w_up': 'delta_w', 'delta_ff_conv_w': 'delta_w', 'delta_ff_conv_b': 'delta_w', 'delta_ff_w_down': 'delta_w', 'new_m_norm_mix': 'new_m', 'new_m_norm_xattn': 'new_m', 'new_m_norm_ffn': 'new_m', 'new_m_norm_final': 'new_m', 'new_m_norm_mem': 'new_m', 'new_m_rel_bias': 'new_m', 'new_m_ev_w_in': 'new_m', 'new_m_ev_conv_w': 'new_m', 'new_m_s5_a_re': 'new_m', 'new_m_s5_a_im': 'new_m', 'new_m_s5_log_dt': 'new_m', 'new_m_s5_b_re': 'new_m', 'new_m_s5_b_im': 'new_m', 'new_m_s5_c_re': 'new_m', 'new_m_s5_c_im': 'new_m', 'new_m_s5_d': 'new_m', 'new_m_s5_glu_w': 'new_m', 'new_m_ev_w_out': 'new_m', 'new_m_od_w_qkv': 'new_m', 'new_m_od_b_qkv': 'new_m', 'new_m_od_sinks': 'new_m', 'new_m_od_w_out': 'new_m', 'new_m_xa_w_q': 'new_m', 'new_m_xa_w_kv': 'new_m', 'new_m_xa_w_o': 'new_m', 'new_m_ff_w_gate': 'new_m', 'new_m_ff_w_up': 'new_m', 'new_m_ff_conv_w': 'new_m', 'new_m_ff_conv_b': 'new_m', 'new_m_ff_w_down': 'new_m', 'new_v_norm_mix': 'new_v', 'new_v_norm_xattn': 'new_v', 'new_v_norm_ffn': 'new_v', 'new_v_norm_final': 'new_v', 'new_v_norm_mem': 'new_v', 'new_v_rel_bias': 'new_v', 'new_v_ev_w_in': 'new_v', 'new_v_ev_conv_w': 'new_v', 'new_v_s5_a_re': 'new_v', 'new_v_s5_a_im': 'new_v', 'new_v_s5_log_dt': 'new_v', 'new_v_s5_b_re': 'new_v', 'new_v_s5_b_im': 'new_v', 'new_v_s5_c_re': 'new_v', 'new_v_s5_c_im': 'new_v', 'new_v_s5_d': 'new_v', 'new_v_s5_glu_w': 'new_v', 'new_v_ev_w_out': 'new_v', 'new_v_od_w_qkv': 'new_v', 'new_v_od_b_qkv': 'new_v', 'new_v_od_sinks': 'new_v', 'new_v_od_w_out': 'new_v', 'new_v_xa_w_q': 'new_v', 'new_v_xa_w_kv': 'new_v', 'new_v_xa_w_o': 'new_v', 'new_v_ff_w_gate': 'new_v', 'new_v_ff_w_up': 'new_v', 'new_v_ff_conv_w': 'new_v', 'new_v_ff_conv_b': 'new_v', 'new_v_ff_w_down': 'new_v'}


def _forward(args):
    return _fwd_reference(*[args[k] for k in FWD_PARAMS])


def _output_shape():
    def fwd():
        inp = _fwd_setup_inputs(0)
        return _fwd_reference(*[inp[k] for k in FWD_PARAMS])
    out = _jax.eval_shape(fwd)
    return out.shape, out.dtype

N_MICROBATCH = 1
ADAM_LR = 0.001
ADAM_B1 = 0.9
ADAM_B2 = 0.999
ADAM_EPS = 1e-08
ADAM_WD = 0.01
ADAM_STEP = 10
PER_EXAMPLE_BATCH_AXIS = {'x': 0, 'mem': 0, 'loss_target': 0}
SHARED_INPUTS = []
_WEIGHT_DTYPES = {'norm_mix': _jnp.float32, 'norm_xattn': _jnp.float32, 'norm_ffn': _jnp.float32, 'norm_final': _jnp.float32, 'norm_mem': _jnp.float32, 'rel_bias': _jnp.float32, 'ev_w_in': _jnp.float32, 'ev_conv_w': _jnp.float32, 's5_a_re': _jnp.float32, 's5_a_im': _jnp.float32, 's5_log_dt': _jnp.float32, 's5_b_re': _jnp.float32, 's5_b_im': _jnp.float32, 's5_c_re': _jnp.float32, 's5_c_im': _jnp.float32, 's5_d': _jnp.float32, 's5_glu_w': _jnp.float32, 'ev_w_out': _jnp.float32, 'od_w_qkv': _jnp.float32, 'od_b_qkv': _jnp.float32, 'od_sinks': _jnp.float32, 'od_w_out': _jnp.float32, 'xa_w_q': _jnp.float32, 'xa_w_kv': _jnp.float32, 'xa_w_o': _jnp.float32, 'ff_w_gate': _jnp.float32, 'ff_w_up': _jnp.float32, 'ff_conv_w': _jnp.float32, 'ff_conv_b': _jnp.float32, 'ff_w_down': _jnp.float32}
MOMENT_SCALE = {'norm_mix': 7.944318e-02, 'norm_xattn': 8.268584e-03, 'norm_ffn': 6.043964e-02, 'norm_final': 1.599610e+01, 'norm_mem': 2.566907e-02, 'rel_bias': 2.522883e-02, 'ev_w_in': 7.743777e-02, 'ev_conv_w': 9.011307e-02, 's5_a_re': 1.687314e-03, 's5_a_im': 1.598858e-03, 's5_log_dt': 7.956990e-01, 's5_b_re': 1.045192e-03, 's5_b_im': 1.049073e-03, 's5_c_re': 2.103228e-03, 's5_c_im': 2.143560e-03, 's5_d': 4.016456e-02, 's5_glu_w': 1.196929e-02, 'ev_w_out': 6.618254e-02, 'od_w_qkv': 2.310760e-02, 'od_b_qkv': 7.711138e-02, 'od_sinks': 1.135892e-02, 'od_w_out': 1.788128e-02, 'xa_w_q': 8.265415e-03, 'xa_w_kv': 8.564902e-03, 'xa_w_o': 8.836043e-03, 'ff_w_gate': 2.600616e-02, 'ff_w_up': 2.526787e-02, 'ff_conv_w': 2.622644e-02, 'ff_conv_b': 2.566742e-02, 'ff_w_down': 4.183829e-02}


def _to_microbatches(a, axis):
    t = _jnp.moveaxis(a, axis, 0)
    t = t.reshape((N_MICROBATCH, t.shape[0] // N_MICROBATCH) + t.shape[1:])
    return _jnp.moveaxis(t, 1, axis + 1)


def setup_inputs(seed: int = 0) -> dict:
    inp = _fwd_setup_inputs(seed)
    key = _jax.random.fold_in(_jax.random.key(seed), 7919)
    shape, _ = _output_shape()
    out = dict(inp)
    out["loss_target"] = _jax.random.normal(_jax.random.fold_in(key, 0), shape, _jnp.float32)
    for i, name in enumerate(TWIN_WEIGHTS):
        w = inp[name].astype(_jnp.float32)
        if MOMENT_SCALE is None:
            s = _jnp.sqrt(_jnp.mean(_jnp.square(w)) + 1e-30)
        else:
            s = MOMENT_SCALE[name]
        km, kv = _jax.random.split(_jax.random.fold_in(key, i + 1))
        out[name] = w
        out["m_" + name] = s * _jax.random.normal(km, w.shape, _jnp.float32)
        out["v_" + name] = (s * s) * _jax.random.uniform(kv, w.shape, _jnp.float32, 0.5, 1.5)
    if N_MICROBATCH > 1:
        for name, axis in PER_EXAMPLE_BATCH_AXIS.items():
            out[name] = _to_microbatches(out[name], axis)
    return {'x': out['x'], 'mem': out['mem'], 'norm_mix': out['norm_mix'], 'norm_xattn': out['norm_xattn'], 'norm_ffn': out['norm_ffn'], 'norm_final': out['norm_final'], 'norm_mem': out['norm_mem'], 'rel_bias': out['rel_bias'], 'ev_w_in': out['ev_w_in'], 'ev_conv_w': out['ev_conv_w'], 's5_a_re': out['s5_a_re'], 's5_a_im': out['s5_a_im'], 's5_log_dt': out['s5_log_dt'], 's5_b_re': out['s5_b_re'], 's5_b_im': out['s5_b_im'], 's5_c_re': out['s5_c_re'], 's5_c_im': out['s5_c_im'], 's5_d': out['s5_d'], 's5_glu_w': out['s5_glu_w'], 'ev_w_out': out['ev_w_out'], 'od_w_qkv': out['od_w_qkv'], 'od_b_qkv': out['od_b_qkv'], 'od_sinks': out['od_sinks'], 'od_w_out': out['od_w_out'], 'xa_w_q': out['xa_w_q'], 'xa_w_kv': out['xa_w_kv'], 'xa_w_o': out['xa_w_o'], 'ff_w_gate': out['ff_w_gate'], 'ff_w_up': out['ff_w_up'], 'ff_conv_w': out['ff_conv_w'], 'ff_conv_b': out['ff_conv_b'], 'ff_w_down': out['ff_w_down'], 'loss_target': out['loss_target'], 'm_norm_mix': out['m_norm_mix'], 'm_norm_xattn': out['m_norm_xattn'], 'm_norm_ffn': out['m_norm_ffn'], 'm_norm_final': out['m_norm_final'], 'm_norm_mem': out['m_norm_mem'], 'm_rel_bias': out['m_rel_bias'], 'm_ev_w_in': out['m_ev_w_in'], 'm_ev_conv_w': out['m_ev_conv_w'], 'm_s5_a_re': out['m_s5_a_re'], 'm_s5_a_im': out['m_s5_a_im'], 'm_s5_log_dt': out['m_s5_log_dt'], 'm_s5_b_re': out['m_s5_b_re'], 'm_s5_b_im': out['m_s5_b_im'], 'm_s5_c_re': out['m_s5_c_re'], 'm_s5_c_im': out['m_s5_c_im'], 'm_s5_d': out['m_s5_d'], 'm_s5_glu_w': out['m_s5_glu_w'], 'm_ev_w_out': out['m_ev_w_out'], 'm_od_w_qkv': out['m_od_w_qkv'], 'm_od_b_qkv': out['m_od_b_qkv'], 'm_od_sinks': out['m_od_sinks'], 'm_od_w_out': out['m_od_w_out'], 'm_xa_w_q': out['m_xa_w_q'], 'm_xa_w_kv': out['m_xa_w_kv'], 'm_xa_w_o': out['m_xa_w_o'], 'm_ff_w_gate': out['m_ff_w_gate'], 'm_ff_w_up': out['m_ff_w_up'], 'm_ff_conv_w': out['m_ff_conv_w'], 'm_ff_conv_b': out['m_ff_conv_b'], 'm_ff_w_down': out['m_ff_w_down'], 'v_norm_mix': out['v_norm_mix'], 'v_norm_xattn': out['v_norm_xattn'], 'v_norm_ffn': out['v_norm_ffn'], 'v_norm_final': out['v_norm_final'], 'v_norm_mem': out['v_norm_mem'], 'v_rel_bias': out['v_rel_bias'], 'v_ev_w_in': out['v_ev_w_in'], 'v_ev_conv_w': out['v_ev_conv_w'], 'v_s5_a_re': out['v_s5_a_re'], 'v_s5_a_im': out['v_s5_a_im'], 'v_s5_log_dt': out['v_s5_log_dt'], 'v_s5_b_re': out['v_s5_b_re'], 'v_s5_b_im': out['v_s5_b_im'], 'v_s5_c_re': out['v_s5_c_re'], 'v_s5_c_im': out['v_s5_c_im'], 'v_s5_d': out['v_s5_d'], 'v_s5_glu_w': out['v_s5_glu_w'], 'v_ev_w_out': out['v_ev_w_out'], 'v_od_w_qkv': out['v_od_w_qkv'], 'v_od_b_qkv': out['v_od_b_qkv'], 'v_od_sinks': out['v_od_sinks'], 'v_od_w_out': out['v_od_w_out'], 'v_xa_w_q': out['v_xa_w_q'], 'v_xa_w_kv': out['v_xa_w_kv'], 'v_xa_w_o': out['v_xa_w_o'], 'v_ff_w_gate': out['v_ff_w_gate'], 'v_ff_w_up': out['v_ff_w_up'], 'v_ff_conv_w': out['v_ff_conv_w'], 'v_ff_conv_b': out['v_ff_conv_b'], 'v_ff_w_down': out['v_ff_w_down']}


def _loss(weights, diff, rest, loss_target):
    with _jax.named_scope("forward"):
        args = {**rest, TWIN_DIFF_INPUT: diff, **{k: w.astype(_WEIGHT_DTYPES[k]) for k, w in weights.items()}}
        y = _forward(args)
    with _jax.named_scope("loss_head"):
        err = _jnp.square(y.astype(_jnp.float32) - loss_target)
        return 0.5 * _jnp.sum(_jnp.mean(err, axis=-1)) if err.ndim else 0.5 * err


def _adamw(w, g, m, v):
    m = ADAM_B1 * m + (1.0 - ADAM_B1) * g
    v = ADAM_B2 * v + (1.0 - ADAM_B2) * _jnp.square(g)
    m_hat = m / (1.0 - ADAM_B1 ** ADAM_STEP)
    v_hat = v / (1.0 - ADAM_B2 ** ADAM_STEP)
    delta = -ADAM_LR * (m_hat / (_jnp.sqrt(v_hat) + ADAM_EPS) + ADAM_WD * w)
    return delta, m, v


def reference(x, mem, norm_mix, norm_xattn, norm_ffn, norm_final, norm_mem, rel_bias, ev_w_in, ev_conv_w, s5_a_re, s5_a_im, s5_log_dt, s5_b_re, s5_b_im, s5_c_re, s5_c_im, s5_d, s5_glu_w, ev_w_out, od_w_qkv, od_b_qkv, od_sinks, od_w_out, xa_w_q, xa_w_kv, xa_w_o, ff_w_gate, ff_w_up, ff_conv_w, ff_conv_b, ff_w_down, loss_target, m_norm_mix, m_norm_xattn, m_norm_ffn, m_norm_final, m_norm_mem, m_rel_bias, m_ev_w_in, m_ev_conv_w, m_s5_a_re, m_s5_a_im, m_s5_log_dt, m_s5_b_re, m_s5_b_im, m_s5_c_re, m_s5_c_im, m_s5_d, m_s5_glu_w, m_ev_w_out, m_od_w_qkv, m_od_b_qkv, m_od_sinks, m_od_w_out, m_xa_w_q, m_xa_w_kv, m_xa_w_o, m_ff_w_gate, m_ff_w_up, m_ff_conv_w, m_ff_conv_b, m_ff_w_down, v_norm_mix, v_norm_xattn, v_norm_ffn, v_norm_final, v_norm_mem, v_rel_bias, v_ev_w_in, v_ev_conv_w, v_s5_a_re, v_s5_a_im, v_s5_log_dt, v_s5_b_re, v_s5_b_im, v_s5_c_re, v_s5_c_im, v_s5_d, v_s5_glu_w, v_ev_w_out, v_od_w_qkv, v_od_b_qkv, v_od_sinks, v_od_w_out, v_xa_w_q, v_xa_w_kv, v_xa_w_o, v_ff_w_gate, v_ff_w_up, v_ff_conv_w, v_ff_conv_b, v_ff_w_down):
    given = dict(x=x, mem=mem, norm_mix=norm_mix, norm_xattn=norm_xattn, norm_ffn=norm_ffn, norm_final=norm_final, norm_mem=norm_mem, rel_bias=rel_bias, ev_w_in=ev_w_in, ev_conv_w=ev_conv_w, s5_a_re=s5_a_re, s5_a_im=s5_a_im, s5_log_dt=s5_log_dt, s5_b_re=s5_b_re, s5_b_im=s5_b_im, s5_c_re=s5_c_re, s5_c_im=s5_c_im, s5_d=s5_d, s5_glu_w=s5_glu_w, ev_w_out=ev_w_out, od_w_qkv=od_w_qkv, od_b_qkv=od_b_qkv, od_sinks=od_sinks, od_w_out=od_w_out, xa_w_q=xa_w_q, xa_w_kv=xa_w_kv, xa_w_o=xa_w_o, ff_w_gate=ff_w_gate, ff_w_up=ff_w_up, ff_conv_w=ff_conv_w, ff_conv_b=ff_conv_b, ff_w_down=ff_w_down, loss_target=loss_target, m_norm_mix=m_norm_mix, m_norm_xattn=m_norm_xattn, m_norm_ffn=m_norm_ffn, m_norm_final=m_norm_final, m_norm_mem=m_norm_mem, m_rel_bias=m_rel_bias, m_ev_w_in=m_ev_w_in, m_ev_conv_w=m_ev_conv_w, m_s5_a_re=m_s5_a_re, m_s5_a_im=m_s5_a_im, m_s5_log_dt=m_s5_log_dt, m_s5_b_re=m_s5_b_re, m_s5_b_im=m_s5_b_im, m_s5_c_re=m_s5_c_re, m_s5_c_im=m_s5_c_im, m_s5_d=m_s5_d, m_s5_glu_w=m_s5_glu_w, m_ev_w_out=m_ev_w_out, m_od_w_qkv=m_od_w_qkv, m_od_b_qkv=m_od_b_qkv, m_od_sinks=m_od_sinks, m_od_w_out=m_od_w_out, m_xa_w_q=m_xa_w_q, m_xa_w_kv=m_xa_w_kv, m_xa_w_o=m_xa_w_o, m_ff_w_gate=m_ff_w_gate, m_ff_w_up=m_ff_w_up, m_ff_conv_w=m_ff_conv_w, m_ff_conv_b=m_ff_conv_b, m_ff_w_down=m_ff_w_down, v_norm_mix=v_norm_mix, v_norm_xattn=v_norm_xattn, v_norm_ffn=v_norm_ffn, v_norm_final=v_norm_final, v_norm_mem=v_norm_mem, v_rel_bias=v_rel_bias, v_ev_w_in=v_ev_w_in, v_ev_conv_w=v_ev_conv_w, v_s5_a_re=v_s5_a_re, v_s5_a_im=v_s5_a_im, v_s5_log_dt=v_s5_log_dt, v_s5_b_re=v_s5_b_re, v_s5_b_im=v_s5_b_im, v_s5_c_re=v_s5_c_re, v_s5_c_im=v_s5_c_im, v_s5_d=v_s5_d, v_s5_glu_w=v_s5_glu_w, v_ev_w_out=v_ev_w_out, v_od_w_qkv=v_od_w_qkv, v_od_b_qkv=v_od_b_qkv, v_od_sinks=v_od_sinks, v_od_w_out=v_od_w_out, v_xa_w_q=v_xa_w_q, v_xa_w_kv=v_xa_w_kv, v_xa_w_o=v_xa_w_o, v_ff_w_gate=v_ff_w_gate, v_ff_w_up=v_ff_w_up, v_ff_conv_w=v_ff_conv_w, v_ff_conv_b=v_ff_conv_b, v_ff_w_down=v_ff_w_down)
    weights = {n: given[n] for n in TWIN_WEIGHTS}
    shared = {n: given[n] for n in SHARED_INPUTS}
    per_example = {n: given[n] for n in ['x', 'mem']}
    grad_fn = _jax.value_and_grad(_loss, argnums=(0, 1))

    def one_microbatch(ex, loss_target):
        ex = dict(ex)
        diff = ex.pop(TWIN_DIFF_INPUT)
        return grad_fn(weights, diff, {**shared, **ex}, loss_target)

    if N_MICROBATCH == 1:
        loss, (grad_w, grad_x) = one_microbatch(per_example, given["loss_target"])
    else:
        def body(carry, xs):
            loss_sum, grad_sum = carry
            l_k, (gw_k, gx_k) = one_microbatch(xs[0], xs[1])
            with _jax.named_scope("update"):
                return (loss_sum + l_k, _jax.tree.map(_jnp.add, grad_sum, gw_k)), gx_k

        init = (_jnp.zeros((), _jnp.float32), _jax.tree.map(_jnp.zeros_like, weights))
        (loss, grad_w), grad_x = _jax.lax.scan(body, init, (per_example, given["loss_target"]))
    with _jax.named_scope("update"):
        delta_w, new_m, new_v = {}, {}, {}
        for n in TWIN_WEIGHTS:
            delta_w[n], new_m[n], new_v[n] = _adamw(weights[n], grad_w[n], given["m_" + n], given["v_" + n])
    return (loss, grad_x, *[grad_w[n] for n in TWIN_WEIGHTS], *[delta_w[n] for n in TWIN_WEIGHTS],
            *[new_m[n] for n in TWIN_WEIGHTS], *[new_v[n] for n in TWIN_WEIGHTS])
```

```python
import functools
import math

import numpy as np
import jax
import jax.numpy as jnp
from jax import lax
from jax.experimental import pallas as pl
from jax.experimental.pallas import tpu as pltpu

F32, BF16 = jnp.float32, jnp.bfloat16
MESH = pl.DeviceIdType.MESH
AXES = ("x", "y", "c")

VMEM_LIMIT_BYTES = 56 * 2**20
SUBLANES, LANES = 8, 128

RMS_EPS = 1e-5
S5_GROUP, S5_STATE = 16, 64
HEAD_DIM, Q_PER_KV, WINDOW = 64, 8, 128
N_BUCKETS, MAX_DISTANCE = 32, 128
X_HEADS = 4
NEG_INF = -1e30
ADAM_LR, ADAM_B1, ADAM_B2, ADAM_EPS, ADAM_WD, ADAM_STEP = 0.001, 0.9, 0.999, 1e-08, 0.01, 10
N_CHIPS = 4
N_SEG = 8
S5_STEPS = 32


def _params(sem=None):
    return pltpu.CompilerParams(dimension_semantics=sem, vmem_limit_bytes=VMEM_LIMIT_BYTES)


def _vspec(shape, index_map):
    return pl.BlockSpec(shape, index_map)


ANY = pl.BlockSpec(memory_space=pl.ANY)


def _tile(n, pref):
    t = (min(pref, n) // LANES) * LANES
    while t >= LANES:
        if n % t == 0:
            return t
        t -= LANES
    return n


def mm_nn(a, w3, *, bias=None, res=None, out_dtype=F32, name):
    M, K = a.shape
    S, K2, ns = w3.shape
    assert K == K2
    tm, tk = _tile(M, 512), _tile(K, 512)
    nk = K // tk
    has_b, has_r = bias is not None, res is not None

    def body(*refs):
        a_ref, w_ref = refs[0], refs[1]
        b_ref = refs[2] if has_b else None
        r_ref = refs[2 + has_b] if has_r else None
        o_ref, acc = refs[2 + has_b + has_r], refs[3 + has_b + has_r]
        k = pl.program_id(2)

        @pl.when(k == 0)
        def _():
            acc[...] = jnp.zeros_like(acc)

        acc[...] += jnp.dot(a_ref[...].astype(BF16), w_ref[...], preferred_element_type=F32)

        @pl.when(k == nk - 1)
        def _():
            r = acc[...]
            if has_b:
                r = r + b_ref[...]
            if has_r:
                r = r + r_ref[...]
            o_ref[...] = r.astype(o_ref.dtype)

    in_specs = [_vspec((tm, tk), lambda i, s, k: (i, k)), _vspec((None, tk, ns), lambda i, s, k: (s, k, 0))]
    ops = [a, w3]
    if has_b:
        in_specs.append(_vspec((1, ns), lambda i, s, k: (0, s)))
        ops.append(bias)
    if has_r:
        in_specs.append(_vspec((tm, ns), lambda i, s, k: (i, s)))
        ops.append(res)
    return pl.pallas_call(
        body, name=name, grid=(M // tm, S, nk), in_specs=in_specs,
        out_specs=_vspec((tm, ns), lambda i, s, k: (i, s)),
        out_shape=jax.ShapeDtypeStruct((M, S * ns), out_dtype),
        scratch_shapes=[pltpu.VMEM((tm, ns), F32)],
        compiler_params=_params(("parallel", "parallel", "arbitrary")),
    )(*ops)


def mm_nt(a, w3, *, res=None, out_dtype=F32, name):
    M, N = a.shape
    S, K, ns = w3.shape
    assert N == S * ns
    tm, tko = _tile(M, 512), _tile(K, 512)
    tc = ns if ns <= 1536 else _tile(ns, 1024)
    ncs = ns // tc
    nc = S * ncs
    has_r = res is not None

    def body(*refs):
        a_ref, w_ref = refs[0], refs[1]
        r_ref = refs[2] if has_r else None
        o_ref, acc = refs[2 + has_r], refs[3 + has_r]
        k = pl.program_id(2)

        @pl.when(k == 0)
        def _():
            acc[...] = jnp.zeros_like(acc)

        acc[...] += lax.dot_general(a_ref[...].astype(BF16), w_ref[...], (((1,), (1,)), ((), ())),
                                    preferred_element_type=F32)

        @pl.when(k == nc - 1)
        def _():
            r = acc[...]
            if has_r:
                r = r + r_ref[...]
            o_ref[...] = r.astype(o_ref.dtype)

    in_specs = [_vspec((tm, tc), lambda i, j, k: (i, k)),
                _vspec((None, tko, tc), lambda i, j, k: (k // ncs, j, k % ncs))]
    ops = [a, w3]
    if has_r:
        in_specs.append(_vspec((tm, tko), lambda i, j, k: (i, j)))
        ops.append(res)
    return pl.pallas_call(
        body, name=name, grid=(M // tm, K // tko, nc), in_specs=in_specs,
        out_specs=_vspec((tm, tko), lambda i, j, k: (i, j)),
        out_shape=jax.ShapeDtypeStruct((M, K), out_dtype),
        scratch_shapes=[pltpu.VMEM((tm, tko), F32)],
        compiler_params=_params(("parallel", "parallel", "arbitrary")),
    )(*ops)


def mm_tn(x, dy, S, *, out_dtype=BF16, name):
    M, K = x.shape
    M2, N = dy.shape
    assert M == M2 and N % S == 0
    ns = N // S
    tk, tmc = _tile(K, 512), _tile(M, 512)
    nm = M // tmc

    def body(x_ref, dy_ref, o_ref, acc):
        m = pl.program_id(2)

        @pl.when(m == 0)
        def _():
            acc[...] = jnp.zeros_like(acc)

        acc[...] += lax.dot_general(x_ref[...].astype(BF16), dy_ref[...].astype(BF16), (((0,), (0,)), ((), ())),
                                    preferred_element_type=F32)

        @pl.when(m == nm - 1)
        def _():
            o_ref[...] = acc[...].astype(o_ref.dtype)

    return pl.pallas_call(
        body, name=name, grid=(S, K // tk, nm),
        in_specs=[_vspec((tmc, tk), lambda s, i, m: (m, i)), _vspec((tmc, ns), lambda s, i, m: (m, s))],
        out_specs=_vspec((None, tk, ns), lambda s, i, m: (s, i, 0)),
        out_shape=jax.ShapeDtypeStruct((S, K, ns), out_dtype),
        scratch_shapes=[pltpu.VMEM((tk, ns), F32)],
        compiler_params=_params(("parallel", "parallel", "arbitrary")),
    )(x, dy)


def rms_fwd(h, g, *, name="rms_fwd"):
    R, D = h.shape
    tr = _tile8(R, 256)

    def body(h_ref, g_ref, o_ref):
        x = h_ref[...]
        r = lax.rsqrt(jnp.mean(x * x, axis=-1, keepdims=True) + RMS_EPS)
        o_ref[...] = (x * r * g_ref[...]).astype(o_ref.dtype)

    return pl.pallas_call(
        body, name=name, grid=(R // tr,),
        in_specs=[_vspec((tr, D), lambda i: (i, 0)), _vspec((1, D), lambda i: (0, 0))],
        out_specs=_vspec((tr, D), lambda i: (i, 0)),
        out_shape=jax.ShapeDtypeStruct((R, D), BF16),
        compiler_params=_params(("parallel",)),
    )(h, g)


def _tile8(n, pref):
    t = (min(pref, n) // SUBLANES) * SUBLANES
    while t >= SUBLANES:
        if n % t == 0:
            return t
        t -= SUBLANES
    return n


def rms_bwd(h, g, dhn, dres, *, name="rms_bwd"):
    R, D = h.shape
    tr = _tile8(R, 256)

    def body(h_ref, g_ref, dhn_ref, dres_ref, dh_ref, dg_ref):
        @pl.when(pl.program_id(0) == 0)
        def _():
            dg_ref[...] = jnp.zeros_like(dg_ref)

        x = h_ref[...]
        d = dhn_ref[...].astype(F32)
        r = lax.rsqrt(jnp.mean(x * x, axis=-1, keepdims=True) + RMS_EPS)
        xhat = x * r
        dg_ref[...] += jnp.sum(d * xhat, axis=0, keepdims=True)
        t = d * g_ref[...]
        dh_ref[...] = dres_ref[...] + r * (t - xhat * jnp.mean(t * xhat, axis=-1, keepdims=True))

    return pl.pallas_call(
        body, name=name, grid=(R // tr,),
        in_specs=[_vspec((tr, D), lambda i: (i, 0)), _vspec((1, D), lambda i: (0, 0)),
                  _vspec((tr, D), lambda i: (i, 0)), _vspec((tr, D), lambda i: (i, 0))],
        out_specs=[_vspec((tr, D), lambda i: (i, 0)), _vspec((1, D), lambda i: (0, 0))],
        out_shape=[jax.ShapeDtypeStruct((R, D), F32), jax.ShapeDtypeStruct((1, D), F32)],
        compiler_params=_params(("arbitrary",)),
    )(h, g, dhn, dres)


def final_loss(h, g, target, *, name="final_loss"):
    R, D = h.shape
    tr = _tile8(R, 256)

    def body(h_ref, g_ref, t_ref, loss_ref, dh_ref, dg_ref):
        @pl.when(pl.program_id(0) == 0)
        def _():
            dg_ref[...] = jnp.zeros_like(dg_ref)
            loss_ref[...] = jnp.zeros_like(loss_ref)

        x = h_ref[...]
        r = lax.rsqrt(jnp.mean(x * x, axis=-1, keepdims=True) + RMS_EPS)
        xhat = x * r
        err = xhat * g_ref[...] - t_ref[...]
        row = jnp.mean(err * err, axis=-1, keepdims=True)
        loss_ref[...] += 0.5 * jnp.sum(row, axis=0, keepdims=True)
        d = err * (1.0 / D)
        dg_ref[...] += jnp.sum(d * xhat, axis=0, keepdims=True)
        t = d * g_ref[...]
        dh_ref[...] = r * (t - xhat * jnp.mean(t * xhat, axis=-1, keepdims=True))

    return pl.pallas_call(
        body, name=name, grid=(R // tr,),
        in_specs=[_vspec((tr, D), lambda i: (i, 0)), _vspec((1, D), lambda i: (0, 0)), _vspec((tr, D), lambda i: (i, 0))],
        out_specs=[_vspec((1, 1), lambda i: (0, 0)), _vspec((tr, D), lambda i: (i, 0)), _vspec((1, D), lambda i: (0, 0))],
        out_shape=[jax.ShapeDtypeStruct((1, 1), F32), jax.ShapeDtypeStruct((R, D), F32), jax.ShapeDtypeStruct((1, D), F32)],
        compiler_params=_params(("arbitrary",)),
    )(h, g, target)


def _shift_down(v, k):
    rows = lax.broadcasted_iota(jnp.int32, v.shape, 0)
    return jnp.where(rows >= k, pltpu.roll(v, k, axis=0), 0.0)


def _shift_up(v, k):
    L = v.shape[0]
    rows = lax.broadcasted_iota(jnp.int32, v.shape, 0)
    return jnp.where(rows < L - k, pltpu.roll(v, L - k, axis=0), 0.0)


def _conv(v, w):
    return w[2:3, :] * v + w[1:2, :] * _shift_down(v, 1) + w[0:1, :] * _shift_down(v, 2)


def _conv_t(d, w):
    return w[2:3, :] * d + w[1:2, :] * _shift_up(d, 1) + w[0:1, :] * _shift_up(d, 2)


def _conv_dw(d, v):
    return jnp.concatenate([
        jnp.sum(d * _shift_down(v, 2), axis=0, keepdims=True),
        jnp.sum(d * _shift_down(v, 1), axis=0, keepdims=True),
        jnp.sum(d * v, axis=0, keepdims=True)], axis=0)


COL_BLOCK = 128


def conv_mixer_fwd(z, cw, *, name="conv_mixer_fwd"):
    L = z.shape[0]
    A = cw.shape[1]
    cb = _tile(A, COL_BLOCK)
    nb = A // cb

    def body(gb_ref, gc_ref, xa_ref, w_ref, o_ref):
        v = gc_ref[...] * xa_ref[...]
        o_ref[...] = (gb_ref[...] * _conv(v, w_ref[...])).astype(o_ref.dtype)

    return pl.pallas_call(
        body, name=name, grid=(nb,),
        in_specs=[_vspec((L, cb), lambda j: (0, j)), _vspec((L, cb), lambda j: (0, nb + j)),
                  _vspec((L, cb), lambda j: (0, 2 * nb + j)), _vspec((3, cb), lambda j: (0, j))],
        out_specs=_vspec((L, cb), lambda j: (0, j)),
        out_shape=jax.ShapeDtypeStruct((L, A), BF16),
        compiler_params=_params(("parallel",)),
    )(z, z, z, cw)


def conv_mixer_bwd(z, cw, dya, *, name="conv_mixer_bwd"):
    L = z.shape[0]
    A = cw.shape[1]
    cb = _tile(A, COL_BLOCK)
    nb = A // cb

    def body(gb_ref, gc_ref, xa_ref, w_ref, d_ref, dgb_ref, dgc_ref, dxa_ref, dw_ref):
        gc, xa, w, d = gc_ref[...], xa_ref[...], w_ref[...], d_ref[...]
        v = gc * xa
        dgb_ref[...] = (d * _conv(v, w)).astype(dgb_ref.dtype)
        dc = d * gb_ref[...]
        dw_ref[...] = _conv_dw(dc, v)
        dv = _conv_t(dc, w)
        dgc_ref[...] = (dv * xa).astype(dgc_ref.dtype)
        dxa_ref[...] = (dv * gc).astype(dxa_ref.dtype)

    col = lambda j: (0, j)
    outs = pl.pallas_call(
        body, name=name, grid=(nb,),
        in_specs=[_vspec((L, cb), col), _vspec((L, cb), lambda j: (0, nb + j)),
                  _vspec((L, cb), lambda j: (0, 2 * nb + j)), _vspec((3, cb), col), _vspec((L, cb), col)],
        out_specs=[_vspec((L, cb), col), _vspec((L, cb), col), _vspec((L, cb), col), _vspec((3, cb), col)],
        out_shape=[jax.ShapeDtypeStruct((L, A), BF16)] * 3 + [jax.ShapeDtypeStruct((3, A), F32)],
        compiler_params=_params(("parallel",)),
    )(z, z, z, cw, dya)
    return outs[0], outs[1], outs[2], outs[3]


def ffn_act_fwd(gpre, up, cw, cbias, *, name="ffn_act_fwd"):
    L, Fd = gpre.shape
    cb = _tile(Fd, COL_BLOCK)

    def body(g_ref, u_ref, w_ref, b_ref, o_ref):
        g = _conv(g_ref[...], w_ref[...]) + b_ref[...]
        o_ref[...] = (g * jax.nn.sigmoid(g) * u_ref[...]).astype(o_ref.dtype)

    col = lambda j: (0, j)
    return pl.pallas_call(
        body, name=name, grid=(Fd // cb,),
        in_specs=[_vspec((L, cb), col), _vspec((L, cb), col), _vspec((3, cb), col), _vspec((1, cb), col)],
        out_specs=_vspec((L, cb), col),
        out_shape=jax.ShapeDtypeStruct((L, Fd), BF16),
        compiler_params=_params(("parallel",)),
    )(gpre, up, cw, cbias)


def ffn_act_bwd(dact, gpre, up, cw, cbias, *, name="ffn_act_bwd"):
    L, Fd = gpre.shape
    cb = _tile(Fd, COL_BLOCK)

    def body(d_ref, g_ref, u_ref, w_ref, b_ref, dg_ref, du_ref, dw_ref, db_ref):
        gp, w, d = g_ref[...], w_ref[...], d_ref[...]
        g = _conv(gp, w) + b_ref[...]
        sg = jax.nn.sigmoid(g)
        du_ref[...] = (d * (g * sg)).astype(du_ref.dtype)
        dg = d * u_ref[...] * (sg * (1.0 + g * (1.0 - sg)))
        db_ref[...] = jnp.sum(dg, axis=0, keepdims=True)
        dw_ref[...] = _conv_dw(dg, gp)
        dg_ref[...] = _conv_t(dg, w).astype(dg_ref.dtype)

    col = lambda j: (0, j)
    return pl.pallas_call(
        body, name=name, grid=(Fd // cb,),
        in_specs=[_vspec((L, cb), col)] * 3 + [_vspec((3, cb), col), _vspec((1, cb), col)],
        out_specs=[_vspec((L, cb), col), _vspec((L, cb), col), _vspec((3, cb), col), _vspec((1, cb), col)],
        out_shape=[jax.ShapeDtypeStruct((L, Fd), BF16), jax.ShapeDtypeStruct((L, Fd), BF16),
                   jax.ShapeDtypeStruct((3, Fd), F32), jax.ShapeDtypeStruct((1, Fd), F32)],
        compiler_params=_params(("parallel",)),
    )(dact, gpre, up, cw, cbias)


def col_sum(x, *, name="col_sum"):
    R, C = x.shape
    tr = _tile8(R, 512)

    def body(x_ref, o_ref):
        @pl.when(pl.program_id(0) == 0)
        def _():
            o_ref[...] = jnp.zeros_like(o_ref)

        o_ref[...] += jnp.sum(x_ref[...].astype(F32), axis=0, keepdims=True)

    return pl.pallas_call(
        body, name=name, grid=(R // tr,),
        in_specs=[_vspec((tr, C), lambda i: (i, 0))], out_specs=_vspec((1, C), lambda i: (0, 0)),
        out_shape=jax.ShapeDtypeStruct((1, C), F32), compiler_params=_params(("arbitrary",)),
    )(x)


def _cmul_add(ar, ai, sr, si, br, bi):
    return ar * sr - ai * si + br, ar * si + ai * sr + bi


def _segment_starts(fin, pr, pi, reverse):
    H = fin.shape[1] // 2
    rows = lax.broadcasted_iota(jnp.int32, fin.shape, 0)
    cr = jnp.zeros((1, H), F32)
    ci = jnp.zeros((1, H), F32)
    out = jnp.zeros(fin.shape, F32)
    order = range(N_SEG - 1, -1, -1) if reverse else range(N_SEG)
    for k in order:
        out = jnp.where(rows == k, jnp.concatenate([cr, ci], axis=1), out)
        cr, ci = _cmul_add(pr, pi, cr, ci, fin[k:k + 1, :H], fin[k:k + 1, H:])
    return out


def _gelu(y):
    c0 = math.sqrt(2.0 / math.pi)
    t = jnp.tanh(c0 * (y + 0.044715 * y * y * y))
    return 0.5 * y * (1.0 + t), t


def s5_scan_fwd(u_p, a_l, apow_l, bm, cm, d_l, glu, fin, *, name):
    L, C = u_p.shape
    NK, _, SW = bm.shape
    H = SW // 2
    RB = S5_STEPS * N_SEG
    NC = L // RB
    final_only = fin is None

    def scan_chunk(a_ref, buf, st):
        ar = jnp.broadcast_to(a_ref[:, :H], (N_SEG, H))
        ai = jnp.broadcast_to(a_ref[:, H:], (N_SEG, H))

        def step(j, carry):
            sr, si = carry
            rows = pl.ds(pl.multiple_of(j * N_SEG, N_SEG), N_SEG)
            sr, si = _cmul_add(ar, ai, sr, si, buf[rows, :H], buf[rows, H:])
            buf[rows, :H] = sr
            buf[rows, H:] = si
            return sr, si

        sr, si = lax.fori_loop(0, S5_STEPS, step, (st[:, :H], st[:, H:]))
        st[:, :H] = sr
        st[:, H:] = si

    if final_only:
        def body(u_ref, a_ref, bm_ref, fin_ref, buf, st):
            @pl.when(pl.program_id(1) == 0)
            def _():
                st[...] = jnp.zeros_like(st)

            buf[...] = jnp.dot(u_ref[...].astype(BF16), bm_ref[...], preferred_element_type=F32)
            scan_chunk(a_ref, buf, st)
            fin_ref[...] = st[...]

        return pl.pallas_call(
            body, name=name, grid=(NK, NC),
            in_specs=[_vspec((RB, LANES), lambda k, j: (j, k)), _vspec((None, 1, SW), lambda k, j: (k, 0, 0)),
                      _vspec((None, LANES, SW), lambda k, j: (k, 0, 0))],
            out_specs=_vspec((None, N_SEG, SW), lambda k, j: (k, 0, 0)),
            out_shape=jax.ShapeDtypeStruct((NK, N_SEG, SW), F32),
            scratch_shapes=[pltpu.VMEM((RB, SW), F32), pltpu.VMEM((N_SEG, SW), F32)],
            compiler_params=_params(("parallel", "arbitrary")),
        )(u_p, a_l, bm)

    def body(u_ref, a_ref, ap_ref, bm_ref, cm_ref, d_ref, glu_ref, fin_ref, o_ref, y_ref, s_ref, start_ref, buf, st):
        @pl.when(pl.program_id(1) == 0)
        def _():
            st[...] = _segment_starts(fin_ref[...], ap_ref[:, :H], ap_ref[:, H:], False)
            start_ref[...] = st[...]

        u = u_ref[...]
        buf[...] = jnp.dot(u.astype(BF16), bm_ref[...], preferred_element_type=F32)
        scan_chunk(a_ref, buf, st)
        states = buf[...]
        s_ref[...] = states
        y = jnp.dot(states.astype(BF16), cm_ref[...], preferred_element_type=F32) + d_ref[...] * u
        y_ref[...] = y
        yg, _ = _gelu(y)
        gate = jnp.dot(yg.astype(BF16), glu_ref[...], preferred_element_type=F32)
        o_ref[...] = (yg * jax.nn.sigmoid(gate)).astype(o_ref.dtype)

    blk = lambda k, j: (j, k)
    per_k = lambda k, j: (k, 0, 0)
    return pl.pallas_call(
        body, name=name, grid=(NK, NC),
        in_specs=[_vspec((RB, LANES), blk), _vspec((None, 1, SW), per_k), _vspec((None, 1, SW), per_k),
                  _vspec((None, LANES, SW), per_k), _vspec((None, SW, LANES), per_k), _vspec((1, LANES), lambda k, j: (0, k)),
                  _vspec((None, LANES, LANES), per_k), _vspec((None, N_SEG, SW), per_k)],
        out_specs=[_vspec((RB, LANES), blk), _vspec((RB, LANES), blk), _vspec((RB, SW), blk),
                   _vspec((None, N_SEG, SW), per_k)],
        out_shape=[jax.ShapeDtypeStruct((L, C), BF16), jax.ShapeDtypeStruct((L, C), F32),
                   jax.ShapeDtypeStruct((L, NK * SW), F32), jax.ShapeDtypeStruct((NK, N_SEG, SW), F32)],
        scratch_shapes=[pltpu.VMEM((RB, SW), F32), pltpu.VMEM((N_SEG, SW), F32)],
        compiler_params=_params(("parallel", "arbitrary")),
    )(u_p, a_l, apow_l, bm, cm, d_l, glu, fin)


def s5_out_bwd(dout_p, y_p, u_p, glu, *, name="s5_out_bwd"):
    L, C = y_p.shape
    NK = C // LANES
    tr = _tile8(L, 512)

    def body(do_ref, y_ref, u_ref, glu_ref, dy_ref, dglu_ref, dd_ref):
        @pl.when(pl.program_id(1) == 0)
        def _():
            dglu_ref[...] = jnp.zeros_like(dglu_ref)
            dd_ref[...] = jnp.zeros_like(dd_ref)

        y, do, w = y_ref[...], do_ref[...].astype(F32), glu_ref[...]
        yg, t = _gelu(y)
        sg = jax.nn.sigmoid(jnp.dot(yg.astype(BF16), w, preferred_element_type=F32))
        dgate = (do * yg * sg * (1.0 - sg)).astype(BF16)
        dyg = do * sg + lax.dot_general(dgate, w, (((1,), (1,)), ((), ())), preferred_element_type=F32)
        dglu_ref[...] += lax.dot_general(yg.astype(BF16), dgate, (((0,), (0,)), ((), ())), preferred_element_type=F32)
        c0 = math.sqrt(2.0 / math.pi)
        dgelu = 0.5 * (1.0 + t) + 0.5 * y * (1.0 - t * t) * c0 * (1.0 + 3.0 * 0.044715 * y * y)
        dy = dyg * dgelu
        dy_ref[...] = dy
        dd_ref[...] += jnp.sum(dy * u_ref[...], axis=0, keepdims=True)

    blk = lambda k, i: (i, k)
    return pl.pallas_call(
        body, name=name, grid=(NK, L // tr),
        in_specs=[_vspec((tr, LANES), blk)] * 3 + [_vspec((None, LANES, LANES), lambda k, i: (k, 0, 0))],
        out_specs=[_vspec((tr, LANES), blk), _vspec((None, LANES, LANES), lambda k, i: (k, 0, 0)),
                   _vspec((1, LANES), lambda k, i: (0, k))],
        out_shape=[jax.ShapeDtypeStruct((L, C), F32), jax.ShapeDtypeStruct((NK, LANES, LANES), F32),
                   jax.ShapeDtypeStruct((1, C), F32)],
        compiler_params=_params(("parallel", "arbitrary")),
    )(dout_p, y_p, u_p, glu)


def s5_scan_bwd(dy_p, a_l, apow_l, cmt, gfin, states=None, starts=None, u_p=None, bmt=None, d_l=None, *, name):
    L, C = dy_p.shape
    NK, _, SW = cmt.shape
    H = SW // 2
    RB = S5_STEPS * N_SEG
    NC = L // RB
    final_only = gfin is None

    def scan_chunk(a_ref, buf, st):
        ar = jnp.broadcast_to(a_ref[:, :H], (N_SEG, H))
        ai = -jnp.broadcast_to(a_ref[:, H:], (N_SEG, H))

        def step(jj, carry):
            gr, gi = carry
            j = S5_STEPS - 1 - jj
            rows = pl.ds(pl.multiple_of(j * N_SEG, N_SEG), N_SEG)
            gr, gi = _cmul_add(ar, ai, gr, gi, buf[rows, :H], buf[rows, H:])
            buf[rows, :H] = gr
            buf[rows, H:] = gi
            return gr, gi

        gr, gi = lax.fori_loop(0, S5_STEPS, step, (st[:, :H], st[:, H:]))
        st[:, :H] = gr
        st[:, H:] = gi

    rblk = lambda k, j: (NC - 1 - j, k)
    per_k = lambda k, j: (k, 0, 0)

    if final_only:
        def body(dy_ref, a_ref, cmt_ref, fin_ref, buf, st):
            @pl.when(pl.program_id(1) == 0)
            def _():
                st[...] = jnp.zeros_like(st)

            buf[...] = jnp.dot(dy_ref[...].astype(BF16), cmt_ref[...], preferred_element_type=F32)
            scan_chunk(a_ref, buf, st)
            fin_ref[...] = st[...]

        return pl.pallas_call(
            body, name=name, grid=(NK, NC),
            in_specs=[_vspec((RB, LANES), rblk), _vspec((None, 1, SW), per_k), _vspec((None, LANES, SW), per_k)],
            out_specs=_vspec((None, N_SEG, SW), per_k),
            out_shape=jax.ShapeDtypeStruct((NK, N_SEG, SW), F32),
            scratch_shapes=[pltpu.VMEM((RB, SW), F32), pltpu.VMEM((N_SEG, SW), F32)],
            compiler_params=_params(("parallel", "arbitrary")),
        )(dy_p, a_l, cmt)

    def body(dy_ref, a_ref, ap_ref, cmt_ref, gfin_ref, s_ref, sprev_ref, start_ref, u_ref, bmt_ref, d_ref,
             du_ref, da_ref, dbm_ref, dcm_ref, buf, st):
        jc = pl.program_id(1)

        @pl.when(jc == 0)
        def _():
            st[...] = _segment_starts(gfin_ref[...], ap_ref[:, :H], -ap_ref[:, H:], True)
            da_ref[...] = jnp.zeros_like(da_ref)
            dbm_ref[...] = jnp.zeros_like(dbm_ref)
            dcm_ref[...] = jnp.zeros_like(dcm_ref)

        dy = dy_ref[...]
        dyb = dy.astype(BF16)
        buf[...] = jnp.dot(dyb, cmt_ref[...], preferred_element_type=F32)
        scan_chunk(a_ref, buf, st)
        g = buf[...]
        s = s_ref[...]
        first = jnp.where(jc == NC - 1, start_ref[...], sprev_ref[...])
        sp = jnp.concatenate([first, s[:RB - N_SEG, :]], axis=0)
        gr, gi, pr, pi = g[:, :H], g[:, H:], sp[:, :H], sp[:, H:]
        da_ref[...] += jnp.concatenate([jnp.sum(gr * pr + gi * pi, axis=0, keepdims=True),
                                        jnp.sum(gi * pr - gr * pi, axis=0, keepdims=True)], axis=1)
        gb = g.astype(BF16)
        u = u_ref[...]
        du_ref[...] = (jnp.dot(gb, bmt_ref[...], preferred_element_type=F32) + dy * d_ref[...]).astype(du_ref.dtype)
        dbm_ref[...] += lax.dot_general(u.astype(BF16), gb, (((0,), (0,)), ((), ())), preferred_element_type=F32)
        dcm_ref[...] += lax.dot_general(s.astype(BF16), dyb, (((0,), (0,)), ((), ())), preferred_element_type=F32)

    prev8 = lambda k, j: (jnp.maximum((NC - 1 - j) * S5_STEPS - 1, 0), k)
    return pl.pallas_call(
        body, name=name, grid=(NK, NC),
        in_specs=[_vspec((RB, LANES), rblk), _vspec((None, 1, SW), per_k), _vspec((None, 1, SW), per_k),
                  _vspec((None, LANES, SW), per_k), _vspec((None, N_SEG, SW), per_k), _vspec((RB, SW), rblk),
                  _vspec((N_SEG, SW), prev8), _vspec((None, N_SEG, SW), per_k), _vspec((RB, LANES), rblk),
                  _vspec((None, SW, LANES), per_k), _vspec((1, LANES), lambda k, j: (0, k))],
        out_specs=[_vspec((RB, LANES), rblk), _vspec((None, 1, SW), per_k), _vspec((None, LANES, SW), per_k),
                   _vspec((None, SW, LANES), per_k)],
        out_shape=[jax.ShapeDtypeStruct((L, C), BF16), jax.ShapeDtypeStruct((NK, 1, SW), F32),
                   jax.ShapeDtypeStruct((NK, LANES, SW), F32), jax.ShapeDtypeStruct((NK, SW, LANES), F32)],
        scratch_shapes=[pltpu.VMEM((RB, SW), F32), pltpu.VMEM((N_SEG, SW), F32)],
        compiler_params=_params(("parallel", "arbitrary")),
    )(dy_p, a_l, apow_l, cmt, gfin, states, states, starts, u_p, bmt, d_l)


def _s5_prep(a_re, a_im, log_dt, b_re, b_im, c_re, c_im, d, glu_w, seg_len):
    G, P = a_re.shape
    Hc = b_re.shape[-1]
    gl = LANES // Hc
    nk = G // gl
    dt = jnp.exp(log_dt)[:, None]
    er = jnp.exp(a_re * dt)
    ab_r, ab_i = er * jnp.cos(a_im * dt), er * jnp.sin(a_im * dt)
    den = a_re * a_re + a_im * a_im
    nr, ni = ab_r - 1.0, ab_i
    q_r, q_i = (nr * a_re + ni * a_im) / den, (ni * a_re - nr * a_im) / den
    bb_r = q_r[..., None] * b_re - q_i[..., None] * b_im
    bb_i = q_r[..., None] * b_im + q_i[..., None] * b_re
    ep = jnp.exp(a_re * dt * seg_len)
    ap_r, ap_i = ep * jnp.cos(a_im * dt * seg_len), ep * jnp.sin(a_im * dt * seg_len)
    eye = jnp.eye(gl, dtype=F32)

    def lanes(t):
        return t.reshape(nk, 1, gl * P)

    def b_mat(t):
        return jnp.einsum("kgph,gq->kghqp", t.reshape(nk, gl, P, Hc), eye).reshape(nk, gl * Hc, gl * P)

    def c_mat(t):
        return jnp.einsum("kghp,gq->kgpqh", t.reshape(nk, gl, Hc, P), eye).reshape(nk, gl * P, gl * Hc)

    a_l = jnp.concatenate([lanes(ab_r), lanes(ab_i)], axis=-1)
    apow_l = jnp.concatenate([lanes(ap_r), lanes(ap_i)], axis=-1)
    bm = jnp.concatenate([b_mat(bb_r), b_mat(bb_i)], axis=-1)
    cm = jnp.concatenate([c_mat(c_re), -c_mat(c_im)], axis=1)
    glu = jnp.einsum("kgho,gq->kghqo", glu_w.reshape(nk, gl, Hc, Hc), eye).reshape(nk, gl * Hc, gl * Hc)
    return a_l, apow_l, bm, cm, d.reshape(1, G * Hc), glu


def _to_segments(t):
    L, C = t.shape
    return t.reshape(N_SEG, L // N_SEG, C).transpose(1, 0, 2).reshape(L, C)


def _from_segments(t):
    L, C = t.shape
    return t.reshape(L // N_SEG, N_SEG, C).transpose(1, 0, 2).reshape(L, C)


def _swa_probs(q_ref, kp_ref, kc_ref, bias_ref, sink_ref, n):
    QB = WINDOW
    rows = Q_PER_KV * QB
    q = q_ref[...].reshape(rows, HEAD_DIM)
    kk = jnp.concatenate([kp_ref[...], kc_ref[...]], axis=0)
    s = lax.dot_general(q, kk, (((1,), (1,)), ((), ())), preferred_element_type=F32) * (HEAD_DIM ** -0.5)
    s = s + bias_ref[...].reshape(rows, 2 * QB)
    qi = lax.broadcasted_iota(jnp.int32, (rows, 2 * QB), 0) % QB
    kj = lax.broadcasted_iota(jnp.int32, (rows, 2 * QB), 1)
    valid = ((kj < QB) & (kj > qi) & (n > 0)) | ((kj >= QB) & (kj - QB <= qi))
    s = jnp.where(valid, s, NEG_INF)
    sink = sink_ref[...]
    m = jnp.maximum(jnp.max(s, axis=1, keepdims=True), sink)
    e = jnp.exp(s - m)
    es = jnp.exp(sink - m)
    inv = 1.0 / (jnp.sum(e, axis=1, keepdims=True) + es)
    return q, kk, e * inv, es * inv


def _swa_specs(nq):
    qs = _vspec((Q_PER_KV, WINDOW, HEAD_DIM), lambda g, n: (g, n, 0))
    kprev = _vspec((None, WINDOW, HEAD_DIM), lambda g, n: (g, jnp.maximum(n - 1, 0), 0))
    kcur = _vspec((None, WINDOW, HEAD_DIM), lambda g, n: (g, n, 0))
    bias = _vspec((Q_PER_KV, WINDOW, 2 * WINDOW), lambda g, n: (g, 0, 0))
    sink = _vspec((Q_PER_KV * WINDOW, 1), lambda g, n: (g, 0))
    return qs, kprev, kcur, bias, sink


def swa_fwd(qT, kT, vT, bias, sink_col, *, name="swa_fwd"):
    NQ, L, _ = qT.shape
    NKV = kT.shape[0]
    qs, kprev, kcur, bs, sk = _swa_specs(NQ)

    def body(q_ref, kp_ref, kc_ref, vp_ref, vc_ref, bias_ref, sink_ref, o_ref):
        _, _, p, _ = _swa_probs(q_ref, kp_ref, kc_ref, bias_ref, sink_ref, pl.program_id(1))
        vv = jnp.concatenate([vp_ref[...], vc_ref[...]], axis=0)
        o = jnp.dot(p.astype(BF16), vv, preferred_element_type=F32)
        o_ref[...] = o.reshape(Q_PER_KV, WINDOW, HEAD_DIM).astype(o_ref.dtype)

    return pl.pallas_call(
        body, name=name, grid=(NKV, L // WINDOW),
        in_specs=[qs, kprev, kcur, kprev, kcur, bs, sk], out_specs=qs,
        out_shape=jax.ShapeDtypeStruct((NQ, L, HEAD_DIM), BF16),
        compiler_params=_params(("parallel", "arbitrary")),
    )(qT, kT, kT, vT, vT, bias, sink_col)


def swa_bwd(qT, kT, vT, bias, sink_col, doT, *, name="swa_bwd"):
    NQ, L, _ = qT.shape
    NKV = kT.shape[0]
    qs, kprev, kcur, bs, sk = _swa_specs(NQ)
    W = WINDOW

    def body(q_ref, kp_ref, kc_ref, vp_ref, vc_ref, bias_ref, sink_ref, do_ref,
             dq_ref, dk_ref, dv_ref, dbias_ref, dsink_ref):
        n = pl.program_id(1)

        @pl.when(n == 0)
        def _():
            dk_ref[...] = jnp.zeros_like(dk_ref)
            dv_ref[...] = jnp.zeros_like(dv_ref)
            dbias_ref[...] = jnp.zeros_like(dbias_ref)
            dsink_ref[...] = jnp.zeros_like(dsink_ref)

        q, kk, p, ps = _swa_probs(q_ref, kp_ref, kc_ref, bias_ref, sink_ref, n)
        vv = jnp.concatenate([vp_ref[...], vc_ref[...]], axis=0)
        do = do_ref[...].reshape(Q_PER_KV * W, HEAD_DIM)
        dp = lax.dot_general(do, vv, (((1,), (1,)), ((), ())), preferred_element_type=F32)
        delta = jnp.sum(p * dp, axis=1, keepdims=True)
        ds = p * (dp - delta)
        dsink_ref[...] += -ps * delta
        dbias_ref[...] += ds.reshape(Q_PER_KV, W, 2 * W)
        dsb = ds.astype(BF16)
        scale = HEAD_DIM ** -0.5
        dq = jnp.dot(dsb, kk, preferred_element_type=F32) * scale
        dq_ref[...] = dq.reshape(Q_PER_KV, W, HEAD_DIM).astype(dq_ref.dtype)
        dkk = lax.dot_general(dsb, q, (((0,), (0,)), ((), ())), preferred_element_type=F32) * scale
        dvv = lax.dot_general(p.astype(BF16), do, (((0,), (0,)), ((), ())), preferred_element_type=F32)

        @pl.when(n == 0)
        def _():
            dk_ref[0:W, :] += dkk[W:, :]
            dv_ref[0:W, :] += dvv[W:, :]

        @pl.when(n > 0)
        def _():
            rows = pl.ds(pl.multiple_of((n - 1) * W, W), 2 * W)
            dk_ref[rows, :] += dkk
            dv_ref[rows, :] += dvv

    whole = _vspec((None, L, HEAD_DIM), lambda g, n: (g, 0, 0))
    return pl.pallas_call(
        body, name=name, grid=(NKV, L // W),
        in_specs=[qs, kprev, kcur, kprev, kcur, bs, sk, qs],
        out_specs=[qs, whole, whole, bs, sk],
        out_shape=[jax.ShapeDtypeStruct((NQ, L, HEAD_DIM), BF16), jax.ShapeDtypeStruct((NKV, L, HEAD_DIM), F32),
                   jax.ShapeDtypeStruct((NKV, L, HEAD_DIM), F32), jax.ShapeDtypeStruct((NQ, W, 2 * W), F32),
                   jax.ShapeDtypeStruct((NQ * W, 1), F32)],
        compiler_params=_params(("parallel", "arbitrary")),
    )(qT, kT, kT, vT, vT, bias, sink_col, doT)


def _bucket_table():
    qi = np.arange(WINDOW)[:, None]
    kj = np.arange(2 * WINDOW)[None, :]
    rel = qi + WINDOW - kj
    max_exact = N_BUCKETS // 2
    n = np.maximum(rel, 0)
    nf = np.maximum(n, max_exact).astype(np.float32)
    large = max_exact + (np.log(nf / max_exact) / math.log(MAX_DISTANCE / max_exact) * (N_BUCKETS - max_exact)).astype(np.int32)
    large = np.minimum(large, N_BUCKETS - 1)
    return np.where(n < max_exact, n, large).astype(np.int32).reshape(-1)


def _xa_probs(q, k):
    hd = q.shape[1]
    s = lax.dot_general(q, k, (((1,), (1,)), ((), ())), preferred_element_type=F32) * (hd ** -0.5)
    e = jnp.exp(s - jnp.max(s, axis=1, keepdims=True))
    return e / jnp.sum(e, axis=1, keepdims=True)


def xattn_fwd(q, kv, *, name="xattn_fwd"):
    L, D = q.shape
    Mm = kv.shape[0]
    hd = D // X_HEADS
    tq = _tile8(L, 512)

    def body(q_ref, kv_ref, o_ref):
        for h in range(X_HEADS):
            cols = slice(h * hd, (h + 1) * hd)
            p = _xa_probs(q_ref[:, cols], kv_ref[:, cols])
            o_ref[:, cols] = jnp.dot(p.astype(BF16), kv_ref[:, D + h * hd:D + (h + 1) * hd],
                                     preferred_element_type=F32).astype(o_ref.dtype)

    return pl.pallas_call(
        body, name=name, grid=(L // tq,),
        in_specs=[_vspec((tq, D), lambda i: (i, 0)), _vspec((Mm, 2 * D), lambda i: (0, 0))],
        out_specs=_vspec((tq, D), lambda i: (i, 0)),
        out_shape=jax.ShapeDtypeStruct((L, D), BF16),
        compiler_params=_params(("parallel",)),
    )(q, kv)


def xattn_bwd(q, kv, do, *, name="xattn_bwd"):
    L, D = q.shape
    Mm = kv.shape[0]
    hd = D // X_HEADS
    tq = _tile8(L, 512)

    def body(q_ref, kv_ref, do_ref, dq_ref, dkv_ref):
        @pl.when(pl.program_id(0) == 0)
        def _():
            dkv_ref[...] = jnp.zeros_like(dkv_ref)

        for h in range(X_HEADS):
            cols = slice(h * hd, (h + 1) * hd)
            vcols = slice(D + h * hd, D + (h + 1) * hd)
            qh, kh, vh, doh = q_ref[:, cols], kv_ref[:, cols], kv_ref[:, vcols], do_ref[:, cols]
            p = _xa_probs(qh, kh)
            dp = lax.dot_general(doh, vh, (((1,), (1,)), ((), ())), preferred_element_type=F32)
            ds = (p * (dp - jnp.sum(p * dp, axis=1, keepdims=True)) * (hd ** -0.5)).astype(BF16)
            dq_ref[:, cols] = jnp.dot(ds, kh, preferred_element_type=F32).astype(dq_ref.dtype)
            dkv_ref[:, cols] += lax.dot_general(ds, qh, (((0,), (0,)), ((), ())), preferred_element_type=F32)
            dkv_ref[:, vcols] += lax.dot_general(p.astype(BF16), doh, (((0,), (0,)), ((), ())), preferred_element_type=F32)

    return pl.pallas_call(
        body, name=name, grid=(L // tq,),
        in_specs=[_vspec((tq, D), lambda i: (i, 0)), _vspec((Mm, 2 * D), lambda i: (0, 0)), _vspec((tq, D), lambda i: (i, 0))],
        out_specs=[_vspec((tq, D), lambda i: (i, 0)), _vspec((Mm, 2 * D), lambda i: (0, 0))],
        out_shape=[jax.ShapeDtypeStruct((L, D), BF16), jax.ShapeDtypeStruct((Mm, 2 * D), F32)],
        compiler_params=_params(("arbitrary",)),
    )(q, kv, do)


def adamw(w, g, m, v, *, name="adamw"):
    R, C = w.shape
    tr = _tile8(R, max(SUBLANES, (256 * 1024) // C // SUBLANES * SUBLANES))

    def body(w_ref, g_ref, m_ref, v_ref, d_ref, nm_ref, nv_ref):
        g_ = g_ref[...]
        nm = ADAM_B1 * m_ref[...] + (1.0 - ADAM_B1) * g_
        nv = ADAM_B2 * v_ref[...] + (1.0 - ADAM_B2) * (g_ * g_)
        m_hat = nm / (1.0 - ADAM_B1 ** ADAM_STEP)
        v_hat = nv / (1.0 - ADAM_B2 ** ADAM_STEP)
        d_ref[...] = -ADAM_LR * (m_hat / (jnp.sqrt(v_hat) + ADAM_EPS) + ADAM_WD * w_ref[...])
        nm_ref[...] = nm
        nv_ref[...] = nv

    spec = _vspec((tr, C), lambda i: (i, 0))
    return pl.pallas_call(
        body, name=name, grid=(R // tr,), in_specs=[spec] * 4, out_specs=[spec] * 3,
        out_shape=[jax.ShapeDtypeStruct((R, C), F32)] * 3, compiler_params=_params(("parallel",)),
    )(w, g, m, v)


def _place():
    x, y, c = lax.axis_index("x"), lax.axis_index("y"), lax.axis_index("c")
    chips = [(1 - x, y), (x, 1 - y), (1 - x, 1 - y)]
    return x, y, c, chips


def _remote(src, dst, send, recv, k, to):
    return pltpu.make_async_remote_copy(src_ref=src, dst_ref=dst, send_sem=send.at[k], recv_sem=recv.at[k],
                                        device_id=to, device_id_type=MESH)


def gather_shards(arrs, *, name="gather_shards"):
    n = len(arrs)

    def body(*refs):
        ins, outs = refs[:n], refs[n:2 * n]
        send, recv, lsem = refs[2 * n:]
        x, y, c, chips = _place()
        me = 2 * x + y
        sib = (x, y, 1 - c)
        local, sends, passes = [], [], []
        for i in range(n):
            cp = pltpu.make_async_copy(ins[i], outs[i].at[:, me], lsem.at[i])
            cp.start()
            local.append(cp)
        for i in range(n):
            h = ins[i].shape[0] // 2
            mine = pl.ds(c * h, h)
            for j, (px, py) in enumerate(chips):
                cp = _remote(ins[i].at[mine], outs[i].at[mine, me], send, recv, 6 * i + j, (px, py, c))
                cp.start()
                sends.append(cp)
        for i in range(n):
            h = ins[i].shape[0] // 2
            mine = pl.ds(c * h, h)
            for j, (px, py) in enumerate(chips):
                landed = outs[i].at[mine, 2 * px + py]
                _remote(ins[i].at[mine], landed, send, recv, 6 * i + j, (px, py, c)).wait_recv()
                cp = _remote(landed, landed, send, recv, 6 * i + 3 + j, sib)
                cp.start()
                passes.append(cp)
        for i in range(n):
            h = ins[i].shape[0] // 2
            theirs = pl.ds((1 - c) * h, h)
            for j, (px, py) in enumerate(chips):
                _remote(ins[i].at[theirs], outs[i].at[theirs, 2 * px + py], send, recv, 6 * i + 3 + j, sib).wait_recv()
        for cp in sends + passes:
            cp.wait_send()
        for cp in local:
            cp.wait()

    return pl.pallas_call(
        body, name=name, in_specs=[ANY] * n, out_specs=[ANY] * n,
        out_shape=[jax.ShapeDtypeStruct((a.shape[0], N_CHIPS) + a.shape[1:], a.dtype) for a in arrs],
        scratch_shapes=[pltpu.SemaphoreType.DMA((6 * n,)), pltpu.SemaphoreType.DMA((6 * n,)), pltpu.SemaphoreType.DMA((n,))],
        compiler_params=pltpu.CompilerParams(has_side_effects=True),
    )(*arrs)


def exchange_halves(gs, *, name="rs_exchange_halves"):
    n = len(gs)

    def body(*refs):
        ins, outs = refs[:n], refs[n:2 * n]
        send, recv = refs[2 * n:]
        x, y, c, _ = _place()
        sib = (x, y, 1 - c)
        cps = []
        for i in range(n):
            r2 = ins[i].shape[1] // 2
            cp = _remote(ins[i].at[:, pl.ds((1 - c) * r2, r2)], outs[i], send, recv, i, sib)
            cp.start()
            cps.append(cp)
        for cp in cps:
            cp.wait()

    return pl.pallas_call(
        body, name=name, in_specs=[ANY] * n, out_specs=[ANY] * n,
        out_shape=[jax.ShapeDtypeStruct((g.shape[0], g.shape[1] // 2, g.shape[2]), g.dtype) for g in gs],
        scratch_shapes=[pltpu.SemaphoreType.DMA((n,)), pltpu.SemaphoreType.DMA((n,))],
        compiler_params=pltpu.CompilerParams(has_side_effects=True),
    )(*gs)


def add_half(g, other, c_idx, *, name="rs_add_half"):
    S, R, C = g.shape
    r2 = R // 2
    tr = _tile8(r2, max(SUBLANES, (512 * 1024) // C // SUBLANES * SUBLANES))
    nb = r2 // tr

    def body(c_ref, g_ref, o_ref, out_ref):
        out_ref[...] = (g_ref[...].astype(F32) + o_ref[...].astype(F32)).astype(out_ref.dtype)

    return pl.pallas_call(
        body, name=name,
        grid_spec=pltpu.PrefetchScalarGridSpec(
            num_scalar_prefetch=1, grid=(S, nb),
            in_specs=[pl.BlockSpec((None, tr, C), lambda s, i, c_ref: (s, c_ref[0] * nb + i, 0)),
                      pl.BlockSpec((None, tr, C), lambda s, i, c_ref: (s, i, 0))],
            out_specs=pl.BlockSpec((None, tr, C), lambda s, i, c_ref: (s, i, 0))),
        out_shape=jax.ShapeDtypeStruct((S, r2, C), BF16),
        compiler_params=_params(("parallel", "parallel")),
    )(c_idx, g, other)


def scatter_partials(ps, *, name="rs_scatter_partials"):
    n = len(ps)

    def body(*refs):
        ins, outs = refs[:n], refs[n:2 * n]
        send, recv = refs[2 * n:]
        x, y, c, chips = _place()
        cps = []
        for i in range(n):
            for j, (px, py) in enumerate(chips):
                cp = _remote(ins[i].at[2 * px + py], outs[i].at[j], send, recv, 3 * i + j, (px, py, c))
                cp.start()
                cps.append(cp)
        for cp in cps:
            cp.wait()

    return pl.pallas_call(
        body, name=name, in_specs=[ANY] * n, out_specs=[ANY] * n,
        out_shape=[jax.ShapeDtypeStruct((N_CHIPS - 1,) + p.shape[1:], p.dtype) for p in ps],
        scratch_shapes=[pltpu.SemaphoreType.DMA((3 * n,)), pltpu.SemaphoreType.DMA((3 * n,))],
        compiler_params=pltpu.CompilerParams(has_side_effects=True),
    )(*ps)


def add_partials(p, got, me_idx, *, name="rs_add_partials"):
    S, r2, C = p.shape
    tr = _tile8(r2, max(SUBLANES, (512 * 1024) // C // SUBLANES * SUBLANES))

    def body(me_ref, p_ref, g_ref, out_ref):
        out_ref[...] = ((p_ref[...].astype(F32) + g_ref[0].astype(F32)) + g_ref[1].astype(F32)) + g_ref[2].astype(F32)

    return pl.pallas_call(
        body, name=name,
        grid_spec=pltpu.PrefetchScalarGridSpec(
            num_scalar_prefetch=1, grid=(r2 // tr,),
            in_specs=[pl.BlockSpec((None, tr, C), lambda i, me_ref: (me_ref[0], i, 0)),
                      pl.BlockSpec((N_CHIPS - 1, tr, C), lambda i, me_ref: (0, i, 0))],
            out_specs=pl.BlockSpec((tr, C), lambda i, me_ref: (i, 0))),
        out_shape=jax.ShapeDtypeStruct((r2, C), F32),
        compiler_params=_params(("parallel",)),
    )(me_idx, p, got)


def join_halves(fs, groups, *, name="rs_join_halves"):
    n = len(fs)
    n_out = len(groups)
    where = {}
    for o, (_, idxs) in enumerate(groups):
        for l, i in enumerate(idxs):
            where[i] = (o, l)

    def body(*refs):
        ins, outs = refs[:n], refs[n:n + n_out]
        send, recv, lsem = refs[n + n_out:]
        x, y, c, _ = _place()
        sib = (x, y, 1 - c)
        cps, local = [], []
        for i in range(n):
            o, l = where[i]
            r2 = ins[i].shape[0]
            dst = outs[o].at[l, pl.ds(c * r2, r2)]
            lc = pltpu.make_async_copy(ins[i], dst, lsem.at[i])
            lc.start()
            local.append(lc)
            cp = _remote(ins[i], dst, send, recv, i, sib)
            cp.start()
            cps.append(cp)
        for i in range(n):
            o, l = where[i]
            r2 = ins[i].shape[0]
            _remote(ins[i], outs[o].at[l, pl.ds((1 - c) * r2, r2)], send, recv, i, sib).wait_recv()
        for cp in cps:
            cp.wait_send()
        for lc in local:
            lc.wait()

    return pl.pallas_call(
        body, name=name, in_specs=[ANY] * n, out_specs=[ANY] * n_out,
        out_shape=[jax.ShapeDtypeStruct(shape, F32) for shape, _ in groups],
        scratch_shapes=[pltpu.SemaphoreType.DMA((n,)), pltpu.SemaphoreType.DMA((n,)), pltpu.SemaphoreType.DMA((n,))],
        compiler_params=pltpu.CompilerParams(has_side_effects=True),
    )(*fs)


def all_gather_rows(v, *, name="all_gather_small"):
    m, ncol = v.shape

    def body(x_ref, out_ref, send, recv, lsem):
        x, y, c, chips = _place()
        me, sib = (x, y, c), (x, y, 1 - c)

        def rows(px, py, pc):
            return out_ref.at[pl.ds((4 * px + 2 * py + pc) * m, m), :]

        def copy(k, block, to, src=None):
            return _remote(rows(*block) if src is None else src, rows(*block), send, recv, k, to)

        mine = pltpu.make_async_copy(x_ref, rows(*me), lsem)
        mine.start()
        first = [copy(0, me, sib, src=x_ref)]
        first += [copy(1 + j, me, (*chip, c), src=x_ref) for j, chip in enumerate(chips)]
        for cp in first:
            cp.start()
        passed = [copy(4 + j, (*chip, c), sib) for j, chip in enumerate(chips)]
        for j, chip in enumerate(chips):
            copy(1 + j, (*chip, c), me).wait_recv()
            passed[j].start()
        copy(0, sib, me).wait_recv()
        for j, chip in enumerate(chips):
            copy(4 + j, (*chip, 1 - c), me).wait_recv()
        for cp in first + passed:
            cp.wait_send()
        mine.wait()

    return pl.pallas_call(
        body, name=name,
        in_specs=[pl.BlockSpec(memory_space=pltpu.VMEM)], out_specs=pl.BlockSpec(memory_space=pltpu.VMEM),
        out_shape=jax.ShapeDtypeStruct((8 * m, ncol), v.dtype),
        scratch_shapes=[pltpu.SemaphoreType.DMA((7,)), pltpu.SemaphoreType.DMA((7,)), pltpu.SemaphoreType.DMA],
        compiler_params=pltpu.CompilerParams(vmem_limit_bytes=VMEM_LIMIT_BYTES, has_side_effects=True),
    )(v)


def sum_blocks(g8, *, name="sum_blocks"):
    nb, m, ncol = g8.shape
    tr = _tile8(m, 512)

    def body(g_ref, o_ref):
        acc = g_ref[0]
        for k in range(1, nb):
            acc = acc + g_ref[k]
        o_ref[...] = acc

    return pl.pallas_call(
        body, name=name, grid=(m // tr,),
        in_specs=[_vspec((nb, tr, ncol), lambda i: (0, i, 0))], out_specs=_vspec((tr, ncol), lambda i: (i, 0)),
        out_shape=jax.ShapeDtypeStruct((m, ncol), F32), compiler_params=_params(("parallel",)),
    )(g8)


def _pack(arrs, mult):
    flat = jnp.concatenate([a.reshape(-1) for a in arrs])
    pad = (-flat.shape[0]) % (mult * LANES)
    return jnp.pad(flat, (0, pad)).reshape(-1, LANES)


def _unpack(packed, like):
    flat = packed.reshape(-1)
    out, off = [], 0
    for a in like:
        out.append(flat[off:off + a.size].reshape(a.shape))
        off += a.size
    return out


def _rows2d(a):
    return a.reshape(-1, a.shape[-1])


def kernel(x, mem, norm_mix, norm_xattn, norm_ffn, norm_final, norm_mem, rel_bias, ev_w_in, ev_conv_w, s5_a_re, s5_a_im, s5_log_dt, s5_b_re, s5_b_im, s5_c_re, s5_c_im, s5_d, s5_glu_w, ev_w_out, od_w_qkv, od_b_qkv, od_sinks, od_w_out, xa_w_q, xa_w_kv, xa_w_o, ff_w_gate, ff_w_up, ff_conv_w, ff_conv_b, ff_w_down, loss_target, m_norm_mix, m_norm_xattn, m_norm_ffn, m_norm_final, m_norm_mem, m_rel_bias, m_ev_w_in, m_ev_conv_w, m_s5_a_re, m_s5_a_im, m_s5_log_dt, m_s5_b_re, m_s5_b_im, m_s5_c_re, m_s5_c_im, m_s5_d, m_s5_glu_w, m_ev_w_out, m_od_w_qkv, m_od_b_qkv, m_od_sinks, m_od_w_out, m_xa_w_q, m_xa_w_kv, m_xa_w_o, m_ff_w_gate, m_ff_w_up, m_ff_conv_w, m_ff_conv_b, m_ff_w_down, v_norm_mix, v_norm_xattn, v_norm_ffn, v_norm_final, v_norm_mem, v_rel_bias, v_ev_w_in, v_ev_conv_w, v_s5_a_re, v_s5_a_im, v_s5_log_dt, v_s5_b_re, v_s5_b_im, v_s5_c_re, v_s5_c_im, v_s5_d, v_s5_glu_w, v_ev_w_out, v_od_w_qkv, v_od_b_qkv, v_od_sinks, v_od_w_out, v_xa_w_q, v_xa_w_kv, v_xa_w_o, v_ff_w_gate, v_ff_w_up, v_ff_conv_w, v_ff_conv_b, v_ff_w_down):
    names = ["norm_mix", "norm_xattn", "norm_ffn", "norm_final", "norm_mem", "rel_bias", "ev_w_in", "ev_conv_w",
             "s5_a_re", "s5_a_im", "s5_log_dt", "s5_b_re", "s5_b_im", "s5_c_re", "s5_c_im", "s5_d", "s5_glu_w",
             "ev_w_out", "od_w_qkv", "od_b_qkv", "od_sinks", "od_w_out", "xa_w_q", "xa_w_kv", "xa_w_o",
             "ff_w_gate", "ff_w_up", "ff_conv_w", "ff_conv_b", "ff_w_down"]
    env = dict(locals())
    W = {k: env[k] for k in names}
    Mo = {k: env["m_" + k] for k in names}
    Vo = {k: env["v_" + k] for k in names}

    h = x[0]
    target = loss_target[0]
    L, D = h.shape
    depth = norm_mix.shape[0]
    c_idx = lax.axis_index("c").astype(jnp.int32).reshape(1)
    me_idx = (2 * lax.axis_index("x") + lax.axis_index("y")).astype(jnp.int32).reshape(1)

    col_sharded = ["ev_w_in", "od_w_qkv", "xa_w_kv", "ff_w_gate", "ff_w_up"]
    row_sharded = ["ev_w_out", "od_w_out", "xa_w_q", "xa_w_o", "ff_w_down"]
    small_sharded = ["ev_conv_w", "od_b_qkv", "ff_conv_w"]
    big = col_sharded + row_sharded
    gathered = gather_shards([W[k].astype(BF16) for k in big] + [W[k] for k in small_sharded])
    G = dict(zip(big + small_sharded, gathered))

    def wcol(k, l):
        return G[k][l]

    def wrow(k, l):
        g = G[k][l]
        return g.reshape(1, g.shape[0] * g.shape[1], g.shape[2])

    ev_conv_w_f = G["ev_conv_w"].transpose(0, 2, 1, 3).reshape(ev_conv_w.shape[0], 3, -1)
    od_b_qkv_f = G["od_b_qkv"].reshape(od_b_qkv.shape[0], 1, -1)
    ff_conv_w_f = G["ff_conv_w"].transpose(0, 2, 1, 3).reshape(ff_conv_w.shape[0], 3, -1)

    buckets = _bucket_table()
    NQ = D // HEAD_DIM
    NKV = NQ // Q_PER_KV
    bias_tab = jnp.take(rel_bias.T, buckets, axis=1).reshape(NQ, WINDOW, 2 * WINDOW)

    mem_n = rms_fwd(mem[0], norm_mem.reshape(1, D), name="rms_fwd_mem")

    saved = []
    for l in range(depth):
        i = l // 2
        s = {"h0": h}
        hn = rms_fwd(h, norm_mix[l].reshape(1, D))
        s["hn"] = hn
        if l % 2 == 0:
            A = ev_conv_w_f.shape[-1]
            z = mm_nn(hn, wcol("ev_w_in", i), name="mm_ev_in")
            ya = conv_mixer_fwd(z, ev_conv_w_f[i])
            prep = functools.partial(_s5_prep, seg_len=L // N_SEG)
            s5p = (s5_a_re[i], s5_a_im[i], s5_log_dt[i], s5_b_re[i], s5_b_im[i], s5_c_re[i], s5_c_im[i], s5_d[i], s5_glu_w[i])
            (a_l, apow_l, bm, cm, d_l, glu), prep_vjp = jax.vjp(prep, *s5p)
            bm16, cm16, glu16 = bm.astype(BF16), cm.astype(BF16), glu.astype(BF16)
            u_p = _to_segments(z[:, 3 * A:])
            fin = s5_scan_fwd(u_p, a_l, apow_l, bm16, cm16, d_l, glu16, None, name="s5_fwd_ends")
            ys_p, y_p, states, starts = s5_scan_fwd(u_p, a_l, apow_l, bm16, cm16, d_l, glu16, fin, name="s5_fwd")
            ycat = jnp.concatenate([ya, _from_segments(ys_p)], axis=1)
            s.update(z=z, u_p=u_p, y_p=y_p, states=states, starts=starts, ycat=ycat, prep_vjp=prep_vjp,
                     s5ops=(a_l, apow_l, bm16, cm16, d_l, glu16))
            h = mm_nn(ycat, wrow("ev_w_out", i), res=h, name="mm_ev_out")
        else:
            z = mm_nn(hn, wcol("od_w_qkv", i), bias=od_b_qkv_f[i], out_dtype=BF16, name="mm_od_qkv")
            qT = z[:, :NQ * HEAD_DIM].reshape(L, NQ, HEAD_DIM).transpose(1, 0, 2)
            kT = z[:, NQ * HEAD_DIM:(NQ + NKV) * HEAD_DIM].reshape(L, NKV, HEAD_DIM).transpose(1, 0, 2)
            vT = z[:, (NQ + NKV) * HEAD_DIM:].reshape(L, NKV, HEAD_DIM).transpose(1, 0, 2)
            sink_col = jnp.repeat(od_sinks[i], WINDOW).reshape(NQ * WINDOW, 1)
            oT = swa_fwd(qT, kT, vT, bias_tab, sink_col)
            o = oT.transpose(1, 0, 2).reshape(L, D)
            s.update(qT=qT, kT=kT, vT=vT, sink_col=sink_col, o=o)
            h = mm_nn(o, wrow("od_w_out", i), res=h, name="mm_od_out")
        s["h1"] = h
        hn2 = rms_fwd(h, norm_xattn[l].reshape(1, D))
        q = mm_nn(hn2, wrow("xa_w_q", l), out_dtype=BF16, name="mm_xa_q")
        kv = mm_nn(mem_n, wcol("xa_w_kv", l), out_dtype=BF16, name="mm_xa_kv")
        ox = xattn_fwd(q, kv)
        s.update(hn2=hn2, q=q, kv=kv, ox=ox)
        h = mm_nn(ox, wrow("xa_w_o", l), res=h, name="mm_xa_o")
        s["h2"] = h
        hn3 = rms_fwd(h, norm_ffn[l].reshape(1, D))
        gpre = mm_nn(hn3, wcol("ff_w_gate", l), name="mm_ff_gate")
        up = mm_nn(hn3, wcol("ff_w_up", l), name="mm_ff_up")
        act = ffn_act_fwd(gpre, up, ff_conv_w_f[l], ff_conv_b[l].reshape(1, -1))
        s.update(hn3=hn3, gpre=gpre, up=up, act=act)
        h = mm_nn(act, wrow("ff_w_down", l), res=h, name="mm_ff_down")
        saved.append(s)

    loss11, dh, dg_final = final_loss(h, norm_final.reshape(1, D), target)
    loss = lax.psum(loss11[0, 0], AXES)

    gw = {k: [None] * W[k].shape[0] for k in big}
    gs = {k: [None] * W[k].shape[0] for k in names if k not in big and W[k].ndim > 1 and k != "rel_bias"}
    dmem_n = None
    dbias_tab = jnp.zeros_like(bias_tab)

    def as_rows(g3, k):
        return g3.reshape(N_CHIPS, g3.shape[1] // N_CHIPS, g3.shape[2])

    for l in reversed(range(depth)):
        i = l // 2
        s = saved[l]
        dact = mm_nt(dh, wrow("ff_w_down", l), name="mm_ff_down_dx")
        gw["ff_w_down"][l] = as_rows(mm_tn(s["act"], dh, 1, name="mm_ff_down_dw"), "ff_w_down")
        dgpre, dup, dcw, dcb = ffn_act_bwd(dact, s["gpre"], s["up"], ff_conv_w_f[l], ff_conv_b[l].reshape(1, -1))
        gs["ff_conv_w"][l], gs["ff_conv_b"][l] = dcw, dcb[0]
        dhn3 = mm_nt(dgpre, wcol("ff_w_gate", l), name="mm_ff_gate_dx")
        dhn3 = mm_nt(dup, wcol("ff_w_up", l), res=dhn3, name="mm_ff_up_dx")
        gw["ff_w_gate"][l] = mm_tn(s["hn3"], dgpre, N_CHIPS, name="mm_ff_gate_dw")
        gw["ff_w_up"][l] = mm_tn(s["hn3"], dup, N_CHIPS, name="mm_ff_up_dw")
        dh, dg = rms_bwd(s["h2"], norm_ffn[l].reshape(1, D), dhn3, dh)
        gs["norm_ffn"][l] = dg[0]
        dox = mm_nt(dh, wrow("xa_w_o", l), out_dtype=BF16, name="mm_xa_o_dx")
        gw["xa_w_o"][l] = as_rows(mm_tn(s["ox"], dh, 1, name="mm_xa_o_dw"), "xa_w_o")
        dq, dkv = xattn_bwd(s["q"], s["kv"], dox)
        dhn2 = mm_nt(dq, wrow("xa_w_q", l), name="mm_xa_q_dx")
        gw["xa_w_q"][l] = as_rows(mm_tn(s["hn2"], dq, 1, name="mm_xa_q_dw"), "xa_w_q")
        gw["xa_w_kv"][l] = mm_tn(mem_n, dkv, N_CHIPS, name="mm_xa_kv_dw")
        dmem_n = mm_nt(dkv, wcol("xa_w_kv", l), res=dmem_n, name="mm_xa_kv_dx")
        dh, dg = rms_bwd(s["h1"], norm_xattn[l].reshape(1, D), dhn2, dh)
        gs["norm_xattn"][l] = dg[0]
        if l % 2 == 0:
            A = ev_conv_w_f.shape[-1]
            dycat = mm_nt(dh, wrow("ev_w_out", i), name="mm_ev_out_dx")
            gw["ev_w_out"][i] = as_rows(mm_tn(s["ycat"], dh, 1, name="mm_ev_out_dw"), "ev_w_out")
            dgb, dgc, dxa, dcw = conv_mixer_bwd(s["z"], ev_conv_w_f[i], dycat[:, :A])
            gs["ev_conv_w"][i] = dcw
            a_l, apow_l, bm16, cm16, d_l, glu16 = s["s5ops"]
            dys_p = _to_segments(dycat[:, A:])
            dy_p, dglu, dd = s5_out_bwd(dys_p, s["y_p"], s["u_p"], glu16)
            cmt = cm16.transpose(0, 2, 1)
            bmt = bm16.transpose(0, 2, 1)
            gfin = s5_scan_bwd(dy_p, a_l, apow_l, cmt, None, name="s5_bwd_ends")
            du_p, da, dbm, dcm = s5_scan_bwd(dy_p, a_l, apow_l, cmt, gfin, s["states"], s["starts"], s["u_p"], bmt, d_l,
                                            name="s5_bwd")
            dprm = s["prep_vjp"]((da, jnp.zeros_like(apow_l), dbm, dcm, dd, dglu))
            for k, g in zip(["s5_a_re", "s5_a_im", "s5_log_dt", "s5_b_re", "s5_b_im", "s5_c_re", "s5_c_im", "s5_d", "s5_glu_w"], dprm):
                gs[k][i] = g
            dz = jnp.concatenate([dgb, dgc, dxa, _from_segments(du_p)], axis=1)
            dhn = mm_nt(dz, wcol("ev_w_in", i), name="mm_ev_in_dx")
            gw["ev_w_in"][i] = mm_tn(s["hn"], dz, N_CHIPS, name="mm_ev_in_dw")
        else:
            do = mm_nt(dh, wrow("od_w_out", i), out_dtype=BF16, name="mm_od_out_dx")
            gw["od_w_out"][i] = as_rows(mm_tn(s["o"], dh, 1, name="mm_od_out_dw"), "od_w_out")
            doT = do.reshape(L, NQ, HEAD_DIM).transpose(1, 0, 2)
            dqT, dkT, dvT, dbias, dsink = swa_bwd(s["qT"], s["kT"], s["vT"], bias_tab, s["sink_col"], doT)
            dbias_tab = dbias_tab + dbias
            gs["od_sinks"][i] = jnp.sum(dsink.reshape(NQ, WINDOW), axis=1)
            dz = jnp.concatenate([dqT.transpose(1, 0, 2).reshape(L, NQ * HEAD_DIM),
                                  dkT.astype(BF16).transpose(1, 0, 2).reshape(L, NKV * HEAD_DIM),
                                  dvT.astype(BF16).transpose(1, 0, 2).reshape(L, NKV * HEAD_DIM)], axis=1)
            gs["od_b_qkv"][i] = col_sum(dz)[0]
            dhn = mm_nt(dz, wcol("od_w_qkv", i), name="mm_od_qkv_dx")
            gw["od_w_qkv"][i] = mm_tn(s["hn"], dz, N_CHIPS, name="mm_od_qkv_dw")
        dh, dg = rms_bwd(s["h0"], norm_mix[l].reshape(1, D), dhn, dh)
        gs["norm_mix"][l] = dg[0]

    grad_x = dh[None]
    _, dg_mem = rms_bwd(mem[0], norm_mem.reshape(1, D), dmem_n, jnp.zeros_like(dmem_n), name="rms_bwd_mem")
    onehot = (buckets[:, None] == np.arange(N_BUCKETS)[None, :]).astype(np.float32)
    d_rel_bias = jnp.dot(dbias_tab.reshape(NQ, -1), jnp.asarray(onehot), precision=lax.Precision.HIGHEST).T

    small = [k for k in names if k not in big]
    local_small = {k: (jnp.stack(gs[k]) if k in gs else None) for k in small}
    local_small["norm_final"] = dg_final[0]
    local_small["norm_mem"] = dg_mem[0]
    local_small["rel_bias"] = d_rel_bias
    full_shape = {k: W[k].shape for k in small}
    for k in small_sharded:
        full_shape[k] = local_small[k].shape
    lst = [local_small[k].reshape(full_shape[k]).astype(F32) for k in small]
    packed = _pack(lst, SUBLANES)
    m_rows = packed.shape[0]
    summed = sum_blocks(all_gather_rows(packed).reshape(8, m_rows, LANES))
    gsum = dict(zip(small, _unpack(summed, lst)))
    for k in small_sharded:
        n4 = W[k].shape[-1]
        gsum[k] = lax.dynamic_slice_in_dim(gsum[k], me_idx[0] * n4, n4, axis=gsum[k].ndim - 1)

    flat, layer_of = [], []
    for k in big:
        for l, g in enumerate(gw[k]):
            layer_of.append((k, l))
            flat.append(g)
    other = exchange_halves(flat)
    partial = [add_half(g, o, c_idx) for g, o in zip(flat, other)]
    got = scatter_partials(partial)
    halves = [add_partials(p, r, me_idx) for p, r in zip(partial, got)]
    groups = []
    for k in big:
        idxs = [j for j, (kk, _) in enumerate(layer_of) if kk == k]
        nl = len(idxs)
        r2, C = halves[idxs[0]].shape
        groups.append(((nl, 2 * r2, C), idxs))
    joined = join_halves(halves, groups)
    gbig = {k: g.reshape(W[k].shape) for k, g in zip(big, joined)}

    grads = {**gsum, **gbig}
    delta, new_m, new_v = {}, {}, {}
    for k in big:
        d_, m_, v_ = adamw(_rows2d(W[k]), _rows2d(grads[k]), _rows2d(Mo[k]), _rows2d(Vo[k]), name="adamw_" + k)
        delta[k], new_m[k], new_v[k] = d_.reshape(W[k].shape), m_.reshape(W[k].shape), v_.reshape(W[k].shape)
    sw = [W[k] for k in small]
    d_, m_, v_ = adamw(_pack(sw, SUBLANES), _pack([grads[k] for k in small], SUBLANES),
                       _pack([Mo[k] for k in small], SUBLANES), _pack([Vo[k] for k in small], SUBLANES), name="adamw_small")
    for k, a, b, c_ in zip(small, _unpack(d_, sw), _unpack(m_, sw), _unpack(v_, sw)):
        delta[k], new_m[k], new_v[k] = a, b, c_

    return (loss, grad_x, *[grads[k] for k in names], *[delta[k] for k in names],
            *[new_m[k] for k in names], *[new_v[k] for k in names])
```

```python
import functools
import math

import numpy as np
import jax
import jax.numpy as jnp
from jax import lax
from jax.experimental import pallas as pl
from jax.experimental.pallas import tpu as pltpu

F32, BF16 = jnp.float32, jnp.bfloat16
MESH = pl.DeviceIdType.MESH
AXES = ("x", "y", "c")

VMEM_LIMIT_BYTES = 56 * 2**20
SUBLANES, LANES = 8, 128

RMS_EPS = 1e-5
S5_GROUP, S5_STATE = 16, 64
HEAD_DIM, Q_PER_KV, WINDOW = 64, 8, 128
N_BUCKETS, MAX_DISTANCE = 32, 128
X_HEADS = 4
NEG_INF = -1e30
ADAM_LR, ADAM_B1, ADAM_B2, ADAM_EPS, ADAM_WD, ADAM_STEP = 0.001, 0.9, 0.999, 1e-08, 0.01, 10
N_CHIPS = 4
N_SEG = 8
S5_STEPS = 32


def _params(sem=None):
    return pltpu.CompilerParams(dimension_semantics=sem, vmem_limit_bytes=VMEM_LIMIT_BYTES)


def _vspec(shape, index_map):
    return pl.BlockSpec(shape, index_map)


ANY = pl.BlockSpec(memory_space=pl.ANY)


def _tile(n, pref):
    t = (min(pref, n) // LANES) * LANES
    while t >= LANES:
        if n % t == 0:
            return t
        t -= LANES
    return n


def _acc_matmul(nk, k, acc, partial, finish):
    if nk == 1:
        finish(partial())
        return

    @pl.when(k == 0)
    def _():
        acc[...] = partial()

    @pl.when(jnp.logical_and(k > 0, k < nk - 1))
    def _():
        acc[...] += partial()

    @pl.when(k == nk - 1)
    def _():
        finish(acc[...] + partial())


MM_MAX_K = 2048


def mm_nn(a, w3, *, bias=None, res=None, out_dtype=F32, name):
    M, K = a.shape
    S, K2, ns = w3.shape
    assert K == K2
    tm = _tile(M, 512)
    tk = K if K <= MM_MAX_K else _tile(K, 1536)
    nk = K // tk
    has_b, has_r = bias is not None, res is not None

    def body(*refs):
        a_ref, w_ref = refs[0], refs[1]
        b_ref = refs[2] if has_b else None
        r_ref = refs[2 + has_b] if has_r else None
        o_ref, acc = refs[2 + has_b + has_r], refs[3 + has_b + has_r]

        def partial():
            return jnp.dot(a_ref[...].astype(BF16), w_ref[...], preferred_element_type=F32)

        def finish(r):
            if has_b:
                r = r + b_ref[...]
            if has_r:
                r = r + r_ref[...]
            o_ref[...] = r.astype(o_ref.dtype)

        _acc_matmul(nk, pl.program_id(2), acc, partial, finish)

    in_specs = [_vspec((tm, tk), lambda s, i, k: (i, k)), _vspec((None, tk, ns), lambda s, i, k: (s, k, 0))]
    ops = [a, w3]
    if has_b:
        in_specs.append(_vspec((1, ns), lambda s, i, k: (0, s)))
        ops.append(bias)
    if has_r:
        in_specs.append(_vspec((tm, ns), lambda s, i, k: (i, s)))
        ops.append(res)
    return pl.pallas_call(
        body, name=name, grid=(S, M // tm, nk), in_specs=in_specs,
        out_specs=_vspec((tm, ns), lambda s, i, k: (i, s)),
        out_shape=jax.ShapeDtypeStruct((M, S * ns), out_dtype),
        scratch_shapes=[pltpu.VMEM((tm, ns) if nk > 1 else (SUBLANES, LANES), F32)],
        compiler_params=_params(("parallel", "parallel", "arbitrary")),
    )(*ops)


def mm_nt(a, w3, *, res=None, out_dtype=F32, name):
    M, N = a.shape
    S, K, ns = w3.shape
    assert N == S * ns
    tko = K if K <= MM_MAX_K else _tile(K, 1024)
    tm = _tile(M, 512 if tko == K else 1024)
    tc = ns if ns <= MM_MAX_K else _tile(ns, 1024)
    ncs = ns // tc
    nc = S * ncs
    has_r = res is not None

    def body(*refs):
        a_ref, w_ref = refs[0], refs[1]
        r_ref = refs[2] if has_r else None
        o_ref, acc = refs[2 + has_r], refs[3 + has_r]

        def partial():
            return lax.dot_general(a_ref[...].astype(BF16), w_ref[...], (((1,), (1,)), ((), ())),
                                   preferred_element_type=F32)

        def finish(r):
            if has_r:
                r = r + r_ref[...]
            o_ref[...] = r.astype(o_ref.dtype)

        _acc_matmul(nc, pl.program_id(2), acc, partial, finish)

    in_specs = [_vspec((tm, tc), lambda i, j, k: (i, k)),
                _vspec((None, tko, tc), lambda i, j, k: (k // ncs, j, k % ncs))]
    ops = [a, w3]
    if has_r:
        in_specs.append(_vspec((tm, tko), lambda i, j, k: (i, j)))
        ops.append(res)
    return pl.pallas_call(
        body, name=name, grid=(M // tm, K // tko, nc), in_specs=in_specs,
        out_specs=_vspec((tm, tko), lambda i, j, k: (i, j)),
        out_shape=jax.ShapeDtypeStruct((M, K), out_dtype),
        scratch_shapes=[pltpu.VMEM((tm, tko) if nc > 1 else (SUBLANES, LANES), F32)],
        compiler_params=_params(("parallel", "parallel", "arbitrary")),
    )(*ops)


def mm_tn(x, dy, S, *, out_dtype=BF16, name):
    M, K = x.shape
    M2, N = dy.shape
    assert M == M2 and N % S == 0
    ns = N // S
    tk, tmc = _tile(K, 1024), _tile(M, 1024)
    nm = M // tmc

    def body(x_ref, dy_ref, o_ref, acc):
        def partial():
            return lax.dot_general(x_ref[...].astype(BF16), dy_ref[...].astype(BF16), (((0,), (0,)), ((), ())),
                                   preferred_element_type=F32)

        def finish(r):
            o_ref[...] = r.astype(o_ref.dtype)

        _acc_matmul(nm, pl.program_id(2), acc, partial, finish)

    return pl.pallas_call(
        body, name=name, grid=(S, K // tk, nm),
        in_specs=[_vspec((tmc, tk), lambda s, i, m: (m, i)), _vspec((tmc, ns), lambda s, i, m: (m, s))],
        out_specs=_vspec((None, tk, ns), lambda s, i, m: (s, i, 0)),
        out_shape=jax.ShapeDtypeStruct((S, K, ns), out_dtype),
        scratch_shapes=[pltpu.VMEM((tk, ns) if nm > 1 else (SUBLANES, LANES), F32)],
        compiler_params=_params(("parallel", "parallel", "arbitrary")),
    )(x, dy)


def rms_fwd(h, g, *, name="rms_fwd"):
    R, D = h.shape
    tr = _tile8(R, 256)

    def body(h_ref, g_ref, o_ref):
        x = h_ref[...]
        r = lax.rsqrt(jnp.mean(x * x, axis=-1, keepdims=True) + RMS_EPS)
        o_ref[...] = (x * r * g_ref[...]).astype(o_ref.dtype)

    return pl.pallas_call(
        body, name=name, grid=(R // tr,),
        in_specs=[_vspec((tr, D), lambda i: (i, 0)), _vspec((1, D), lambda i: (0, 0))],
        out_specs=_vspec((tr, D), lambda i: (i, 0)),
        out_shape=jax.ShapeDtypeStruct((R, D), BF16),
        compiler_params=_params(("parallel",)),
    )(h, g)


def _tile8(n, pref):
    t = (min(pref, n) // SUBLANES) * SUBLANES
    while t >= SUBLANES:
        if n % t == 0:
            return t
        t -= SUBLANES
    return n


def rms_bwd(h, g, dhn, dres, *, name="rms_bwd"):
    R, D = h.shape
    tr = _tile8(R, 256)

    def body(h_ref, g_ref, dhn_ref, dres_ref, dh_ref, dg_ref):
        @pl.when(pl.program_id(0) == 0)
        def _():
            dg_ref[...] = jnp.zeros_like(dg_ref)

        x = h_ref[...]
        d = dhn_ref[...].astype(F32)
        r = lax.rsqrt(jnp.mean(x * x, axis=-1, keepdims=True) + RMS_EPS)
        xhat = x * r
        dg_ref[...] += jnp.sum(d * xhat, axis=0, keepdims=True)
        t = d * g_ref[...]
        dh_ref[...] = dres_ref[...] + r * (t - xhat * jnp.mean(t * xhat, axis=-1, keepdims=True))

    return pl.pallas_call(
        body, name=name, grid=(R // tr,),
        in_specs=[_vspec((tr, D), lambda i: (i, 0)), _vspec((1, D), lambda i: (0, 0)),
                  _vspec((tr, D), lambda i: (i, 0)), _vspec((tr, D), lambda i: (i, 0))],
        out_specs=[_vspec((tr, D), lambda i: (i, 0)), _vspec((1, D), lambda i: (0, 0))],
        out_shape=[jax.ShapeDtypeStruct((R, D), F32), jax.ShapeDtypeStruct((1, D), F32)],
        compiler_params=_params(("arbitrary",)),
    )(h, g, dhn, dres)


def final_loss(h, g, target, *, name="final_loss"):
    R, D = h.shape
    tr = _tile8(R, 256)

    def body(h_ref, g_ref, t_ref, loss_ref, dh_ref, dg_ref):
        @pl.when(pl.program_id(0) == 0)
        def _():
            dg_ref[...] = jnp.zeros_like(dg_ref)
            loss_ref[...] = jnp.zeros_like(loss_ref)

        x = h_ref[...]
        r = lax.rsqrt(jnp.mean(x * x, axis=-1, keepdims=True) + RMS_EPS)
        xhat = x * r
        err = xhat * g_ref[...] - t_ref[...]
        row = jnp.mean(err * err, axis=-1, keepdims=True)
        loss_ref[...] += 0.5 * jnp.sum(row, axis=0, keepdims=True)
        d = err * (1.0 / D)
        dg_ref[...] += jnp.sum(d * xhat, axis=0, keepdims=True)
        t = d * g_ref[...]
        dh_ref[...] = r * (t - xhat * jnp.mean(t * xhat, axis=-1, keepdims=True))

    return pl.pallas_call(
        body, name=name, grid=(R // tr,),
        in_specs=[_vspec((tr, D), lambda i: (i, 0)), _vspec((1, D), lambda i: (0, 0)), _vspec((tr, D), lambda i: (i, 0))],
        out_specs=[_vspec((1, 1), lambda i: (0, 0)), _vspec((tr, D), lambda i: (i, 0)), _vspec((1, D), lambda i: (0, 0))],
        out_shape=[jax.ShapeDtypeStruct((1, 1), F32), jax.ShapeDtypeStruct((R, D), F32), jax.ShapeDtypeStruct((1, D), F32)],
        compiler_params=_params(("arbitrary",)),
    )(h, g, target)


def _shift_down(v, k):
    rows = lax.broadcasted_iota(jnp.int32, v.shape, 0)
    return jnp.where(rows >= k, pltpu.roll(v, k, axis=0), 0.0)


def _shift_up(v, k):
    L = v.shape[0]
    rows = lax.broadcasted_iota(jnp.int32, v.shape, 0)
    return jnp.where(rows < L - k, pltpu.roll(v, L - k, axis=0), 0.0)


def _conv(v, w):
    return w[2:3, :] * v + w[1:2, :] * _shift_down(v, 1) + w[0:1, :] * _shift_down(v, 2)


def _conv_t(d, w):
    return w[2:3, :] * d + w[1:2, :] * _shift_up(d, 1) + w[0:1, :] * _shift_up(d, 2)


def _conv_dw(d, v):
    return jnp.concatenate([
        jnp.sum(d * _shift_down(v, 2), axis=0, keepdims=True),
        jnp.sum(d * _shift_down(v, 1), axis=0, keepdims=True),
        jnp.sum(d * v, axis=0, keepdims=True)], axis=0)


COL_BLOCK = 128


def conv_mixer_fwd(z, cw, *, name="conv_mixer_fwd"):
    L = z.shape[0]
    A = cw.shape[1]
    cb = _tile(A, COL_BLOCK)
    nb = A // cb

    def body(gb_ref, gc_ref, xa_ref, w_ref, o_ref):
        v = gc_ref[...] * xa_ref[...]
        o_ref[...] = (gb_ref[...] * _conv(v, w_ref[...])).astype(o_ref.dtype)

    return pl.pallas_call(
        body, name=name, grid=(nb,),
        in_specs=[_vspec((L, cb), lambda j: (0, j)), _vspec((L, cb), lambda j: (0, nb + j)),
                  _vspec((L, cb), lambda j: (0, 2 * nb + j)), _vspec((3, cb), lambda j: (0, j))],
        out_specs=_vspec((L, cb), lambda j: (0, j)),
        out_shape=jax.ShapeDtypeStruct((L, A), BF16),
        compiler_params=_params(("parallel",)),
    )(z, z, z, cw)


def conv_mixer_bwd(z, cw, dya, *, name="conv_mixer_bwd"):
    L = z.shape[0]
    A = cw.shape[1]
    cb = _tile(A, COL_BLOCK)
    nb = A // cb

    def body(gb_ref, gc_ref, xa_ref, w_ref, d_ref, dgb_ref, dgc_ref, dxa_ref, dw_ref):
        gc, xa, w, d = gc_ref[...], xa_ref[...], w_ref[...], d_ref[...]
        v = gc * xa
        dgb_ref[...] = (d * _conv(v, w)).astype(dgb_ref.dtype)
        dc = d * gb_ref[...]
        dw_ref[...] = _conv_dw(dc, v)
        dv = _conv_t(dc, w)
        dgc_ref[...] = (dv * xa).astype(dgc_ref.dtype)
        dxa_ref[...] = (dv * gc).astype(dxa_ref.dtype)

    col = lambda j: (0, j)
    outs = pl.pallas_call(
        body, name=name, grid=(nb,),
        in_specs=[_vspec((L, cb), col), _vspec((L, cb), lambda j: (0, nb + j)),
                  _vspec((L, cb), lambda j: (0, 2 * nb + j)), _vspec((3, cb), col), _vspec((L, cb), col)],
        out_specs=[_vspec((L, cb), col), _vspec((L, cb), col), _vspec((L, cb), col), _vspec((3, cb), col)],
        out_shape=[jax.ShapeDtypeStruct((L, A), BF16)] * 3 + [jax.ShapeDtypeStruct((3, A), F32)],
        compiler_params=_params(("parallel",)),
    )(z, z, z, cw, dya)
    return outs[0], outs[1], outs[2], outs[3]


def ffn_act_fwd(gpre, up, cw, cbias, *, name="ffn_act_fwd"):
    L, Fd = gpre.shape
    cb = _tile(Fd, COL_BLOCK)

    def body(g_ref, u_ref, w_ref, b_ref, o_ref):
        g = _conv(g_ref[...], w_ref[...]) + b_ref[...]
        o_ref[...] = (g * jax.nn.sigmoid(g) * u_ref[...]).astype(o_ref.dtype)

    col = lambda j: (0, j)
    return pl.pallas_call(
        body, name=name, grid=(Fd // cb,),
        in_specs=[_vspec((L, cb), col), _vspec((L, cb), col), _vspec((3, cb), col), _vspec((1, cb), col)],
        out_specs=_vspec((L, cb), col),
        out_shape=jax.ShapeDtypeStruct((L, Fd), BF16),
        compiler_params=_params(("parallel",)),
    )(gpre, up, cw, cbias)


def ffn_act_bwd(dact, gpre, up, cw, cbias, *, name="ffn_act_bwd"):
    L, Fd = gpre.shape
    cb = _tile(Fd, COL_BLOCK)

    def body(d_ref, g_ref, u_ref, w_ref, b_ref, dg_ref, du_ref, dw_ref, db_ref):
        gp, w, d = g_ref[...], w_ref[...], d_ref[...]
        g = _conv(gp, w) + b_ref[...]
        sg = jax.nn.sigmoid(g)
        du_ref[...] = (d * (g * sg)).astype(du_ref.dtype)
        dg = d * u_ref[...] * (sg * (1.0 + g * (1.0 - sg)))
        db_ref[...] = jnp.sum(dg, axis=0, keepdims=True)
        dw_ref[...] = _conv_dw(dg, gp)
        dg_ref[...] = _conv_t(dg, w).astype(dg_ref.dtype)

    col = lambda j: (0, j)
    return pl.pallas_call(
        body, name=name, grid=(Fd // cb,),
        in_specs=[_vspec((L, cb), col)] * 3 + [_vspec((3, cb), col), _vspec((1, cb), col)],
        out_specs=[_vspec((L, cb), col), _vspec((L, cb), col), _vspec((3, cb), col), _vspec((1, cb), col)],
        out_shape=[jax.ShapeDtypeStruct((L, Fd), BF16), jax.ShapeDtypeStruct((L, Fd), BF16),
                   jax.ShapeDtypeStruct((3, Fd), F32), jax.ShapeDtypeStruct((1, Fd), F32)],
        compiler_params=_params(("parallel",)),
    )(dact, gpre, up, cw, cbias)


def col_sum(x, *, name="col_sum"):
    R, C = x.shape
    tr = _tile8(R, 512)

    def body(x_ref, o_ref):
        @pl.when(pl.program_id(0) == 0)
        def _():
            o_ref[...] = jnp.zeros_like(o_ref)

        o_ref[...] += jnp.sum(x_ref[...].astype(F32), axis=0, keepdims=True)

    return pl.pallas_call(
        body, name=name, grid=(R // tr,),
        in_specs=[_vspec((tr, C), lambda i: (i, 0))], out_specs=_vspec((1, C), lambda i: (0, 0)),
        out_shape=jax.ShapeDtypeStruct((1, C), F32), compiler_params=_params(("arbitrary",)),
    )(x)


def _cmul_add(ar, ai, sr, si, br, bi):
    return ar * sr - ai * si + br, ar * si + ai * sr + bi


def _segment_starts(fin, pr, pi, reverse):
    H = fin.shape[1] // 2
    rows = lax.broadcasted_iota(jnp.int32, fin.shape, 0)
    cr = jnp.zeros((1, H), F32)
    ci = jnp.zeros((1, H), F32)
    out = jnp.zeros(fin.shape, F32)
    order = range(N_SEG - 1, -1, -1) if reverse else range(N_SEG)
    for k in order:
        out = jnp.where(rows == k, jnp.concatenate([cr, ci], axis=1), out)
        cr, ci = _cmul_add(pr, pi, cr, ci, fin[k:k + 1, :H], fin[k:k + 1, H:])
    return out


def _gelu(y):
    c0 = math.sqrt(2.0 / math.pi)
    t = jnp.tanh(c0 * (y + 0.044715 * y * y * y))
    return 0.5 * y * (1.0 + t), t


def s5_scan_fwd(u_p, a_l, apow_l, bm, cm, d_l, glu, fin, *, name):
    L, C = u_p.shape
    NK, _, SW = bm.shape
    H = SW // 2
    RB = S5_STEPS * N_SEG
    NC = L // RB
    final_only = fin is None

    def scan_chunk(a_ref, buf, st):
        ar = jnp.broadcast_to(a_ref[:, :H], (N_SEG, H))
        ai = jnp.broadcast_to(a_ref[:, H:], (N_SEG, H))

        def step(j, carry):
            sr, si = carry
            rows = pl.ds(pl.multiple_of(j * N_SEG, N_SEG), N_SEG)
            sr, si = _cmul_add(ar, ai, sr, si, buf[rows, :H], buf[rows, H:])
            buf[rows, :H] = sr
            buf[rows, H:] = si
            return sr, si

        sr, si = lax.fori_loop(0, S5_STEPS, step, (st[:, :H], st[:, H:]))
        st[:, :H] = sr
        st[:, H:] = si

    if final_only:
        def body(u_ref, a_ref, bm_ref, fin_ref, buf, st):
            @pl.when(pl.program_id(1) == 0)
            def _():
                st[...] = jnp.zeros_like(st)

            buf[...] = jnp.dot(u_ref[...].astype(BF16), bm_ref[...], preferred_element_type=F32)
            scan_chunk(a_ref, buf, st)
            fin_ref[...] = st[...]

        return pl.pallas_call(
            body, name=name, grid=(NK, NC),
            in_specs=[_vspec((RB, LANES), lambda k, j: (j, k)), _vspec((None, 1, SW), lambda k, j: (k, 0, 0)),
                      _vspec((None, LANES, SW), lambda k, j: (k, 0, 0))],
            out_specs=_vspec((None, N_SEG, SW), lambda k, j: (k, 0, 0)),
            out_shape=jax.ShapeDtypeStruct((NK, N_SEG, SW), F32),
            scratch_shapes=[pltpu.VMEM((RB, SW), F32), pltpu.VMEM((N_SEG, SW), F32)],
            compiler_params=_params(("parallel", "arbitrary")),
        )(u_p, a_l, bm)

    def body(u_ref, a_ref, ap_ref, bm_ref, cm_ref, d_ref, glu_ref, fin_ref, o_ref, y_ref, s_ref, start_ref, buf, st):
        @pl.when(pl.program_id(1) == 0)
        def _():
            st[...] = _segment_starts(fin_ref[...], ap_ref[:, :H], ap_ref[:, H:], False)
            start_ref[...] = st[...]

        u = u_ref[...]
        buf[...] = jnp.dot(u.astype(BF16), bm_ref[...], preferred_element_type=F32)
        scan_chunk(a_ref, buf, st)
        states = buf[...]
        s_ref[...] = states
        y = jnp.dot(states.astype(BF16), cm_ref[...], preferred_element_type=F32) + d_ref[...] * u
        y_ref[...] = y
        yg, _ = _gelu(y)
        gate = jnp.dot(yg.astype(BF16), glu_ref[...], preferred_element_type=F32)
        o_ref[...] = (yg * jax.nn.sigmoid(gate)).astype(o_ref.dtype)

    blk = lambda k, j: (j, k)
    per_k = lambda k, j: (k, 0, 0)
    return pl.pallas_call(
        body, name=name, grid=(NK, NC),
        in_specs=[_vspec((RB, LANES), blk), _vspec((None, 1, SW), per_k), _vspec((None, 1, SW), per_k),
                  _vspec((None, LANES, SW), per_k), _vspec((None, SW, LANES), per_k), _vspec((1, LANES), lambda k, j: (0, k)),
                  _vspec((None, LANES, LANES), per_k), _vspec((None, N_SEG, SW), per_k)],
        out_specs=[_vspec((RB, LANES), blk), _vspec((RB, LANES), blk), _vspec((RB, SW), blk),
                   _vspec((None, N_SEG, SW), per_k)],
        out_shape=[jax.ShapeDtypeStruct((L, C), BF16), jax.ShapeDtypeStruct((L, C), F32),
                   jax.ShapeDtypeStruct((L, NK * SW), F32), jax.ShapeDtypeStruct((NK, N_SEG, SW), F32)],
        scratch_shapes=[pltpu.VMEM((RB, SW), F32), pltpu.VMEM((N_SEG, SW), F32)],
        compiler_params=_params(("parallel", "arbitrary")),
    )(u_p, a_l, apow_l, bm, cm, d_l, glu, fin)


def s5_out_bwd(dout_p, y_p, u_p, glu, *, name="s5_out_bwd"):
    L, C = y_p.shape
    NK = C // LANES
    tr = _tile8(L, 512)

    def body(do_ref, y_ref, u_ref, glu_ref, dy_ref, dglu_ref, dd_ref):
        @pl.when(pl.program_id(1) == 0)
        def _():
            dglu_ref[...] = jnp.zeros_like(dglu_ref)
            dd_ref[...] = jnp.zeros_like(dd_ref)

        y, do, w = y_ref[...], do_ref[...].astype(F32), glu_ref[...]
        yg, t = _gelu(y)
        sg = jax.nn.sigmoid(jnp.dot(yg.astype(BF16), w, preferred_element_type=F32))
        dgate = (do * yg * sg * (1.0 - sg)).astype(BF16)
        dyg = do * sg + lax.dot_general(dgate, w, (((1,), (1,)), ((), ())), preferred_element_type=F32)
        dglu_ref[...] += lax.dot_general(yg.astype(BF16), dgate, (((0,), (0,)), ((), ())), preferred_element_type=F32)
        c0 = math.sqrt(2.0 / math.pi)
        dgelu = 0.5 * (1.0 + t) + 0.5 * y * (1.0 - t * t) * c0 * (1.0 + 3.0 * 0.044715 * y * y)
        dy = dyg * dgelu
        dy_ref[...] = dy
        dd_ref[...] += jnp.sum(dy * u_ref[...], axis=0, keepdims=True)

    blk = lambda k, i: (i, k)
    return pl.pallas_call(
        body, name=name, grid=(NK, L // tr),
        in_specs=[_vspec((tr, LANES), blk)] * 3 + [_vspec((None, LANES, LANES), lambda k, i: (k, 0, 0))],
        out_specs=[_vspec((tr, LANES), blk), _vspec((None, LANES, LANES), lambda k, i: (k, 0, 0)),
                   _vspec((1, LANES), lambda k, i: (0, k))],
        out_shape=[jax.ShapeDtypeStruct((L, C), F32), jax.ShapeDtypeStruct((NK, LANES, LANES), F32),
                   jax.ShapeDtypeStruct((1, C), F32)],
        compiler_params=_params(("parallel", "arbitrary")),
    )(dout_p, y_p, u_p, glu)


def s5_scan_bwd(dy_p, a_l, apow_l, cmt, gfin, states=None, starts=None, u_p=None, bmt=None, d_l=None, *, name):
    L, C = dy_p.shape
    NK, _, SW = cmt.shape
    H = SW // 2
    RB = S5_STEPS * N_SEG
    NC = L // RB
    final_only = gfin is None

    def scan_chunk(a_ref, buf, st):
        ar = jnp.broadcast_to(a_ref[:, :H], (N_SEG, H))
        ai = -jnp.broadcast_to(a_ref[:, H:], (N_SEG, H))

        def step(jj, carry):
            gr, gi = carry
            j = S5_STEPS - 1 - jj
            rows = pl.ds(pl.multiple_of(j * N_SEG, N_SEG), N_SEG)
            gr, gi = _cmul_add(ar, ai, gr, gi, buf[rows, :H], buf[rows, H:])
            buf[rows, :H] = gr
            buf[rows, H:] = gi
            return gr, gi

        gr, gi = lax.fori_loop(0, S5_STEPS, step, (st[:, :H], st[:, H:]))
        st[:, :H] = gr
        st[:, H:] = gi

    rblk = lambda k, j: (NC - 1 - j, k)
    per_k = lambda k, j: (k, 0, 0)

    if final_only:
        def body(dy_ref, a_ref, cmt_ref, fin_ref, buf, st):
            @pl.when(pl.program_id(1) == 0)
            def _():
                st[...] = jnp.zeros_like(st)

            buf[...] = jnp.dot(dy_ref[...].astype(BF16), cmt_ref[...], preferred_element_type=F32)
            scan_chunk(a_ref, buf, st)
            fin_ref[...] = st[...]

        return pl.pallas_call(
            body, name=name, grid=(NK, NC),
            in_specs=[_vspec((RB, LANES), rblk), _vspec((None, 1, SW), per_k), _vspec((None, LANES, SW), per_k)],
            out_specs=_vspec((None, N_SEG, SW), per_k),
            out_shape=jax.ShapeDtypeStruct((NK, N_SEG, SW), F32),
            scratch_shapes=[pltpu.VMEM((RB, SW), F32), pltpu.VMEM((N_SEG, SW), F32)],
            compiler_params=_params(("parallel", "arbitrary")),
        )(dy_p, a_l, cmt)

    def body(dy_ref, a_ref, ap_ref, cmt_ref, gfin_ref, s_ref, sprev_ref, start_ref, u_ref, bmt_ref, d_ref,
             du_ref, da_ref, dbm_ref, dcm_ref, buf, st):
        jc = pl.program_id(1)

        @pl.when(jc == 0)
        def _():
            st[...] = _segment_starts(gfin_ref[...], ap_ref[:, :H], -ap_ref[:, H:], True)
            da_ref[...] = jnp.zeros_like(da_ref)
            dbm_ref[...] = jnp.zeros_like(dbm_ref)
            dcm_ref[...] = jnp.zeros_like(dcm_ref)

        dy = dy_ref[...]
        dyb = dy.astype(BF16)
        buf[...] = jnp.dot(dyb, cmt_ref[...], preferred_element_type=F32)
        scan_chunk(a_ref, buf, st)
        g = buf[...]
        s = s_ref[...]
        first = jnp.where(jc == NC - 1, start_ref[...], sprev_ref[...])
        sp = jnp.concatenate([first, s[:RB - N_SEG, :]], axis=0)
        gr, gi, pr, pi = g[:, :H], g[:, H:], sp[:, :H], sp[:, H:]
        da_ref[...] += jnp.concatenate([jnp.sum(gr * pr + gi * pi, axis=0, keepdims=True),
                                        jnp.sum(gi * pr - gr * pi, axis=0, keepdims=True)], axis=1)
        gb = g.astype(BF16)
        u = u_ref[...]
        du_ref[...] = (jnp.dot(gb, bmt_ref[...], preferred_element_type=F32) + dy * d_ref[...]).astype(du_ref.dtype)
        dbm_ref[...] += lax.dot_general(u.astype(BF16), gb, (((0,), (0,)), ((), ())), preferred_element_type=F32)
        dcm_ref[...] += lax.dot_general(s.astype(BF16), dyb, (((0,), (0,)), ((), ())), preferred_element_type=F32)

    prev8 = lambda k, j: (jnp.maximum((NC - 1 - j) * S5_STEPS - 1, 0), k)
    return pl.pallas_call(
        body, name=name, grid=(NK, NC),
        in_specs=[_vspec((RB, LANES), rblk), _vspec((None, 1, SW), per_k), _vspec((None, 1, SW), per_k),
                  _vspec((None, LANES, SW), per_k), _vspec((None, N_SEG, SW), per_k), _vspec((RB, SW), rblk),
                  _vspec((N_SEG, SW), prev8), _vspec((None, N_SEG, SW), per_k), _vspec((RB, LANES), rblk),
                  _vspec((None, SW, LANES), per_k), _vspec((1, LANES), lambda k, j: (0, k))],
        out_specs=[_vspec((RB, LANES), rblk), _vspec((None, 1, SW), per_k), _vspec((None, LANES, SW), per_k),
                   _vspec((None, SW, LANES), per_k)],
        out_shape=[jax.ShapeDtypeStruct((L, C), BF16), jax.ShapeDtypeStruct((NK, 1, SW), F32),
                   jax.ShapeDtypeStruct((NK, LANES, SW), F32), jax.ShapeDtypeStruct((NK, SW, LANES), F32)],
        scratch_shapes=[pltpu.VMEM((RB, SW), F32), pltpu.VMEM((N_SEG, SW), F32)],
        compiler_params=_params(("parallel", "arbitrary")),
    )(dy_p, a_l, apow_l, cmt, gfin, states, states, starts, u_p, bmt, d_l)


def _s5_prep(a_re, a_im, log_dt, b_re, b_im, c_re, c_im, d, glu_w, seg_len):
    G, P = a_re.shape
    Hc = b_re.shape[-1]
    gl = LANES // Hc
    nk = G // gl
    dt = jnp.exp(log_dt)[:, None]
    er = jnp.exp(a_re * dt)
    ab_r, ab_i = er * jnp.cos(a_im * dt), er * jnp.sin(a_im * dt)
    den = a_re * a_re + a_im * a_im
    nr, ni = ab_r - 1.0, ab_i
    q_r, q_i = (nr * a_re + ni * a_im) / den, (ni * a_re - nr * a_im) / den
    bb_r = q_r[..., None] * b_re - q_i[..., None] * b_im
    bb_i = q_r[..., None] * b_im + q_i[..., None] * b_re
    ep = jnp.exp(a_re * dt * seg_len)
    ap_r, ap_i = ep * jnp.cos(a_im * dt * seg_len), ep * jnp.sin(a_im * dt * seg_len)
    eye = jnp.eye(gl, dtype=F32)

    def lanes(t):
        return t.reshape(nk, 1, gl * P)

    def b_mat(t):
        return jnp.einsum("kgph,gq->kghqp", t.reshape(nk, gl, P, Hc), eye).reshape(nk, gl * Hc, gl * P)

    def c_mat(t):
        return jnp.einsum("kghp,gq->kgpqh", t.reshape(nk, gl, Hc, P), eye).reshape(nk, gl * P, gl * Hc)

    a_l = jnp.concatenate([lanes(ab_r), lanes(ab_i)], axis=-1)
    apow_l = jnp.concatenate([lanes(ap_r), lanes(ap_i)], axis=-1)
    bm = jnp.concatenate([b_mat(bb_r), b_mat(bb_i)], axis=-1)
    cm = jnp.concatenate([c_mat(c_re), -c_mat(c_im)], axis=1)
    glu = jnp.einsum("kgho,gq->kghqo", glu_w.reshape(nk, gl, Hc, Hc), eye).reshape(nk, gl * Hc, gl * Hc)
    return a_l, apow_l, bm, cm, d.reshape(1, G * Hc), glu


def _to_segments(t):
    L, C = t.shape
    return t.reshape(N_SEG, L // N_SEG, C).transpose(1, 0, 2).reshape(L, C)


def _from_segments(t):
    L, C = t.shape
    return t.reshape(L // N_SEG, N_SEG, C).transpose(1, 0, 2).reshape(L, C)


def _swa_probs(q_ref, kp_ref, kc_ref, bias_ref, sink_ref, n):
    QB = WINDOW
    rows = Q_PER_KV * QB
    q = q_ref[...].reshape(rows, HEAD_DIM)
    kk = jnp.concatenate([kp_ref[...], kc_ref[...]], axis=0)
    s = lax.dot_general(q, kk, (((1,), (1,)), ((), ())), preferred_element_type=F32) * (HEAD_DIM ** -0.5)
    s = s + bias_ref[...].reshape(rows, 2 * QB)
    qi = lax.broadcasted_iota(jnp.int32, (rows, 2 * QB), 0) % QB
    kj = lax.broadcasted_iota(jnp.int32, (rows, 2 * QB), 1)
    valid = ((kj < QB) & (kj > qi) & (n > 0)) | ((kj >= QB) & (kj - QB <= qi))
    s = jnp.where(valid, s, NEG_INF)
    sink = sink_ref[...]
    m = jnp.maximum(jnp.max(s, axis=1, keepdims=True), sink)
    e = jnp.exp(s - m)
    es = jnp.exp(sink - m)
    inv = 1.0 / (jnp.sum(e, axis=1, keepdims=True) + es)
    return q, kk, e * inv, es * inv


def _swa_specs(nq):
    qs = _vspec((Q_PER_KV, WINDOW, HEAD_DIM), lambda g, n: (g, n, 0))
    kprev = _vspec((None, WINDOW, HEAD_DIM), lambda g, n: (g, jnp.maximum(n - 1, 0), 0))
    kcur = _vspec((None, WINDOW, HEAD_DIM), lambda g, n: (g, n, 0))
    bias = _vspec((Q_PER_KV, WINDOW, 2 * WINDOW), lambda g, n: (g, 0, 0))
    sink = _vspec((Q_PER_KV * WINDOW, 1), lambda g, n: (g, 0))
    return qs, kprev, kcur, bias, sink


def swa_fwd(qT, kT, vT, bias, sink_col, *, name="swa_fwd"):
    NQ, L, _ = qT.shape
    NKV = kT.shape[0]
    qs, kprev, kcur, bs, sk = _swa_specs(NQ)

    def body(q_ref, kp_ref, kc_ref, vp_ref, vc_ref, bias_ref, sink_ref, o_ref):
        _, _, p, _ = _swa_probs(q_ref, kp_ref, kc_ref, bias_ref, sink_ref, pl.program_id(1))
        vv = jnp.concatenate([vp_ref[...], vc_ref[...]], axis=0)
        o = jnp.dot(p.astype(BF16), vv, preferred_element_type=F32)
        o_ref[...] = o.reshape(Q_PER_KV, WINDOW, HEAD_DIM).astype(o_ref.dtype)

    return pl.pallas_call(
        body, name=name, grid=(NKV, L // WINDOW),
        in_specs=[qs, kprev, kcur, kprev, kcur, bs, sk], out_specs=qs,
        out_shape=jax.ShapeDtypeStruct((NQ, L, HEAD_DIM), BF16),
        compiler_params=_params(("parallel", "arbitrary")),
    )(qT, kT, kT, vT, vT, bias, sink_col)


def swa_bwd(qT, kT, vT, bias, sink_col, doT, *, name="swa_bwd"):
    NQ, L, _ = qT.shape
    NKV = kT.shape[0]
    qs, kprev, kcur, bs, sk = _swa_specs(NQ)
    W = WINDOW

    def body(q_ref, kp_ref, kc_ref, vp_ref, vc_ref, bias_ref, sink_ref, do_ref,
             dq_ref, dk_ref, dv_ref, dbias_ref, dsink_ref):
        n = pl.program_id(1)

        @pl.when(n == 0)
        def _():
            dk_ref[...] = jnp.zeros_like(dk_ref)
            dv_ref[...] = jnp.zeros_like(dv_ref)
            dbias_ref[...] = jnp.zeros_like(dbias_ref)
            dsink_ref[...] = jnp.zeros_like(dsink_ref)

        q, kk, p, ps = _swa_probs(q_ref, kp_ref, kc_ref, bias_ref, sink_ref, n)
        vv = jnp.concatenate([vp_ref[...], vc_ref[...]], axis=0)
        do = do_ref[...].reshape(Q_PER_KV * W, HEAD_DIM)
        dp = lax.dot_general(do, vv, (((1,), (1,)), ((), ())), preferred_element_type=F32)
        delta = jnp.sum(p * dp, axis=1, keepdims=True)
        ds = p * (dp - delta)
        dsink_ref[...] += -ps * delta
        dbias_ref[...] += ds.reshape(Q_PER_KV, W, 2 * W)
        dsb = ds.astype(BF16)
        scale = HEAD_DIM ** -0.5
        dq = jnp.dot(dsb, kk, preferred_element_type=F32) * scale
        dq_ref[...] = dq.reshape(Q_PER_KV, W, HEAD_DIM).astype(dq_ref.dtype)
        dkk = lax.dot_general(dsb, q, (((0,), (0,)), ((), ())), preferred_element_type=F32) * scale
        dvv = lax.dot_general(p.astype(BF16), do, (((0,), (0,)), ((), ())), preferred_element_type=F32)

        @pl.when(n == 0)
        def _():
            dk_ref[0:W, :] += dkk[W:, :]
            dv_ref[0:W, :] += dvv[W:, :]

        @pl.when(n > 0)
        def _():
            rows = pl.ds(pl.multiple_of((n - 1) * W, W), 2 * W)
            dk_ref[rows, :] += dkk
            dv_ref[rows, :] += dvv

    whole = _vspec((None, L, HEAD_DIM), lambda g, n: (g, 0, 0))
    return pl.pallas_call(
        body, name=name, grid=(NKV, L // W),
        in_specs=[qs, kprev, kcur, kprev, kcur, bs, sk, qs],
        out_specs=[qs, whole, whole, bs, sk],
        out_shape=[jax.ShapeDtypeStruct((NQ, L, HEAD_DIM), BF16), jax.ShapeDtypeStruct((NKV, L, HEAD_DIM), F32),
                   jax.ShapeDtypeStruct((NKV, L, HEAD_DIM), F32), jax.ShapeDtypeStruct((NQ, W, 2 * W), F32),
                   jax.ShapeDtypeStruct((NQ * W, 1), F32)],
        compiler_params=_params(("parallel", "arbitrary")),
    )(qT, kT, kT, vT, vT, bias, sink_col, doT)


def _bucket_table():
    qi = np.arange(WINDOW)[:, None]
    kj = np.arange(2 * WINDOW)[None, :]
    rel = qi + WINDOW - kj
    max_exact = N_BUCKETS // 2
    n = np.maximum(rel, 0)
    nf = np.maximum(n, max_exact).astype(np.float32)
    large = max_exact + (np.log(nf / max_exact) / math.log(MAX_DISTANCE / max_exact) * (N_BUCKETS - max_exact)).astype(np.int32)
    large = np.minimum(large, N_BUCKETS - 1)
    return np.where(n < max_exact, n, large).astype(np.int32).reshape(-1)


def _xa_probs(q, k):
    hd = q.shape[1]
    s = lax.dot_general(q, k, (((1,), (1,)), ((), ())), preferred_element_type=F32) * (hd ** -0.5)
    e = jnp.exp(s - jnp.max(s, axis=1, keepdims=True))
    return e / jnp.sum(e, axis=1, keepdims=True)


def xattn_fwd(q, kv, *, name="xattn_fwd"):
    L, D = q.shape
    Mm = kv.shape[0]
    hd = D // X_HEADS
    tq = _tile8(L, 512)

    def body(q_ref, kv_ref, o_ref):
        for h in range(X_HEADS):
            cols = slice(h * hd, (h + 1) * hd)
            p = _xa_probs(q_ref[:, cols], kv_ref[:, cols])
            o_ref[:, cols] = jnp.dot(p.astype(BF16), kv_ref[:, D + h * hd:D + (h + 1) * hd],
                                     preferred_element_type=F32).astype(o_ref.dtype)

    return pl.pallas_call(
        body, name=name, grid=(L // tq,),
        in_specs=[_vspec((tq, D), lambda i: (i, 0)), _vspec((Mm, 2 * D), lambda i: (0, 0))],
        out_specs=_vspec((tq, D), lambda i: (i, 0)),
        out_shape=jax.ShapeDtypeStruct((L, D), BF16),
        compiler_params=_params(("parallel",)),
    )(q, kv)


def xattn_bwd(q, kv, do, *, name="xattn_bwd"):
    L, D = q.shape
    Mm = kv.shape[0]
    hd = D // X_HEADS
    tq = _tile8(L, 512)

    def body(q_ref, kv_ref, do_ref, dq_ref, dkv_ref):
        @pl.when(pl.program_id(0) == 0)
        def _():
            dkv_ref[...] = jnp.zeros_like(dkv_ref)

        for h in range(X_HEADS):
            cols = slice(h * hd, (h + 1) * hd)
            vcols = slice(D + h * hd, D + (h + 1) * hd)
            qh, kh, vh, doh = q_ref[:, cols], kv_ref[:, cols], kv_ref[:, vcols], do_ref[:, cols]
            p = _xa_probs(qh, kh)
            dp = lax.dot_general(doh, vh, (((1,), (1,)), ((), ())), preferred_element_type=F32)
            ds = (p * (dp - jnp.sum(p * dp, axis=1, keepdims=True)) * (hd ** -0.5)).astype(BF16)
            dq_ref[:, cols] = jnp.dot(ds, kh, preferred_element_type=F32).astype(dq_ref.dtype)
            dkv_ref[:, cols] += lax.dot_general(ds, qh, (((0,), (0,)), ((), ())), preferred_element_type=F32)
            dkv_ref[:, vcols] += lax.dot_general(p.astype(BF16), doh, (((0,), (0,)), ((), ())), preferred_element_type=F32)

    return pl.pallas_call(
        body, name=name, grid=(L // tq,),
        in_specs=[_vspec((tq, D), lambda i: (i, 0)), _vspec((Mm, 2 * D), lambda i: (0, 0)), _vspec((tq, D), lambda i: (i, 0))],
        out_specs=[_vspec((tq, D), lambda i: (i, 0)), _vspec((Mm, 2 * D), lambda i: (0, 0))],
        out_shape=[jax.ShapeDtypeStruct((L, D), BF16), jax.ShapeDtypeStruct((Mm, 2 * D), F32)],
        compiler_params=_params(("arbitrary",)),
    )(q, kv, do)


def adamw(w, g, m, v, *, name="adamw"):
    R, C = w.shape
    tr = _tile8(R, max(SUBLANES, (256 * 1024) // C // SUBLANES * SUBLANES))

    def body(w_ref, g_ref, m_ref, v_ref, d_ref, nm_ref, nv_ref):
        g_ = g_ref[...]
        nm = ADAM_B1 * m_ref[...] + (1.0 - ADAM_B1) * g_
        nv = ADAM_B2 * v_ref[...] + (1.0 - ADAM_B2) * (g_ * g_)
        m_hat = nm / (1.0 - ADAM_B1 ** ADAM_STEP)
        v_hat = nv / (1.0 - ADAM_B2 ** ADAM_STEP)
        d_ref[...] = -ADAM_LR * (m_hat / (jnp.sqrt(v_hat) + ADAM_EPS) + ADAM_WD * w_ref[...])
        nm_ref[...] = nm
        nv_ref[...] = nv

    spec = _vspec((tr, C), lambda i: (i, 0))
    return pl.pallas_call(
        body, name=name, grid=(R // tr,), in_specs=[spec] * 4, out_specs=[spec] * 3,
        out_shape=[jax.ShapeDtypeStruct((R, C), F32)] * 3, compiler_params=_params(("parallel",)),
    )(w, g, m, v)


def _place():
    x, y, c = lax.axis_index("x"), lax.axis_index("y"), lax.axis_index("c")
    chips = [(1 - x, y), (x, 1 - y), (1 - x, 1 - y)]
    return x, y, c, chips


def _remote(src, dst, send, recv, k, to):
    return pltpu.make_async_remote_copy(src_ref=src, dst_ref=dst, send_sem=send.at[k], recv_sem=recv.at[k],
                                        device_id=to, device_id_type=MESH)


def gather_shards(arrs, *, name="gather_shards"):
    n = len(arrs)

    def body(*refs):
        ins, outs = refs[:n], refs[n:2 * n]
        send, recv = refs[2 * n:]
        x, y, c, chips = _place()
        me = 2 * x + y
        sib = (x, y, 1 - c)
        sends, passes = [], []
        for i in range(n):
            h = ins[i].shape[0] // 2
            mine = pl.ds(c * h, h)
            for j, (px, py) in enumerate(chips):
                cp = _remote(ins[i].at[mine], outs[i].at[mine, me], send, recv, 7 * i + j, (px, py, c))
                cp.start()
                sends.append(cp)
        for i in range(n):
            cp = _remote(ins[i], outs[i].at[:, me], send, recv, 7 * i + 6, sib)
            cp.start()
            sends.append(cp)
        for i in range(n):
            h = ins[i].shape[0] // 2
            mine = pl.ds(c * h, h)
            for j, (px, py) in enumerate(chips):
                landed = outs[i].at[mine, 2 * px + py]
                _remote(ins[i].at[mine], landed, send, recv, 7 * i + j, (px, py, c)).wait_recv()
                cp = _remote(landed, landed, send, recv, 7 * i + 3 + j, sib)
                cp.start()
                passes.append(cp)
        for i in range(n):
            h = ins[i].shape[0] // 2
            theirs = pl.ds((1 - c) * h, h)
            _remote(ins[i], outs[i].at[:, me], send, recv, 7 * i + 6, sib).wait_recv()
            for j, (px, py) in enumerate(chips):
                _remote(ins[i].at[theirs], outs[i].at[theirs, 2 * px + py], send, recv, 7 * i + 3 + j, sib).wait_recv()
        for cp in sends + passes:
            cp.wait_send()

    return pl.pallas_call(
        body, name=name, in_specs=[ANY] * n, out_specs=[ANY] * n,
        out_shape=[jax.ShapeDtypeStruct((a.shape[0], N_CHIPS) + a.shape[1:], a.dtype) for a in arrs],
        scratch_shapes=[pltpu.SemaphoreType.DMA((7 * n,)), pltpu.SemaphoreType.DMA((7 * n,))],
        compiler_params=pltpu.CompilerParams(has_side_effects=True),
    )(*arrs)


def exchange_halves(gs, *, name="rs_exchange_halves"):
    n = len(gs)

    def body(*refs):
        ins, outs = refs[:n], refs[n:2 * n]
        send, recv = refs[2 * n:]
        x, y, c, _ = _place()
        sib = (x, y, 1 - c)
        cps = []
        for i in range(n):
            r2 = ins[i].shape[1] // 2
            cp = _remote(ins[i].at[:, pl.ds((1 - c) * r2, r2)], outs[i], send, recv, i, sib)
            cp.start()
            cps.append(cp)
        for cp in cps:
            cp.wait()

    return pl.pallas_call(
        body, name=name, in_specs=[ANY] * n, out_specs=[ANY] * n,
        out_shape=[jax.ShapeDtypeStruct((g.shape[0], g.shape[1] // 2, g.shape[2]), g.dtype) for g in gs],
        scratch_shapes=[pltpu.SemaphoreType.DMA((n,)), pltpu.SemaphoreType.DMA((n,))],
        compiler_params=pltpu.CompilerParams(has_side_effects=True),
    )(*gs)


def add_half(g, other, c_idx, *, name="rs_add_half"):
    S, R, C = g.shape
    r2 = R // 2
    tr = _tile8(r2, max(SUBLANES, (512 * 1024) // C // SUBLANES * SUBLANES))
    nb = r2 // tr

    def body(c_ref, g_ref, o_ref, out_ref):
        out_ref[...] = (g_ref[...].astype(F32) + o_ref[...].astype(F32)).astype(out_ref.dtype)

    return pl.pallas_call(
        body, name=name,
        grid_spec=pltpu.PrefetchScalarGridSpec(
            num_scalar_prefetch=1, grid=(S, nb),
            in_specs=[pl.BlockSpec((None, tr, C), lambda s, i, c_ref: (s, c_ref[0] * nb + i, 0)),
                      pl.BlockSpec((None, tr, C), lambda s, i, c_ref: (s, i, 0))],
            out_specs=pl.BlockSpec((None, tr, C), lambda s, i, c_ref: (s, i, 0))),
        out_shape=jax.ShapeDtypeStruct((S, r2, C), BF16),
        compiler_params=_params(("parallel", "parallel")),
    )(c_idx, g, other)


def scatter_partials(ps, *, name="rs_scatter_partials"):
    n = len(ps)

    def body(*refs):
        ins, outs = refs[:n], refs[n:2 * n]
        send, recv = refs[2 * n:]
        x, y, c, chips = _place()
        cps = []
        for i in range(n):
            for j, (px, py) in enumerate(chips):
                cp = _remote(ins[i].at[2 * px + py], outs[i].at[j], send, recv, 3 * i + j, (px, py, c))
                cp.start()
                cps.append(cp)
        for cp in cps:
            cp.wait()

    return pl.pallas_call(
        body, name=name, in_specs=[ANY] * n, out_specs=[ANY] * n,
        out_shape=[jax.ShapeDtypeStruct((N_CHIPS - 1,) + p.shape[1:], p.dtype) for p in ps],
        scratch_shapes=[pltpu.SemaphoreType.DMA((3 * n,)), pltpu.SemaphoreType.DMA((3 * n,))],
        compiler_params=pltpu.CompilerParams(has_side_effects=True),
    )(*ps)


def add_partials(p, got, place_idx, gbuf, layer, *, name="rs_add_partials"):
    S, r2, C = p.shape
    tr = _tile8(r2, max(SUBLANES, (512 * 1024) // C // SUBLANES * SUBLANES))
    nb = r2 // tr

    def body(pi_ref, p_ref, g_ref, buf_ref, out_ref):
        out_ref[...] = ((p_ref[...].astype(F32) + g_ref[0].astype(F32)) + g_ref[1].astype(F32)) + g_ref[2].astype(F32)

    return pl.pallas_call(
        body, name=name,
        grid_spec=pltpu.PrefetchScalarGridSpec(
            num_scalar_prefetch=1, grid=(nb,),
            in_specs=[pl.BlockSpec((None, tr, C), lambda i, pi: (pi[0], i, 0)),
                      pl.BlockSpec((N_CHIPS - 1, tr, C), lambda i, pi: (0, i, 0)),
                      ANY],
            out_specs=pl.BlockSpec((None, tr, C), lambda i, pi: (layer, pi[1] * nb + i, 0))),
        out_shape=jax.ShapeDtypeStruct(gbuf.shape, F32),
        input_output_aliases={3: 0},
        compiler_params=_params(("parallel",)),
    )(place_idx, p, got, gbuf)


def join_halves(gbufs, *, name="rs_join_halves"):
    n = len(gbufs)

    def body(*refs):
        outs = refs[n:2 * n]
        send, recv = refs[2 * n:]
        x, y, c, _ = _place()
        sib = (x, y, 1 - c)
        cps = []
        for i in range(n):
            r2 = outs[i].shape[1] // 2
            mine = outs[i].at[:, pl.ds(c * r2, r2)]
            cp = _remote(mine, mine, send, recv, i, sib)
            cp.start()
            cps.append(cp)
        for i in range(n):
            r2 = outs[i].shape[1] // 2
            theirs = outs[i].at[:, pl.ds((1 - c) * r2, r2)]
            _remote(theirs, theirs, send, recv, i, sib).wait_recv()
        for cp in cps:
            cp.wait_send()

    return pl.pallas_call(
        body, name=name, in_specs=[ANY] * n, out_specs=[ANY] * n,
        out_shape=[jax.ShapeDtypeStruct(g.shape, F32) for g in gbufs],
        input_output_aliases={i: i for i in range(n)},
        scratch_shapes=[pltpu.SemaphoreType.DMA((n,)), pltpu.SemaphoreType.DMA((n,))],
        compiler_params=pltpu.CompilerParams(has_side_effects=True),
    )(*gbufs)


def all_gather_rows(v, *, name="all_gather_small"):
    m, ncol = v.shape

    def body(x_ref, out_ref, send, recv, lsem):
        x, y, c, chips = _place()
        me, sib = (x, y, c), (x, y, 1 - c)

        def rows(px, py, pc):
            return out_ref.at[pl.ds((4 * px + 2 * py + pc) * m, m), :]

        def copy(k, block, to, src=None):
            return _remote(rows(*block) if src is None else src, rows(*block), send, recv, k, to)

        mine = pltpu.make_async_copy(x_ref, rows(*me), lsem)
        mine.start()
        first = [copy(0, me, sib, src=x_ref)]
        first += [copy(1 + j, me, (*chip, c), src=x_ref) for j, chip in enumerate(chips)]
        for cp in first:
            cp.start()
        passed = [copy(4 + j, (*chip, c), sib) for j, chip in enumerate(chips)]
        for j, chip in enumerate(chips):
            copy(1 + j, (*chip, c), me).wait_recv()
            passed[j].start()
        copy(0, sib, me).wait_recv()
        for j, chip in enumerate(chips):
            copy(4 + j, (*chip, 1 - c), me).wait_recv()
        for cp in first + passed:
            cp.wait_send()
        mine.wait()

    return pl.pallas_call(
        body, name=name,
        in_specs=[pl.BlockSpec(memory_space=pltpu.VMEM)], out_specs=pl.BlockSpec(memory_space=pltpu.VMEM),
        out_shape=jax.ShapeDtypeStruct((8 * m, ncol), v.dtype),
        scratch_shapes=[pltpu.SemaphoreType.DMA((7,)), pltpu.SemaphoreType.DMA((7,)), pltpu.SemaphoreType.DMA],
        compiler_params=pltpu.CompilerParams(vmem_limit_bytes=VMEM_LIMIT_BYTES, has_side_effects=True),
    )(v)


def sum_blocks(g8, *, name="sum_blocks"):
    nb, m, ncol = g8.shape
    tr = _tile8(m, 512)

    def body(g_ref, o_ref):
        acc = g_ref[0]
        for k in range(1, nb):
            acc = acc + g_ref[k]
        o_ref[...] = acc

    return pl.pallas_call(
        body, name=name, grid=(m // tr,),
        in_specs=[_vspec((nb, tr, ncol), lambda i: (0, i, 0))], out_specs=_vspec((tr, ncol), lambda i: (i, 0)),
        out_shape=jax.ShapeDtypeStruct((m, ncol), F32), compiler_params=_params(("parallel",)),
    )(g8)


PACK_ROWS = 256


def _pack(arrs, mult):
    flat = jnp.concatenate([a.reshape(-1) for a in arrs])
    pad = (-flat.shape[0]) % (mult * LANES)
    return jnp.pad(flat, (0, pad)).reshape(-1, LANES)


def _unpack(packed, like):
    flat = packed.reshape(-1)
    out, off = [], 0
    for a in like:
        out.append(flat[off:off + a.size].reshape(a.shape))
        off += a.size
    return out


def _rows2d(a):
    return a.reshape(-1, a.shape[-1])


def kernel(x, mem, norm_mix, norm_xattn, norm_ffn, norm_final, norm_mem, rel_bias, ev_w_in, ev_conv_w, s5_a_re, s5_a_im, s5_log_dt, s5_b_re, s5_b_im, s5_c_re, s5_c_im, s5_d, s5_glu_w, ev_w_out, od_w_qkv, od_b_qkv, od_sinks, od_w_out, xa_w_q, xa_w_kv, xa_w_o, ff_w_gate, ff_w_up, ff_conv_w, ff_conv_b, ff_w_down, loss_target, m_norm_mix, m_norm_xattn, m_norm_ffn, m_norm_final, m_norm_mem, m_rel_bias, m_ev_w_in, m_ev_conv_w, m_s5_a_re, m_s5_a_im, m_s5_log_dt, m_s5_b_re, m_s5_b_im, m_s5_c_re, m_s5_c_im, m_s5_d, m_s5_glu_w, m_ev_w_out, m_od_w_qkv, m_od_b_qkv, m_od_sinks, m_od_w_out, m_xa_w_q, m_xa_w_kv, m_xa_w_o, m_ff_w_gate, m_ff_w_up, m_ff_conv_w, m_ff_conv_b, m_ff_w_down, v_norm_mix, v_norm_xattn, v_norm_ffn, v_norm_final, v_norm_mem, v_rel_bias, v_ev_w_in, v_ev_conv_w, v_s5_a_re, v_s5_a_im, v_s5_log_dt, v_s5_b_re, v_s5_b_im, v_s5_c_re, v_s5_c_im, v_s5_d, v_s5_glu_w, v_ev_w_out, v_od_w_qkv, v_od_b_qkv, v_od_sinks, v_od_w_out, v_xa_w_q, v_xa_w_kv, v_xa_w_o, v_ff_w_gate, v_ff_w_up, v_ff_conv_w, v_ff_conv_b, v_ff_w_down):
    names = ["norm_mix", "norm_xattn", "norm_ffn", "norm_final", "norm_mem", "rel_bias", "ev_w_in", "ev_conv_w",
             "s5_a_re", "s5_a_im", "s5_log_dt", "s5_b_re", "s5_b_im", "s5_c_re", "s5_c_im", "s5_d", "s5_glu_w",
             "ev_w_out", "od_w_qkv", "od_b_qkv", "od_sinks", "od_w_out", "xa_w_q", "xa_w_kv", "xa_w_o",
             "ff_w_gate", "ff_w_up", "ff_conv_w", "ff_conv_b", "ff_w_down"]
    env = dict(locals())
    W = {k: env[k] for k in names}
    Mo = {k: env["m_" + k] for k in names}
    Vo = {k: env["v_" + k] for k in names}

    h = x[0]
    target = loss_target[0]
    L, D = h.shape
    depth = norm_mix.shape[0]
    c_idx = lax.axis_index("c").astype(jnp.int32).reshape(1)
    me_idx = (2 * lax.axis_index("x") + lax.axis_index("y")).astype(jnp.int32).reshape(1)

    col_sharded = ["ev_w_in", "od_w_qkv", "xa_w_kv", "ff_w_gate", "ff_w_up"]
    row_sharded = ["ev_w_out", "od_w_out", "xa_w_q", "xa_w_o", "ff_w_down"]
    small_sharded = ["ev_conv_w", "od_b_qkv", "ff_conv_w"]
    big = col_sharded + row_sharded
    gathered = gather_shards([W[k].astype(BF16) for k in big] + [W[k] for k in small_sharded])
    G = dict(zip(big + small_sharded, gathered))

    def wcol(k, l):
        return G[k][l]

    def wrow(k, l):
        g = G[k][l]
        return g.reshape(1, g.shape[0] * g.shape[1], g.shape[2])

    ev_conv_w_f = G["ev_conv_w"].transpose(0, 2, 1, 3).reshape(ev_conv_w.shape[0], 3, -1)
    od_b_qkv_f = G["od_b_qkv"].reshape(od_b_qkv.shape[0], 1, -1)
    ff_conv_w_f = G["ff_conv_w"].transpose(0, 2, 1, 3).reshape(ff_conv_w.shape[0], 3, -1)

    buckets = _bucket_table()
    NQ = D // HEAD_DIM
    NKV = NQ // Q_PER_KV
    onehot = jnp.asarray((buckets[:, None] == np.arange(N_BUCKETS)[None, :]).astype(np.float32))
    bias_tab = jnp.dot(rel_bias.T, onehot.T, precision=lax.Precision.HIGHEST).reshape(NQ, WINDOW, 2 * WINDOW)

    mem_n = rms_fwd(mem[0], norm_mem.reshape(1, D), name="rms_fwd_mem")

    saved = []
    for l in range(depth):
        i = l // 2
        s = {"h0": h}
        hn = rms_fwd(h, norm_mix[l].reshape(1, D))
        s["hn"] = hn
        if l % 2 == 0:
            A = ev_conv_w_f.shape[-1]
            z = mm_nn(hn, wcol("ev_w_in", i), name="mm_ev_in")
            ya = conv_mixer_fwd(z, ev_conv_w_f[i])
            prep = functools.partial(_s5_prep, seg_len=L // N_SEG)
            s5p = (s5_a_re[i], s5_a_im[i], s5_log_dt[i], s5_b_re[i], s5_b_im[i], s5_c_re[i], s5_c_im[i], s5_d[i], s5_glu_w[i])
            (a_l, apow_l, bm, cm, d_l, glu), prep_vjp = jax.vjp(prep, *s5p)
            bm16, cm16, glu16 = bm.astype(BF16), cm.astype(BF16), glu.astype(BF16)
            u_p = _to_segments(z[:, 3 * A:])
            fin = s5_scan_fwd(u_p, a_l, apow_l, bm16, cm16, d_l, glu16, None, name="s5_fwd_ends")
            ys_p, y_p, states, starts = s5_scan_fwd(u_p, a_l, apow_l, bm16, cm16, d_l, glu16, fin, name="s5_fwd")
            ycat = jnp.concatenate([ya, _from_segments(ys_p)], axis=1)
            s.update(z=z, u_p=u_p, y_p=y_p, states=states, starts=starts, ycat=ycat, prep_vjp=prep_vjp,
                     s5ops=(a_l, apow_l, bm16, cm16, d_l, glu16))
            h = mm_nn(ycat, wrow("ev_w_out", i), res=h, name="mm_ev_out")
        else:
            z = mm_nn(hn, wcol("od_w_qkv", i), bias=od_b_qkv_f[i], out_dtype=BF16, name="mm_od_qkv")
            qT = z[:, :NQ * HEAD_DIM].reshape(L, NQ, HEAD_DIM).transpose(1, 0, 2)
            kT = z[:, NQ * HEAD_DIM:(NQ + NKV) * HEAD_DIM].reshape(L, NKV, HEAD_DIM).transpose(1, 0, 2)
            vT = z[:, (NQ + NKV) * HEAD_DIM:].reshape(L, NKV, HEAD_DIM).transpose(1, 0, 2)
            sink_col = jnp.repeat(od_sinks[i], WINDOW).reshape(NQ * WINDOW, 1)
            oT = swa_fwd(qT, kT, vT, bias_tab, sink_col)
            o = oT.transpose(1, 0, 2).reshape(L, D)
            s.update(qT=qT, kT=kT, vT=vT, sink_col=sink_col, o=o)
            h = mm_nn(o, wrow("od_w_out", i), res=h, name="mm_od_out")
        s["h1"] = h
        hn2 = rms_fwd(h, norm_xattn[l].reshape(1, D))
        q = mm_nn(hn2, wrow("xa_w_q", l), out_dtype=BF16, name="mm_xa_q")
        kv = mm_nn(mem_n, wcol("xa_w_kv", l), out_dtype=BF16, name="mm_xa_kv")
        ox = xattn_fwd(q, kv)
        s.update(hn2=hn2, q=q, kv=kv, ox=ox)
        h = mm_nn(ox, wrow("xa_w_o", l), res=h, name="mm_xa_o")
        s["h2"] = h
        hn3 = rms_fwd(h, norm_ffn[l].reshape(1, D))
        gpre = mm_nn(hn3, wcol("ff_w_gate", l), name="mm_ff_gate")
        up = mm_nn(hn3, wcol("ff_w_up", l), name="mm_ff_up")
        act = ffn_act_fwd(gpre, up, ff_conv_w_f[l], ff_conv_b[l].reshape(1, -1))
        s.update(hn3=hn3, gpre=gpre, up=up, act=act)
        h = mm_nn(act, wrow("ff_w_down", l), res=h, name="mm_ff_down")
        saved.append(s)

    loss11, dh, dg_final = final_loss(h, norm_final.reshape(1, D), target)
    loss = lax.psum(loss11[0, 0], AXES)

    gw = {k: [None] * W[k].shape[0] for k in big}
    gs = {k: [None] * W[k].shape[0] for k in names if k not in big and W[k].ndim > 1 and k != "rel_bias"}
    dmem_n = None
    dbias_tab = jnp.zeros_like(bias_tab)

    def as_rows(g3, k):
        return g3.reshape(N_CHIPS, g3.shape[1] // N_CHIPS, g3.shape[2])

    for l in reversed(range(depth)):
        i = l // 2
        s = saved[l]
        dact = mm_nt(dh, wrow("ff_w_down", l), name="mm_ff_down_dx")
        gw["ff_w_down"][l] = as_rows(mm_tn(s["act"], dh, 1, name="mm_ff_down_dw"), "ff_w_down")
        dgpre, dup, dcw, dcb = ffn_act_bwd(dact, s["gpre"], s["up"], ff_conv_w_f[l], ff_conv_b[l].reshape(1, -1))
        gs["ff_conv_w"][l], gs["ff_conv_b"][l] = dcw, dcb[0]
        dhn3 = mm_nt(dgpre, wcol("ff_w_gate", l), name="mm_ff_gate_dx")
        dhn3 = mm_nt(dup, wcol("ff_w_up", l), res=dhn3, name="mm_ff_up_dx")
        gw["ff_w_gate"][l] = mm_tn(s["hn3"], dgpre, N_CHIPS, name="mm_ff_gate_dw")
        gw["ff_w_up"][l] = mm_tn(s["hn3"], dup, N_CHIPS, name="mm_ff_up_dw")
        dh, dg = rms_bwd(s["h2"], norm_ffn[l].reshape(1, D), dhn3, dh)
        gs["norm_ffn"][l] = dg[0]
        dox = mm_nt(dh, wrow("xa_w_o", l), out_dtype=BF16, name="mm_xa_o_dx")
        gw["xa_w_o"][l] = as_rows(mm_tn(s["ox"], dh, 1, name="mm_xa_o_dw"), "xa_w_o")
        dq, dkv = xattn_bwd(s["q"], s["kv"], dox)
        dhn2 = mm_nt(dq, wrow("xa_w_q", l), name="mm_xa_q_dx")
        gw["xa_w_q"][l] = as_rows(mm_tn(s["hn2"], dq, 1, name="mm_xa_q_dw"), "xa_w_q")
        gw["xa_w_kv"][l] = mm_tn(mem_n, dkv, N_CHIPS, name="mm_xa_kv_dw")
        dmem_n = mm_nt(dkv, wcol("xa_w_kv", l), res=dmem_n, name="mm_xa_kv_dx")
        dh, dg = rms_bwd(s["h1"], norm_xattn[l].reshape(1, D), dhn2, dh)
        gs["norm_xattn"][l] = dg[0]
        if l % 2 == 0:
            A = ev_conv_w_f.shape[-1]
            dycat = mm_nt(dh, wrow("ev_w_out", i), name="mm_ev_out_dx")
            gw["ev_w_out"][i] = as_rows(mm_tn(s["ycat"], dh, 1, name="mm_ev_out_dw"), "ev_w_out")
            dgb, dgc, dxa, dcw = conv_mixer_bwd(s["z"], ev_conv_w_f[i], dycat[:, :A])
            gs["ev_conv_w"][i] = dcw
            a_l, apow_l, bm16, cm16, d_l, glu16 = s["s5ops"]
            dys_p = _to_segments(dycat[:, A:])
            dy_p, dglu, dd = s5_out_bwd(dys_p, s["y_p"], s["u_p"], glu16)
            cmt = cm16.transpose(0, 2, 1)
            bmt = bm16.transpose(0, 2, 1)
            gfin = s5_scan_bwd(dy_p, a_l, apow_l, cmt, None, name="s5_bwd_ends")
            du_p, da, dbm, dcm = s5_scan_bwd(dy_p, a_l, apow_l, cmt, gfin, s["states"], s["starts"], s["u_p"], bmt, d_l,
                                            name="s5_bwd")
            dprm = s["prep_vjp"]((da, jnp.zeros_like(apow_l), dbm, dcm, dd, dglu))
            for k, g in zip(["s5_a_re", "s5_a_im", "s5_log_dt", "s5_b_re", "s5_b_im", "s5_c_re", "s5_c_im", "s5_d", "s5_glu_w"], dprm):
                gs[k][i] = g
            dz = jnp.concatenate([dgb, dgc, dxa, _from_segments(du_p)], axis=1)
            dhn = mm_nt(dz, wcol("ev_w_in", i), name="mm_ev_in_dx")
            gw["ev_w_in"][i] = mm_tn(s["hn"], dz, N_CHIPS, name="mm_ev_in_dw")
        else:
            do = mm_nt(dh, wrow("od_w_out", i), out_dtype=BF16, name="mm_od_out_dx")
            gw["od_w_out"][i] = as_rows(mm_tn(s["o"], dh, 1, name="mm_od_out_dw"), "od_w_out")
            doT = do.reshape(L, NQ, HEAD_DIM).transpose(1, 0, 2)
            dqT, dkT, dvT, dbias, dsink = swa_bwd(s["qT"], s["kT"], s["vT"], bias_tab, s["sink_col"], doT)
            dbias_tab = dbias_tab + dbias
            gs["od_sinks"][i] = jnp.sum(dsink.reshape(NQ, WINDOW), axis=1)
            dz = jnp.concatenate([dqT.transpose(1, 0, 2).reshape(L, NQ * HEAD_DIM),
                                  dkT.astype(BF16).transpose(1, 0, 2).reshape(L, NKV * HEAD_DIM),
                                  dvT.astype(BF16).transpose(1, 0, 2).reshape(L, NKV * HEAD_DIM)], axis=1)
            gs["od_b_qkv"][i] = col_sum(dz)[0]
            dhn = mm_nt(dz, wcol("od_w_qkv", i), name="mm_od_qkv_dx")
            gw["od_w_qkv"][i] = mm_tn(s["hn"], dz, N_CHIPS, name="mm_od_qkv_dw")
        dh, dg = rms_bwd(s["h0"], norm_mix[l].reshape(1, D), dhn, dh)
        gs["norm_mix"][l] = dg[0]

    grad_x = dh[None]
    _, dg_mem = rms_bwd(mem[0], norm_mem.reshape(1, D), dmem_n, jnp.zeros_like(dmem_n), name="rms_bwd_mem")
    d_rel_bias = jnp.dot(dbias_tab.reshape(NQ, -1), onehot, precision=lax.Precision.HIGHEST).T

    small = [k for k in names if k not in big]
    local_small = {k: (jnp.stack(gs[k]) if k in gs else None) for k in small}
    local_small["norm_final"] = dg_final[0]
    local_small["norm_mem"] = dg_mem[0]
    local_small["rel_bias"] = d_rel_bias
    full_shape = {k: W[k].shape for k in small}
    for k in small_sharded:
        full_shape[k] = local_small[k].shape
    lst = [local_small[k].reshape(full_shape[k]).astype(F32) for k in small]
    packed = _pack(lst, PACK_ROWS)
    m_rows = packed.shape[0]
    summed = sum_blocks(all_gather_rows(packed).reshape(8, m_rows, LANES))
    gsum = dict(zip(small, _unpack(summed, lst)))
    for k in small_sharded:
        n4 = W[k].shape[-1]
        gsum[k] = lax.dynamic_slice_in_dim(gsum[k], me_idx[0] * n4, n4, axis=gsum[k].ndim - 1)

    flat, layer_of = [], []
    for k in big:
        for l, g in enumerate(gw[k]):
            layer_of.append((k, l))
            flat.append(g)
    other = exchange_halves(flat)
    partial = [add_half(g, o, c_idx) for g, o in zip(flat, other)]
    got = scatter_partials(partial)
    place_idx = jnp.concatenate([me_idx, c_idx])
    gbufs = {k: jnp.zeros((len(gw[k]),) + gw[k][0].shape[1:], F32) for k in big}
    for (k, l), p, r in zip(layer_of, partial, got):
        gbufs[k] = add_partials(p, r, place_idx, gbufs[k], l)
    joined = join_halves([gbufs[k] for k in big])
    gbig = {k: g.reshape(W[k].shape) for k, g in zip(big, joined)}

    grads = {**gsum, **gbig}
    delta, new_m, new_v = {}, {}, {}
    for k in big:
        d_, m_, v_ = adamw(_rows2d(W[k]), _rows2d(grads[k]), _rows2d(Mo[k]), _rows2d(Vo[k]), name="adamw_" + k)
        delta[k], new_m[k], new_v[k] = d_.reshape(W[k].shape), m_.reshape(W[k].shape), v_.reshape(W[k].shape)
    sw = [W[k] for k in small]
    d_, m_, v_ = adamw(_pack(sw, PACK_ROWS), _pack([grads[k] for k in small], PACK_ROWS),
                       _pack([Mo[k] for k in small], PACK_ROWS), _pack([Vo[k] for k in small], PACK_ROWS), name="adamw_small")
    for k, a, b, c_ in zip(small, _unpack(d_, sw), _unpack(m_, sw), _unpack(v_, sw)):
        delta[k], new_m[k], new_v[k] = a, b, c_

    return (loss, grad_x, *[grads[k] for k in names], *[delta[k] for k in names],
            *[new_m[k] for k in names], *[new_v[k] for k in names])
```

```python
import functools
import math

import numpy as np
import jax
import jax.numpy as jnp
from jax import lax
from jax.experimental import pallas as pl
from jax.experimental.pallas import tpu as pltpu

F32, BF16 = jnp.float32, jnp.bfloat16
MESH = pl.DeviceIdType.MESH
AXES = ("x", "y", "c")

VMEM_LIMIT_BYTES = 56 * 2**20
SUBLANES, LANES = 8, 128

RMS_EPS = 1e-5
S5_GROUP, S5_STATE = 16, 64
HEAD_DIM, Q_PER_KV, WINDOW = 64, 8, 128
N_BUCKETS, MAX_DISTANCE = 32, 128
X_HEADS = 4
NEG_INF = -1e30
ADAM_LR, ADAM_B1, ADAM_B2, ADAM_EPS, ADAM_WD, ADAM_STEP = 0.001, 0.9, 0.999, 1e-08, 0.01, 10
N_CHIPS = 4
N_SEG = 8
S5_STEPS = 32


def _params(sem=None):
    return pltpu.CompilerParams(dimension_semantics=sem, vmem_limit_bytes=VMEM_LIMIT_BYTES)


def _vspec(shape, index_map):
    return pl.BlockSpec(shape, index_map)


ANY = pl.BlockSpec(memory_space=pl.ANY)


def _tile(n, pref):
    t = (min(pref, n) // LANES) * LANES
    while t >= LANES:
        if n % t == 0:
            return t
        t -= LANES
    return n


def _acc_matmul(nk, k, acc, partial, finish):
    if nk == 1:
        finish(partial())
        return

    @pl.when(k == 0)
    def _():
        acc[...] = partial()

    @pl.when(jnp.logical_and(k > 0, k < nk - 1))
    def _():
        acc[...] += partial()

    @pl.when(k == nk - 1)
    def _():
        finish(acc[...] + partial())


MM_MAX_K = 2048


def _mm_call(core, grid, in_specs, ops, out_spec, out_shape, acc_shape, name, comm):
    n_in = len(ops)
    cops = [a for j in comm for a in j.ops]
    couts = [s for j in comm for s in j.outs]
    nsem = sum(j.nsem for j in comm)

    def body(*refs):
        ins, cin = refs[:n_in], refs[n_in:n_in + len(cops)]
        o_ref = refs[n_in + len(cops)]
        cout = refs[n_in + len(cops) + 1:n_in + len(cops) + 1 + len(couts)]
        rest = refs[n_in + len(cops) + 1 + len(couts):]

        def each(phase):
            ii = io = base = 0
            for j in comm:
                getattr(j, phase)(cin[ii:ii + len(j.ops)], cout[io:io + len(j.outs)], rest[1], rest[2], base)
                ii, io, base = ii + len(j.ops), io + len(j.outs), base + j.nsem

        if comm:
            pids = [pl.program_id(d) for d in range(len(grid))]
            first = functools.reduce(jnp.logical_and, [p == 0 for p in pids])
            last = functools.reduce(jnp.logical_and, [p == g - 1 for p, g in zip(pids, grid)])
            pl.when(first)(lambda: each("start"))
        core(ins, o_ref, rest[0])
        if comm:
            pl.when(last)(lambda: each("finish"))

    scratch = [pltpu.VMEM(acc_shape, F32)]
    if comm:
        scratch += [pltpu.SemaphoreType.DMA((nsem,)), pltpu.SemaphoreType.DMA((nsem,))]
        params = pltpu.CompilerParams(dimension_semantics=("arbitrary",) * len(grid),
                                      vmem_limit_bytes=VMEM_LIMIT_BYTES, has_side_effects=True)
    else:
        params = _params(("parallel", "parallel", "arbitrary"))
    res = pl.pallas_call(
        body, name=name, grid=grid, in_specs=in_specs + [ANY] * len(cops),
        out_specs=[out_spec] + [ANY] * len(couts), out_shape=[out_shape] + couts,
        scratch_shapes=scratch, compiler_params=params,
    )(*ops, *cops)
    io = 1
    for j in comm:
        j.result = list(res[io:io + len(j.outs)])
        io += len(j.outs)
    return res[0]


def mm_nn(a, w3, *, bias=None, res=None, out_dtype=F32, name, comm=()):
    M, K = a.shape
    S, K2, ns = w3.shape
    assert K == K2
    tm = _tile(M, 512)
    tk = K if K <= MM_MAX_K else _tile(K, 1536)
    nk = K // tk
    has_b, has_r = bias is not None, res is not None

    def core(ins, o_ref, acc):
        a_ref, w_ref = ins[0], ins[1]
        b_ref = ins[2] if has_b else None
        r_ref = ins[2 + has_b] if has_r else None

        def partial():
            return jnp.dot(a_ref[...].astype(BF16), w_ref[...], preferred_element_type=F32)

        def finish(r):
            if has_b:
                r = r + b_ref[...]
            if has_r:
                r = r + r_ref[...]
            o_ref[...] = r.astype(o_ref.dtype)

        _acc_matmul(nk, pl.program_id(2), acc, partial, finish)

    in_specs = [_vspec((tm, tk), lambda s, i, k: (i, k)), _vspec((None, tk, ns), lambda s, i, k: (s, k, 0))]
    ops = [a, w3]
    if has_b:
        in_specs.append(_vspec((1, ns), lambda s, i, k: (0, s)))
        ops.append(bias)
    if has_r:
        in_specs.append(_vspec((tm, ns), lambda s, i, k: (i, s)))
        ops.append(res)
    return _mm_call(core, (S, M // tm, nk), in_specs, ops, _vspec((tm, ns), lambda s, i, k: (i, s)),
                    jax.ShapeDtypeStruct((M, S * ns), out_dtype), (tm, ns) if nk > 1 else (SUBLANES, LANES), name, comm)


def mm_nt(a, w3, *, res=None, out_dtype=F32, name, comm=()):
    M, N = a.shape
    S, K, ns = w3.shape
    assert N == S * ns
    tko = K if K <= MM_MAX_K else _tile(K, 1024)
    tm = _tile(M, 512 if tko == K else 1024)
    tc = ns if ns <= MM_MAX_K else _tile(ns, 1024)
    ncs = ns // tc
    nc = S * ncs
    has_r = res is not None

    def core(ins, o_ref, acc):
        a_ref, w_ref = ins[0], ins[1]
        r_ref = ins[2] if has_r else None

        def partial():
            return lax.dot_general(a_ref[...].astype(BF16), w_ref[...], (((1,), (1,)), ((), ())),
                                   preferred_element_type=F32)

        def finish(r):
            if has_r:
                r = r + r_ref[...]
            o_ref[...] = r.astype(o_ref.dtype)

        _acc_matmul(nc, pl.program_id(2), acc, partial, finish)

    in_specs = [_vspec((tm, tc), lambda i, j, k: (i, k)),
                _vspec((None, tko, tc), lambda i, j, k: (k // ncs, j, k % ncs))]
    ops = [a, w3]
    if has_r:
        in_specs.append(_vspec((tm, tko), lambda i, j, k: (i, j)))
        ops.append(res)
    return _mm_call(core, (M // tm, K // tko, nc), in_specs, ops, _vspec((tm, tko), lambda i, j, k: (i, j)),
                    jax.ShapeDtypeStruct((M, K), out_dtype), (tm, tko) if nc > 1 else (SUBLANES, LANES), name, comm)


def mm_tn(x, dy, S, *, out_dtype=BF16, name, comm=()):
    M, K = x.shape
    M2, N = dy.shape
    assert M == M2 and N % S == 0
    ns = N // S
    tk, tmc = _tile(K, 1024), _tile(M, 1024)
    nm = M // tmc

    def core(ins, o_ref, acc):
        x_ref, dy_ref = ins

        def partial():
            return lax.dot_general(x_ref[...].astype(BF16), dy_ref[...].astype(BF16), (((0,), (0,)), ((), ())),
                                   preferred_element_type=F32)

        def finish(r):
            o_ref[...] = r.astype(o_ref.dtype)

        _acc_matmul(nm, pl.program_id(2), acc, partial, finish)

    in_specs = [_vspec((tmc, tk), lambda s, i, m: (m, i)), _vspec((tmc, ns), lambda s, i, m: (m, s))]
    return _mm_call(core, (S, K // tk, nm), in_specs, [x, dy], _vspec((None, tk, ns), lambda s, i, m: (s, i, 0)),
                    jax.ShapeDtypeStruct((S, K, ns), out_dtype), (tk, ns) if nm > 1 else (SUBLANES, LANES), name, comm)


def rms_fwd(h, g, *, name="rms_fwd"):
    R, D = h.shape
    tr = _tile8(R, 256)

    def body(h_ref, g_ref, o_ref):
        x = h_ref[...]
        r = lax.rsqrt(jnp.mean(x * x, axis=-1, keepdims=True) + RMS_EPS)
        o_ref[...] = (x * r * g_ref[...]).astype(o_ref.dtype)

    return pl.pallas_call(
        body, name=name, grid=(R // tr,),
        in_specs=[_vspec((tr, D), lambda i: (i, 0)), _vspec((1, D), lambda i: (0, 0))],
        out_specs=_vspec((tr, D), lambda i: (i, 0)),
        out_shape=jax.ShapeDtypeStruct((R, D), BF16),
        compiler_params=_params(("parallel",)),
    )(h, g)


def _tile8(n, pref):
    t = (min(pref, n) // SUBLANES) * SUBLANES
    while t >= SUBLANES:
        if n % t == 0:
            return t
        t -= SUBLANES
    return n


def rms_bwd(h, g, dhn, dres, *, name="rms_bwd"):
    R, D = h.shape
    tr = _tile8(R, 256)

    def body(h_ref, g_ref, dhn_ref, dres_ref, dh_ref, dg_ref):
        @pl.when(pl.program_id(0) == 0)
        def _():
            dg_ref[...] = jnp.zeros_like(dg_ref)

        x = h_ref[...]
        d = dhn_ref[...].astype(F32)
        r = lax.rsqrt(jnp.mean(x * x, axis=-1, keepdims=True) + RMS_EPS)
        xhat = x * r
        dg_ref[...] += jnp.sum(d * xhat, axis=0, keepdims=True)
        t = d * g_ref[...]
        dh_ref[...] = dres_ref[...] + r * (t - xhat * jnp.mean(t * xhat, axis=-1, keepdims=True))

    return pl.pallas_call(
        body, name=name, grid=(R // tr,),
        in_specs=[_vspec((tr, D), lambda i: (i, 0)), _vspec((1, D), lambda i: (0, 0)),
                  _vspec((tr, D), lambda i: (i, 0)), _vspec((tr, D), lambda i: (i, 0))],
        out_specs=[_vspec((tr, D), lambda i: (i, 0)), _vspec((1, D), lambda i: (0, 0))],
        out_shape=[jax.ShapeDtypeStruct((R, D), F32), jax.ShapeDtypeStruct((1, D), F32)],
        compiler_params=_params(("arbitrary",)),
    )(h, g, dhn, dres)


def final_loss(h, g, target, *, name="final_loss"):
    R, D = h.shape
    tr = _tile8(R, 256)

    def body(h_ref, g_ref, t_ref, loss_ref, dh_ref, dg_ref):
        @pl.when(pl.program_id(0) == 0)
        def _():
            dg_ref[...] = jnp.zeros_like(dg_ref)
            loss_ref[...] = jnp.zeros_like(loss_ref)

        x = h_ref[...]
        r = lax.rsqrt(jnp.mean(x * x, axis=-1, keepdims=True) + RMS_EPS)
        xhat = x * r
        err = xhat * g_ref[...] - t_ref[...]
        row = jnp.mean(err * err, axis=-1, keepdims=True)
        loss_ref[...] += 0.5 * jnp.sum(row, axis=0, keepdims=True)
        d = err * (1.0 / D)
        dg_ref[...] += jnp.sum(d * xhat, axis=0, keepdims=True)
        t = d * g_ref[...]
        dh_ref[...] = r * (t - xhat * jnp.mean(t * xhat, axis=-1, keepdims=True))

    return pl.pallas_call(
        body, name=name, grid=(R // tr,),
        in_specs=[_vspec((tr, D), lambda i: (i, 0)), _vspec((1, D), lambda i: (0, 0)), _vspec((tr, D), lambda i: (i, 0))],
        out_specs=[_vspec((1, 1), lambda i: (0, 0)), _vspec((tr, D), lambda i: (i, 0)), _vspec((1, D), lambda i: (0, 0))],
        out_shape=[jax.ShapeDtypeStruct((1, 1), F32), jax.ShapeDtypeStruct((R, D), F32), jax.ShapeDtypeStruct((1, D), F32)],
        compiler_params=_params(("arbitrary",)),
    )(h, g, target)


def _shift_down(v, k):
    rows = lax.broadcasted_iota(jnp.int32, v.shape, 0)
    return jnp.where(rows >= k, pltpu.roll(v, k, axis=0), 0.0)


def _shift_up(v, k):
    L = v.shape[0]
    rows = lax.broadcasted_iota(jnp.int32, v.shape, 0)
    return jnp.where(rows < L - k, pltpu.roll(v, L - k, axis=0), 0.0)


def _conv(v, w):
    return w[2:3, :] * v + w[1:2, :] * _shift_down(v, 1) + w[0:1, :] * _shift_down(v, 2)


def _conv_t(d, w):
    return w[2:3, :] * d + w[1:2, :] * _shift_up(d, 1) + w[0:1, :] * _shift_up(d, 2)


def _conv_dw(d, v):
    return jnp.concatenate([
        jnp.sum(d * _shift_down(v, 2), axis=0, keepdims=True),
        jnp.sum(d * _shift_down(v, 1), axis=0, keepdims=True),
        jnp.sum(d * v, axis=0, keepdims=True)], axis=0)


COL_BLOCK = 128


def conv_mixer_fwd(z, cw, *, name="conv_mixer_fwd"):
    L = z.shape[0]
    A = cw.shape[1]
    cb = _tile(A, COL_BLOCK)
    nb = A // cb

    def body(gb_ref, gc_ref, xa_ref, w_ref, o_ref):
        v = gc_ref[...] * xa_ref[...]
        o_ref[...] = (gb_ref[...] * _conv(v, w_ref[...])).astype(o_ref.dtype)

    return pl.pallas_call(
        body, name=name, grid=(nb,),
        in_specs=[_vspec((L, cb), lambda j: (0, j)), _vspec((L, cb), lambda j: (0, nb + j)),
                  _vspec((L, cb), lambda j: (0, 2 * nb + j)), _vspec((3, cb), lambda j: (0, j))],
        out_specs=_vspec((L, cb), lambda j: (0, j)),
        out_shape=jax.ShapeDtypeStruct((L, A), BF16),
        compiler_params=_params(("parallel",)),
    )(z, z, z, cw)


def conv_mixer_bwd(z, cw, dya, *, name="conv_mixer_bwd"):
    L = z.shape[0]
    A = cw.shape[1]
    cb = _tile(A, COL_BLOCK)
    nb = A // cb

    def body(gb_ref, gc_ref, xa_ref, w_ref, d_ref, dgb_ref, dgc_ref, dxa_ref, dw_ref):
        gc, xa, w, d = gc_ref[...], xa_ref[...], w_ref[...], d_ref[...]
        v = gc * xa
        dgb_ref[...] = (d * _conv(v, w)).astype(dgb_ref.dtype)
        dc = d * gb_ref[...]
        dw_ref[...] = _conv_dw(dc, v)
        dv = _conv_t(dc, w)
        dgc_ref[...] = (dv * xa).astype(dgc_ref.dtype)
        dxa_ref[...] = (dv * gc).astype(dxa_ref.dtype)

    col = lambda j: (0, j)
    outs = pl.pallas_call(
        body, name=name, grid=(nb,),
        in_specs=[_vspec((L, cb), col), _vspec((L, cb), lambda j: (0, nb + j)),
                  _vspec((L, cb), lambda j: (0, 2 * nb + j)), _vspec((3, cb), col), _vspec((L, cb), col)],
        out_specs=[_vspec((L, cb), col), _vspec((L, cb), col), _vspec((L, cb), col), _vspec((3, cb), col)],
        out_shape=[jax.ShapeDtypeStruct((L, A), BF16)] * 3 + [jax.ShapeDtypeStruct((3, A), F32)],
        compiler_params=_params(("parallel",)),
    )(z, z, z, cw, dya)
    return outs[0], outs[1], outs[2], outs[3]


def ffn_act_fwd(gpre, up, cw, cbias, *, name="ffn_act_fwd"):
    L, Fd = gpre.shape
    cb = _tile(Fd, COL_BLOCK)

    def body(g_ref, u_ref, w_ref, b_ref, o_ref):
        g = _conv(g_ref[...], w_ref[...]) + b_ref[...]
        o_ref[...] = (g * jax.nn.sigmoid(g) * u_ref[...]).astype(o_ref.dtype)

    col = lambda j: (0, j)
    return pl.pallas_call(
        body, name=name, grid=(Fd // cb,),
        in_specs=[_vspec((L, cb), col), _vspec((L, cb), col), _vspec((3, cb), col), _vspec((1, cb), col)],
        out_specs=_vspec((L, cb), col),
        out_shape=jax.ShapeDtypeStruct((L, Fd), BF16),
        compiler_params=_params(("parallel",)),
    )(gpre, up, cw, cbias)


def ffn_act_bwd(dact, gpre, up, cw, cbias, *, name="ffn_act_bwd"):
    L, Fd = gpre.shape
    cb = _tile(Fd, COL_BLOCK)

    def body(d_ref, g_ref, u_ref, w_ref, b_ref, dg_ref, du_ref, dw_ref, db_ref):
        gp, w, d = g_ref[...], w_ref[...], d_ref[...]
        g = _conv(gp, w) + b_ref[...]
        sg = jax.nn.sigmoid(g)
        du_ref[...] = (d * (g * sg)).astype(du_ref.dtype)
        dg = d * u_ref[...] * (sg * (1.0 + g * (1.0 - sg)))
        db_ref[...] = jnp.sum(dg, axis=0, keepdims=True)
        dw_ref[...] = _conv_dw(dg, gp)
        dg_ref[...] = _conv_t(dg, w).astype(dg_ref.dtype)

    col = lambda j: (0, j)
    return pl.pallas_call(
        body, name=name, grid=(Fd // cb,),
        in_specs=[_vspec((L, cb), col)] * 3 + [_vspec((3, cb), col), _vspec((1, cb), col)],
        out_specs=[_vspec((L, cb), col), _vspec((L, cb), col), _vspec((3, cb), col), _vspec((1, cb), col)],
        out_shape=[jax.ShapeDtypeStruct((L, Fd), BF16), jax.ShapeDtypeStruct((L, Fd), BF16),
                   jax.ShapeDtypeStruct((3, Fd), F32), jax.ShapeDtypeStruct((1, Fd), F32)],
        compiler_params=_params(("parallel",)),
    )(dact, gpre, up, cw, cbias)


def col_sum(x, *, name="col_sum"):
    R, C = x.shape
    tr = _tile8(R, 512)

    def body(x_ref, o_ref):
        @pl.when(pl.program_id(0) == 0)
        def _():
            o_ref[...] = jnp.zeros_like(o_ref)

        o_ref[...] += jnp.sum(x_ref[...].astype(F32), axis=0, keepdims=True)

    return pl.pallas_call(
        body, name=name, grid=(R // tr,),
        in_specs=[_vspec((tr, C), lambda i: (i, 0))], out_specs=_vspec((1, C), lambda i: (0, 0)),
        out_shape=jax.ShapeDtypeStruct((1, C), F32), compiler_params=_params(("arbitrary",)),
    )(x)


def _cmul_add(ar, ai, sr, si, br, bi):
    return ar * sr - ai * si + br, ar * si + ai * sr + bi


def _segment_starts(fin, pr, pi, reverse):
    H = fin.shape[1] // 2
    rows = lax.broadcasted_iota(jnp.int32, fin.shape, 0)
    cr = jnp.zeros((1, H), F32)
    ci = jnp.zeros((1, H), F32)
    out = jnp.zeros(fin.shape, F32)
    order = range(N_SEG - 1, -1, -1) if reverse else range(N_SEG)
    for k in order:
        out = jnp.where(rows == k, jnp.concatenate([cr, ci], axis=1), out)
        cr, ci = _cmul_add(pr, pi, cr, ci, fin[k:k + 1, :H], fin[k:k + 1, H:])
    return out


def _gelu(y):
    c0 = math.sqrt(2.0 / math.pi)
    t = jnp.tanh(c0 * (y + 0.044715 * y * y * y))
    return 0.5 * y * (1.0 + t), t


def s5_scan_fwd(u_p, a_l, apow_l, bm, cm, d_l, glu, fin, *, name):
    L, C = u_p.shape
    NK, _, SW = bm.shape
    H = SW // 2
    RB = S5_STEPS * N_SEG
    NC = L // RB
    final_only = fin is None

    def scan_chunk(a_ref, buf, st):
        ar = jnp.broadcast_to(a_ref[:, :H], (N_SEG, H))
        ai = jnp.broadcast_to(a_ref[:, H:], (N_SEG, H))

        def step(j, carry):
            sr, si = carry
            rows = pl.ds(pl.multiple_of(j * N_SEG, N_SEG), N_SEG)
            sr, si = _cmul_add(ar, ai, sr, si, buf[rows, :H], buf[rows, H:])
            buf[rows, :H] = sr
            buf[rows, H:] = si
            return sr, si

        sr, si = lax.fori_loop(0, S5_STEPS, step, (st[:, :H], st[:, H:]))
        st[:, :H] = sr
        st[:, H:] = si

    if final_only:
        def body(u_ref, a_ref, bm_ref, fin_ref, buf, st):
            @pl.when(pl.program_id(1) == 0)
            def _():
                st[...] = jnp.zeros_like(st)

            buf[...] = jnp.dot(u_ref[...].astype(BF16), bm_ref[...], preferred_element_type=F32)
            scan_chunk(a_ref, buf, st)
            fin_ref[...] = st[...]

        return pl.pallas_call(
            body, name=name, grid=(NK, NC),
            in_specs=[_vspec((RB, LANES), lambda k, j: (j, k)), _vspec((None, 1, SW), lambda k, j: (k, 0, 0)),
                      _vspec((None, LANES, SW), lambda k, j: (k, 0, 0))],
            out_specs=_vspec((None, N_SEG, SW), lambda k, j: (k, 0, 0)),
            out_shape=jax.ShapeDtypeStruct((NK, N_SEG, SW), F32),
            scratch_shapes=[pltpu.VMEM((RB, SW), F32), pltpu.VMEM((N_SEG, SW), F32)],
            compiler_params=_params(("parallel", "arbitrary")),
        )(u_p, a_l, bm)

    def body(u_ref, a_ref, ap_ref, bm_ref, cm_ref, d_ref, glu_ref, fin_ref, o_ref, y_ref, s_ref, start_ref, buf, st):
        @pl.when(pl.program_id(1) == 0)
        def _():
            st[...] = _segment_starts(fin_ref[...], ap_ref[:, :H], ap_ref[:, H:], False)
            start_ref[...] = st[...]

        u = u_ref[...]
        buf[...] = jnp.dot(u.astype(BF16), bm_ref[...], preferred_element_type=F32)
        scan_chunk(a_ref, buf, st)
        states = buf[...]
        s_ref[...] = states
        y = jnp.dot(states.astype(BF16), cm_ref[...], preferred_element_type=F32) + d_ref[...] * u
        y_ref[...] = y
        yg, _ = _gelu(y)
        gate = jnp.dot(yg.astype(BF16), glu_ref[...], preferred_element_type=F32)
        o_ref[...] = (yg * jax.nn.sigmoid(gate)).astype(o_ref.dtype)

    blk = lambda k, j: (j, k)
    per_k = lambda k, j: (k, 0, 0)
    return pl.pallas_call(
        body, name=name, grid=(NK, NC),
        in_specs=[_vspec((RB, LANES), blk), _vspec((None, 1, SW), per_k), _vspec((None, 1, SW), per_k),
                  _vspec((None, LANES, SW), per_k), _vspec((None, SW, LANES), per_k), _vspec((1, LANES), lambda k, j: (0, k)),
                  _vspec((None, LANES, LANES), per_k), _vspec((None, N_SEG, SW), per_k)],
        out_specs=[_vspec((RB, LANES), blk), _vspec((RB, LANES), blk), _vspec((RB, SW), blk),
                   _vspec((None, N_SEG, SW), per_k)],
        out_shape=[jax.ShapeDtypeStruct((L, C), BF16), jax.ShapeDtypeStruct((L, C), F32),
                   jax.ShapeDtypeStruct((L, NK * SW), F32), jax.ShapeDtypeStruct((NK, N_SEG, SW), F32)],
        scratch_shapes=[pltpu.VMEM((RB, SW), F32), pltpu.VMEM((N_SEG, SW), F32)],
        compiler_params=_params(("parallel", "arbitrary")),
    )(u_p, a_l, apow_l, bm, cm, d_l, glu, fin)


def s5_out_bwd(dout_p, y_p, u_p, glu, *, name="s5_out_bwd"):
    L, C = y_p.shape
    NK = C // LANES
    tr = _tile8(L, 512)

    def body(do_ref, y_ref, u_ref, glu_ref, dy_ref, dglu_ref, dd_ref):
        @pl.when(pl.program_id(1) == 0)
        def _():
            dglu_ref[...] = jnp.zeros_like(dglu_ref)
            dd_ref[...] = jnp.zeros_like(dd_ref)

        y, do, w = y_ref[...], do_ref[...].astype(F32), glu_ref[...]
        yg, t = _gelu(y)
        sg = jax.nn.sigmoid(jnp.dot(yg.astype(BF16), w, preferred_element_type=F32))
        dgate = (do * yg * sg * (1.0 - sg)).astype(BF16)
        dyg = do * sg + lax.dot_general(dgate, w, (((1,), (1,)), ((), ())), preferred_element_type=F32)
        dglu_ref[...] += lax.dot_general(yg.astype(BF16), dgate, (((0,), (0,)), ((), ())), preferred_element_type=F32)
        c0 = math.sqrt(2.0 / math.pi)
        dgelu = 0.5 * (1.0 + t) + 0.5 * y * (1.0 - t * t) * c0 * (1.0 + 3.0 * 0.044715 * y * y)
        dy = dyg * dgelu
        dy_ref[...] = dy
        dd_ref[...] += jnp.sum(dy * u_ref[...], axis=0, keepdims=True)

    blk = lambda k, i: (i, k)
    return pl.pallas_call(
        body, name=name, grid=(NK, L // tr),
        in_specs=[_vspec((tr, LANES), blk)] * 3 + [_vspec((None, LANES, LANES), lambda k, i: (k, 0, 0))],
        out_specs=[_vspec((tr, LANES), blk), _vspec((None, LANES, LANES), lambda k, i: (k, 0, 0)),
                   _vspec((1, LANES), lambda k, i: (0, k))],
        out_shape=[jax.ShapeDtypeStruct((L, C), F32), jax.ShapeDtypeStruct((NK, LANES, LANES), F32),
                   jax.ShapeDtypeStruct((1, C), F32)],
        compiler_params=_params(("parallel", "arbitrary")),
    )(dout_p, y_p, u_p, glu)


def s5_scan_bwd(dy_p, a_l, apow_l, cmt, gfin, states=None, starts=None, u_p=None, bmt=None, d_l=None, *, name):
    L, C = dy_p.shape
    NK, _, SW = cmt.shape
    H = SW // 2
    RB = S5_STEPS * N_SEG
    NC = L // RB
    final_only = gfin is None

    def scan_chunk(a_ref, buf, st):
        ar = jnp.broadcast_to(a_ref[:, :H], (N_SEG, H))
        ai = -jnp.broadcast_to(a_ref[:, H:], (N_SEG, H))

        def step(jj, carry):
            gr, gi = carry
            j = S5_STEPS - 1 - jj
            rows = pl.ds(pl.multiple_of(j * N_SEG, N_SEG), N_SEG)
            gr, gi = _cmul_add(ar, ai, gr, gi, buf[rows, :H], buf[rows, H:])
            buf[rows, :H] = gr
            buf[rows, H:] = gi
            return gr, gi

        gr, gi = lax.fori_loop(0, S5_STEPS, step, (st[:, :H], st[:, H:]))
        st[:, :H] = gr
        st[:, H:] = gi

    rblk = lambda k, j: (NC - 1 - j, k)
    per_k = lambda k, j: (k, 0, 0)

    if final_only:
        def body(dy_ref, a_ref, cmt_ref, fin_ref, buf, st):
            @pl.when(pl.program_id(1) == 0)
            def _():
                st[...] = jnp.zeros_like(st)

            buf[...] = jnp.dot(dy_ref[...].astype(BF16), cmt_ref[...], preferred_element_type=F32)
            scan_chunk(a_ref, buf, st)
            fin_ref[...] = st[...]

        return pl.pallas_call(
            body, name=name, grid=(NK, NC),
            in_specs=[_vspec((RB, LANES), rblk), _vspec((None, 1, SW), per_k), _vspec((None, LANES, SW), per_k)],
            out_specs=_vspec((None, N_SEG, SW), per_k),
            out_shape=jax.ShapeDtypeStruct((NK, N_SEG, SW), F32),
            scratch_shapes=[pltpu.VMEM((RB, SW), F32), pltpu.VMEM((N_SEG, SW), F32)],
            compiler_params=_params(("parallel", "arbitrary")),
        )(dy_p, a_l, cmt)

    def body(dy_ref, a_ref, ap_ref, cmt_ref, gfin_ref, s_ref, sprev_ref, start_ref, u_ref, bmt_ref, d_ref,
             du_ref, da_ref, dbm_ref, dcm_ref, buf, st):
        jc = pl.program_id(1)

        @pl.when(jc == 0)
        def _():
            st[...] = _segment_starts(gfin_ref[...], ap_ref[:, :H], -ap_ref[:, H:], True)
            da_ref[...] = jnp.zeros_like(da_ref)
            dbm_ref[...] = jnp.zeros_like(dbm_ref)
            dcm_ref[...] = jnp.zeros_like(dcm_ref)

        dy = dy_ref[...]
        dyb = dy.astype(BF16)
        buf[...] = jnp.dot(dyb, cmt_ref[...], preferred_element_type=F32)
        scan_chunk(a_ref, buf, st)
        g = buf[...]
        s = s_ref[...]
        first = jnp.where(jc == NC - 1, start_ref[...], sprev_ref[...])
        sp = jnp.concatenate([first, s[:RB - N_SEG, :]], axis=0)
        gr, gi, pr, pi = g[:, :H], g[:, H:], sp[:, :H], sp[:, H:]
        da_ref[...] += jnp.concatenate([jnp.sum(gr * pr + gi * pi, axis=0, keepdims=True),
                                        jnp.sum(gi * pr - gr * pi, axis=0, keepdims=True)], axis=1)
        gb = g.astype(BF16)
        u = u_ref[...]
        du_ref[...] = (jnp.dot(gb, bmt_ref[...], preferred_element_type=F32) + dy * d_ref[...]).astype(du_ref.dtype)
        dbm_ref[...] += lax.dot_general(u.astype(BF16), gb, (((0,), (0,)), ((), ())), preferred_element_type=F32)
        dcm_ref[...] += lax.dot_general(s.astype(BF16), dyb, (((0,), (0,)), ((), ())), preferred_element_type=F32)

    prev8 = lambda k, j: (jnp.maximum((NC - 1 - j) * S5_STEPS - 1, 0), k)
    return pl.pallas_call(
        body, name=name, grid=(NK, NC),
        in_specs=[_vspec((RB, LANES), rblk), _vspec((None, 1, SW), per_k), _vspec((None, 1, SW), per_k),
                  _vspec((None, LANES, SW), per_k), _vspec((None, N_SEG, SW), per_k), _vspec((RB, SW), rblk),
                  _vspec((N_SEG, SW), prev8), _vspec((None, N_SEG, SW), per_k), _vspec((RB, LANES), rblk),
                  _vspec((None, SW, LANES), per_k), _vspec((1, LANES), lambda k, j: (0, k))],
        out_specs=[_vspec((RB, LANES), rblk), _vspec((None, 1, SW), per_k), _vspec((None, LANES, SW), per_k),
                   _vspec((None, SW, LANES), per_k)],
        out_shape=[jax.ShapeDtypeStruct((L, C), BF16), jax.ShapeDtypeStruct((NK, 1, SW), F32),
                   jax.ShapeDtypeStruct((NK, LANES, SW), F32), jax.ShapeDtypeStruct((NK, SW, LANES), F32)],
        scratch_shapes=[pltpu.VMEM((RB, SW), F32), pltpu.VMEM((N_SEG, SW), F32)],
        compiler_params=_params(("parallel", "arbitrary")),
    )(dy_p, a_l, apow_l, cmt, gfin, states, states, starts, u_p, bmt, d_l)


def _s5_prep(a_re, a_im, log_dt, b_re, b_im, c_re, c_im, d, glu_w, seg_len):
    G, P = a_re.shape
    Hc = b_re.shape[-1]
    gl = LANES // Hc
    nk = G // gl
    dt = jnp.exp(log_dt)[:, None]
    er = jnp.exp(a_re * dt)
    ab_r, ab_i = er * jnp.cos(a_im * dt), er * jnp.sin(a_im * dt)
    den = a_re * a_re + a_im * a_im
    nr, ni = ab_r - 1.0, ab_i
    q_r, q_i = (nr * a_re + ni * a_im) / den, (ni * a_re - nr * a_im) / den
    bb_r = q_r[..., None] * b_re - q_i[..., None] * b_im
    bb_i = q_r[..., None] * b_im + q_i[..., None] * b_re
    ep = jnp.exp(a_re * dt * seg_len)
    ap_r, ap_i = ep * jnp.cos(a_im * dt * seg_len), ep * jnp.sin(a_im * dt * seg_len)
    eye = jnp.eye(gl, dtype=F32)

    def lanes(t):
        return t.reshape(nk, 1, gl * P)

    def b_mat(t):
        return jnp.einsum("kgph,gq->kghqp", t.reshape(nk, gl, P, Hc), eye).reshape(nk, gl * Hc, gl * P)

    def c_mat(t):
        return jnp.einsum("kghp,gq->kgpqh", t.reshape(nk, gl, Hc, P), eye).reshape(nk, gl * P, gl * Hc)

    a_l = jnp.concatenate([lanes(ab_r), lanes(ab_i)], axis=-1)
    apow_l = jnp.concatenate([lanes(ap_r), lanes(ap_i)], axis=-1)
    bm = jnp.concatenate([b_mat(bb_r), b_mat(bb_i)], axis=-1)
    cm = jnp.concatenate([c_mat(c_re), -c_mat(c_im)], axis=1)
    glu = jnp.einsum("kgho,gq->kghqo", glu_w.reshape(nk, gl, Hc, Hc), eye).reshape(nk, gl * Hc, gl * Hc)
    return a_l, apow_l, bm, cm, d.reshape(1, G * Hc), glu


def _to_segments(t):
    L, C = t.shape
    return t.reshape(N_SEG, L // N_SEG, C).transpose(1, 0, 2).reshape(L, C)


def _from_segments(t):
    L, C = t.shape
    return t.reshape(L // N_SEG, N_SEG, C).transpose(1, 0, 2).reshape(L, C)


def _swa_probs(q_ref, kp_ref, kc_ref, bias_ref, sink_ref, n):
    QB = WINDOW
    rows = Q_PER_KV * QB
    q = q_ref[...].reshape(rows, HEAD_DIM)
    kk = jnp.concatenate([kp_ref[...], kc_ref[...]], axis=0)
    s = lax.dot_general(q, kk, (((1,), (1,)), ((), ())), preferred_element_type=F32) * (HEAD_DIM ** -0.5)
    s = s + bias_ref[...].reshape(rows, 2 * QB)
    qi = lax.broadcasted_iota(jnp.int32, (rows, 2 * QB), 0) % QB
    kj = lax.broadcasted_iota(jnp.int32, (rows, 2 * QB), 1)
    valid = ((kj < QB) & (kj > qi) & (n > 0)) | ((kj >= QB) & (kj - QB <= qi))
    s = jnp.where(valid, s, NEG_INF)
    sink = sink_ref[...]
    m = jnp.maximum(jnp.max(s, axis=1, keepdims=True), sink)
    e = jnp.exp(s - m)
    es = jnp.exp(sink - m)
    inv = 1.0 / (jnp.sum(e, axis=1, keepdims=True) + es)
    return q, kk, e * inv, es * inv


def _swa_specs(nq):
    qs = _vspec((Q_PER_KV, WINDOW, HEAD_DIM), lambda g, n: (g, n, 0))
    kprev = _vspec((None, WINDOW, HEAD_DIM), lambda g, n: (g, jnp.maximum(n - 1, 0), 0))
    kcur = _vspec((None, WINDOW, HEAD_DIM), lambda g, n: (g, n, 0))
    bias = _vspec((Q_PER_KV, WINDOW, 2 * WINDOW), lambda g, n: (g, 0, 0))
    sink = _vspec((Q_PER_KV * WINDOW, 1), lambda g, n: (g, 0))
    return qs, kprev, kcur, bias, sink


def swa_fwd(qT, kT, vT, bias, sink_col, *, name="swa_fwd"):
    NQ, L, _ = qT.shape
    NKV = kT.shape[0]
    qs, kprev, kcur, bs, sk = _swa_specs(NQ)

    def body(q_ref, kp_ref, kc_ref, vp_ref, vc_ref, bias_ref, sink_ref, o_ref):
        _, _, p, _ = _swa_probs(q_ref, kp_ref, kc_ref, bias_ref, sink_ref, pl.program_id(1))
        vv = jnp.concatenate([vp_ref[...], vc_ref[...]], axis=0)
        o = jnp.dot(p.astype(BF16), vv, preferred_element_type=F32)
        o_ref[...] = o.reshape(Q_PER_KV, WINDOW, HEAD_DIM).astype(o_ref.dtype)

    return pl.pallas_call(
        body, name=name, grid=(NKV, L // WINDOW),
        in_specs=[qs, kprev, kcur, kprev, kcur, bs, sk], out_specs=qs,
        out_shape=jax.ShapeDtypeStruct((NQ, L, HEAD_DIM), BF16),
        compiler_params=_params(("parallel", "arbitrary")),
    )(qT, kT, kT, vT, vT, bias, sink_col)


def swa_bwd(qT, kT, vT, bias, sink_col, doT, *, name="swa_bwd"):
    NQ, L, _ = qT.shape
    NKV = kT.shape[0]
    qs, kprev, kcur, bs, sk = _swa_specs(NQ)
    W = WINDOW

    def body(q_ref, kp_ref, kc_ref, vp_ref, vc_ref, bias_ref, sink_ref, do_ref,
             dq_ref, dk_ref, dv_ref, dbias_ref, dsink_ref):
        n = pl.program_id(1)

        @pl.when(n == 0)
        def _():
            dk_ref[...] = jnp.zeros_like(dk_ref)
            dv_ref[...] = jnp.zeros_like(dv_ref)
            dbias_ref[...] = jnp.zeros_like(dbias_ref)
            dsink_ref[...] = jnp.zeros_like(dsink_ref)

        q, kk, p, ps = _swa_probs(q_ref, kp_ref, kc_ref, bias_ref, sink_ref, n)
        vv = jnp.concatenate([vp_ref[...], vc_ref[...]], axis=0)
        do = do_ref[...].reshape(Q_PER_KV * W, HEAD_DIM)
        dp = lax.dot_general(do, vv, (((1,), (1,)), ((), ())), preferred_element_type=F32)
        delta = jnp.sum(p * dp, axis=1, keepdims=True)
        ds = p * (dp - delta)
        dsink_ref[...] += -ps * delta
        dbias_ref[...] += ds.reshape(Q_PER_KV, W, 2 * W)
        dsb = ds.astype(BF16)
        scale = HEAD_DIM ** -0.5
        dq = jnp.dot(dsb, kk, preferred_element_type=F32) * scale
        dq_ref[...] = dq.reshape(Q_PER_KV, W, HEAD_DIM).astype(dq_ref.dtype)
        dkk = lax.dot_general(dsb, q, (((0,), (0,)), ((), ())), preferred_element_type=F32) * scale
        dvv = lax.dot_general(p.astype(BF16), do, (((0,), (0,)), ((), ())), preferred_element_type=F32)

        @pl.when(n == 0)
        def _():
            dk_ref[0:W, :] += dkk[W:, :]
            dv_ref[0:W, :] += dvv[W:, :]

        @pl.when(n > 0)
        def _():
            rows = pl.ds(pl.multiple_of((n - 1) * W, W), 2 * W)
            dk_ref[rows, :] += dkk
            dv_ref[rows, :] += dvv

    whole = _vspec((None, L, HEAD_DIM), lambda g, n: (g, 0, 0))
    return pl.pallas_call(
        body, name=name, grid=(NKV, L // W),
        in_specs=[qs, kprev, kcur, kprev, kcur, bs, sk, qs],
        out_specs=[qs, whole, whole, bs, sk],
        out_shape=[jax.ShapeDtypeStruct((NQ, L, HEAD_DIM), BF16), jax.ShapeDtypeStruct((NKV, L, HEAD_DIM), F32),
                   jax.ShapeDtypeStruct((NKV, L, HEAD_DIM), F32), jax.ShapeDtypeStruct((NQ, W, 2 * W), F32),
                   jax.ShapeDtypeStruct((NQ * W, 1), F32)],
        compiler_params=_params(("parallel", "arbitrary")),
    )(qT, kT, kT, vT, vT, bias, sink_col, doT)


def _bucket_table():
    qi = np.arange(WINDOW)[:, None]
    kj = np.arange(2 * WINDOW)[None, :]
    rel = qi + WINDOW - kj
    max_exact = N_BUCKETS // 2
    n = np.maximum(rel, 0)
    nf = np.maximum(n, max_exact).astype(np.float32)
    large = max_exact + (np.log(nf / max_exact) / math.log(MAX_DISTANCE / max_exact) * (N_BUCKETS - max_exact)).astype(np.int32)
    large = np.minimum(large, N_BUCKETS - 1)
    return np.where(n < max_exact, n, large).astype(np.int32).reshape(-1)


def _xa_probs(q, k):
    hd = q.shape[1]
    s = lax.dot_general(q, k, (((1,), (1,)), ((), ())), preferred_element_type=F32) * (hd ** -0.5)
    e = jnp.exp(s - jnp.max(s, axis=1, keepdims=True))
    return e / jnp.sum(e, axis=1, keepdims=True)


def xattn_fwd(q, kv, *, name="xattn_fwd"):
    L, D = q.shape
    Mm = kv.shape[0]
    hd = D // X_HEADS
    tq = _tile8(L, 512)

    def body(q_ref, kv_ref, o_ref):
        for h in range(X_HEADS):
            cols = slice(h * hd, (h + 1) * hd)
            p = _xa_probs(q_ref[:, cols], kv_ref[:, cols])
            o_ref[:, cols] = jnp.dot(p.astype(BF16), kv_ref[:, D + h * hd:D + (h + 1) * hd],
                                     preferred_element_type=F32).astype(o_ref.dtype)

    return pl.pallas_call(
        body, name=name, grid=(L // tq,),
        in_specs=[_vspec((tq, D), lambda i: (i, 0)), _vspec((Mm, 2 * D), lambda i: (0, 0))],
        out_specs=_vspec((tq, D), lambda i: (i, 0)),
        out_shape=jax.ShapeDtypeStruct((L, D), BF16),
        compiler_params=_params(("parallel",)),
    )(q, kv)


def xattn_bwd(q, kv, do, *, name="xattn_bwd"):
    L, D = q.shape
    Mm = kv.shape[0]
    hd = D // X_HEADS
    tq = _tile8(L, 512)

    def body(q_ref, kv_ref, do_ref, dq_ref, dkv_ref):
        @pl.when(pl.program_id(0) == 0)
        def _():
            dkv_ref[...] = jnp.zeros_like(dkv_ref)

        for h in range(X_HEADS):
            cols = slice(h * hd, (h + 1) * hd)
            vcols = slice(D + h * hd, D + (h + 1) * hd)
            qh, kh, vh, doh = q_ref[:, cols], kv_ref[:, cols], kv_ref[:, vcols], do_ref[:, cols]
            p = _xa_probs(qh, kh)
            dp = lax.dot_general(doh, vh, (((1,), (1,)), ((), ())), preferred_element_type=F32)
            ds = (p * (dp - jnp.sum(p * dp, axis=1, keepdims=True)) * (hd ** -0.5)).astype(BF16)
            dq_ref[:, cols] = jnp.dot(ds, kh, preferred_element_type=F32).astype(dq_ref.dtype)
            dkv_ref[:, cols] += lax.dot_general(ds, qh, (((0,), (0,)), ((), ())), preferred_element_type=F32)
            dkv_ref[:, vcols] += lax.dot_general(p.astype(BF16), doh, (((0,), (0,)), ((), ())), preferred_element_type=F32)

    return pl.pallas_call(
        body, name=name, grid=(L // tq,),
        in_specs=[_vspec((tq, D), lambda i: (i, 0)), _vspec((Mm, 2 * D), lambda i: (0, 0)), _vspec((tq, D), lambda i: (i, 0))],
        out_specs=[_vspec((tq, D), lambda i: (i, 0)), _vspec((Mm, 2 * D), lambda i: (0, 0))],
        out_shape=[jax.ShapeDtypeStruct((L, D), BF16), jax.ShapeDtypeStruct((Mm, 2 * D), F32)],
        compiler_params=_params(("arbitrary",)),
    )(q, kv, do)


def adamw(w, g, m, v, *, name="adamw"):
    R, C = w.shape
    tr = _tile8(R, max(SUBLANES, (256 * 1024) // C // SUBLANES * SUBLANES))

    def body(w_ref, g_ref, m_ref, v_ref, d_ref, nm_ref, nv_ref):
        g_ = g_ref[...]
        nm = ADAM_B1 * m_ref[...] + (1.0 - ADAM_B1) * g_
        nv = ADAM_B2 * v_ref[...] + (1.0 - ADAM_B2) * (g_ * g_)
        m_hat = nm / (1.0 - ADAM_B1 ** ADAM_STEP)
        v_hat = nv / (1.0 - ADAM_B2 ** ADAM_STEP)
        d_ref[...] = -ADAM_LR * (m_hat / (jnp.sqrt(v_hat) + ADAM_EPS) + ADAM_WD * w_ref[...])
        nm_ref[...] = nm
        nv_ref[...] = nv

    spec = _vspec((tr, C), lambda i: (i, 0))
    return pl.pallas_call(
        body, name=name, grid=(R // tr,), in_specs=[spec] * 4, out_specs=[spec] * 3,
        out_shape=[jax.ShapeDtypeStruct((R, C), F32)] * 3, compiler_params=_params(("parallel",)),
    )(w, g, m, v)


def _place():
    x, y, c = lax.axis_index("x"), lax.axis_index("y"), lax.axis_index("c")
    chips = [(1 - x, y), (x, 1 - y), (1 - x, 1 - y)]
    return x, y, c, chips


def _remote(src, dst, send, recv, k, to):
    return pltpu.make_async_remote_copy(src_ref=src, dst_ref=dst, send_sem=send.at[k], recv_sem=recv.at[k],
                                        device_id=to, device_id_type=MESH)


class _Job:
    result = None

    def start(self, ins, outs, send, recv, base):
        for cp in self.copies(ins, outs, send, recv, base)[0]:
            cp.start()


class GatherJob(_Job):
    nsem = 7

    def __init__(self, w):
        self.ops = [w]
        self.outs = [jax.ShapeDtypeStruct((N_CHIPS,) + w.shape, w.dtype)]
        self.cost = N_CHIPS * w.size

    def copies(self, ins, outs, send, recv, base, first_only=True):
        w, out = ins[0], outs[0]
        x, y, c, chips = _place()
        me, sib = 2 * x + y, (x, y, 1 - c)
        h = w.shape[0] // 2
        mine, theirs = pl.ds(c * h, h), pl.ds((1 - c) * h, h)
        first = [_remote(w.at[mine], out.at[me, mine], send, recv, base + j, (px, py, c))
                 for j, (px, py) in enumerate(chips)]
        first.append(_remote(w, out.at[me], send, recv, base + 6, sib))
        if first_only:
            return first,
        landed = [out.at[2 * px + py, mine] for px, py in chips]
        lands = [_remote(w.at[mine], landed[j], send, recv, base + j, (px, py, c)) for j, (px, py) in enumerate(chips)]
        passes = [_remote(landed[j], landed[j], send, recv, base + 3 + j, sib) for j in range(3)]
        arrives = [_remote(w.at[theirs], out.at[2 * px + py, theirs], send, recv, base + 3 + j, sib)
                   for j, (px, py) in enumerate(chips)]
        return first, lands, passes, arrives

    def finish(self, ins, outs, send, recv, base):
        first, lands, passes, arrives = self.copies(ins, outs, send, recv, base, first_only=False)
        for land, fwd in zip(lands, passes):
            land.wait_recv()
            fwd.start()
        first[3].wait_recv()
        for cp in arrives:
            cp.wait_recv()
        for cp in first + passes:
            cp.wait_send()


class ExchangeJob(_Job):
    nsem = 1

    def __init__(self, g):
        self.ops = [g]
        self.outs = [jax.ShapeDtypeStruct((g.shape[0], g.shape[1] // 2, g.shape[2]), g.dtype)]
        self.cost = 0.15 * g.size

    def copies(self, ins, outs, send, recv, base):
        x, y, c, _ = _place()
        r2 = ins[0].shape[1] // 2
        return [_remote(ins[0].at[:, pl.ds((1 - c) * r2, r2)], outs[0], send, recv, base, (x, y, 1 - c))],

    def finish(self, ins, outs, send, recv, base):
        self.copies(ins, outs, send, recv, base)[0][0].wait()


class ScatterJob(_Job):
    nsem = 3

    def __init__(self, p):
        self.ops = [p]
        self.outs = [jax.ShapeDtypeStruct((N_CHIPS - 1,) + p.shape[1:], p.dtype)]
        self.cost = 2 * p.size

    def copies(self, ins, outs, send, recv, base):
        x, y, c, chips = _place()
        return [_remote(ins[0].at[2 * px + py], outs[0].at[j], send, recv, base + j, (px, py, c))
                for j, (px, py) in enumerate(chips)],

    def finish(self, ins, outs, send, recv, base):
        for cp in self.copies(ins, outs, send, recv, base)[0]:
            cp.wait()


def run_comm(jobs, *, name):
    cops = [a for j in jobs for a in j.ops]
    couts = [s for j in jobs for s in j.outs]
    nsem = sum(j.nsem for j in jobs)

    def body(*refs):
        cin, cout = refs[:len(cops)], refs[len(cops):len(cops) + len(couts)]
        send, recv = refs[len(cops) + len(couts):]
        for phase in ("start", "finish"):
            ii = io = base = 0
            for j in jobs:
                getattr(j, phase)(cin[ii:ii + len(j.ops)], cout[io:io + len(j.outs)], send, recv, base)
                ii, io, base = ii + len(j.ops), io + len(j.outs), base + j.nsem

    res = pl.pallas_call(
        body, name=name, in_specs=[ANY] * len(cops), out_specs=[ANY] * len(couts), out_shape=couts,
        scratch_shapes=[pltpu.SemaphoreType.DMA((nsem,)), pltpu.SemaphoreType.DMA((nsem,))],
        compiler_params=pltpu.CompilerParams(has_side_effects=True),
    )(*cops)
    io = 0
    for j in jobs:
        j.result = list(res[io:io + len(j.outs)])
        io += len(j.outs)


def add_half(g, other, c_idx, *, name="rs_add_half"):
    S, R, C = g.shape
    r2 = R // 2
    tr = _tile8(r2, max(SUBLANES, (512 * 1024) // C // SUBLANES * SUBLANES))
    nb = r2 // tr

    def body(c_ref, g_ref, o_ref, out_ref):
        out_ref[...] = (g_ref[...].astype(F32) + o_ref[...].astype(F32)).astype(out_ref.dtype)

    return pl.pallas_call(
        body, name=name,
        grid_spec=pltpu.PrefetchScalarGridSpec(
            num_scalar_prefetch=1, grid=(S, nb),
            in_specs=[pl.BlockSpec((None, tr, C), lambda s, i, c_ref: (s, c_ref[0] * nb + i, 0)),
                      pl.BlockSpec((None, tr, C), lambda s, i, c_ref: (s, i, 0))],
            out_specs=pl.BlockSpec((None, tr, C), lambda s, i, c_ref: (s, i, 0))),
        out_shape=jax.ShapeDtypeStruct((S, r2, C), BF16),
        compiler_params=_params(("parallel", "parallel")),
    )(c_idx, g, other)


def add_partials(p, got, place_idx, gbuf, layer, *, name="rs_add_partials"):
    S, r2, C = p.shape
    tr = _tile8(r2, max(SUBLANES, (512 * 1024) // C // SUBLANES * SUBLANES))
    nb = r2 // tr

    def body(pi_ref, p_ref, g_ref, buf_ref, out_ref):
        out_ref[...] = ((p_ref[...].astype(F32) + g_ref[0].astype(F32)) + g_ref[1].astype(F32)) + g_ref[2].astype(F32)

    return pl.pallas_call(
        body, name=name,
        grid_spec=pltpu.PrefetchScalarGridSpec(
            num_scalar_prefetch=1, grid=(nb,),
            in_specs=[pl.BlockSpec((None, tr, C), lambda i, pi: (pi[0], i, 0)),
                      pl.BlockSpec((N_CHIPS - 1, tr, C), lambda i, pi: (0, i, 0)),
                      ANY],
            out_specs=pl.BlockSpec((None, tr, C), lambda i, pi: (layer, pi[1] * nb + i, 0))),
        out_shape=jax.ShapeDtypeStruct(gbuf.shape, F32),
        input_output_aliases={3: 0},
        compiler_params=_params(("parallel",)),
    )(place_idx, p, got, gbuf)


def join_halves(gbufs, *, name="rs_join_halves"):
    n = len(gbufs)

    def body(*refs):
        outs = refs[n:2 * n]
        send, recv = refs[2 * n:]
        x, y, c, _ = _place()
        sib = (x, y, 1 - c)
        cps = []
        for i in range(n):
            r2 = outs[i].shape[1] // 2
            mine = outs[i].at[:, pl.ds(c * r2, r2)]
            cp = _remote(mine, mine, send, recv, i, sib)
            cp.start()
            cps.append(cp)
        for i in range(n):
            r2 = outs[i].shape[1] // 2
            theirs = outs[i].at[:, pl.ds((1 - c) * r2, r2)]
            _remote(theirs, theirs, send, recv, i, sib).wait_recv()
        for cp in cps:
            cp.wait_send()

    return pl.pallas_call(
        body, name=name, in_specs=[ANY] * n, out_specs=[ANY] * n,
        out_shape=[jax.ShapeDtypeStruct(g.shape, F32) for g in gbufs],
        input_output_aliases={i: i for i in range(n)},
        scratch_shapes=[pltpu.SemaphoreType.DMA((n,)), pltpu.SemaphoreType.DMA((n,))],
        compiler_params=pltpu.CompilerParams(has_side_effects=True),
    )(*gbufs)


def all_gather_rows(v, *, name="all_gather_small"):
    m, ncol = v.shape

    def body(x_ref, out_ref, send, recv, lsem):
        x, y, c, chips = _place()
        me, sib = (x, y, c), (x, y, 1 - c)

        def rows(px, py, pc):
            return out_ref.at[pl.ds((4 * px + 2 * py + pc) * m, m), :]

        def copy(k, block, to, src=None):
            return _remote(rows(*block) if src is None else src, rows(*block), send, recv, k, to)

        mine = pltpu.make_async_copy(x_ref, rows(*me), lsem)
        mine.start()
        first = [copy(0, me, sib, src=x_ref)]
        first += [copy(1 + j, me, (*chip, c), src=x_ref) for j, chip in enumerate(chips)]
        for cp in first:
            cp.start()
        passed = [copy(4 + j, (*chip, c), sib) for j, chip in enumerate(chips)]
        for j, chip in enumerate(chips):
            copy(1 + j, (*chip, c), me).wait_recv()
            passed[j].start()
        copy(0, sib, me).wait_recv()
        for j, chip in enumerate(chips):
            copy(4 + j, (*chip, 1 - c), me).wait_recv()
        for cp in first + passed:
            cp.wait_send()
        mine.wait()

    return pl.pallas_call(
        body, name=name,
        in_specs=[pl.BlockSpec(memory_space=pltpu.VMEM)], out_specs=pl.BlockSpec(memory_space=pltpu.VMEM),
        out_shape=jax.ShapeDtypeStruct((8 * m, ncol), v.dtype),
        scratch_shapes=[pltpu.SemaphoreType.DMA((7,)), pltpu.SemaphoreType.DMA((7,)), pltpu.SemaphoreType.DMA],
        compiler_params=pltpu.CompilerParams(vmem_limit_bytes=VMEM_LIMIT_BYTES, has_side_effects=True),
    )(v)


def sum_blocks(g8, *, name="sum_blocks"):
    nb, m, ncol = g8.shape
    tr = _tile8(m, 512)

    def body(g_ref, o_ref):
        acc = g_ref[0]
        for k in range(1, nb):
            acc = acc + g_ref[k]
        o_ref[...] = acc

    return pl.pallas_call(
        body, name=name, grid=(m // tr,),
        in_specs=[_vspec((nb, tr, ncol), lambda i: (0, i, 0))], out_specs=_vspec((tr, ncol), lambda i: (i, 0)),
        out_shape=jax.ShapeDtypeStruct((m, ncol), F32), compiler_params=_params(("parallel",)),
    )(g8)


PACK_ROWS = 256


def _pack(arrs, mult):
    flat = jnp.concatenate([a.reshape(-1) for a in arrs])
    pad = (-flat.shape[0]) % (mult * LANES)
    return jnp.pad(flat, (0, pad)).reshape(-1, LANES)


def _unpack(packed, like):
    flat = packed.reshape(-1)
    out, off = [], 0
    for a in like:
        out.append(flat[off:off + a.size].reshape(a.shape))
        off += a.size
    return out


def _rows2d(a):
    return a.reshape(-1, a.shape[-1])


def kernel(x, mem, norm_mix, norm_xattn, norm_ffn, norm_final, norm_mem, rel_bias, ev_w_in, ev_conv_w, s5_a_re, s5_a_im, s5_log_dt, s5_b_re, s5_b_im, s5_c_re, s5_c_im, s5_d, s5_glu_w, ev_w_out, od_w_qkv, od_b_qkv, od_sinks, od_w_out, xa_w_q, xa_w_kv, xa_w_o, ff_w_gate, ff_w_up, ff_conv_w, ff_conv_b, ff_w_down, loss_target, m_norm_mix, m_norm_xattn, m_norm_ffn, m_norm_final, m_norm_mem, m_rel_bias, m_ev_w_in, m_ev_conv_w, m_s5_a_re, m_s5_a_im, m_s5_log_dt, m_s5_b_re, m_s5_b_im, m_s5_c_re, m_s5_c_im, m_s5_d, m_s5_glu_w, m_ev_w_out, m_od_w_qkv, m_od_b_qkv, m_od_sinks, m_od_w_out, m_xa_w_q, m_xa_w_kv, m_xa_w_o, m_ff_w_gate, m_ff_w_up, m_ff_conv_w, m_ff_conv_b, m_ff_w_down, v_norm_mix, v_norm_xattn, v_norm_ffn, v_norm_final, v_norm_mem, v_rel_bias, v_ev_w_in, v_ev_conv_w, v_s5_a_re, v_s5_a_im, v_s5_log_dt, v_s5_b_re, v_s5_b_im, v_s5_c_re, v_s5_c_im, v_s5_d, v_s5_glu_w, v_ev_w_out, v_od_w_qkv, v_od_b_qkv, v_od_sinks, v_od_w_out, v_xa_w_q, v_xa_w_kv, v_xa_w_o, v_ff_w_gate, v_ff_w_up, v_ff_conv_w, v_ff_conv_b, v_ff_w_down):
    names = ["norm_mix", "norm_xattn", "norm_ffn", "norm_final", "norm_mem", "rel_bias", "ev_w_in", "ev_conv_w",
             "s5_a_re", "s5_a_im", "s5_log_dt", "s5_b_re", "s5_b_im", "s5_c_re", "s5_c_im", "s5_d", "s5_glu_w",
             "ev_w_out", "od_w_qkv", "od_b_qkv", "od_sinks", "od_w_out", "xa_w_q", "xa_w_kv", "xa_w_o",
             "ff_w_gate", "ff_w_up", "ff_conv_w", "ff_conv_b", "ff_w_down"]
    env = dict(locals())
    W = {k: env[k] for k in names}
    Mo = {k: env["m_" + k] for k in names}
    Vo = {k: env["v_" + k] for k in names}

    h = x[0]
    target = loss_target[0]
    L, D = h.shape
    depth = norm_mix.shape[0]
    c_idx = lax.axis_index("c").astype(jnp.int32).reshape(1)
    me_idx = (2 * lax.axis_index("x") + lax.axis_index("y")).astype(jnp.int32).reshape(1)

    col_sharded = ["ev_w_in", "od_w_qkv", "xa_w_kv", "ff_w_gate", "ff_w_up"]
    row_sharded = ["ev_w_out", "od_w_out", "xa_w_q", "xa_w_o", "ff_w_down"]
    small_sharded = ["ev_conv_w", "od_b_qkv", "ff_conv_w"]
    big = col_sharded + row_sharded

    def layer_weights(l):
        mix = [("ev_w_in", l // 2), ("ev_w_out", l // 2)] if l % 2 == 0 else [("od_w_qkv", l // 2), ("od_w_out", l // 2)]
        return mix + [(k, l) for k in ("xa_w_q", "xa_w_kv", "xa_w_o", "ff_w_gate", "ff_w_up", "ff_w_down")]

    gjob = {kl: GatherJob(W[kl[0]][kl[1]].astype(BF16)) for l in range(depth) for kl in layer_weights(l)}
    small_jobs = [GatherJob(W[k]) for k in small_sharded]
    run_comm([gjob[kl] for kl in layer_weights(0)] + small_jobs, name="gather_first")
    pending = [gjob[kl] for l in range(1, depth) for kl in layer_weights(l)]
    flushes = []

    def take(queue, host_cost):
        jobs, acc = [], 0.0
        while queue and ((not jobs and host_cost >= 0.3 * queue[0].cost) or acc + queue[0].cost <= 1.5 * host_cost):
            acc += queue[0].cost
            jobs.append(queue.pop(0))
        return jobs

    def weight(k, l):
        job = gjob[(k, l)]
        if job.result is None:
            n = pending.index(job) + 1
            run_comm(pending[:n], name="gather_flush_%d" % len(flushes))
            flushes.append(n)
            del pending[:n]
        return job.result[0]

    def wcol(k, l):
        return weight(k, l)

    def wrow(k, l):
        g = weight(k, l)
        return g.reshape(1, g.shape[0] * g.shape[1], g.shape[2])

    def fwd_mm(a, w3, **kw):
        return mm_nn(a, w3, comm=take(pending, w3.size * a.shape[0] / L), **kw)

    sg = [j.result[0] for j in small_jobs]
    ev_conv_w_f = sg[0].transpose(1, 2, 0, 3).reshape(ev_conv_w.shape[0], 3, -1)
    od_b_qkv_f = sg[1].transpose(1, 0, 2).reshape(od_b_qkv.shape[0], 1, -1)
    ff_conv_w_f = sg[2].transpose(1, 2, 0, 3).reshape(ff_conv_w.shape[0], 3, -1)

    buckets = _bucket_table()
    NQ = D // HEAD_DIM
    NKV = NQ // Q_PER_KV
    onehot = jnp.asarray((buckets[:, None] == np.arange(N_BUCKETS)[None, :]).astype(np.float32))
    bias_tab = jnp.dot(rel_bias.T, onehot.T, precision=lax.Precision.HIGHEST).reshape(NQ, WINDOW, 2 * WINDOW)

    mem_n = rms_fwd(mem[0], norm_mem.reshape(1, D), name="rms_fwd_mem")

    saved = []
    for l in range(depth):
        i = l // 2
        s = {"h0": h}
        hn = rms_fwd(h, norm_mix[l].reshape(1, D))
        s["hn"] = hn
        if l % 2 == 0:
            A = ev_conv_w_f.shape[-1]
            z = fwd_mm(hn, wcol("ev_w_in", i), name="mm_ev_in")
            ya = conv_mixer_fwd(z, ev_conv_w_f[i])
            prep = functools.partial(_s5_prep, seg_len=L // N_SEG)
            s5p = (s5_a_re[i], s5_a_im[i], s5_log_dt[i], s5_b_re[i], s5_b_im[i], s5_c_re[i], s5_c_im[i], s5_d[i], s5_glu_w[i])
            (a_l, apow_l, bm, cm, d_l, glu), prep_vjp = jax.vjp(prep, *s5p)
            bm16, cm16, glu16 = bm.astype(BF16), cm.astype(BF16), glu.astype(BF16)
            u_p = _to_segments(z[:, 3 * A:])
            fin = s5_scan_fwd(u_p, a_l, apow_l, bm16, cm16, d_l, glu16, None, name="s5_fwd_ends")
            ys_p, y_p, states, starts = s5_scan_fwd(u_p, a_l, apow_l, bm16, cm16, d_l, glu16, fin, name="s5_fwd")
            ycat = jnp.concatenate([ya, _from_segments(ys_p)], axis=1)
            s.update(z=z, u_p=u_p, y_p=y_p, states=states, starts=starts, ycat=ycat, prep_vjp=prep_vjp,
                     s5ops=(a_l, apow_l, bm16, cm16, d_l, glu16))
            h = fwd_mm(ycat, wrow("ev_w_out", i), res=h, name="mm_ev_out")
        else:
            z = fwd_mm(hn, wcol("od_w_qkv", i), bias=od_b_qkv_f[i], out_dtype=BF16, name="mm_od_qkv")
            qT = z[:, :NQ * HEAD_DIM].reshape(L, NQ, HEAD_DIM).transpose(1, 0, 2)
            kT = z[:, NQ * HEAD_DIM:(NQ + NKV) * HEAD_DIM].reshape(L, NKV, HEAD_DIM).transpose(1, 0, 2)
            vT = z[:, (NQ + NKV) * HEAD_DIM:].reshape(L, NKV, HEAD_DIM).transpose(1, 0, 2)
            sink_col = jnp.repeat(od_sinks[i], WINDOW).reshape(NQ * WINDOW, 1)
            oT = swa_fwd(qT, kT, vT, bias_tab, sink_col)
            o = oT.transpose(1, 0, 2).reshape(L, D)
            s.update(qT=qT, kT=kT, vT=vT, sink_col=sink_col, o=o)
            h = fwd_mm(o, wrow("od_w_out", i), res=h, name="mm_od_out")
        s["h1"] = h
        hn2 = rms_fwd(h, norm_xattn[l].reshape(1, D))
        q = fwd_mm(hn2, wrow("xa_w_q", l), out_dtype=BF16, name="mm_xa_q")
        kv = fwd_mm(mem_n, wcol("xa_w_kv", l), out_dtype=BF16, name="mm_xa_kv")
        ox = xattn_fwd(q, kv)
        s.update(hn2=hn2, q=q, kv=kv, ox=ox)
        h = fwd_mm(ox, wrow("xa_w_o", l), res=h, name="mm_xa_o")
        s["h2"] = h
        hn3 = rms_fwd(h, norm_ffn[l].reshape(1, D))
        gpre = fwd_mm(hn3, wcol("ff_w_gate", l), name="mm_ff_gate")
        up = fwd_mm(hn3, wcol("ff_w_up", l), name="mm_ff_up")
        act = ffn_act_fwd(gpre, up, ff_conv_w_f[l], ff_conv_b[l].reshape(1, -1))
        s.update(hn3=hn3, gpre=gpre, up=up, act=act)
        h = fwd_mm(act, wrow("ff_w_down", l), res=h, name="mm_ff_down")
        saved.append(s)

    loss11, dh, dg_final = final_loss(h, norm_final.reshape(1, D), target)
    loss = lax.psum(loss11[0, 0], AXES)

    gs = {k: [None] * W[k].shape[0] for k in names if k not in big and W[k].ndim > 1 and k != "rel_bias"}
    dmem_n = None
    dbias_tab = jnp.zeros_like(bias_tab)
    place_idx = jnp.concatenate([me_idx, c_idx])
    gbufs, exchanges, scatters = {}, [], []

    def settle(jobs):
        for j in jobs:
            k, l = j.tag
            if isinstance(j, ExchangeJob):
                nxt = ScatterJob(add_half(j.ops[0], j.result[0], c_idx))
                nxt.tag = j.tag
                scatters.append(nxt)
            else:
                p = j.ops[0]
                if k not in gbufs:
                    gbufs[k] = jnp.zeros((W[k].shape[0], 2 * p.shape[1], p.shape[2]), F32)
                gbufs[k] = add_partials(p, j.result[0], place_idx, gbufs[k], l)

    def take_fit(queue, budget):
        jobs = []
        for j in list(queue):
            if j.cost <= budget:
                budget -= j.cost
                jobs.append(j)
                queue.remove(j)
        return jobs

    def bwd_mm(fn, *args, cost, **kw):
        jobs = take_fit(exchanges, 0.6 * cost) + take_fit(scatters, 1.2 * cost)
        out = fn(*args, comm=jobs, **kw)
        settle(jobs)
        return out

    def dx_mm(a, w3, **kw):
        return bwd_mm(mm_nt, a, w3, cost=w3.size * a.shape[0] / L, **kw)

    def dw_mm(k, l, xx, dy, S, **kw):
        g3 = bwd_mm(mm_tn, xx, dy, S, cost=xx.shape[1] * dy.shape[1] * xx.shape[0] / L, **kw)
        if S == 1:
            g3 = g3.reshape(N_CHIPS, g3.shape[1] // N_CHIPS, g3.shape[2])
        job = ExchangeJob(g3)
        job.tag = (k, l)
        exchanges.append(job)

    for l in reversed(range(depth)):
        i = l // 2
        s = saved[l]
        dact = dx_mm(dh, wrow("ff_w_down", l), name="mm_ff_down_dx")
        dw_mm("ff_w_down", l, s["act"], dh, 1, name="mm_ff_down_dw")
        dgpre, dup, dcw, dcb = ffn_act_bwd(dact, s["gpre"], s["up"], ff_conv_w_f[l], ff_conv_b[l].reshape(1, -1))
        gs["ff_conv_w"][l], gs["ff_conv_b"][l] = dcw, dcb[0]
        dhn3 = dx_mm(dgpre, wcol("ff_w_gate", l), name="mm_ff_gate_dx")
        dhn3 = dx_mm(dup, wcol("ff_w_up", l), res=dhn3, name="mm_ff_up_dx")
        dw_mm("ff_w_gate", l, s["hn3"], dgpre, N_CHIPS, name="mm_ff_gate_dw")
        dw_mm("ff_w_up", l, s["hn3"], dup, N_CHIPS, name="mm_ff_up_dw")
        dh, dg = rms_bwd(s["h2"], norm_ffn[l].reshape(1, D), dhn3, dh)
        gs["norm_ffn"][l] = dg[0]
        dox = dx_mm(dh, wrow("xa_w_o", l), out_dtype=BF16, name="mm_xa_o_dx")
        dw_mm("xa_w_o", l, s["ox"], dh, 1, name="mm_xa_o_dw")
        dq, dkv = xattn_bwd(s["q"], s["kv"], dox)
        dhn2 = dx_mm(dq, wrow("xa_w_q", l), name="mm_xa_q_dx")
        dw_mm("xa_w_q", l, s["hn2"], dq, 1, name="mm_xa_q_dw")
        dw_mm("xa_w_kv", l, mem_n, dkv, N_CHIPS, name="mm_xa_kv_dw")
        dmem_n = dx_mm(dkv, wcol("xa_w_kv", l), res=dmem_n, name="mm_xa_kv_dx")
        dh, dg = rms_bwd(s["h1"], norm_xattn[l].reshape(1, D), dhn2, dh)
        gs["norm_xattn"][l] = dg[0]
        if l % 2 == 0:
            A = ev_conv_w_f.shape[-1]
            dycat = dx_mm(dh, wrow("ev_w_out", i), name="mm_ev_out_dx")
            dw_mm("ev_w_out", i, s["ycat"], dh, 1, name="mm_ev_out_dw")
            dgb, dgc, dxa, dcw = conv_mixer_bwd(s["z"], ev_conv_w_f[i], dycat[:, :A])
            gs["ev_conv_w"][i] = dcw
            a_l, apow_l, bm16, cm16, d_l, glu16 = s["s5ops"]
            dys_p = _to_segments(dycat[:, A:])
            dy_p, dglu, dd = s5_out_bwd(dys_p, s["y_p"], s["u_p"], glu16)
            cmt = cm16.transpose(0, 2, 1)
            bmt = bm16.transpose(0, 2, 1)
            gfin = s5_scan_bwd(dy_p, a_l, apow_l, cmt, None, name="s5_bwd_ends")
            du_p, da, dbm, dcm = s5_scan_bwd(dy_p, a_l, apow_l, cmt, gfin, s["states"], s["starts"], s["u_p"], bmt, d_l,
                                            name="s5_bwd")
            dprm = s["prep_vjp"]((da, jnp.zeros_like(apow_l), dbm, dcm, dd, dglu))
            for k, g in zip(["s5_a_re", "s5_a_im", "s5_log_dt", "s5_b_re", "s5_b_im", "s5_c_re", "s5_c_im", "s5_d", "s5_glu_w"], dprm):
                gs[k][i] = g
            dz = jnp.concatenate([dgb, dgc, dxa, _from_segments(du_p)], axis=1)
            dhn = dx_mm(dz, wcol("ev_w_in", i), name="mm_ev_in_dx")
            dw_mm("ev_w_in", i, s["hn"], dz, N_CHIPS, name="mm_ev_in_dw")
        else:
            do = dx_mm(dh, wrow("od_w_out", i), out_dtype=BF16, name="mm_od_out_dx")
            dw_mm("od_w_out", i, s["o"], dh, 1, name="mm_od_out_dw")
            doT = do.reshape(L, NQ, HEAD_DIM).transpose(1, 0, 2)
            dqT, dkT, dvT, dbias, dsink = swa_bwd(s["qT"], s["kT"], s["vT"], bias_tab, s["sink_col"], doT)
            dbias_tab = dbias_tab + dbias
            gs["od_sinks"][i] = jnp.sum(dsink.reshape(NQ, WINDOW), axis=1)
            dz = jnp.concatenate([dqT.transpose(1, 0, 2).reshape(L, NQ * HEAD_DIM),
                                  dkT.astype(BF16).transpose(1, 0, 2).reshape(L, NKV * HEAD_DIM),
                                  dvT.astype(BF16).transpose(1, 0, 2).reshape(L, NKV * HEAD_DIM)], axis=1)
            gs["od_b_qkv"][i] = col_sum(dz)[0]
            dhn = dx_mm(dz, wcol("od_w_qkv", i), name="mm_od_qkv_dx")
            dw_mm("od_w_qkv", i, s["hn"], dz, N_CHIPS, name="mm_od_qkv_dw")
        dh, dg = rms_bwd(s["h0"], norm_mix[l].reshape(1, D), dhn, dh)
        gs["norm_mix"][l] = dg[0]

    grad_x = dh[None]
    _, dg_mem = rms_bwd(mem[0], norm_mem.reshape(1, D), dmem_n, jnp.zeros_like(dmem_n), name="rms_bwd_mem")
    d_rel_bias = jnp.dot(dbias_tab.reshape(NQ, -1), onehot, precision=lax.Precision.HIGHEST).T

    small = [k for k in names if k not in big]
    local_small = {k: (jnp.stack(gs[k]) if k in gs else None) for k in small}
    local_small["norm_final"] = dg_final[0]
    local_small["norm_mem"] = dg_mem[0]
    local_small["rel_bias"] = d_rel_bias
    full_shape = {k: W[k].shape for k in small}
    for k in small_sharded:
        full_shape[k] = local_small[k].shape
    lst = [local_small[k].reshape(full_shape[k]).astype(F32) for k in small]
    packed = _pack(lst, PACK_ROWS)
    m_rows = packed.shape[0]
    summed = sum_blocks(all_gather_rows(packed).reshape(8, m_rows, LANES))
    gsum = dict(zip(small, _unpack(summed, lst)))
    for k in small_sharded:
        n4 = W[k].shape[-1]
        gsum[k] = lax.dynamic_slice_in_dim(gsum[k], me_idx[0] * n4, n4, axis=gsum[k].ndim - 1)

    for queue, name in ((exchanges, "rs_exchange_rest"), (scatters, "rs_scatter_rest")):
        jobs = queue[:]
        del queue[:]
        if jobs:
            run_comm(jobs, name=name)
            settle(jobs)
    joined = join_halves([gbufs[k] for k in big])
    gbig = {k: g.reshape(W[k].shape) for k, g in zip(big, joined)}

    grads = {**gsum, **gbig}
    delta, new_m, new_v = {}, {}, {}
    for k in big:
        d_, m_, v_ = adamw(_rows2d(W[k]), _rows2d(grads[k]), _rows2d(Mo[k]), _rows2d(Vo[k]), name="adamw_" + k)
        delta[k], new_m[k], new_v[k] = d_.reshape(W[k].shape), m_.reshape(W[k].shape), v_.reshape(W[k].shape)
    sw = [W[k] for k in small]
    d_, m_, v_ = adamw(_pack(sw, PACK_ROWS), _pack([grads[k] for k in small], PACK_ROWS),
                       _pack([Mo[k] for k in small], PACK_ROWS), _pack([Vo[k] for k in small], PACK_ROWS), name="adamw_small")
    for k, a, b, c_ in zip(small, _unpack(d_, sw), _unpack(m_, sw), _unpack(v_, sw)):
        delta[k], new_m[k], new_v[k] = a, b, c_

    return (loss, grad_x, *[grads[k] for k in names], *[delta[k] for k in names],
            *[new_m[k] for k in names], *[new_v[k] for k in names])
```

```python
import functools
import math

import numpy as np
import jax
import jax.numpy as jnp
from jax import lax
from jax.experimental import pallas as pl
from jax.experimental.pallas import tpu as pltpu

F32, BF16 = jnp.float32, jnp.bfloat16
MESH = pl.DeviceIdType.MESH
AXES = ("x", "y", "c")

VMEM_LIMIT_BYTES = 56 * 2**20
SUBLANES, LANES = 8, 128

RMS_EPS = 1e-5
S5_GROUP, S5_STATE = 16, 64
HEAD_DIM, Q_PER_KV, WINDOW = 64, 8, 128
N_BUCKETS, MAX_DISTANCE = 32, 128
X_HEADS = 4
NEG_INF = -1e30
ADAM_LR, ADAM_B1, ADAM_B2, ADAM_EPS, ADAM_WD, ADAM_STEP = 0.001, 0.9, 0.999, 1e-08, 0.01, 10
N_CHIPS = 4
N_SEG = 8
S5_STEPS = 32
HOST_UNITS = {"rms_bwd": 0.57, "ffn_act_fwd": 0.39, "ffn_act_bwd": 0.63, "swa_fwd": 3.1, "swa_bwd": 6.0,
              "s5_fwd": 3.8, "s5_ends": 2.3, "s5_bwd": 4.5, "s5_out_bwd": 1.4}


def _params(sem=None):
    return pltpu.CompilerParams(dimension_semantics=sem, vmem_limit_bytes=VMEM_LIMIT_BYTES)


def _vspec(shape, index_map):
    return pl.BlockSpec(shape, index_map)


ANY = pl.BlockSpec(memory_space=pl.ANY)


def _tile(n, pref):
    t = (min(pref, n) // LANES) * LANES
    while t >= LANES:
        if n % t == 0:
            return t
        t -= LANES
    return n


def _acc_matmul(nk, k, acc, partial, finish):
    if nk == 1:
        finish(partial())
        return

    @pl.when(k == 0)
    def _():
        acc[...] = partial()

    @pl.when(jnp.logical_and(k > 0, k < nk - 1))
    def _():
        acc[...] += partial()

    @pl.when(k == nk - 1)
    def _():
        finish(acc[...] + partial())


MM_MAX_K = 2048


def hosted_call(body, *, name, grid, in_specs, out_specs, out_shape, ops, scratch_shapes=(), semantics, comm=()):
    if not comm:
        return pl.pallas_call(body, name=name, grid=grid, in_specs=in_specs, out_specs=out_specs, out_shape=out_shape,
                              scratch_shapes=list(scratch_shapes), compiler_params=_params(semantics))(*ops)
    n_in, n_out, n_scr = len(ops), len(out_shape), len(scratch_shapes)
    cops = [a for j in comm for a in j.ops]
    couts = [s for j in comm for s in j.outs]
    nsem = sum(j.nsem for j in comm)

    def hosted(*refs):
        ins, refs = refs[:n_in], refs[n_in:]
        cin, refs = refs[:len(cops)], refs[len(cops):]
        outs, refs = refs[:n_out], refs[n_out:]
        cout, refs = refs[:len(couts)], refs[len(couts):]
        scr, (send, recv) = refs[:n_scr], refs[n_scr:]

        def each(phase):
            ii = io = base = 0
            for j in comm:
                getattr(j, phase)(cin[ii:ii + len(j.ops)], cout[io:io + len(j.outs)], send, recv, base)
                ii, io, base = ii + len(j.ops), io + len(j.outs), base + j.nsem

        pids = [pl.program_id(d) for d in range(len(grid))]
        first = functools.reduce(jnp.logical_and, [p == 0 for p in pids])
        last = functools.reduce(jnp.logical_and, [p == g - 1 for p, g in zip(pids, grid)])
        pl.when(first)(lambda: each("start"))
        body(*ins, *outs, *scr)
        pl.when(last)(lambda: each("finish"))

    res = pl.pallas_call(
        hosted, name=name, grid=grid, in_specs=list(in_specs) + [ANY] * len(cops),
        out_specs=list(out_specs) + [ANY] * len(couts), out_shape=list(out_shape) + couts,
        scratch_shapes=list(scratch_shapes) + [pltpu.SemaphoreType.DMA((nsem,)), pltpu.SemaphoreType.DMA((nsem,))],
        compiler_params=pltpu.CompilerParams(dimension_semantics=("arbitrary",) * len(grid),
                                             vmem_limit_bytes=VMEM_LIMIT_BYTES, has_side_effects=True),
    )(*ops, *cops)
    io = n_out
    for j in comm:
        j.result = list(res[io:io + len(j.outs)])
        io += len(j.outs)
    return list(res[:n_out])


def _pcall(body, *, name, grid, in_specs, out_specs, out_shape, scratch_shapes=(), sem, comm=()):
    single = not isinstance(out_shape, (list, tuple))

    def run(*ops):
        res = hosted_call(body, name=name, grid=grid, in_specs=list(in_specs),
                          out_specs=[out_specs] if single else list(out_specs),
                          out_shape=[out_shape] if single else list(out_shape), ops=list(ops),
                          scratch_shapes=scratch_shapes, semantics=sem, comm=comm)
        return res[0] if single else res

    return run


def _mm_call(core, grid, in_specs, ops, out_spec, out_shape, acc_shape, name, comm):
    n_in = len(ops)

    def body(*refs):
        core(refs[:n_in], refs[n_in], refs[n_in + 1])

    return hosted_call(body, name=name, grid=grid, in_specs=in_specs, out_specs=[out_spec], out_shape=[out_shape],
                       ops=ops, scratch_shapes=[pltpu.VMEM(acc_shape, F32)],
                       semantics=("parallel", "parallel", "arbitrary"), comm=comm)[0]


def mm_nn(a, w3, *, bias=None, res=None, out_dtype=F32, name, comm=()):
    M, K = a.shape
    S, K2, ns = w3.shape
    assert K == K2
    tm = _tile(M, 512)
    tk = K if K <= MM_MAX_K else _tile(K, 1536)
    nk = K // tk
    has_b, has_r = bias is not None, res is not None

    def core(ins, o_ref, acc):
        a_ref, w_ref = ins[0], ins[1]
        b_ref = ins[2] if has_b else None
        r_ref = ins[2 + has_b] if has_r else None

        def partial():
            return jnp.dot(a_ref[...].astype(BF16), w_ref[...], preferred_element_type=F32)

        def finish(r):
            if has_b:
                r = r + b_ref[...]
            if has_r:
                r = r + r_ref[...]
            o_ref[...] = r.astype(o_ref.dtype)

        _acc_matmul(nk, pl.program_id(2), acc, partial, finish)

    in_specs = [_vspec((tm, tk), lambda s, i, k: (i, k)), _vspec((None, tk, ns), lambda s, i, k: (s, k, 0))]
    ops = [a, w3]
    if has_b:
        in_specs.append(_vspec((1, ns), lambda s, i, k: (0, s)))
        ops.append(bias)
    if has_r:
        in_specs.append(_vspec((tm, ns), lambda s, i, k: (i, s)))
        ops.append(res)
    return _mm_call(core, (S, M // tm, nk), in_specs, ops, _vspec((tm, ns), lambda s, i, k: (i, s)),
                    jax.ShapeDtypeStruct((M, S * ns), out_dtype), (tm, ns) if nk > 1 else (SUBLANES, LANES), name, comm)


def mm_nt(a, w3, *, res=None, out_dtype=F32, name, comm=()):
    M, N = a.shape
    S, K, ns = w3.shape
    assert N == S * ns
    tko = K if K <= MM_MAX_K else _tile(K, 1024)
    tm = _tile(M, 512 if tko == K else 1024)
    tc = ns if ns <= MM_MAX_K else _tile(ns, 1024)
    ncs = ns // tc
    nc = S * ncs
    has_r = res is not None

    def core(ins, o_ref, acc):
        a_ref, w_ref = ins[0], ins[1]
        r_ref = ins[2] if has_r else None

        def partial():
            return lax.dot_general(a_ref[...].astype(BF16), w_ref[...], (((1,), (1,)), ((), ())),
                                   preferred_element_type=F32)

        def finish(r):
            if has_r:
                r = r + r_ref[...]
            o_ref[...] = r.astype(o_ref.dtype)

        _acc_matmul(nc, pl.program_id(2), acc, partial, finish)

    in_specs = [_vspec((tm, tc), lambda i, j, k: (i, k)),
                _vspec((None, tko, tc), lambda i, j, k: (k // ncs, j, k % ncs))]
    ops = [a, w3]
    if has_r:
        in_specs.append(_vspec((tm, tko), lambda i, j, k: (i, j)))
        ops.append(res)
    return _mm_call(core, (M // tm, K // tko, nc), in_specs, ops, _vspec((tm, tko), lambda i, j, k: (i, j)),
                    jax.ShapeDtypeStruct((M, K), out_dtype), (tm, tko) if nc > 1 else (SUBLANES, LANES), name, comm)


def mm_tn(x, dy, S, *, out_dtype=BF16, name, comm=()):
    M, K = x.shape
    M2, N = dy.shape
    assert M == M2 and N % S == 0
    ns = N // S
    tk, tmc = _tile(K, 1024), _tile(M, 1024)
    nm = M // tmc

    def core(ins, o_ref, acc):
        x_ref, dy_ref = ins

        def partial():
            return lax.dot_general(x_ref[...].astype(BF16), dy_ref[...].astype(BF16), (((0,), (0,)), ((), ())),
                                   preferred_element_type=F32)

        def finish(r):
            o_ref[...] = r.astype(o_ref.dtype)

        _acc_matmul(nm, pl.program_id(2), acc, partial, finish)

    in_specs = [_vspec((tmc, tk), lambda s, i, m: (m, i)), _vspec((tmc, ns), lambda s, i, m: (m, s))]
    return _mm_call(core, (S, K // tk, nm), in_specs, [x, dy], _vspec((None, tk, ns), lambda s, i, m: (s, i, 0)),
                    jax.ShapeDtypeStruct((S, K, ns), out_dtype), (tk, ns) if nm > 1 else (SUBLANES, LANES), name, comm)


def rms_fwd(h, g, *, name="rms_fwd"):
    R, D = h.shape
    tr = _tile8(R, 256)

    def body(h_ref, g_ref, o_ref):
        x = h_ref[...]
        r = lax.rsqrt(jnp.mean(x * x, axis=-1, keepdims=True) + RMS_EPS)
        o_ref[...] = (x * r * g_ref[...]).astype(o_ref.dtype)

    return pl.pallas_call(
        body, name=name, grid=(R // tr,),
        in_specs=[_vspec((tr, D), lambda i: (i, 0)), _vspec((1, D), lambda i: (0, 0))],
        out_specs=_vspec((tr, D), lambda i: (i, 0)),
        out_shape=jax.ShapeDtypeStruct((R, D), BF16),
        compiler_params=_params(("parallel",)),
    )(h, g)


def _tile8(n, pref):
    t = (min(pref, n) // SUBLANES) * SUBLANES
    while t >= SUBLANES:
        if n % t == 0:
            return t
        t -= SUBLANES
    return n


def rms_bwd(h, g, dhn, dres, *, name="rms_bwd", comm=()):
    R, D = h.shape
    tr = _tile8(R, 256)

    def body(h_ref, g_ref, dhn_ref, dres_ref, dh_ref, dg_ref):
        @pl.when(pl.program_id(0) == 0)
        def _():
            dg_ref[...] = jnp.zeros_like(dg_ref)

        x = h_ref[...]
        d = dhn_ref[...].astype(F32)
        r = lax.rsqrt(jnp.mean(x * x, axis=-1, keepdims=True) + RMS_EPS)
        xhat = x * r
        dg_ref[...] += jnp.sum(d * xhat, axis=0, keepdims=True)
        t = d * g_ref[...]
        dh_ref[...] = dres_ref[...] + r * (t - xhat * jnp.mean(t * xhat, axis=-1, keepdims=True))

    return _pcall(
        body, name=name, grid=(R // tr,),
        in_specs=[_vspec((tr, D), lambda i: (i, 0)), _vspec((1, D), lambda i: (0, 0)),
                  _vspec((tr, D), lambda i: (i, 0)), _vspec((tr, D), lambda i: (i, 0))],
        out_specs=[_vspec((tr, D), lambda i: (i, 0)), _vspec((1, D), lambda i: (0, 0))],
        out_shape=[jax.ShapeDtypeStruct((R, D), F32), jax.ShapeDtypeStruct((1, D), F32)],
        sem=("arbitrary",), comm=comm,
    )(h, g, dhn, dres)


def final_loss(h, g, target, *, name="final_loss"):
    R, D = h.shape
    tr = _tile8(R, 256)

    def body(h_ref, g_ref, t_ref, loss_ref, dh_ref, dg_ref):
        @pl.when(pl.program_id(0) == 0)
        def _():
            dg_ref[...] = jnp.zeros_like(dg_ref)
            loss_ref[...] = jnp.zeros_like(loss_ref)

        x = h_ref[...]
        r = lax.rsqrt(jnp.mean(x * x, axis=-1, keepdims=True) + RMS_EPS)
        xhat = x * r
        err = xhat * g_ref[...] - t_ref[...]
        row = jnp.mean(err * err, axis=-1, keepdims=True)
        loss_ref[...] += 0.5 * jnp.sum(row, axis=0, keepdims=True)
        d = err * (1.0 / D)
        dg_ref[...] += jnp.sum(d * xhat, axis=0, keepdims=True)
        t = d * g_ref[...]
        dh_ref[...] = r * (t - xhat * jnp.mean(t * xhat, axis=-1, keepdims=True))

    return pl.pallas_call(
        body, name=name, grid=(R // tr,),
        in_specs=[_vspec((tr, D), lambda i: (i, 0)), _vspec((1, D), lambda i: (0, 0)), _vspec((tr, D), lambda i: (i, 0))],
        out_specs=[_vspec((1, 1), lambda i: (0, 0)), _vspec((tr, D), lambda i: (i, 0)), _vspec((1, D), lambda i: (0, 0))],
        out_shape=[jax.ShapeDtypeStruct((1, 1), F32), jax.ShapeDtypeStruct((R, D), F32), jax.ShapeDtypeStruct((1, D), F32)],
        compiler_params=_params(("arbitrary",)),
    )(h, g, target)


def _shift_down(v, k):
    rows = lax.broadcasted_iota(jnp.int32, v.shape, 0)
    return jnp.where(rows >= k, pltpu.roll(v, k, axis=0), 0.0)


def _shift_up(v, k):
    L = v.shape[0]
    rows = lax.broadcasted_iota(jnp.int32, v.shape, 0)
    return jnp.where(rows < L - k, pltpu.roll(v, L - k, axis=0), 0.0)


def _conv(v, w):
    return w[2:3, :] * v + w[1:2, :] * _shift_down(v, 1) + w[0:1, :] * _shift_down(v, 2)


def _conv_t(d, w):
    return w[2:3, :] * d + w[1:2, :] * _shift_up(d, 1) + w[0:1, :] * _shift_up(d, 2)


def _conv_dw(d, v):
    return jnp.concatenate([
        jnp.sum(d * _shift_down(v, 2), axis=0, keepdims=True),
        jnp.sum(d * _shift_down(v, 1), axis=0, keepdims=True),
        jnp.sum(d * v, axis=0, keepdims=True)], axis=0)


COL_BLOCK = 128


def conv_mixer_fwd(z, cw, *, name="conv_mixer_fwd"):
    L = z.shape[0]
    A = cw.shape[1]
    cb = _tile(A, COL_BLOCK)
    nb = A // cb

    def body(gb_ref, gc_ref, xa_ref, w_ref, o_ref):
        v = gc_ref[...] * xa_ref[...]
        o_ref[...] = (gb_ref[...] * _conv(v, w_ref[...])).astype(o_ref.dtype)

    return pl.pallas_call(
        body, name=name, grid=(nb,),
        in_specs=[_vspec((L, cb), lambda j: (0, j)), _vspec((L, cb), lambda j: (0, nb + j)),
                  _vspec((L, cb), lambda j: (0, 2 * nb + j)), _vspec((3, cb), lambda j: (0, j))],
        out_specs=_vspec((L, cb), lambda j: (0, j)),
        out_shape=jax.ShapeDtypeStruct((L, A), BF16),
        compiler_params=_params(("parallel",)),
    )(z, z, z, cw)


def conv_mixer_bwd(z, cw, dya, *, name="conv_mixer_bwd"):
    L = z.shape[0]
    A = cw.shape[1]
    cb = _tile(A, COL_BLOCK)
    nb = A // cb

    def body(gb_ref, gc_ref, xa_ref, w_ref, d_ref, dgb_ref, dgc_ref, dxa_ref, dw_ref):
        gc, xa, w, d = gc_ref[...], xa_ref[...], w_ref[...], d_ref[...]
        v = gc * xa
        dgb_ref[...] = (d * _conv(v, w)).astype(dgb_ref.dtype)
        dc = d * gb_ref[...]
        dw_ref[...] = _conv_dw(dc, v)
        dv = _conv_t(dc, w)
        dgc_ref[...] = (dv * xa).astype(dgc_ref.dtype)
        dxa_ref[...] = (dv * gc).astype(dxa_ref.dtype)

    col = lambda j: (0, j)
    outs = pl.pallas_call(
        body, name=name, grid=(nb,),
        in_specs=[_vspec((L, cb), col), _vspec((L, cb), lambda j: (0, nb + j)),
                  _vspec((L, cb), lambda j: (0, 2 * nb + j)), _vspec((3, cb), col), _vspec((L, cb), col)],
        out_specs=[_vspec((L, cb), col), _vspec((L, cb), col), _vspec((L, cb), col), _vspec((3, cb), col)],
        out_shape=[jax.ShapeDtypeStruct((L, A), BF16)] * 3 + [jax.ShapeDtypeStruct((3, A), F32)],
        compiler_params=_params(("parallel",)),
    )(z, z, z, cw, dya)
    return outs[0], outs[1], outs[2], outs[3]


def ffn_act_fwd(gpre, up, cw, cbias, *, name="ffn_act_fwd", comm=()):
    L, Fd = gpre.shape
    cb = _tile(Fd, COL_BLOCK)

    def body(g_ref, u_ref, w_ref, b_ref, o_ref):
        g = _conv(g_ref[...].astype(F32), w_ref[...]) + b_ref[...]
        o_ref[...] = (g * jax.nn.sigmoid(g) * u_ref[...].astype(F32)).astype(o_ref.dtype)

    col = lambda j: (0, j)
    return _pcall(
        body, name=name, grid=(Fd // cb,),
        in_specs=[_vspec((L, cb), col), _vspec((L, cb), col), _vspec((3, cb), col), _vspec((1, cb), col)],
        out_specs=_vspec((L, cb), col),
        out_shape=jax.ShapeDtypeStruct((L, Fd), BF16),
        sem=("parallel",), comm=comm,
    )(gpre, up, cw, cbias)


def ffn_act_bwd(dact, gpre, up, cw, cbias, *, name="ffn_act_bwd", comm=()):
    L, Fd = gpre.shape
    cb = _tile(Fd, COL_BLOCK)

    def body(d_ref, g_ref, u_ref, w_ref, b_ref, dg_ref, du_ref, dw_ref, db_ref):
        gp, w, d = g_ref[...].astype(F32), w_ref[...], d_ref[...].astype(F32)
        g = _conv(gp, w) + b_ref[...]
        sg = jax.nn.sigmoid(g)
        du_ref[...] = (d * (g * sg)).astype(du_ref.dtype)
        dg = d * u_ref[...].astype(F32) * (sg * (1.0 + g * (1.0 - sg)))
        db_ref[...] = jnp.sum(dg, axis=0, keepdims=True)
        dw_ref[...] = _conv_dw(dg, gp)
        dg_ref[...] = _conv_t(dg, w).astype(dg_ref.dtype)

    col = lambda j: (0, j)
    return _pcall(
        body, name=name, grid=(Fd // cb,),
        in_specs=[_vspec((L, cb), col)] * 3 + [_vspec((3, cb), col), _vspec((1, cb), col)],
        out_specs=[_vspec((L, cb), col), _vspec((L, cb), col), _vspec((3, cb), col), _vspec((1, cb), col)],
        out_shape=[jax.ShapeDtypeStruct((L, Fd), BF16), jax.ShapeDtypeStruct((L, Fd), BF16),
                   jax.ShapeDtypeStruct((3, Fd), F32), jax.ShapeDtypeStruct((1, Fd), F32)],
        sem=("parallel",), comm=comm,
    )(dact, gpre, up, cw, cbias)


def col_sum(x, *, name="col_sum"):
    R, C = x.shape
    tr = _tile8(R, 512)

    def body(x_ref, o_ref):
        @pl.when(pl.program_id(0) == 0)
        def _():
            o_ref[...] = jnp.zeros_like(o_ref)

        o_ref[...] += jnp.sum(x_ref[...].astype(F32), axis=0, keepdims=True)

    return pl.pallas_call(
        body, name=name, grid=(R // tr,),
        in_specs=[_vspec((tr, C), lambda i: (i, 0))], out_specs=_vspec((1, C), lambda i: (0, 0)),
        out_shape=jax.ShapeDtypeStruct((1, C), F32), compiler_params=_params(("arbitrary",)),
    )(x)


def _cmul_add(ar, ai, sr, si, br, bi):
    return ar * sr - ai * si + br, ar * si + ai * sr + bi


def _segment_starts(fin, pr, pi, reverse):
    H = fin.shape[1] // 2
    rows = lax.broadcasted_iota(jnp.int32, fin.shape, 0)
    cr = jnp.zeros((1, H), F32)
    ci = jnp.zeros((1, H), F32)
    out = jnp.zeros(fin.shape, F32)
    order = range(N_SEG - 1, -1, -1) if reverse else range(N_SEG)
    for k in order:
        out = jnp.where(rows == k, jnp.concatenate([cr, ci], axis=1), out)
        cr, ci = _cmul_add(pr, pi, cr, ci, fin[k:k + 1, :H], fin[k:k + 1, H:])
    return out


def _gelu(y):
    c0 = math.sqrt(2.0 / math.pi)
    t = jnp.tanh(c0 * (y + 0.044715 * y * y * y))
    return 0.5 * y * (1.0 + t), t


def s5_scan_fwd(u_p, a_l, apow_l, bm, cm, d_l, glu, fin, *, name, comm=()):
    L, C = u_p.shape
    NK, _, SW = bm.shape
    H = SW // 2
    RB = S5_STEPS * N_SEG
    NC = L // RB
    final_only = fin is None

    def scan_chunk(a_ref, buf, st):
        ar = jnp.broadcast_to(a_ref[:, :H], (N_SEG, H))
        ai = jnp.broadcast_to(a_ref[:, H:], (N_SEG, H))

        def step(j, carry):
            sr, si = carry
            rows = pl.ds(pl.multiple_of(j * N_SEG, N_SEG), N_SEG)
            sr, si = _cmul_add(ar, ai, sr, si, buf[rows, :H], buf[rows, H:])
            buf[rows, :H] = sr
            buf[rows, H:] = si
            return sr, si

        sr, si = lax.fori_loop(0, S5_STEPS, step, (st[:, :H], st[:, H:]), unroll=4)
        st[:, :H] = sr
        st[:, H:] = si

    if final_only:
        def body(u_ref, a_ref, bm_ref, fin_ref, buf, st):
            @pl.when(pl.program_id(1) == 0)
            def _():
                st[...] = jnp.zeros_like(st)

            buf[...] = jnp.dot(u_ref[...].astype(BF16), bm_ref[...], preferred_element_type=F32)
            scan_chunk(a_ref, buf, st)
            fin_ref[...] = st[...]

        return _pcall(
            body, name=name, grid=(NK, NC),
            in_specs=[_vspec((RB, LANES), lambda k, j: (j, k)), _vspec((None, 1, SW), lambda k, j: (k, 0, 0)),
                      _vspec((None, LANES, SW), lambda k, j: (k, 0, 0))],
            out_specs=_vspec((None, N_SEG, SW), lambda k, j: (k, 0, 0)),
            out_shape=jax.ShapeDtypeStruct((NK, N_SEG, SW), F32),
            scratch_shapes=[pltpu.VMEM((RB, SW), F32), pltpu.VMEM((N_SEG, SW), F32)],
            sem=("parallel", "arbitrary"), comm=comm,
        )(u_p, a_l, bm)

    def body(u_ref, a_ref, ap_ref, bm_ref, cm_ref, d_ref, glu_ref, fin_ref, o_ref, y_ref, s_ref, start_ref, buf, st):
        @pl.when(pl.program_id(1) == 0)
        def _():
            st[...] = _segment_starts(fin_ref[...], ap_ref[:, :H], ap_ref[:, H:], False)
            start_ref[...] = st[...]

        u = u_ref[...]
        buf[...] = jnp.dot(u.astype(BF16), bm_ref[...], preferred_element_type=F32)
        scan_chunk(a_ref, buf, st)
        states = buf[...]
        s_ref[...] = states
        y = jnp.dot(states.astype(BF16), cm_ref[...], preferred_element_type=F32) + d_ref[...] * u
        y_ref[...] = y
        yg, _ = _gelu(y)
        gate = jnp.dot(yg.astype(BF16), glu_ref[...], preferred_element_type=F32)
        o_ref[...] = (yg * jax.nn.sigmoid(gate)).astype(o_ref.dtype)

    blk = lambda k, j: (j, k)
    per_k = lambda k, j: (k, 0, 0)
    return _pcall(
        body, name=name, grid=(NK, NC),
        in_specs=[_vspec((RB, LANES), blk), _vspec((None, 1, SW), per_k), _vspec((None, 1, SW), per_k),
                  _vspec((None, LANES, SW), per_k), _vspec((None, SW, LANES), per_k), _vspec((1, LANES), lambda k, j: (0, k)),
                  _vspec((None, LANES, LANES), per_k), _vspec((None, N_SEG, SW), per_k)],
        out_specs=[_vspec((RB, LANES), blk), _vspec((RB, LANES), blk), _vspec((RB, SW), blk),
                   _vspec((None, N_SEG, SW), per_k)],
        out_shape=[jax.ShapeDtypeStruct((L, C), BF16), jax.ShapeDtypeStruct((L, C), F32),
                   jax.ShapeDtypeStruct((L, NK * SW), F32), jax.ShapeDtypeStruct((NK, N_SEG, SW), F32)],
        scratch_shapes=[pltpu.VMEM((RB, SW), F32), pltpu.VMEM((N_SEG, SW), F32)],
        sem=("parallel", "arbitrary"), comm=comm,
    )(u_p, a_l, apow_l, bm, cm, d_l, glu, fin)


def s5_out_bwd(dout_p, y_p, u_p, glu, *, name="s5_out_bwd", comm=()):
    L, C = y_p.shape
    NK = C // LANES
    tr = _tile8(L, 512)

    def body(do_ref, y_ref, u_ref, glu_ref, dy_ref, dglu_ref, dd_ref):
        @pl.when(pl.program_id(1) == 0)
        def _():
            dglu_ref[...] = jnp.zeros_like(dglu_ref)
            dd_ref[...] = jnp.zeros_like(dd_ref)

        y, do, w = y_ref[...], do_ref[...].astype(F32), glu_ref[...]
        yg, t = _gelu(y)
        sg = jax.nn.sigmoid(jnp.dot(yg.astype(BF16), w, preferred_element_type=F32))
        dgate = (do * yg * sg * (1.0 - sg)).astype(BF16)
        dyg = do * sg + lax.dot_general(dgate, w, (((1,), (1,)), ((), ())), preferred_element_type=F32)
        dglu_ref[...] += lax.dot_general(yg.astype(BF16), dgate, (((0,), (0,)), ((), ())), preferred_element_type=F32)
        c0 = math.sqrt(2.0 / math.pi)
        dgelu = 0.5 * (1.0 + t) + 0.5 * y * (1.0 - t * t) * c0 * (1.0 + 3.0 * 0.044715 * y * y)
        dy = dyg * dgelu
        dy_ref[...] = dy
        dd_ref[...] += jnp.sum(dy * u_ref[...], axis=0, keepdims=True)

    blk = lambda k, i: (i, k)
    return _pcall(
        body, name=name, grid=(NK, L // tr),
        in_specs=[_vspec((tr, LANES), blk)] * 3 + [_vspec((None, LANES, LANES), lambda k, i: (k, 0, 0))],
        out_specs=[_vspec((tr, LANES), blk), _vspec((None, LANES, LANES), lambda k, i: (k, 0, 0)),
                   _vspec((1, LANES), lambda k, i: (0, k))],
        out_shape=[jax.ShapeDtypeStruct((L, C), F32), jax.ShapeDtypeStruct((NK, LANES, LANES), F32),
                   jax.ShapeDtypeStruct((1, C), F32)],
        sem=("parallel", "arbitrary"), comm=comm,
    )(dout_p, y_p, u_p, glu)


def s5_scan_bwd(dy_p, a_l, apow_l, cmt, gfin, states=None, starts=None, u_p=None, bmt=None, d_l=None, *, name, comm=()):
    L, C = dy_p.shape
    NK, _, SW = cmt.shape
    H = SW // 2
    RB = S5_STEPS * N_SEG
    NC = L // RB
    final_only = gfin is None

    def scan_chunk(a_ref, buf, st):
        ar = jnp.broadcast_to(a_ref[:, :H], (N_SEG, H))
        ai = -jnp.broadcast_to(a_ref[:, H:], (N_SEG, H))

        def step(jj, carry):
            gr, gi = carry
            j = S5_STEPS - 1 - jj
            rows = pl.ds(pl.multiple_of(j * N_SEG, N_SEG), N_SEG)
            gr, gi = _cmul_add(ar, ai, gr, gi, buf[rows, :H], buf[rows, H:])
            buf[rows, :H] = gr
            buf[rows, H:] = gi
            return gr, gi

        gr, gi = lax.fori_loop(0, S5_STEPS, step, (st[:, :H], st[:, H:]), unroll=4)
        st[:, :H] = gr
        st[:, H:] = gi

    rblk = lambda k, j: (NC - 1 - j, k)
    per_k = lambda k, j: (k, 0, 0)

    if final_only:
        def body(dy_ref, a_ref, cmt_ref, fin_ref, buf, st):
            @pl.when(pl.program_id(1) == 0)
            def _():
                st[...] = jnp.zeros_like(st)

            buf[...] = jnp.dot(dy_ref[...].astype(BF16), cmt_ref[...], preferred_element_type=F32)
            scan_chunk(a_ref, buf, st)
            fin_ref[...] = st[...]

        return _pcall(
            body, name=name, grid=(NK, NC),
            in_specs=[_vspec((RB, LANES), rblk), _vspec((None, 1, SW), per_k), _vspec((None, LANES, SW), per_k)],
            out_specs=_vspec((None, N_SEG, SW), per_k),
            out_shape=jax.ShapeDtypeStruct((NK, N_SEG, SW), F32),
            scratch_shapes=[pltpu.VMEM((RB, SW), F32), pltpu.VMEM((N_SEG, SW), F32)],
            sem=("parallel", "arbitrary"), comm=comm,
        )(dy_p, a_l, cmt)

    def body(dy_ref, a_ref, ap_ref, cmt_ref, gfin_ref, s_ref, sprev_ref, start_ref, u_ref, bmt_ref, d_ref,
             du_ref, da_ref, dbm_ref, dcm_ref, buf, st):
        jc = pl.program_id(1)

        @pl.when(jc == 0)
        def _():
            st[...] = _segment_starts(gfin_ref[...], ap_ref[:, :H], -ap_ref[:, H:], True)
            da_ref[...] = jnp.zeros_like(da_ref)
            dbm_ref[...] = jnp.zeros_like(dbm_ref)
            dcm_ref[...] = jnp.zeros_like(dcm_ref)

        dy = dy_ref[...]
        dyb = dy.astype(BF16)
        buf[...] = jnp.dot(dyb, cmt_ref[...], preferred_element_type=F32)
        scan_chunk(a_ref, buf, st)
        g = buf[...]
        s = s_ref[...]
        first = jnp.where(jc == NC - 1, start_ref[...], sprev_ref[...])
        sp = jnp.concatenate([first, s[:RB - N_SEG, :]], axis=0)
        gr, gi, pr, pi = g[:, :H], g[:, H:], sp[:, :H], sp[:, H:]
        da_ref[...] += jnp.concatenate([jnp.sum(gr * pr + gi * pi, axis=0, keepdims=True),
                                        jnp.sum(gi * pr - gr * pi, axis=0, keepdims=True)], axis=1)
        gb = g.astype(BF16)
        u = u_ref[...]
        du_ref[...] = (jnp.dot(gb, bmt_ref[...], preferred_element_type=F32) + dy * d_ref[...]).astype(du_ref.dtype)
        dbm_ref[...] += lax.dot_general(u.astype(BF16), gb, (((0,), (0,)), ((), ())), preferred_element_type=F32)
        dcm_ref[...] += lax.dot_general(s.astype(BF16), dyb, (((0,), (0,)), ((), ())), preferred_element_type=F32)

    prev8 = lambda k, j: (jnp.maximum((NC - 1 - j) * S5_STEPS - 1, 0), k)
    return _pcall(
        body, name=name, grid=(NK, NC),
        in_specs=[_vspec((RB, LANES), rblk), _vspec((None, 1, SW), per_k), _vspec((None, 1, SW), per_k),
                  _vspec((None, LANES, SW), per_k), _vspec((None, N_SEG, SW), per_k), _vspec((RB, SW), rblk),
                  _vspec((N_SEG, SW), prev8), _vspec((None, N_SEG, SW), per_k), _vspec((RB, LANES), rblk),
                  _vspec((None, SW, LANES), per_k), _vspec((1, LANES), lambda k, j: (0, k))],
        out_specs=[_vspec((RB, LANES), rblk), _vspec((None, 1, SW), per_k), _vspec((None, LANES, SW), per_k),
                   _vspec((None, SW, LANES), per_k)],
        out_shape=[jax.ShapeDtypeStruct((L, C), BF16), jax.ShapeDtypeStruct((NK, 1, SW), F32),
                   jax.ShapeDtypeStruct((NK, LANES, SW), F32), jax.ShapeDtypeStruct((NK, SW, LANES), F32)],
        scratch_shapes=[pltpu.VMEM((RB, SW), F32), pltpu.VMEM((N_SEG, SW), F32)],
        sem=("parallel", "arbitrary"), comm=comm,
    )(dy_p, a_l, apow_l, cmt, gfin, states, states, starts, u_p, bmt, d_l)


def _s5_prep(a_re, a_im, log_dt, b_re, b_im, c_re, c_im, d, glu_w, seg_len):
    G, P = a_re.shape
    Hc = b_re.shape[-1]
    gl = LANES // Hc
    nk = G // gl
    dt = jnp.exp(log_dt)[:, None]
    er = jnp.exp(a_re * dt)
    ab_r, ab_i = er * jnp.cos(a_im * dt), er * jnp.sin(a_im * dt)
    den = a_re * a_re + a_im * a_im
    nr, ni = ab_r - 1.0, ab_i
    q_r, q_i = (nr * a_re + ni * a_im) / den, (ni * a_re - nr * a_im) / den
    bb_r = q_r[..., None] * b_re - q_i[..., None] * b_im
    bb_i = q_r[..., None] * b_im + q_i[..., None] * b_re
    ep = jnp.exp(a_re * dt * seg_len)
    ap_r, ap_i = ep * jnp.cos(a_im * dt * seg_len), ep * jnp.sin(a_im * dt * seg_len)
    eye = jnp.eye(gl, dtype=F32)

    def lanes(t):
        return t.reshape(nk, 1, gl * P)

    def b_mat(t):
        return jnp.einsum("kgph,gq->kghqp", t.reshape(nk, gl, P, Hc), eye).reshape(nk, gl * Hc, gl * P)

    def c_mat(t):
        return jnp.einsum("kghp,gq->kgpqh", t.reshape(nk, gl, Hc, P), eye).reshape(nk, gl * P, gl * Hc)

    a_l = jnp.concatenate([lanes(ab_r), lanes(ab_i)], axis=-1)
    apow_l = jnp.concatenate([lanes(ap_r), lanes(ap_i)], axis=-1)
    bm = jnp.concatenate([b_mat(bb_r), b_mat(bb_i)], axis=-1)
    cm = jnp.concatenate([c_mat(c_re), -c_mat(c_im)], axis=1)
    glu = jnp.einsum("kgho,gq->kghqo", glu_w.reshape(nk, gl, Hc, Hc), eye).reshape(nk, gl * Hc, gl * Hc)
    return a_l, apow_l, bm, cm, d.reshape(1, G * Hc), glu


def _to_segments(t):
    L, C = t.shape
    return t.reshape(N_SEG, L // N_SEG, C).transpose(1, 0, 2).reshape(L, C)


def _from_segments(t):
    L, C = t.shape
    return t.reshape(L // N_SEG, N_SEG, C).transpose(1, 0, 2).reshape(L, C)


def _swa_probs(q_ref, kp_ref, kc_ref, bias_ref, sink_ref):
    QB = WINDOW
    rows = Q_PER_KV * QB
    q = q_ref[...].reshape(rows, HEAD_DIM)
    kk = jnp.concatenate([kp_ref[...], kc_ref[...]], axis=0)
    s = lax.dot_general(q, kk, (((1,), (1,)), ((), ())), preferred_element_type=F32) * (HEAD_DIM ** -0.5)
    s = s + bias_ref[...].reshape(rows, 2 * QB)
    sink = sink_ref[...]
    m = jnp.maximum(jnp.max(s, axis=1, keepdims=True), sink)
    e = jnp.exp(s - m)
    es = jnp.exp(sink - m)
    inv = 1.0 / (jnp.sum(e, axis=1, keepdims=True) + es)
    return q, kk, e * inv, es * inv


def _swa_specs(nq):
    qs = _vspec((Q_PER_KV, WINDOW, HEAD_DIM), lambda g, n: (g, n, 0))
    kprev = _vspec((None, WINDOW, HEAD_DIM), lambda g, n: (g, jnp.maximum(n - 1, 0), 0))
    kcur = _vspec((None, WINDOW, HEAD_DIM), lambda g, n: (g, n, 0))
    bias = _vspec((None, Q_PER_KV, WINDOW, 2 * WINDOW), lambda g, n: (jnp.minimum(n, 1), g, 0, 0))
    dbias = _vspec((Q_PER_KV, WINDOW, 2 * WINDOW), lambda g, n: (g, 0, 0))
    sink = _vspec((Q_PER_KV * WINDOW, 1), lambda g, n: (g, 0))
    return qs, kprev, kcur, bias, dbias, sink


def _masked_bias(bias_tab):
    qi = np.arange(WINDOW)[:, None]
    kj = np.arange(2 * WINDOW)[None, :]
    valid = ((kj < WINDOW) & (kj > qi)) | ((kj >= WINDOW) & (kj - WINDOW <= qi))
    first = valid & (kj >= WINDOW)
    return jnp.stack([jnp.where(first[None], bias_tab, NEG_INF), jnp.where(valid[None], bias_tab, NEG_INF)])


def swa_fwd(qT, kT, vT, bias2, sink_col, *, name="swa_fwd", comm=()):
    NQ, L, _ = qT.shape
    NKV = kT.shape[0]
    qs, kprev, kcur, bs, _, sk = _swa_specs(NQ)

    def body(q_ref, kp_ref, kc_ref, vp_ref, vc_ref, bias_ref, sink_ref, o_ref):
        _, _, p, _ = _swa_probs(q_ref, kp_ref, kc_ref, bias_ref, sink_ref)
        vv = jnp.concatenate([vp_ref[...], vc_ref[...]], axis=0)
        o = jnp.dot(p.astype(BF16), vv, preferred_element_type=F32)
        o_ref[...] = o.reshape(Q_PER_KV, WINDOW, HEAD_DIM).astype(o_ref.dtype)

    return _pcall(
        body, name=name, grid=(NKV, L // WINDOW),
        in_specs=[qs, kprev, kcur, kprev, kcur, bs, sk], out_specs=qs,
        out_shape=jax.ShapeDtypeStruct((NQ, L, HEAD_DIM), BF16),
        sem=("parallel", "arbitrary"), comm=comm,
    )(qT, kT, kT, vT, vT, bias2, sink_col)


def swa_bwd(qT, kT, vT, bias2, sink_col, doT, *, name="swa_bwd", comm=()):
    NQ, L, _ = qT.shape
    NKV = kT.shape[0]
    qs, kprev, kcur, bs, dbs, sk = _swa_specs(NQ)
    W = WINDOW

    def body(q_ref, kp_ref, kc_ref, vp_ref, vc_ref, bias_ref, sink_ref, do_ref,
             dq_ref, dk_ref, dv_ref, dbias_ref, dsink_ref):
        n = pl.program_id(1)

        @pl.when(n == 0)
        def _():
            dk_ref[...] = jnp.zeros_like(dk_ref)
            dv_ref[...] = jnp.zeros_like(dv_ref)
            dbias_ref[...] = jnp.zeros_like(dbias_ref)
            dsink_ref[...] = jnp.zeros_like(dsink_ref)

        q, kk, p, ps = _swa_probs(q_ref, kp_ref, kc_ref, bias_ref, sink_ref)
        vv = jnp.concatenate([vp_ref[...], vc_ref[...]], axis=0)
        do = do_ref[...].reshape(Q_PER_KV * W, HEAD_DIM)
        dp = lax.dot_general(do, vv, (((1,), (1,)), ((), ())), preferred_element_type=F32)
        delta = jnp.sum(p * dp, axis=1, keepdims=True)
        ds = p * (dp - delta)
        dsink_ref[...] += -ps * delta
        dbias_ref[...] += ds.reshape(Q_PER_KV, W, 2 * W)
        dsb = ds.astype(BF16)
        scale = HEAD_DIM ** -0.5
        dq = jnp.dot(dsb, kk, preferred_element_type=F32) * scale
        dq_ref[...] = dq.reshape(Q_PER_KV, W, HEAD_DIM).astype(dq_ref.dtype)
        dkk = lax.dot_general(dsb, q, (((0,), (0,)), ((), ())), preferred_element_type=F32) * scale
        dvv = lax.dot_general(p.astype(BF16), do, (((0,), (0,)), ((), ())), preferred_element_type=F32)

        @pl.when(n == 0)
        def _():
            dk_ref[0:W, :] += dkk[W:, :]
            dv_ref[0:W, :] += dvv[W:, :]

        @pl.when(n > 0)
        def _():
            rows = pl.ds(pl.multiple_of((n - 1) * W, W), 2 * W)
            dk_ref[rows, :] += dkk
            dv_ref[rows, :] += dvv

    whole = _vspec((None, L, HEAD_DIM), lambda g, n: (g, 0, 0))
    return _pcall(
        body, name=name, grid=(NKV, L // W),
        in_specs=[qs, kprev, kcur, kprev, kcur, bs, sk, qs],
        out_specs=[qs, whole, whole, dbs, sk],
        out_shape=[jax.ShapeDtypeStruct((NQ, L, HEAD_DIM), BF16), jax.ShapeDtypeStruct((NKV, L, HEAD_DIM), F32),
                   jax.ShapeDtypeStruct((NKV, L, HEAD_DIM), F32), jax.ShapeDtypeStruct((NQ, W, 2 * W), F32),
                   jax.ShapeDtypeStruct((NQ * W, 1), F32)],
        sem=("parallel", "arbitrary"), comm=comm,
    )(qT, kT, kT, vT, vT, bias2, sink_col, doT)


def _bucket_table():
    qi = np.arange(WINDOW)[:, None]
    kj = np.arange(2 * WINDOW)[None, :]
    rel = qi + WINDOW - kj
    max_exact = N_BUCKETS // 2
    n = np.maximum(rel, 0)
    nf = np.maximum(n, max_exact).astype(np.float32)
    large = max_exact + (np.log(nf / max_exact) / math.log(MAX_DISTANCE / max_exact) * (N_BUCKETS - max_exact)).astype(np.int32)
    large = np.minimum(large, N_BUCKETS - 1)
    return np.where(n < max_exact, n, large).astype(np.int32).reshape(-1)


def _xa_probs(q, k):
    hd = q.shape[1]
    s = lax.dot_general(q, k, (((1,), (1,)), ((), ())), preferred_element_type=F32) * (hd ** -0.5)
    e = jnp.exp(s - jnp.max(s, axis=1, keepdims=True))
    return e / jnp.sum(e, axis=1, keepdims=True)


def xattn_fwd(q, kv, *, name="xattn_fwd"):
    L, D = q.shape
    Mm = kv.shape[0]
    hd = D // X_HEADS
    tq = _tile8(L, 512)

    def body(q_ref, kv_ref, o_ref):
        for h in range(X_HEADS):
            cols = slice(h * hd, (h + 1) * hd)
            p = _xa_probs(q_ref[:, cols], kv_ref[:, cols])
            o_ref[:, cols] = jnp.dot(p.astype(BF16), kv_ref[:, D + h * hd:D + (h + 1) * hd],
                                     preferred_element_type=F32).astype(o_ref.dtype)

    return pl.pallas_call(
        body, name=name, grid=(L // tq,),
        in_specs=[_vspec((tq, D), lambda i: (i, 0)), _vspec((Mm, 2 * D), lambda i: (0, 0))],
        out_specs=_vspec((tq, D), lambda i: (i, 0)),
        out_shape=jax.ShapeDtypeStruct((L, D), BF16),
        compiler_params=_params(("parallel",)),
    )(q, kv)


def xattn_bwd(q, kv, do, *, name="xattn_bwd"):
    L, D = q.shape
    Mm = kv.shape[0]
    hd = D // X_HEADS
    tq = _tile8(L, 512)

    def body(q_ref, kv_ref, do_ref, dq_ref, dkv_ref):
        @pl.when(pl.program_id(0) == 0)
        def _():
            dkv_ref[...] = jnp.zeros_like(dkv_ref)

        for h in range(X_HEADS):
            cols = slice(h * hd, (h + 1) * hd)
            vcols = slice(D + h * hd, D + (h + 1) * hd)
            qh, kh, vh, doh = q_ref[:, cols], kv_ref[:, cols], kv_ref[:, vcols], do_ref[:, cols]
            p = _xa_probs(qh, kh)
            dp = lax.dot_general(doh, vh, (((1,), (1,)), ((), ())), preferred_element_type=F32)
            ds = (p * (dp - jnp.sum(p * dp, axis=1, keepdims=True)) * (hd ** -0.5)).astype(BF16)
            dq_ref[:, cols] = jnp.dot(ds, kh, preferred_element_type=F32).astype(dq_ref.dtype)
            dkv_ref[:, cols] += lax.dot_general(ds, qh, (((0,), (0,)), ((), ())), preferred_element_type=F32)
            dkv_ref[:, vcols] += lax.dot_general(p.astype(BF16), doh, (((0,), (0,)), ((), ())), preferred_element_type=F32)

    return pl.pallas_call(
        body, name=name, grid=(L // tq,),
        in_specs=[_vspec((tq, D), lambda i: (i, 0)), _vspec((Mm, 2 * D), lambda i: (0, 0)), _vspec((tq, D), lambda i: (i, 0))],
        out_specs=[_vspec((tq, D), lambda i: (i, 0)), _vspec((Mm, 2 * D), lambda i: (0, 0))],
        out_shape=[jax.ShapeDtypeStruct((L, D), BF16), jax.ShapeDtypeStruct((Mm, 2 * D), F32)],
        compiler_params=_params(("arbitrary",)),
    )(q, kv, do)


def adamw(w, g, m, v, *, name="adamw"):
    R, C = w.shape
    tr = _tile8(R, max(SUBLANES, (256 * 1024) // C // SUBLANES * SUBLANES))

    def body(w_ref, g_ref, m_ref, v_ref, d_ref, nm_ref, nv_ref):
        g_ = g_ref[...]
        nm = ADAM_B1 * m_ref[...] + (1.0 - ADAM_B1) * g_
        nv = ADAM_B2 * v_ref[...] + (1.0 - ADAM_B2) * (g_ * g_)
        m_hat = nm / (1.0 - ADAM_B1 ** ADAM_STEP)
        v_hat = nv / (1.0 - ADAM_B2 ** ADAM_STEP)
        d_ref[...] = -ADAM_LR * (m_hat / (jnp.sqrt(v_hat) + ADAM_EPS) + ADAM_WD * w_ref[...])
        nm_ref[...] = nm
        nv_ref[...] = nv

    spec = _vspec((tr, C), lambda i: (i, 0))
    return pl.pallas_call(
        body, name=name, grid=(R // tr,), in_specs=[spec] * 4, out_specs=[spec] * 3,
        out_shape=[jax.ShapeDtypeStruct((R, C), F32)] * 3, compiler_params=_params(("parallel",)),
    )(w, g, m, v)


def _place():
    x, y, c = lax.axis_index("x"), lax.axis_index("y"), lax.axis_index("c")
    chips = [(1 - x, y), (x, 1 - y), (1 - x, 1 - y)]
    return x, y, c, chips


def _remote(src, dst, send, recv, k, to):
    return pltpu.make_async_remote_copy(src_ref=src, dst_ref=dst, send_sem=send.at[k], recv_sem=recv.at[k],
                                        device_id=to, device_id_type=MESH)


class _Job:
    result = None

    def start(self, ins, outs, send, recv, base):
        for cp in self.copies(ins, outs, send, recv, base)[0]:
            cp.start()


class GatherJob(_Job):
    nsem = 7

    def __init__(self, w):
        self.ops = [w]
        self.outs = [jax.ShapeDtypeStruct((N_CHIPS,) + w.shape, w.dtype)]
        self.cost = N_CHIPS * w.size

    def copies(self, ins, outs, send, recv, base, first_only=True):
        w, out = ins[0], outs[0]
        x, y, c, chips = _place()
        me, sib = 2 * x + y, (x, y, 1 - c)
        h = w.shape[0] // 2
        mine, theirs = pl.ds(c * h, h), pl.ds((1 - c) * h, h)
        first = [_remote(w.at[mine], out.at[me, mine], send, recv, base + j, (px, py, c))
                 for j, (px, py) in enumerate(chips)]
        first.append(_remote(w, out.at[me], send, recv, base + 6, sib))
        if first_only:
            return first,
        landed = [out.at[2 * px + py, mine] for px, py in chips]
        lands = [_remote(w.at[mine], landed[j], send, recv, base + j, (px, py, c)) for j, (px, py) in enumerate(chips)]
        passes = [_remote(landed[j], landed[j], send, recv, base + 3 + j, sib) for j in range(3)]
        arrives = [_remote(w.at[theirs], out.at[2 * px + py, theirs], send, recv, base + 3 + j, sib)
                   for j, (px, py) in enumerate(chips)]
        return first, lands, passes, arrives

    def finish(self, ins, outs, send, recv, base):
        first, lands, passes, arrives = self.copies(ins, outs, send, recv, base, first_only=False)
        for land, fwd in zip(lands, passes):
            land.wait_recv()
            fwd.start()
        first[3].wait_recv()
        for cp in arrives:
            cp.wait_recv()
        for cp in first + passes:
            cp.wait_send()


class ExchangeJob(_Job):
    nsem = 1

    def __init__(self, g):
        self.ops = [g]
        self.outs = [jax.ShapeDtypeStruct((g.shape[0], g.shape[1] // 2, g.shape[2]), g.dtype)]
        self.cost = 0.15 * g.size

    def copies(self, ins, outs, send, recv, base):
        x, y, c, _ = _place()
        r2 = ins[0].shape[1] // 2
        return [_remote(ins[0].at[:, pl.ds((1 - c) * r2, r2)], outs[0], send, recv, base, (x, y, 1 - c))],

    def finish(self, ins, outs, send, recv, base):
        self.copies(ins, outs, send, recv, base)[0][0].wait()


class ScatterJob(_Job):
    nsem = 3

    def __init__(self, p):
        self.ops = [p]
        self.outs = [jax.ShapeDtypeStruct((N_CHIPS - 1,) + p.shape[1:], p.dtype)]
        self.cost = 2 * p.size

    def copies(self, ins, outs, send, recv, base):
        x, y, c, chips = _place()
        return [_remote(ins[0].at[2 * px + py], outs[0].at[j], send, recv, base + j, (px, py, c))
                for j, (px, py) in enumerate(chips)],

    def finish(self, ins, outs, send, recv, base):
        for cp in self.copies(ins, outs, send, recv, base)[0]:
            cp.wait()


def run_comm(jobs, *, name):
    cops = [a for j in jobs for a in j.ops]
    couts = [s for j in jobs for s in j.outs]
    nsem = sum(j.nsem for j in jobs)

    def body(*refs):
        cin, cout = refs[:len(cops)], refs[len(cops):len(cops) + len(couts)]
        send, recv = refs[len(cops) + len(couts):]
        for phase in ("start", "finish"):
            ii = io = base = 0
            for j in jobs:
                getattr(j, phase)(cin[ii:ii + len(j.ops)], cout[io:io + len(j.outs)], send, recv, base)
                ii, io, base = ii + len(j.ops), io + len(j.outs), base + j.nsem

    res = pl.pallas_call(
        body, name=name, in_specs=[ANY] * len(cops), out_specs=[ANY] * len(couts), out_shape=couts,
        scratch_shapes=[pltpu.SemaphoreType.DMA((nsem,)), pltpu.SemaphoreType.DMA((nsem,))],
        compiler_params=pltpu.CompilerParams(has_side_effects=True),
    )(*cops)
    io = 0
    for j in jobs:
        j.result = list(res[io:io + len(j.outs)])
        io += len(j.outs)


def add_half(g, other, c_idx, *, name="rs_add_half"):
    S, R, C = g.shape
    r2 = R // 2
    tr = _tile8(r2, max(SUBLANES, (512 * 1024) // C // SUBLANES * SUBLANES))
    nb = r2 // tr

    def body(c_ref, g_ref, o_ref, out_ref):
        out_ref[...] = (g_ref[...].astype(F32) + o_ref[...].astype(F32)).astype(out_ref.dtype)

    return pl.pallas_call(
        body, name=name,
        grid_spec=pltpu.PrefetchScalarGridSpec(
            num_scalar_prefetch=1, grid=(S, nb),
            in_specs=[pl.BlockSpec((None, tr, C), lambda s, i, c_ref: (s, c_ref[0] * nb + i, 0)),
                      pl.BlockSpec((None, tr, C), lambda s, i, c_ref: (s, i, 0))],
            out_specs=pl.BlockSpec((None, tr, C), lambda s, i, c_ref: (s, i, 0))),
        out_shape=jax.ShapeDtypeStruct((S, r2, C), BF16),
        compiler_params=_params(("parallel", "parallel")),
    )(c_idx, g, other)


def add_partials(p, got, place_idx, gbuf, layer, *, name="rs_add_partials"):
    S, r2, C = p.shape
    tr = _tile8(r2, max(SUBLANES, (512 * 1024) // C // SUBLANES * SUBLANES))
    nb = r2 // tr

    def body(pi_ref, p_ref, g_ref, buf_ref, out_ref):
        out_ref[...] = ((p_ref[...].astype(F32) + g_ref[0].astype(F32)) + g_ref[1].astype(F32)) + g_ref[2].astype(F32)

    return pl.pallas_call(
        body, name=name,
        grid_spec=pltpu.PrefetchScalarGridSpec(
            num_scalar_prefetch=1, grid=(nb,),
            in_specs=[pl.BlockSpec((None, tr, C), lambda i, pi: (pi[0], i, 0)),
                      pl.BlockSpec((N_CHIPS - 1, tr, C), lambda i, pi: (0, i, 0)),
                      ANY],
            out_specs=pl.BlockSpec((None, tr, C), lambda i, pi: (layer, pi[1] * nb + i, 0))),
        out_shape=jax.ShapeDtypeStruct(gbuf.shape, F32),
        input_output_aliases={3: 0},
        compiler_params=_params(("parallel",)),
    )(place_idx, p, got, gbuf)


def join_halves(gbufs, *, name="rs_join_halves"):
    n = len(gbufs)

    def body(*refs):
        outs = refs[n:2 * n]
        send, recv = refs[2 * n:]
        x, y, c, _ = _place()
        sib = (x, y, 1 - c)
        cps = []
        for i in range(n):
            r2 = outs[i].shape[1] // 2
            mine = outs[i].at[:, pl.ds(c * r2, r2)]
            cp = _remote(mine, mine, send, recv, i, sib)
            cp.start()
            cps.append(cp)
        for i in range(n):
            r2 = outs[i].shape[1] // 2
            theirs = outs[i].at[:, pl.ds((1 - c) * r2, r2)]
            _remote(theirs, theirs, send, recv, i, sib).wait_recv()
        for cp in cps:
            cp.wait_send()

    return pl.pallas_call(
        body, name=name, in_specs=[ANY] * n, out_specs=[ANY] * n,
        out_shape=[jax.ShapeDtypeStruct(g.shape, F32) for g in gbufs],
        input_output_aliases={i: i for i in range(n)},
        scratch_shapes=[pltpu.SemaphoreType.DMA((n,)), pltpu.SemaphoreType.DMA((n,))],
        compiler_params=pltpu.CompilerParams(has_side_effects=True),
    )(*gbufs)


def all_gather_rows(v, *, name="all_gather_small"):
    m, ncol = v.shape

    def body(x_ref, out_ref, send, recv, lsem):
        x, y, c, chips = _place()
        me, sib = (x, y, c), (x, y, 1 - c)

        def rows(px, py, pc):
            return out_ref.at[pl.ds((4 * px + 2 * py + pc) * m, m), :]

        def copy(k, block, to, src=None):
            return _remote(rows(*block) if src is None else src, rows(*block), send, recv, k, to)

        mine = pltpu.make_async_copy(x_ref, rows(*me), lsem)
        mine.start()
        first = [copy(0, me, sib, src=x_ref)]
        first += [copy(1 + j, me, (*chip, c), src=x_ref) for j, chip in enumerate(chips)]
        for cp in first:
            cp.start()
        passed = [copy(4 + j, (*chip, c), sib) for j, chip in enumerate(chips)]
        for j, chip in enumerate(chips):
            copy(1 + j, (*chip, c), me).wait_recv()
            passed[j].start()
        copy(0, sib, me).wait_recv()
        for j, chip in enumerate(chips):
            copy(4 + j, (*chip, 1 - c), me).wait_recv()
        for cp in first + passed:
            cp.wait_send()
        mine.wait()

    return pl.pallas_call(
        body, name=name,
        in_specs=[pl.BlockSpec(memory_space=pltpu.VMEM)], out_specs=pl.BlockSpec(memory_space=pltpu.VMEM),
        out_shape=jax.ShapeDtypeStruct((8 * m, ncol), v.dtype),
        scratch_shapes=[pltpu.SemaphoreType.DMA((7,)), pltpu.SemaphoreType.DMA((7,)), pltpu.SemaphoreType.DMA],
        compiler_params=pltpu.CompilerParams(vmem_limit_bytes=VMEM_LIMIT_BYTES, has_side_effects=True),
    )(v)


def sum_blocks(g8, *, name="sum_blocks"):
    nb, m, ncol = g8.shape
    tr = _tile8(m, 512)

    def body(g_ref, o_ref):
        acc = g_ref[0]
        for k in range(1, nb):
            acc = acc + g_ref[k]
        o_ref[...] = acc

    return pl.pallas_call(
        body, name=name, grid=(m // tr,),
        in_specs=[_vspec((nb, tr, ncol), lambda i: (0, i, 0))], out_specs=_vspec((tr, ncol), lambda i: (i, 0)),
        out_shape=jax.ShapeDtypeStruct((m, ncol), F32), compiler_params=_params(("parallel",)),
    )(g8)


PACK_ROWS = 256


def _pack(arrs, mult):
    flat = jnp.concatenate([a.reshape(-1) for a in arrs])
    pad = (-flat.shape[0]) % (mult * LANES)
    return jnp.pad(flat, (0, pad)).reshape(-1, LANES)


def _unpack(packed, like):
    flat = packed.reshape(-1)
    out, off = [], 0
    for a in like:
        out.append(flat[off:off + a.size].reshape(a.shape))
        off += a.size
    return out


def _rows2d(a):
    return a.reshape(-1, a.shape[-1])


def kernel(x, mem, norm_mix, norm_xattn, norm_ffn, norm_final, norm_mem, rel_bias, ev_w_in, ev_conv_w, s5_a_re, s5_a_im, s5_log_dt, s5_b_re, s5_b_im, s5_c_re, s5_c_im, s5_d, s5_glu_w, ev_w_out, od_w_qkv, od_b_qkv, od_sinks, od_w_out, xa_w_q, xa_w_kv, xa_w_o, ff_w_gate, ff_w_up, ff_conv_w, ff_conv_b, ff_w_down, loss_target, m_norm_mix, m_norm_xattn, m_norm_ffn, m_norm_final, m_norm_mem, m_rel_bias, m_ev_w_in, m_ev_conv_w, m_s5_a_re, m_s5_a_im, m_s5_log_dt, m_s5_b_re, m_s5_b_im, m_s5_c_re, m_s5_c_im, m_s5_d, m_s5_glu_w, m_ev_w_out, m_od_w_qkv, m_od_b_qkv, m_od_sinks, m_od_w_out, m_xa_w_q, m_xa_w_kv, m_xa_w_o, m_ff_w_gate, m_ff_w_up, m_ff_conv_w, m_ff_conv_b, m_ff_w_down, v_norm_mix, v_norm_xattn, v_norm_ffn, v_norm_final, v_norm_mem, v_rel_bias, v_ev_w_in, v_ev_conv_w, v_s5_a_re, v_s5_a_im, v_s5_log_dt, v_s5_b_re, v_s5_b_im, v_s5_c_re, v_s5_c_im, v_s5_d, v_s5_glu_w, v_ev_w_out, v_od_w_qkv, v_od_b_qkv, v_od_sinks, v_od_w_out, v_xa_w_q, v_xa_w_kv, v_xa_w_o, v_ff_w_gate, v_ff_w_up, v_ff_conv_w, v_ff_conv_b, v_ff_w_down):
    names = ["norm_mix", "norm_xattn", "norm_ffn", "norm_final", "norm_mem", "rel_bias", "ev_w_in", "ev_conv_w",
             "s5_a_re", "s5_a_im", "s5_log_dt", "s5_b_re", "s5_b_im", "s5_c_re", "s5_c_im", "s5_d", "s5_glu_w",
             "ev_w_out", "od_w_qkv", "od_b_qkv", "od_sinks", "od_w_out", "xa_w_q", "xa_w_kv", "xa_w_o",
             "ff_w_gate", "ff_w_up", "ff_conv_w", "ff_conv_b", "ff_w_down"]
    env = dict(locals())
    W = {k: env[k] for k in names}
    Mo = {k: env["m_" + k] for k in names}
    Vo = {k: env["v_" + k] for k in names}

    h = x[0]
    target = loss_target[0]
    L, D = h.shape
    depth = norm_mix.shape[0]
    c_idx = lax.axis_index("c").astype(jnp.int32).reshape(1)
    me_idx = (2 * lax.axis_index("x") + lax.axis_index("y")).astype(jnp.int32).reshape(1)

    col_sharded = ["ev_w_in", "od_w_qkv", "xa_w_kv", "ff_w_gate", "ff_w_up"]
    row_sharded = ["ev_w_out", "od_w_out", "xa_w_q", "xa_w_o", "ff_w_down"]
    small_sharded = ["ev_conv_w", "od_b_qkv", "ff_conv_w"]
    big = col_sharded + row_sharded

    def layer_weights(l):
        mix = [("ev_w_in", l // 2), ("ev_w_out", l // 2)] if l % 2 == 0 else [("od_w_qkv", l // 2), ("od_w_out", l // 2)]
        return mix + [(k, l) for k in ("xa_w_q", "xa_w_kv", "xa_w_o", "ff_w_gate", "ff_w_up", "ff_w_down")]

    gjob = {kl: GatherJob(W[kl[0]][kl[1]].astype(BF16)) for l in range(depth) for kl in layer_weights(l)}
    small_jobs = [GatherJob(W[k]) for k in small_sharded]
    pending = [gjob[kl] for l in range(depth) for kl in layer_weights(l)]
    run_comm([pending.pop(0)] + small_jobs, name="gather_first")
    flushes = []

    def take(queue, host_cost):
        jobs, acc = [], 0.0
        while queue and acc + queue[0].cost <= 1.25 * host_cost:
            acc += queue[0].cost
            jobs.append(queue.pop(0))
        return jobs

    def fwd_host(fn, units, *args, **kw):
        return fn(*args, comm=take(pending, units), **kw)

    def weight(k, l):
        job = gjob[(k, l)]
        if job.result is None:
            n = pending.index(job) + 1
            run_comm(pending[:n], name="gather_flush_%d" % len(flushes))
            flushes.append(n)
            del pending[:n]
        return job.result[0]

    def wcol(k, l):
        return weight(k, l)

    def wrow(k, l):
        g = weight(k, l)
        return g.reshape(1, g.shape[0] * g.shape[1], g.shape[2])

    def fwd_mm(a, w3, **kw):
        return mm_nn(a, w3, comm=take(pending, w3.size * a.shape[0] / L), **kw)

    sg = [j.result[0] for j in small_jobs]
    ev_conv_w_f = sg[0].transpose(1, 2, 0, 3).reshape(ev_conv_w.shape[0], 3, -1)
    od_b_qkv_f = sg[1].transpose(1, 0, 2).reshape(od_b_qkv.shape[0], 1, -1)
    ff_conv_w_f = sg[2].transpose(1, 2, 0, 3).reshape(ff_conv_w.shape[0], 3, -1)

    buckets = _bucket_table()
    NQ = D // HEAD_DIM
    NKV = NQ // Q_PER_KV
    onehot = jnp.asarray((buckets[:, None] == np.arange(N_BUCKETS)[None, :]).astype(np.float32))
    bias_tab = jnp.dot(rel_bias.T, onehot.T, precision=lax.Precision.HIGHEST).reshape(NQ, WINDOW, 2 * WINDOW)
    bias2 = _masked_bias(bias_tab)

    mem_n = rms_fwd(mem[0], norm_mem.reshape(1, D), name="rms_fwd_mem")

    saved = []
    for l in range(depth):
        i = l // 2
        s = {"h0": h}
        hn = rms_fwd(h, norm_mix[l].reshape(1, D))
        s["hn"] = hn
        if l % 2 == 0:
            A = ev_conv_w_f.shape[-1]
            z = fwd_mm(hn, wcol("ev_w_in", i), name="mm_ev_in")
            ya = conv_mixer_fwd(z, ev_conv_w_f[i])
            prep = functools.partial(_s5_prep, seg_len=L // N_SEG)
            s5p = (s5_a_re[i], s5_a_im[i], s5_log_dt[i], s5_b_re[i], s5_b_im[i], s5_c_re[i], s5_c_im[i], s5_d[i], s5_glu_w[i])
            (a_l, apow_l, bm, cm, d_l, glu), prep_vjp = jax.vjp(prep, *s5p)
            bm16, cm16, glu16 = bm.astype(BF16), cm.astype(BF16), glu.astype(BF16)
            u_p = _to_segments(z[:, 3 * A:])
            fin = fwd_host(s5_scan_fwd, HOST_UNITS["s5_ends"] * u_p.size,
                           u_p, a_l, apow_l, bm16, cm16, d_l, glu16, None, name="s5_fwd_ends")
            ys_p, y_p, states, starts = fwd_host(s5_scan_fwd, HOST_UNITS["s5_fwd"] * u_p.size,
                                                 u_p, a_l, apow_l, bm16, cm16, d_l, glu16, fin, name="s5_fwd")
            ycat = jnp.concatenate([ya, _from_segments(ys_p)], axis=1)
            s.update(z=z, u_p=u_p, y_p=y_p, states=states, starts=starts, ycat=ycat, prep_vjp=prep_vjp,
                     s5ops=(a_l, apow_l, bm16, cm16, d_l, glu16))
            h = fwd_mm(ycat, wrow("ev_w_out", i), res=h, name="mm_ev_out")
        else:
            z = fwd_mm(hn, wcol("od_w_qkv", i), bias=od_b_qkv_f[i], out_dtype=BF16, name="mm_od_qkv")
            qT = z[:, :NQ * HEAD_DIM].reshape(L, NQ, HEAD_DIM).transpose(1, 0, 2)
            kT = z[:, NQ * HEAD_DIM:(NQ + NKV) * HEAD_DIM].reshape(L, NKV, HEAD_DIM).transpose(1, 0, 2)
            vT = z[:, (NQ + NKV) * HEAD_DIM:].reshape(L, NKV, HEAD_DIM).transpose(1, 0, 2)
            sink_col = jnp.repeat(od_sinks[i], WINDOW).reshape(NQ * WINDOW, 1)
            oT = fwd_host(swa_fwd, HOST_UNITS["swa_fwd"] * qT.size, qT, kT, vT, bias2, sink_col)
            o = oT.transpose(1, 0, 2).reshape(L, D)
            s.update(qT=qT, kT=kT, vT=vT, sink_col=sink_col, o=o)
            h = fwd_mm(o, wrow("od_w_out", i), res=h, name="mm_od_out")
        s["h1"] = h
        hn2 = rms_fwd(h, norm_xattn[l].reshape(1, D))
        q = fwd_mm(hn2, wrow("xa_w_q", l), out_dtype=BF16, name="mm_xa_q")
        kv = fwd_mm(mem_n, wcol("xa_w_kv", l), out_dtype=BF16, name="mm_xa_kv")
        ox = xattn_fwd(q, kv)
        s.update(hn2=hn2, q=q, kv=kv, ox=ox)
        h = fwd_mm(ox, wrow("xa_w_o", l), res=h, name="mm_xa_o")
        s["h2"] = h
        hn3 = rms_fwd(h, norm_ffn[l].reshape(1, D))
        gpre = fwd_mm(hn3, wcol("ff_w_gate", l), out_dtype=BF16, name="mm_ff_gate")
        up = fwd_mm(hn3, wcol("ff_w_up", l), out_dtype=BF16, name="mm_ff_up")
        act = fwd_host(ffn_act_fwd, HOST_UNITS["ffn_act_fwd"] * gpre.size,
                       gpre, up, ff_conv_w_f[l], ff_conv_b[l].reshape(1, -1))
        s.update(hn3=hn3, gpre=gpre, up=up, act=act)
        h = fwd_mm(act, wrow("ff_w_down", l), res=h, name="mm_ff_down")
        saved.append(s)

    loss11, dh, dg_final = final_loss(h, norm_final.reshape(1, D), target)
    loss = lax.psum(loss11[0, 0], AXES)

    gs = {k: [None] * W[k].shape[0] for k in names if k not in big and W[k].ndim > 1 and k != "rel_bias"}
    dmem_n = None
    dbias_tab = jnp.zeros_like(bias_tab)
    place_idx = jnp.concatenate([me_idx, c_idx])
    gbufs, exchanges, scatters = {}, [], []

    def settle(jobs):
        for j in jobs:
            k, l = j.tag
            if isinstance(j, ExchangeJob):
                nxt = ScatterJob(add_half(j.ops[0], j.result[0], c_idx))
                nxt.tag = j.tag
                scatters.append(nxt)
            else:
                p = j.ops[0]
                if k not in gbufs:
                    gbufs[k] = jnp.zeros((W[k].shape[0], 2 * p.shape[1], p.shape[2]), F32)
                gbufs[k] = add_partials(p, j.result[0], place_idx, gbufs[k], l)

    def take_fit(queue, budget):
        jobs = []
        for j in list(queue):
            if j.cost <= budget:
                budget -= j.cost
                jobs.append(j)
                queue.remove(j)
        return jobs

    def bwd_host(fn, units, *args, **kw):
        jobs = take_fit(exchanges, 0.6 * units) + take_fit(scatters, 1.2 * units)
        out = fn(*args, comm=jobs, **kw)
        settle(jobs)
        return out

    def dx_mm(a, w3, **kw):
        return bwd_host(mm_nt, w3.size * a.shape[0] / L, a, w3, **kw)

    def dw_mm(k, l, xx, dy, S, **kw):
        g3 = bwd_host(mm_tn, xx.shape[1] * dy.shape[1] * xx.shape[0] / L, xx, dy, S, **kw)
        if S == 1:
            g3 = g3.reshape(N_CHIPS, g3.shape[1] // N_CHIPS, g3.shape[2])
        job = ExchangeJob(g3)
        job.tag = (k, l)
        exchanges.append(job)

    for l in reversed(range(depth)):
        i = l // 2
        s = saved[l]
        dact = dx_mm(dh, wrow("ff_w_down", l), out_dtype=BF16, name="mm_ff_down_dx")
        dw_mm("ff_w_down", l, s["act"], dh, 1, name="mm_ff_down_dw")
        dgpre, dup, dcw, dcb = bwd_host(ffn_act_bwd, HOST_UNITS["ffn_act_bwd"] * dact.size,
                                        dact, s["gpre"], s["up"], ff_conv_w_f[l], ff_conv_b[l].reshape(1, -1))
        gs["ff_conv_w"][l], gs["ff_conv_b"][l] = dcw, dcb[0]
        dhn3 = dx_mm(dgpre, wcol("ff_w_gate", l), name="mm_ff_gate_dx")
        dhn3 = dx_mm(dup, wcol("ff_w_up", l), res=dhn3, name="mm_ff_up_dx")
        dw_mm("ff_w_gate", l, s["hn3"], dgpre, N_CHIPS, name="mm_ff_gate_dw")
        dw_mm("ff_w_up", l, s["hn3"], dup, N_CHIPS, name="mm_ff_up_dw")
        dh, dg = bwd_host(rms_bwd, HOST_UNITS["rms_bwd"] * dh.size, s["h2"], norm_ffn[l].reshape(1, D), dhn3, dh)
        gs["norm_ffn"][l] = dg[0]
        dox = dx_mm(dh, wrow("xa_w_o", l), out_dtype=BF16, name="mm_xa_o_dx")
        dw_mm("xa_w_o", l, s["ox"], dh, 1, name="mm_xa_o_dw")
        dq, dkv = xattn_bwd(s["q"], s["kv"], dox)
        dhn2 = dx_mm(dq, wrow("xa_w_q", l), name="mm_xa_q_dx")
        dw_mm("xa_w_q", l, s["hn2"], dq, 1, name="mm_xa_q_dw")
        dw_mm("xa_w_kv", l, mem_n, dkv, N_CHIPS, name="mm_xa_kv_dw")
        dmem_n = dx_mm(dkv, wcol("xa_w_kv", l), res=dmem_n, name="mm_xa_kv_dx")
        dh, dg = bwd_host(rms_bwd, HOST_UNITS["rms_bwd"] * dh.size, s["h1"], norm_xattn[l].reshape(1, D), dhn2, dh)
        gs["norm_xattn"][l] = dg[0]
        if l % 2 == 0:
            A = ev_conv_w_f.shape[-1]
            dycat = dx_mm(dh, wrow("ev_w_out", i), name="mm_ev_out_dx")
            dw_mm("ev_w_out", i, s["ycat"], dh, 1, name="mm_ev_out_dw")
            dgb, dgc, dxa, dcw = conv_mixer_bwd(s["z"], ev_conv_w_f[i], dycat[:, :A])
            gs["ev_conv_w"][i] = dcw
            a_l, apow_l, bm16, cm16, d_l, glu16 = s["s5ops"]
            dys_p = _to_segments(dycat[:, A:])
            dy_p, dglu, dd = bwd_host(s5_out_bwd, HOST_UNITS["s5_out_bwd"] * dys_p.size, dys_p, s["y_p"], s["u_p"], glu16)
            cmt = cm16.transpose(0, 2, 1)
            bmt = bm16.transpose(0, 2, 1)
            gfin = bwd_host(s5_scan_bwd, HOST_UNITS["s5_ends"] * dy_p.size, dy_p, a_l, apow_l, cmt, None, name="s5_bwd_ends")
            du_p, da, dbm, dcm = bwd_host(s5_scan_bwd, HOST_UNITS["s5_bwd"] * dy_p.size, dy_p, a_l, apow_l, cmt, gfin,
                                          s["states"], s["starts"], s["u_p"], bmt, d_l, name="s5_bwd")
            dprm = s["prep_vjp"]((da, jnp.zeros_like(apow_l), dbm, dcm, dd, dglu))
            for k, g in zip(["s5_a_re", "s5_a_im", "s5_log_dt", "s5_b_re", "s5_b_im", "s5_c_re", "s5_c_im", "s5_d", "s5_glu_w"], dprm):
                gs[k][i] = g
            dz = jnp.concatenate([dgb, dgc, dxa, _from_segments(du_p)], axis=1)
            dhn = dx_mm(dz, wcol("ev_w_in", i), name="mm_ev_in_dx")
            dw_mm("ev_w_in", i, s["hn"], dz, N_CHIPS, name="mm_ev_in_dw")
        else:
            do = dx_mm(dh, wrow("od_w_out", i), out_dtype=BF16, name="mm_od_out_dx")
            dw_mm("od_w_out", i, s["o"], dh, 1, name="mm_od_out_dw")
            doT = do.reshape(L, NQ, HEAD_DIM).transpose(1, 0, 2)
            dqT, dkT, dvT, dbias, dsink = bwd_host(swa_bwd, HOST_UNITS["swa_bwd"] * doT.size,
                                                   s["qT"], s["kT"], s["vT"], bias2, s["sink_col"], doT)
            dbias_tab = dbias_tab + dbias
            gs["od_sinks"][i] = jnp.sum(dsink.reshape(NQ, WINDOW), axis=1)
            dz = jnp.concatenate([dqT.transpose(1, 0, 2).reshape(L, NQ * HEAD_DIM),
                                  dkT.astype(BF16).transpose(1, 0, 2).reshape(L, NKV * HEAD_DIM),
                                  dvT.astype(BF16).transpose(1, 0, 2).reshape(L, NKV * HEAD_DIM)], axis=1)
            gs["od_b_qkv"][i] = col_sum(dz)[0]
            dhn = dx_mm(dz, wcol("od_w_qkv", i), name="mm_od_qkv_dx")
            dw_mm("od_w_qkv", i, s["hn"], dz, N_CHIPS, name="mm_od_qkv_dw")
        dh, dg = bwd_host(rms_bwd, HOST_UNITS["rms_bwd"] * dh.size, s["h0"], norm_mix[l].reshape(1, D), dhn, dh)
        gs["norm_mix"][l] = dg[0]

    grad_x = dh[None]
    _, dg_mem = rms_bwd(mem[0], norm_mem.reshape(1, D), dmem_n, jnp.zeros_like(dmem_n), name="rms_bwd_mem")
    d_rel_bias = jnp.dot(dbias_tab.reshape(NQ, -1), onehot, precision=lax.Precision.HIGHEST).T

    small = [k for k in names if k not in big]
    local_small = {k: (jnp.stack(gs[k]) if k in gs else None) for k in small}
    local_small["norm_final"] = dg_final[0]
    local_small["norm_mem"] = dg_mem[0]
    local_small["rel_bias"] = d_rel_bias
    full_shape = {k: W[k].shape for k in small}
    for k in small_sharded:
        full_shape[k] = local_small[k].shape
    lst = [local_small[k].reshape(full_shape[k]).astype(F32) for k in small]
    packed = _pack(lst, PACK_ROWS)
    m_rows = packed.shape[0]
    summed = sum_blocks(all_gather_rows(packed).reshape(8, m_rows, LANES))
    gsum = dict(zip(small, _unpack(summed, lst)))
    for k in small_sharded:
        n4 = W[k].shape[-1]
        gsum[k] = lax.dynamic_slice_in_dim(gsum[k], me_idx[0] * n4, n4, axis=gsum[k].ndim - 1)

    for queue, name in ((exchanges, "rs_exchange_rest"), (scatters, "rs_scatter_rest")):
        jobs = queue[:]
        del queue[:]
        if jobs:
            run_comm(jobs, name=name)
            settle(jobs)
    joined = join_halves([gbufs[k] for k in big])
    gbig = {k: g.reshape(W[k].shape) for k, g in zip(big, joined)}

    grads = {**gsum, **gbig}
    delta, new_m, new_v = {}, {}, {}
    for k in big:
        d_, m_, v_ = adamw(_rows2d(W[k]), _rows2d(grads[k]), _rows2d(Mo[k]), _rows2d(Vo[k]), name="adamw_" + k)
        delta[k], new_m[k], new_v[k] = d_.reshape(W[k].shape), m_.reshape(W[k].shape), v_.reshape(W[k].shape)
    sw = [W[k] for k in small]
    d_, m_, v_ = adamw(_pack(sw, PACK_ROWS), _pack([grads[k] for k in small], PACK_ROWS),
                       _pack([Mo[k] for k in small], PACK_ROWS), _pack([Vo[k] for k in small], PACK_ROWS), name="adamw_small")
    for k, a, b, c_ in zip(small, _unpack(d_, sw), _unpack(m_, sw), _unpack(v_, sw)):
        delta[k], new_m[k], new_v[k] = a, b, c_

    return (loss, grad_x, *[grads[k] for k in names], *[delta[k] for k in names],
            *[new_m[k] for k in names], *[new_v[k] for k in names])
```

```python
import functools
import math

import numpy as np
import jax
import jax.numpy as jnp
from jax import lax
from jax.experimental import pallas as pl
from jax.experimental.pallas import tpu as pltpu

F32, BF16 = jnp.float32, jnp.bfloat16
MESH = pl.DeviceIdType.MESH
AXES = ("x", "y", "c")

VMEM_LIMIT_BYTES = 56 * 2**20
SUBLANES, LANES = 8, 128

RMS_EPS = 1e-5
S5_GROUP, S5_STATE = 16, 64
HEAD_DIM, Q_PER_KV, WINDOW = 64, 8, 128
N_BUCKETS, MAX_DISTANCE = 32, 128
X_HEADS = 4
NEG_INF = -1e30
ADAM_LR, ADAM_B1, ADAM_B2, ADAM_EPS, ADAM_WD, ADAM_STEP = 0.001, 0.9, 0.999, 1e-08, 0.01, 10
N_CHIPS = 4
N_SEG = 8
S5_STEPS = 32
HOST_UNITS = {"rms_bwd": 0.57, "ffn_act_fwd": 0.39, "ffn_act_bwd": 0.63, "swa_fwd": 3.1, "swa_bwd": 6.0,
              "s5_fwd": 3.8, "s5_ends": 2.3, "s5_bwd": 4.5, "s5_out_bwd": 1.4}


def _params(sem=None):
    return pltpu.CompilerParams(dimension_semantics=sem, vmem_limit_bytes=VMEM_LIMIT_BYTES)


def _vspec(shape, index_map):
    return pl.BlockSpec(shape, index_map)


ANY = pl.BlockSpec(memory_space=pl.ANY)


def _tile(n, pref):
    t = (min(pref, n) // LANES) * LANES
    while t >= LANES:
        if n % t == 0:
            return t
        t -= LANES
    return n


def _acc_matmul(nk, k, acc, partial, finish):
    if nk == 1:
        finish(partial())
        return

    @pl.when(k == 0)
    def _():
        acc[...] = partial()

    @pl.when(jnp.logical_and(k > 0, k < nk - 1))
    def _():
        acc[...] += partial()

    @pl.when(k == nk - 1)
    def _():
        finish(acc[...] + partial())


MM_MAX_K = 2048


def hosted_call(body, *, name, grid, in_specs, out_specs, out_shape, ops, scratch_shapes=(), semantics, comm=()):
    if not comm:
        return pl.pallas_call(body, name=name, grid=grid, in_specs=in_specs, out_specs=out_specs, out_shape=out_shape,
                              scratch_shapes=list(scratch_shapes), compiler_params=_params(semantics))(*ops)
    n_in, n_out, n_scr = len(ops), len(out_shape), len(scratch_shapes)
    cops = [a for j in comm for a in j.ops]
    couts = [s for j in comm for s in j.outs]
    nsem = sum(j.nsem for j in comm)

    def hosted(*refs):
        ins, refs = refs[:n_in], refs[n_in:]
        cin, refs = refs[:len(cops)], refs[len(cops):]
        outs, refs = refs[:n_out], refs[n_out:]
        cout, refs = refs[:len(couts)], refs[len(couts):]
        scr, (send, recv) = refs[:n_scr], refs[n_scr:]

        def each(phase):
            ii = io = base = 0
            for j in comm:
                getattr(j, phase)(cin[ii:ii + len(j.ops)], cout[io:io + len(j.outs)], send, recv, base)
                ii, io, base = ii + len(j.ops), io + len(j.outs), base + j.nsem

        pids = [pl.program_id(d) for d in range(len(grid))]
        first = functools.reduce(jnp.logical_and, [p == 0 for p in pids])
        last = functools.reduce(jnp.logical_and, [p == g - 1 for p, g in zip(pids, grid)])
        pl.when(first)(lambda: each("start"))
        body(*ins, *outs, *scr)
        pl.when(last)(lambda: each("finish"))

    res = pl.pallas_call(
        hosted, name=name, grid=grid, in_specs=list(in_specs) + [ANY] * len(cops),
        out_specs=list(out_specs) + [ANY] * len(couts), out_shape=list(out_shape) + couts,
        scratch_shapes=list(scratch_shapes) + [pltpu.SemaphoreType.DMA((nsem,)), pltpu.SemaphoreType.DMA((nsem,))],
        compiler_params=pltpu.CompilerParams(dimension_semantics=("arbitrary",) * len(grid),
                                             vmem_limit_bytes=VMEM_LIMIT_BYTES, has_side_effects=True),
    )(*ops, *cops)
    io = n_out
    for j in comm:
        j.result = list(res[io:io + len(j.outs)])
        io += len(j.outs)
    return list(res[:n_out])


def _pcall(body, *, name, grid, in_specs, out_specs, out_shape, scratch_shapes=(), sem, comm=()):
    single = not isinstance(out_shape, (list, tuple))

    def run(*ops):
        res = hosted_call(body, name=name, grid=grid, in_specs=list(in_specs),
                          out_specs=[out_specs] if single else list(out_specs),
                          out_shape=[out_shape] if single else list(out_shape), ops=list(ops),
                          scratch_shapes=scratch_shapes, semantics=sem, comm=comm)
        return res[0] if single else res

    return run


def _mm_call(core, grid, in_specs, ops, out_spec, out_shape, acc_shape, name, comm):
    n_in = len(ops)

    def body(*refs):
        core(refs[:n_in], refs[n_in], refs[n_in + 1])

    return hosted_call(body, name=name, grid=grid, in_specs=in_specs, out_specs=[out_spec], out_shape=[out_shape],
                       ops=ops, scratch_shapes=[pltpu.VMEM(acc_shape, F32)],
                       semantics=("parallel", "parallel", "arbitrary"), comm=comm)[0]


def mm_nn(a, w3, *, bias=None, res=None, out_dtype=F32, name, comm=()):
    M, K = a.shape
    S, K2, ns = w3.shape
    assert K == K2
    tm = _tile(M, 512)
    tk = K if K <= MM_MAX_K else _tile(K, 1536)
    nk = K // tk
    has_b, has_r = bias is not None, res is not None

    def core(ins, o_ref, acc):
        a_ref, w_ref = ins[0], ins[1]
        b_ref = ins[2] if has_b else None
        r_ref = ins[2 + has_b] if has_r else None

        def partial():
            return jnp.dot(a_ref[...].astype(BF16), w_ref[...], preferred_element_type=F32)

        def finish(r):
            if has_b:
                r = r + b_ref[...]
            if has_r:
                r = r + r_ref[...]
            o_ref[...] = r.astype(o_ref.dtype)

        _acc_matmul(nk, pl.program_id(2), acc, partial, finish)

    in_specs = [_vspec((tm, tk), lambda s, i, k: (i, k)), _vspec((None, tk, ns), lambda s, i, k: (s, k, 0))]
    ops = [a, w3]
    if has_b:
        in_specs.append(_vspec((1, ns), lambda s, i, k: (0, s)))
        ops.append(bias)
    if has_r:
        in_specs.append(_vspec((tm, ns), lambda s, i, k: (i, s)))
        ops.append(res)
    return _mm_call(core, (S, M // tm, nk), in_specs, ops, _vspec((tm, ns), lambda s, i, k: (i, s)),
                    jax.ShapeDtypeStruct((M, S * ns), out_dtype), (tm, ns) if nk > 1 else (SUBLANES, LANES), name, comm)


def mm_nt(a, w3, *, res=None, out_dtype=F32, name, comm=()):
    M, N = a.shape
    S, K, ns = w3.shape
    assert N == S * ns
    tko = K if K <= MM_MAX_K else _tile(K, 1024)
    tm = _tile(M, 512 if tko == K else 1024)
    tc = ns if ns <= MM_MAX_K else _tile(ns, 1024)
    ncs = ns // tc
    nc = S * ncs
    has_r = res is not None

    def core(ins, o_ref, acc):
        a_ref, w_ref = ins[0], ins[1]
        r_ref = ins[2] if has_r else None

        def partial():
            return lax.dot_general(a_ref[...].astype(BF16), w_ref[...], (((1,), (1,)), ((), ())),
                                   preferred_element_type=F32)

        def finish(r):
            if has_r:
                r = r + r_ref[...]
            o_ref[...] = r.astype(o_ref.dtype)

        _acc_matmul(nc, pl.program_id(2), acc, partial, finish)

    in_specs = [_vspec((tm, tc), lambda i, j, k: (i, k)),
                _vspec((None, tko, tc), lambda i, j, k: (k // ncs, j, k % ncs))]
    ops = [a, w3]
    if has_r:
        in_specs.append(_vspec((tm, tko), lambda i, j, k: (i, j)))
        ops.append(res)
    return _mm_call(core, (M // tm, K // tko, nc), in_specs, ops, _vspec((tm, tko), lambda i, j, k: (i, j)),
                    jax.ShapeDtypeStruct((M, K), out_dtype), (tm, tko) if nc > 1 else (SUBLANES, LANES), name, comm)


def mm_tn(x, dy, S, *, out_dtype=BF16, name, comm=()):
    M, K = x.shape
    M2, N = dy.shape
    assert M == M2 and N % S == 0
    ns = N // S
    tk = _tile(K, 1536) if (K > MM_MAX_K and dy.dtype == BF16) else _tile(K, 1024)
    tmc = _tile(M, 512 if tk > 1024 else 1024)
    nm = M // tmc

    def core(ins, o_ref, acc):
        x_ref, dy_ref = ins

        def partial():
            return lax.dot_general(x_ref[...].astype(BF16), dy_ref[...].astype(BF16), (((0,), (0,)), ((), ())),
                                   preferred_element_type=F32)

        def finish(r):
            o_ref[...] = r.astype(o_ref.dtype)

        _acc_matmul(nm, pl.program_id(2), acc, partial, finish)

    in_specs = [_vspec((tmc, tk), lambda s, i, m: (m, i)), _vspec((tmc, ns), lambda s, i, m: (m, s))]
    return _mm_call(core, (S, K // tk, nm), in_specs, [x, dy], _vspec((None, tk, ns), lambda s, i, m: (s, i, 0)),
                    jax.ShapeDtypeStruct((S, K, ns), out_dtype), (tk, ns) if nm > 1 else (SUBLANES, LANES), name, comm)


def rms_fwd(h, g, *, name="rms_fwd"):
    R, D = h.shape
    tr = _tile8(R, 256)

    def body(h_ref, g_ref, o_ref):
        x = h_ref[...]
        r = lax.rsqrt(jnp.mean(x * x, axis=-1, keepdims=True) + RMS_EPS)
        o_ref[...] = (x * r * g_ref[...]).astype(o_ref.dtype)

    return pl.pallas_call(
        body, name=name, grid=(R // tr,),
        in_specs=[_vspec((tr, D), lambda i: (i, 0)), _vspec((1, D), lambda i: (0, 0))],
        out_specs=_vspec((tr, D), lambda i: (i, 0)),
        out_shape=jax.ShapeDtypeStruct((R, D), BF16),
        compiler_params=_params(("parallel",)),
    )(h, g)


def _tile8(n, pref):
    t = (min(pref, n) // SUBLANES) * SUBLANES
    while t >= SUBLANES:
        if n % t == 0:
            return t
        t -= SUBLANES
    return n


def rms_bwd(h, g, dhn, dres, *, name="rms_bwd", comm=()):
    R, D = h.shape
    tr = _tile8(R, 256)

    def body(h_ref, g_ref, dhn_ref, dres_ref, dh_ref, dg_ref, dh16_ref):
        @pl.when(pl.program_id(0) == 0)
        def _():
            dg_ref[...] = jnp.zeros_like(dg_ref)

        x = h_ref[...]
        d = dhn_ref[...].astype(F32)
        r = lax.rsqrt(jnp.mean(x * x, axis=-1, keepdims=True) + RMS_EPS)
        xhat = x * r
        dg_ref[...] += jnp.sum(d * xhat, axis=0, keepdims=True)
        t = d * g_ref[...]
        dh = dres_ref[...] + r * (t - xhat * jnp.mean(t * xhat, axis=-1, keepdims=True))
        dh_ref[...] = dh
        dh16_ref[...] = dh.astype(BF16)

    return _pcall(
        body, name=name, grid=(R // tr,),
        in_specs=[_vspec((tr, D), lambda i: (i, 0)), _vspec((1, D), lambda i: (0, 0)),
                  _vspec((tr, D), lambda i: (i, 0)), _vspec((tr, D), lambda i: (i, 0))],
        out_specs=[_vspec((tr, D), lambda i: (i, 0)), _vspec((1, D), lambda i: (0, 0)), _vspec((tr, D), lambda i: (i, 0))],
        out_shape=[jax.ShapeDtypeStruct((R, D), F32), jax.ShapeDtypeStruct((1, D), F32), jax.ShapeDtypeStruct((R, D), BF16)],
        sem=("arbitrary",), comm=comm,
    )(h, g, dhn, dres)


def final_loss(h, g, target, *, name="final_loss"):
    R, D = h.shape
    tr = _tile8(R, 256)

    def body(h_ref, g_ref, t_ref, loss_ref, dh_ref, dg_ref, dh16_ref):
        @pl.when(pl.program_id(0) == 0)
        def _():
            dg_ref[...] = jnp.zeros_like(dg_ref)
            loss_ref[...] = jnp.zeros_like(loss_ref)

        x = h_ref[...]
        r = lax.rsqrt(jnp.mean(x * x, axis=-1, keepdims=True) + RMS_EPS)
        xhat = x * r
        err = xhat * g_ref[...] - t_ref[...]
        row = jnp.mean(err * err, axis=-1, keepdims=True)
        loss_ref[...] += 0.5 * jnp.sum(row, axis=0, keepdims=True)
        d = err * (1.0 / D)
        dg_ref[...] += jnp.sum(d * xhat, axis=0, keepdims=True)
        t = d * g_ref[...]
        dh = r * (t - xhat * jnp.mean(t * xhat, axis=-1, keepdims=True))
        dh_ref[...] = dh
        dh16_ref[...] = dh.astype(BF16)

    return pl.pallas_call(
        body, name=name, grid=(R // tr,),
        in_specs=[_vspec((tr, D), lambda i: (i, 0)), _vspec((1, D), lambda i: (0, 0)), _vspec((tr, D), lambda i: (i, 0))],
        out_specs=[_vspec((1, 1), lambda i: (0, 0)), _vspec((tr, D), lambda i: (i, 0)), _vspec((1, D), lambda i: (0, 0)),
                   _vspec((tr, D), lambda i: (i, 0))],
        out_shape=[jax.ShapeDtypeStruct((1, 1), F32), jax.ShapeDtypeStruct((R, D), F32), jax.ShapeDtypeStruct((1, D), F32),
                   jax.ShapeDtypeStruct((R, D), BF16)],
        compiler_params=_params(("arbitrary",)),
    )(h, g, target)


def _shift_down(v, k):
    rows = lax.broadcasted_iota(jnp.int32, v.shape, 0)
    return jnp.where(rows >= k, pltpu.roll(v, k, axis=0), 0.0)


def _shift_up(v, k):
    L = v.shape[0]
    rows = lax.broadcasted_iota(jnp.int32, v.shape, 0)
    return jnp.where(rows < L - k, pltpu.roll(v, L - k, axis=0), 0.0)


def _conv(v, w):
    return w[2:3, :] * v + w[1:2, :] * _shift_down(v, 1) + w[0:1, :] * _shift_down(v, 2)


def _conv_t(d, w):
    return w[2:3, :] * d + w[1:2, :] * _shift_up(d, 1) + w[0:1, :] * _shift_up(d, 2)


def _conv_dw(d, v):
    return jnp.concatenate([
        jnp.sum(d * _shift_down(v, 2), axis=0, keepdims=True),
        jnp.sum(d * _shift_down(v, 1), axis=0, keepdims=True),
        jnp.sum(d * v, axis=0, keepdims=True)], axis=0)


COL_BLOCK = 128


def conv_mixer_fwd(z, cw, *, name="conv_mixer_fwd"):
    L = z.shape[0]
    A = cw.shape[1]
    cb = _tile(A, COL_BLOCK)
    nb = A // cb

    def body(gb_ref, gc_ref, xa_ref, w_ref, o_ref):
        v = gc_ref[...] * xa_ref[...]
        o_ref[...] = (gb_ref[...] * _conv(v, w_ref[...])).astype(o_ref.dtype)

    return pl.pallas_call(
        body, name=name, grid=(nb,),
        in_specs=[_vspec((L, cb), lambda j: (0, j)), _vspec((L, cb), lambda j: (0, nb + j)),
                  _vspec((L, cb), lambda j: (0, 2 * nb + j)), _vspec((3, cb), lambda j: (0, j))],
        out_specs=_vspec((L, cb), lambda j: (0, j)),
        out_shape=jax.ShapeDtypeStruct((L, A), BF16),
        compiler_params=_params(("parallel",)),
    )(z, z, z, cw)


def conv_mixer_bwd(z, cw, dya, *, name="conv_mixer_bwd"):
    L = z.shape[0]
    A = cw.shape[1]
    cb = _tile(A, COL_BLOCK)
    nb = A // cb

    def body(gb_ref, gc_ref, xa_ref, w_ref, d_ref, dgb_ref, dgc_ref, dxa_ref, dw_ref):
        gc, xa, w, d = gc_ref[...], xa_ref[...], w_ref[...], d_ref[...]
        v = gc * xa
        dgb_ref[...] = (d * _conv(v, w)).astype(dgb_ref.dtype)
        dc = d * gb_ref[...]
        dw_ref[...] = _conv_dw(dc, v)
        dv = _conv_t(dc, w)
        dgc_ref[...] = (dv * xa).astype(dgc_ref.dtype)
        dxa_ref[...] = (dv * gc).astype(dxa_ref.dtype)

    col = lambda j: (0, j)
    outs = pl.pallas_call(
        body, name=name, grid=(nb,),
        in_specs=[_vspec((L, cb), col), _vspec((L, cb), lambda j: (0, nb + j)),
                  _vspec((L, cb), lambda j: (0, 2 * nb + j)), _vspec((3, cb), col), _vspec((L, cb), col)],
        out_specs=[_vspec((L, cb), col), _vspec((L, cb), col), _vspec((L, cb), col), _vspec((3, cb), col)],
        out_shape=[jax.ShapeDtypeStruct((L, A), BF16)] * 3 + [jax.ShapeDtypeStruct((3, A), F32)],
        compiler_params=_params(("parallel",)),
    )(z, z, z, cw, dya)
    return outs[0], outs[1], outs[2], outs[3]


def ffn_act_fwd(gpre, up, cw, cbias, *, name="ffn_act_fwd", comm=()):
    L, Fd = gpre.shape
    cb = _tile(Fd, COL_BLOCK)

    def body(g_ref, u_ref, w_ref, b_ref, o_ref):
        g = _conv(g_ref[...].astype(F32), w_ref[...]) + b_ref[...]
        o_ref[...] = (g * jax.nn.sigmoid(g) * u_ref[...].astype(F32)).astype(o_ref.dtype)

    col = lambda j: (0, j)
    return _pcall(
        body, name=name, grid=(Fd // cb,),
        in_specs=[_vspec((L, cb), col), _vspec((L, cb), col), _vspec((3, cb), col), _vspec((1, cb), col)],
        out_specs=_vspec((L, cb), col),
        out_shape=jax.ShapeDtypeStruct((L, Fd), BF16),
        sem=("parallel",), comm=comm,
    )(gpre, up, cw, cbias)


def ffn_act_bwd(dact, gpre, up, cw, cbias, *, name="ffn_act_bwd", comm=()):
    L, Fd = gpre.shape
    cb = _tile(Fd, COL_BLOCK)

    def body(d_ref, g_ref, u_ref, w_ref, b_ref, dg_ref, du_ref, dw_ref, db_ref):
        gp, w, d = g_ref[...].astype(F32), w_ref[...], d_ref[...].astype(F32)
        g = _conv(gp, w) + b_ref[...]
        sg = jax.nn.sigmoid(g)
        du_ref[...] = (d * (g * sg)).astype(du_ref.dtype)
        dg = d * u_ref[...].astype(F32) * (sg * (1.0 + g * (1.0 - sg)))
        db_ref[...] = jnp.sum(dg, axis=0, keepdims=True)
        dw_ref[...] = _conv_dw(dg, gp)
        dg_ref[...] = _conv_t(dg, w).astype(dg_ref.dtype)

    col = lambda j: (0, j)
    return _pcall(
        body, name=name, grid=(Fd // cb,),
        in_specs=[_vspec((L, cb), col)] * 3 + [_vspec((3, cb), col), _vspec((1, cb), col)],
        out_specs=[_vspec((L, cb), col), _vspec((L, cb), col), _vspec((3, cb), col), _vspec((1, cb), col)],
        out_shape=[jax.ShapeDtypeStruct((L, Fd), BF16), jax.ShapeDtypeStruct((L, Fd), BF16),
                   jax.ShapeDtypeStruct((3, Fd), F32), jax.ShapeDtypeStruct((1, Fd), F32)],
        sem=("parallel",), comm=comm,
    )(dact, gpre, up, cw, cbias)


def col_sum(x, *, name="col_sum"):
    R, C = x.shape
    tr = _tile8(R, 512)

    def body(x_ref, o_ref):
        @pl.when(pl.program_id(0) == 0)
        def _():
            o_ref[...] = jnp.zeros_like(o_ref)

        o_ref[...] += jnp.sum(x_ref[...].astype(F32), axis=0, keepdims=True)

    return pl.pallas_call(
        body, name=name, grid=(R // tr,),
        in_specs=[_vspec((tr, C), lambda i: (i, 0))], out_specs=_vspec((1, C), lambda i: (0, 0)),
        out_shape=jax.ShapeDtypeStruct((1, C), F32), compiler_params=_params(("arbitrary",)),
    )(x)


def _cmul_add(ar, ai, sr, si, br, bi):
    return ar * sr - ai * si + br, ar * si + ai * sr + bi


def _segment_starts(fin, pr, pi, reverse):
    H = fin.shape[1] // 2
    rows = lax.broadcasted_iota(jnp.int32, fin.shape, 0)
    cr = jnp.zeros((1, H), F32)
    ci = jnp.zeros((1, H), F32)
    out = jnp.zeros(fin.shape, F32)
    order = range(N_SEG - 1, -1, -1) if reverse else range(N_SEG)
    for k in order:
        out = jnp.where(rows == k, jnp.concatenate([cr, ci], axis=1), out)
        cr, ci = _cmul_add(pr, pi, cr, ci, fin[k:k + 1, :H], fin[k:k + 1, H:])
    return out


def _gelu(y):
    c0 = math.sqrt(2.0 / math.pi)
    t = jnp.tanh(c0 * (y + 0.044715 * y * y * y))
    return 0.5 * y * (1.0 + t), t


def s5_scan_fwd(u_p, a_l, apow_l, bm, cm, d_l, glu, fin, *, name, comm=()):
    L, C = u_p.shape
    NK, _, SW = bm.shape
    H = SW // 2
    RB = S5_STEPS * N_SEG
    NC = L // RB
    final_only = fin is None

    def scan_chunk(a_ref, buf, st):
        ar = jnp.broadcast_to(a_ref[:, :H], (N_SEG, H))
        ai = jnp.broadcast_to(a_ref[:, H:], (N_SEG, H))

        def step(j, carry):
            sr, si = carry
            rows = pl.ds(pl.multiple_of(j * N_SEG, N_SEG), N_SEG)
            sr, si = _cmul_add(ar, ai, sr, si, buf[rows, :H], buf[rows, H:])
            buf[rows, :H] = sr
            buf[rows, H:] = si
            return sr, si

        sr, si = lax.fori_loop(0, S5_STEPS, step, (st[:, :H], st[:, H:]), unroll=4)
        st[:, :H] = sr
        st[:, H:] = si

    if final_only:
        def body(u_ref, a_ref, bm_ref, fin_ref, buf, st):
            @pl.when(pl.program_id(1) == 0)
            def _():
                st[...] = jnp.zeros_like(st)

            buf[...] = jnp.dot(u_ref[...].astype(BF16), bm_ref[...], preferred_element_type=F32)
            scan_chunk(a_ref, buf, st)
            fin_ref[...] = st[...]

        return _pcall(
            body, name=name, grid=(NK, NC),
            in_specs=[_vspec((RB, LANES), lambda k, j: (j, k)), _vspec((None, 1, SW), lambda k, j: (k, 0, 0)),
                      _vspec((None, LANES, SW), lambda k, j: (k, 0, 0))],
            out_specs=_vspec((None, N_SEG, SW), lambda k, j: (k, 0, 0)),
            out_shape=jax.ShapeDtypeStruct((NK, N_SEG, SW), F32),
            scratch_shapes=[pltpu.VMEM((RB, SW), F32), pltpu.VMEM((N_SEG, SW), F32)],
            sem=("parallel", "arbitrary"), comm=comm,
        )(u_p, a_l, bm)

    def body(u_ref, a_ref, ap_ref, bm_ref, cm_ref, d_ref, glu_ref, fin_ref, o_ref, y_ref, s_ref, start_ref, buf, st):
        @pl.when(pl.program_id(1) == 0)
        def _():
            st[...] = _segment_starts(fin_ref[...], ap_ref[:, :H], ap_ref[:, H:], False)
            start_ref[...] = st[...]

        u = u_ref[...]
        buf[...] = jnp.dot(u.astype(BF16), bm_ref[...], preferred_element_type=F32)
        scan_chunk(a_ref, buf, st)
        states = buf[...]
        s_ref[...] = states
        y = jnp.dot(states.astype(BF16), cm_ref[...], preferred_element_type=F32) + d_ref[...] * u
        y_ref[...] = y
        yg, _ = _gelu(y)
        gate = jnp.dot(yg.astype(BF16), glu_ref[...], preferred_element_type=F32)
        o_ref[...] = (yg * jax.nn.sigmoid(gate)).astype(o_ref.dtype)

    blk = lambda k, j: (j, k)
    per_k = lambda k, j: (k, 0, 0)
    return _pcall(
        body, name=name, grid=(NK, NC),
        in_specs=[_vspec((RB, LANES), blk), _vspec((None, 1, SW), per_k), _vspec((None, 1, SW), per_k),
                  _vspec((None, LANES, SW), per_k), _vspec((None, SW, LANES), per_k), _vspec((1, LANES), lambda k, j: (0, k)),
                  _vspec((None, LANES, LANES), per_k), _vspec((None, N_SEG, SW), per_k)],
        out_specs=[_vspec((RB, LANES), blk), _vspec((RB, LANES), blk), _vspec((RB, SW), blk),
                   _vspec((None, N_SEG, SW), per_k)],
        out_shape=[jax.ShapeDtypeStruct((L, C), BF16), jax.ShapeDtypeStruct((L, C), F32),
                   jax.ShapeDtypeStruct((L, NK * SW), F32), jax.ShapeDtypeStruct((NK, N_SEG, SW), F32)],
        scratch_shapes=[pltpu.VMEM((RB, SW), F32), pltpu.VMEM((N_SEG, SW), F32)],
        sem=("parallel", "arbitrary"), comm=comm,
    )(u_p, a_l, apow_l, bm, cm, d_l, glu, fin)


def s5_out_bwd(dout_p, y_p, u_p, glu, *, name="s5_out_bwd", comm=()):
    L, C = y_p.shape
    NK = C // LANES
    tr = _tile8(L, 512)

    def body(do_ref, y_ref, u_ref, glu_ref, dy_ref, dglu_ref, dd_ref):
        @pl.when(pl.program_id(1) == 0)
        def _():
            dglu_ref[...] = jnp.zeros_like(dglu_ref)
            dd_ref[...] = jnp.zeros_like(dd_ref)

        y, do, w = y_ref[...], do_ref[...].astype(F32), glu_ref[...]
        yg, t = _gelu(y)
        sg = jax.nn.sigmoid(jnp.dot(yg.astype(BF16), w, preferred_element_type=F32))
        dgate = (do * yg * sg * (1.0 - sg)).astype(BF16)
        dyg = do * sg + lax.dot_general(dgate, w, (((1,), (1,)), ((), ())), preferred_element_type=F32)
        dglu_ref[...] += lax.dot_general(yg.astype(BF16), dgate, (((0,), (0,)), ((), ())), preferred_element_type=F32)
        c0 = math.sqrt(2.0 / math.pi)
        dgelu = 0.5 * (1.0 + t) + 0.5 * y * (1.0 - t * t) * c0 * (1.0 + 3.0 * 0.044715 * y * y)
        dy = dyg * dgelu
        dy_ref[...] = dy
        dd_ref[...] += jnp.sum(dy * u_ref[...], axis=0, keepdims=True)

    blk = lambda k, i: (i, k)
    return _pcall(
        body, name=name, grid=(NK, L // tr),
        in_specs=[_vspec((tr, LANES), blk)] * 3 + [_vspec((None, LANES, LANES), lambda k, i: (k, 0, 0))],
        out_specs=[_vspec((tr, LANES), blk), _vspec((None, LANES, LANES), lambda k, i: (k, 0, 0)),
                   _vspec((1, LANES), lambda k, i: (0, k))],
        out_shape=[jax.ShapeDtypeStruct((L, C), F32), jax.ShapeDtypeStruct((NK, LANES, LANES), F32),
                   jax.ShapeDtypeStruct((1, C), F32)],
        sem=("parallel", "arbitrary"), comm=comm,
    )(dout_p, y_p, u_p, glu)


def s5_scan_bwd(dy_p, a_l, apow_l, cmt, gfin, states=None, starts=None, u_p=None, bmt=None, d_l=None, *, name, comm=()):
    L, C = dy_p.shape
    NK, _, SW = cmt.shape
    H = SW // 2
    RB = S5_STEPS * N_SEG
    NC = L // RB
    final_only = gfin is None

    def scan_chunk(a_ref, buf, st):
        ar = jnp.broadcast_to(a_ref[:, :H], (N_SEG, H))
        ai = -jnp.broadcast_to(a_ref[:, H:], (N_SEG, H))

        def step(jj, carry):
            gr, gi = carry
            j = S5_STEPS - 1 - jj
            rows = pl.ds(pl.multiple_of(j * N_SEG, N_SEG), N_SEG)
            gr, gi = _cmul_add(ar, ai, gr, gi, buf[rows, :H], buf[rows, H:])
            buf[rows, :H] = gr
            buf[rows, H:] = gi
            return gr, gi

        gr, gi = lax.fori_loop(0, S5_STEPS, step, (st[:, :H], st[:, H:]), unroll=4)
        st[:, :H] = gr
        st[:, H:] = gi

    rblk = lambda k, j: (NC - 1 - j, k)
    per_k = lambda k, j: (k, 0, 0)

    if final_only:
        def body(dy_ref, a_ref, cmt_ref, fin_ref, buf, st):
            @pl.when(pl.program_id(1) == 0)
            def _():
                st[...] = jnp.zeros_like(st)

            buf[...] = jnp.dot(dy_ref[...].astype(BF16), cmt_ref[...], preferred_element_type=F32)
            scan_chunk(a_ref, buf, st)
            fin_ref[...] = st[...]

        return _pcall(
            body, name=name, grid=(NK, NC),
            in_specs=[_vspec((RB, LANES), rblk), _vspec((None, 1, SW), per_k), _vspec((None, LANES, SW), per_k)],
            out_specs=_vspec((None, N_SEG, SW), per_k),
            out_shape=jax.ShapeDtypeStruct((NK, N_SEG, SW), F32),
            scratch_shapes=[pltpu.VMEM((RB, SW), F32), pltpu.VMEM((N_SEG, SW), F32)],
            sem=("parallel", "arbitrary"), comm=comm,
        )(dy_p, a_l, cmt)

    def body(dy_ref, a_ref, ap_ref, cmt_ref, gfin_ref, s_ref, sprev_ref, start_ref, u_ref, bmt_ref, d_ref,
             du_ref, da_ref, dbm_ref, dcm_ref, buf, st):
        jc = pl.program_id(1)

        @pl.when(jc == 0)
        def _():
            st[...] = _segment_starts(gfin_ref[...], ap_ref[:, :H], -ap_ref[:, H:], True)
            da_ref[...] = jnp.zeros_like(da_ref)
            dbm_ref[...] = jnp.zeros_like(dbm_ref)
            dcm_ref[...] = jnp.zeros_like(dcm_ref)

        dy = dy_ref[...]
        dyb = dy.astype(BF16)
        buf[...] = jnp.dot(dyb, cmt_ref[...], preferred_element_type=F32)
        scan_chunk(a_ref, buf, st)
        g = buf[...]
        s = s_ref[...]
        first = jnp.where(jc == NC - 1, start_ref[...], sprev_ref[...])
        sp = jnp.concatenate([first, s[:RB - N_SEG, :]], axis=0)
        gr, gi, pr, pi = g[:, :H], g[:, H:], sp[:, :H], sp[:, H:]
        da_ref[...] += jnp.concatenate([jnp.sum(gr * pr + gi * pi, axis=0, keepdims=True),
                                        jnp.sum(gi * pr - gr * pi, axis=0, keepdims=True)], axis=1)
        gb = g.astype(BF16)
        u = u_ref[...]
        du_ref[...] = (jnp.dot(gb, bmt_ref[...], preferred_element_type=F32) + dy * d_ref[...]).astype(du_ref.dtype)
        dbm_ref[...] += lax.dot_general(u.astype(BF16), gb, (((0,), (0,)), ((), ())), preferred_element_type=F32)
        dcm_ref[...] += lax.dot_general(s.astype(BF16), dyb, (((0,), (0,)), ((), ())), preferred_element_type=F32)

    prev8 = lambda k, j: (jnp.maximum((NC - 1 - j) * S5_STEPS - 1, 0), k)
    return _pcall(
        body, name=name, grid=(NK, NC),
        in_specs=[_vspec((RB, LANES), rblk), _vspec((None, 1, SW), per_k), _vspec((None, 1, SW), per_k),
                  _vspec((None, LANES, SW), per_k), _vspec((None, N_SEG, SW), per_k), _vspec((RB, SW), rblk),
                  _vspec((N_SEG, SW), prev8), _vspec((None, N_SEG, SW), per_k), _vspec((RB, LANES), rblk),
                  _vspec((None, SW, LANES), per_k), _vspec((1, LANES), lambda k, j: (0, k))],
        out_specs=[_vspec((RB, LANES), rblk), _vspec((None, 1, SW), per_k), _vspec((None, LANES, SW), per_k),
                   _vspec((None, SW, LANES), per_k)],
        out_shape=[jax.ShapeDtypeStruct((L, C), BF16), jax.ShapeDtypeStruct((NK, 1, SW), F32),
                   jax.ShapeDtypeStruct((NK, LANES, SW), F32), jax.ShapeDtypeStruct((NK, SW, LANES), F32)],
        scratch_shapes=[pltpu.VMEM((RB, SW), F32), pltpu.VMEM((N_SEG, SW), F32)],
        sem=("parallel", "arbitrary"), comm=comm,
    )(dy_p, a_l, apow_l, cmt, gfin, states, states, starts, u_p, bmt, d_l)


def _s5_prep(a_re, a_im, log_dt, b_re, b_im, c_re, c_im, d, glu_w, seg_len):
    G, P = a_re.shape
    Hc = b_re.shape[-1]
    gl = LANES // Hc
    nk = G // gl
    dt = jnp.exp(log_dt)[:, None]
    er = jnp.exp(a_re * dt)
    ab_r, ab_i = er * jnp.cos(a_im * dt), er * jnp.sin(a_im * dt)
    den = a_re * a_re + a_im * a_im
    nr, ni = ab_r - 1.0, ab_i
    q_r, q_i = (nr * a_re + ni * a_im) / den, (ni * a_re - nr * a_im) / den
    bb_r = q_r[..., None] * b_re - q_i[..., None] * b_im
    bb_i = q_r[..., None] * b_im + q_i[..., None] * b_re
    ep = jnp.exp(a_re * dt * seg_len)
    ap_r, ap_i = ep * jnp.cos(a_im * dt * seg_len), ep * jnp.sin(a_im * dt * seg_len)
    eye = jnp.eye(gl, dtype=F32)

    def lanes(t):
        return t.reshape(nk, 1, gl * P)

    def b_mat(t):
        return jnp.einsum("kgph,gq->kghqp", t.reshape(nk, gl, P, Hc), eye).reshape(nk, gl * Hc, gl * P)

    def c_mat(t):
        return jnp.einsum("kghp,gq->kgpqh", t.reshape(nk, gl, Hc, P), eye).reshape(nk, gl * P, gl * Hc)

    a_l = jnp.concatenate([lanes(ab_r), lanes(ab_i)], axis=-1)
    apow_l = jnp.concatenate([lanes(ap_r), lanes(ap_i)], axis=-1)
    bm = jnp.concatenate([b_mat(bb_r), b_mat(bb_i)], axis=-1)
    cm = jnp.concatenate([c_mat(c_re), -c_mat(c_im)], axis=1)
    glu = jnp.einsum("kgho,gq->kghqo", glu_w.reshape(nk, gl, Hc, Hc), eye).reshape(nk, gl * Hc, gl * Hc)
    return a_l, apow_l, bm, cm, d.reshape(1, G * Hc), glu


def _to_segments(t):
    L, C = t.shape
    return t.reshape(N_SEG, L // N_SEG, C).transpose(1, 0, 2).reshape(L, C)


def _from_segments(t):
    L, C = t.shape
    return t.reshape(L // N_SEG, N_SEG, C).transpose(1, 0, 2).reshape(L, C)


def _swa_probs(q, kk, bias, sink):
    s = lax.dot_general(q, kk, (((1,), (1,)), ((), ())), preferred_element_type=F32) * (HEAD_DIM ** -0.5)
    s = s + bias
    m = jnp.maximum(jnp.max(s, axis=1, keepdims=True), sink)
    e = jnp.exp(s - m)
    es = jnp.exp(sink - m)
    inv = 1.0 / (jnp.sum(e, axis=1, keepdims=True) + es)
    return e * inv, es * inv


def _swa_blocks(q_ref, kp_ref, kc_ref, vp_ref, vc_ref, bias_ref, n2):
    W = WINDOW
    rows = Q_PER_KV * W
    k0, k1, v0, v1 = kc_ref[0:W, :], kc_ref[W:, :], vc_ref[0:W, :], vc_ref[W:, :]
    table = bias_ref[jnp.minimum(n2, 1)].reshape(rows, 2 * W)
    return [(slice(0, W), q_ref[:, 0:W, :].reshape(rows, HEAD_DIM), jnp.concatenate([kp_ref[...], k0], axis=0),
             jnp.concatenate([vp_ref[...], v0], axis=0), table),
            (slice(W, 2 * W), q_ref[:, W:, :].reshape(rows, HEAD_DIM), jnp.concatenate([k0, k1], axis=0),
             jnp.concatenate([v0, v1], axis=0), bias_ref[1].reshape(rows, 2 * W))]


def _swa_specs(nq):
    W = WINDOW
    qs = _vspec((Q_PER_KV, 2 * W, HEAD_DIM), lambda g, n: (g, n, 0))
    kprev = _vspec((None, W, HEAD_DIM), lambda g, n: (g, jnp.maximum(2 * n - 1, 0), 0))
    kcur = _vspec((None, 2 * W, HEAD_DIM), lambda g, n: (g, n, 0))
    bias = _vspec((2, Q_PER_KV, W, 2 * W), lambda g, n: (0, g, 0, 0))
    dbias = _vspec((Q_PER_KV, W, 2 * W), lambda g, n: (g, 0, 0))
    sink = _vspec((Q_PER_KV * W, 1), lambda g, n: (g, 0))
    return qs, kprev, kcur, bias, dbias, sink


def _masked_bias(bias_tab):
    qi = np.arange(WINDOW)[:, None]
    kj = np.arange(2 * WINDOW)[None, :]
    valid = ((kj < WINDOW) & (kj > qi)) | ((kj >= WINDOW) & (kj - WINDOW <= qi))
    first = valid & (kj >= WINDOW)
    return jnp.stack([jnp.where(first[None], bias_tab, NEG_INF), jnp.where(valid[None], bias_tab, NEG_INF)])


def swa_fwd(qT, kT, vT, bias2, sink_col, *, name="swa_fwd", comm=()):
    NQ, L, _ = qT.shape
    NKV = kT.shape[0]
    qs, kprev, kcur, bs, _, sk = _swa_specs(NQ)

    def body(q_ref, kp_ref, kc_ref, vp_ref, vc_ref, bias_ref, sink_ref, o_ref):
        sink = sink_ref[...]
        for rows, q, kk, vv, bias in _swa_blocks(q_ref, kp_ref, kc_ref, vp_ref, vc_ref, bias_ref, pl.program_id(1)):
            p, _ = _swa_probs(q, kk, bias, sink)
            o = jnp.dot(p.astype(BF16), vv, preferred_element_type=F32)
            o_ref[:, rows, :] = o.reshape(Q_PER_KV, WINDOW, HEAD_DIM).astype(o_ref.dtype)

    return _pcall(
        body, name=name, grid=(NKV, L // (2 * WINDOW)),
        in_specs=[qs, kprev, kcur, kprev, kcur, bs, sk], out_specs=qs,
        out_shape=jax.ShapeDtypeStruct((NQ, L, HEAD_DIM), BF16),
        sem=("parallel", "arbitrary"), comm=comm,
    )(qT, kT, kT, vT, vT, bias2, sink_col)


def swa_bwd(qT, kT, vT, bias2, sink_col, doT, *, name="swa_bwd", comm=()):
    NQ, L, _ = qT.shape
    NKV = kT.shape[0]
    qs, kprev, kcur, bs, dbs, sk = _swa_specs(NQ)
    W = WINDOW

    def body(q_ref, kp_ref, kc_ref, vp_ref, vc_ref, bias_ref, sink_ref, do_ref,
             dq_ref, dk_ref, dv_ref, dbias_ref, dsink_ref):
        n = pl.program_id(1)

        @pl.when(n == 0)
        def _():
            dk_ref[...] = jnp.zeros_like(dk_ref)
            dv_ref[...] = jnp.zeros_like(dv_ref)
            dbias_ref[...] = jnp.zeros_like(dbias_ref)
            dsink_ref[...] = jnp.zeros_like(dsink_ref)

        sink = sink_ref[...]
        scale = HEAD_DIM ** -0.5
        grads = []
        for rows, q, kk, vv, bias in _swa_blocks(q_ref, kp_ref, kc_ref, vp_ref, vc_ref, bias_ref, n):
            p, ps = _swa_probs(q, kk, bias, sink)
            do = do_ref[:, rows, :].reshape(Q_PER_KV * W, HEAD_DIM)
            dp = lax.dot_general(do, vv, (((1,), (1,)), ((), ())), preferred_element_type=F32)
            delta = jnp.sum(p * dp, axis=1, keepdims=True)
            ds = p * (dp - delta)
            dsink_ref[...] += -ps * delta
            dbias_ref[...] += ds.reshape(Q_PER_KV, W, 2 * W)
            dsb = ds.astype(BF16)
            dq = jnp.dot(dsb, kk, preferred_element_type=F32) * scale
            dq_ref[:, rows, :] = dq.reshape(Q_PER_KV, W, HEAD_DIM).astype(dq_ref.dtype)
            grads.append((lax.dot_general(q, dsb, (((0,), (0,)), ((), ())), preferred_element_type=F32) * scale,
                          lax.dot_general(do, p.astype(BF16), (((0,), (0,)), ((), ())), preferred_element_type=F32)))
        (dk0, dv0), (dk1, dv1) = grads

        @pl.when(n == 0)
        def _():
            dk_ref[:, 0:W] += dk0[:, W:]
            dv_ref[:, 0:W] += dv0[:, W:]

        @pl.when(n > 0)
        def _():
            cols = pl.ds(pl.multiple_of((2 * n - 1) * W, W), 2 * W)
            dk_ref[:, cols] += dk0
            dv_ref[:, cols] += dv0

        cols = pl.ds(pl.multiple_of(2 * n * W, W), 2 * W)
        dk_ref[:, cols] += dk1
        dv_ref[:, cols] += dv1

    whole = _vspec((None, HEAD_DIM, L), lambda g, n: (g, 0, 0))
    return _pcall(
        body, name=name, grid=(NKV, L // (2 * W)),
        in_specs=[qs, kprev, kcur, kprev, kcur, bs, sk, qs],
        out_specs=[qs, whole, whole, dbs, sk],
        out_shape=[jax.ShapeDtypeStruct((NQ, L, HEAD_DIM), BF16), jax.ShapeDtypeStruct((NKV, HEAD_DIM, L), F32),
                   jax.ShapeDtypeStruct((NKV, HEAD_DIM, L), F32), jax.ShapeDtypeStruct((NQ, W, 2 * W), F32),
                   jax.ShapeDtypeStruct((NQ * W, 1), F32)],
        sem=("parallel", "arbitrary"), comm=comm,
    )(qT, kT, kT, vT, vT, bias2, sink_col, doT)


def _bucket_table():
    qi = np.arange(WINDOW)[:, None]
    kj = np.arange(2 * WINDOW)[None, :]
    rel = qi + WINDOW - kj
    max_exact = N_BUCKETS // 2
    n = np.maximum(rel, 0)
    nf = np.maximum(n, max_exact).astype(np.float32)
    large = max_exact + (np.log(nf / max_exact) / math.log(MAX_DISTANCE / max_exact) * (N_BUCKETS - max_exact)).astype(np.int32)
    large = np.minimum(large, N_BUCKETS - 1)
    return np.where(n < max_exact, n, large).astype(np.int32).reshape(-1)


def _xa_probs(q, k):
    hd = q.shape[1]
    s = lax.dot_general(q, k, (((1,), (1,)), ((), ())), preferred_element_type=F32) * (hd ** -0.5)
    e = jnp.exp(s - jnp.max(s, axis=1, keepdims=True))
    return e / jnp.sum(e, axis=1, keepdims=True)


def xattn_fwd(q, kv, *, name="xattn_fwd"):
    L, D = q.shape
    Mm = kv.shape[0]
    hd = D // X_HEADS
    tq = _tile8(L, 512)

    def body(q_ref, kv_ref, o_ref):
        for h in range(X_HEADS):
            cols = slice(h * hd, (h + 1) * hd)
            p = _xa_probs(q_ref[:, cols], kv_ref[:, cols])
            o_ref[:, cols] = jnp.dot(p.astype(BF16), kv_ref[:, D + h * hd:D + (h + 1) * hd],
                                     preferred_element_type=F32).astype(o_ref.dtype)

    return pl.pallas_call(
        body, name=name, grid=(L // tq,),
        in_specs=[_vspec((tq, D), lambda i: (i, 0)), _vspec((Mm, 2 * D), lambda i: (0, 0))],
        out_specs=_vspec((tq, D), lambda i: (i, 0)),
        out_shape=jax.ShapeDtypeStruct((L, D), BF16),
        compiler_params=_params(("parallel",)),
    )(q, kv)


def xattn_bwd(q, kv, do, *, name="xattn_bwd"):
    L, D = q.shape
    Mm = kv.shape[0]
    hd = D // X_HEADS
    tq = _tile8(L, 512)

    def body(q_ref, kv_ref, do_ref, dq_ref, dkv_ref):
        @pl.when(pl.program_id(0) == 0)
        def _():
            dkv_ref[...] = jnp.zeros_like(dkv_ref)

        for h in range(X_HEADS):
            cols = slice(h * hd, (h + 1) * hd)
            vcols = slice(D + h * hd, D + (h + 1) * hd)
            qh, kh, vh, doh = q_ref[:, cols], kv_ref[:, cols], kv_ref[:, vcols], do_ref[:, cols]
            p = _xa_probs(qh, kh)
            dp = lax.dot_general(doh, vh, (((1,), (1,)), ((), ())), preferred_element_type=F32)
            ds = (p * (dp - jnp.sum(p * dp, axis=1, keepdims=True)) * (hd ** -0.5)).astype(BF16)
            dq_ref[:, cols] = jnp.dot(ds, kh, preferred_element_type=F32).astype(dq_ref.dtype)
            dkv_ref[:, cols] += lax.dot_general(ds, qh, (((0,), (0,)), ((), ())), preferred_element_type=F32)
            dkv_ref[:, vcols] += lax.dot_general(p.astype(BF16), doh, (((0,), (0,)), ((), ())), preferred_element_type=F32)

    return pl.pallas_call(
        body, name=name, grid=(L // tq,),
        in_specs=[_vspec((tq, D), lambda i: (i, 0)), _vspec((Mm, 2 * D), lambda i: (0, 0)), _vspec((tq, D), lambda i: (i, 0))],
        out_specs=[_vspec((tq, D), lambda i: (i, 0)), _vspec((Mm, 2 * D), lambda i: (0, 0))],
        out_shape=[jax.ShapeDtypeStruct((L, D), BF16), jax.ShapeDtypeStruct((Mm, 2 * D), F32)],
        compiler_params=_params(("arbitrary",)),
    )(q, kv, do)


def adamw(w, g, m, v, *, name="adamw"):
    R, C = w.shape
    tr = _tile8(R, max(SUBLANES, (256 * 1024) // C // SUBLANES * SUBLANES))

    def body(w_ref, g_ref, m_ref, v_ref, d_ref, nm_ref, nv_ref):
        g_ = g_ref[...]
        nm = ADAM_B1 * m_ref[...] + (1.0 - ADAM_B1) * g_
        nv = ADAM_B2 * v_ref[...] + (1.0 - ADAM_B2) * (g_ * g_)
        m_hat = nm / (1.0 - ADAM_B1 ** ADAM_STEP)
        v_hat = nv / (1.0 - ADAM_B2 ** ADAM_STEP)
        d_ref[...] = -ADAM_LR * (m_hat / (jnp.sqrt(v_hat) + ADAM_EPS) + ADAM_WD * w_ref[...])
        nm_ref[...] = nm
        nv_ref[...] = nv

    spec = _vspec((tr, C), lambda i: (i, 0))
    return pl.pallas_call(
        body, name=name, grid=(R // tr,), in_specs=[spec] * 4, out_specs=[spec] * 3,
        out_shape=[jax.ShapeDtypeStruct((R, C), F32)] * 3, compiler_params=_params(("parallel",)),
    )(w, g, m, v)


def _place():
    x, y, c = lax.axis_index("x"), lax.axis_index("y"), lax.axis_index("c")
    chips = [(1 - x, y), (x, 1 - y), (1 - x, 1 - y)]
    return x, y, c, chips


def _remote(src, dst, send, recv, k, to):
    return pltpu.make_async_remote_copy(src_ref=src, dst_ref=dst, send_sem=send.at[k], recv_sem=recv.at[k],
                                        device_id=to, device_id_type=MESH)


class _Job:
    result = None

    def start(self, ins, outs, send, recv, base):
        for cp in self.copies(ins, outs, send, recv, base)[0]:
            cp.start()


class GatherJob(_Job):
    nsem = 7

    def __init__(self, w):
        self.ops = [w]
        self.outs = [jax.ShapeDtypeStruct((N_CHIPS,) + w.shape, w.dtype)]
        self.cost = N_CHIPS * w.size

    def copies(self, ins, outs, send, recv, base, first_only=True):
        w, out = ins[0], outs[0]
        x, y, c, chips = _place()
        me, sib = 2 * x + y, (x, y, 1 - c)
        h = w.shape[0] // 2
        mine, theirs = pl.ds(c * h, h), pl.ds((1 - c) * h, h)
        first = [_remote(w.at[mine], out.at[me, mine], send, recv, base + j, (px, py, c))
                 for j, (px, py) in enumerate(chips)]
        first.append(_remote(w, out.at[me], send, recv, base + 6, sib))
        if first_only:
            return first,
        landed = [out.at[2 * px + py, mine] for px, py in chips]
        lands = [_remote(w.at[mine], landed[j], send, recv, base + j, (px, py, c)) for j, (px, py) in enumerate(chips)]
        passes = [_remote(landed[j], landed[j], send, recv, base + 3 + j, sib) for j in range(3)]
        arrives = [_remote(w.at[theirs], out.at[2 * px + py, theirs], send, recv, base + 3 + j, sib)
                   for j, (px, py) in enumerate(chips)]
        return first, lands, passes, arrives

    def finish(self, ins, outs, send, recv, base):
        first, lands, passes, arrives = self.copies(ins, outs, send, recv, base, first_only=False)
        for land, fwd in zip(lands, passes):
            land.wait_recv()
            fwd.start()
        first[3].wait_recv()
        for cp in arrives:
            cp.wait_recv()
        for cp in first + passes:
            cp.wait_send()


class ExchangeJob(_Job):
    nsem = 1

    def __init__(self, g):
        self.ops = [g]
        self.outs = [jax.ShapeDtypeStruct((g.shape[0], g.shape[1] // 2, g.shape[2]), g.dtype)]
        self.cost = 0.15 * g.size

    def copies(self, ins, outs, send, recv, base):
        x, y, c, _ = _place()
        r2 = ins[0].shape[1] // 2
        return [_remote(ins[0].at[:, pl.ds((1 - c) * r2, r2)], outs[0], send, recv, base, (x, y, 1 - c))],

    def finish(self, ins, outs, send, recv, base):
        self.copies(ins, outs, send, recv, base)[0][0].wait()


class ScatterJob(_Job):
    nsem = 3

    def __init__(self, p):
        self.ops = [p]
        self.outs = [jax.ShapeDtypeStruct((N_CHIPS - 1,) + p.shape[1:], p.dtype)]
        self.cost = 2 * p.size

    def copies(self, ins, outs, send, recv, base):
        x, y, c, chips = _place()
        return [_remote(ins[0].at[2 * px + py], outs[0].at[j], send, recv, base + j, (px, py, c))
                for j, (px, py) in enumerate(chips)],

    def finish(self, ins, outs, send, recv, base):
        for cp in self.copies(ins, outs, send, recv, base)[0]:
            cp.wait()


def run_comm(jobs, *, name):
    cops = [a for j in jobs for a in j.ops]
    couts = [s for j in jobs for s in j.outs]
    nsem = sum(j.nsem for j in jobs)

    def body(*refs):
        cin, cout = refs[:len(cops)], refs[len(cops):len(cops) + len(couts)]
        send, recv = refs[len(cops) + len(couts):]
        for phase in ("start", "finish"):
            ii = io = base = 0
            for j in jobs:
                getattr(j, phase)(cin[ii:ii + len(j.ops)], cout[io:io + len(j.outs)], send, recv, base)
                ii, io, base = ii + len(j.ops), io + len(j.outs), base + j.nsem

    res = pl.pallas_call(
        body, name=name, in_specs=[ANY] * len(cops), out_specs=[ANY] * len(couts), out_shape=couts,
        scratch_shapes=[pltpu.SemaphoreType.DMA((nsem,)), pltpu.SemaphoreType.DMA((nsem,))],
        compiler_params=pltpu.CompilerParams(has_side_effects=True),
    )(*cops)
    io = 0
    for j in jobs:
        j.result = list(res[io:io + len(j.outs)])
        io += len(j.outs)


def add_half(g, other, c_idx, *, name="rs_add_half"):
    S, R, C = g.shape
    r2 = R // 2
    tr = _tile8(r2, max(SUBLANES, (512 * 1024) // C // SUBLANES * SUBLANES))
    nb = r2 // tr

    def body(c_ref, g_ref, o_ref, out_ref):
        out_ref[...] = (g_ref[...].astype(F32) + o_ref[...].astype(F32)).astype(out_ref.dtype)

    return pl.pallas_call(
        body, name=name,
        grid_spec=pltpu.PrefetchScalarGridSpec(
            num_scalar_prefetch=1, grid=(S, nb),
            in_specs=[pl.BlockSpec((None, tr, C), lambda s, i, c_ref: (s, c_ref[0] * nb + i, 0)),
                      pl.BlockSpec((None, tr, C), lambda s, i, c_ref: (s, i, 0))],
            out_specs=pl.BlockSpec((None, tr, C), lambda s, i, c_ref: (s, i, 0))),
        out_shape=jax.ShapeDtypeStruct((S, r2, C), BF16),
        compiler_params=_params(("parallel", "parallel")),
    )(c_idx, g, other)


def add_partials(p, got, place_idx, gbuf, layer, *, name="rs_add_partials"):
    S, r2, C = p.shape
    tr = _tile8(r2, max(SUBLANES, (512 * 1024) // C // SUBLANES * SUBLANES))
    nb = r2 // tr

    def body(pi_ref, p_ref, g_ref, buf_ref, out_ref):
        out_ref[...] = ((p_ref[...].astype(F32) + g_ref[0].astype(F32)) + g_ref[1].astype(F32)) + g_ref[2].astype(F32)

    return pl.pallas_call(
        body, name=name,
        grid_spec=pltpu.PrefetchScalarGridSpec(
            num_scalar_prefetch=1, grid=(nb,),
            in_specs=[pl.BlockSpec((None, tr, C), lambda i, pi: (pi[0], i, 0)),
                      pl.BlockSpec((N_CHIPS - 1, tr, C), lambda i, pi: (0, i, 0)),
                      ANY],
            out_specs=pl.BlockSpec((None, tr, C), lambda i, pi: (layer, pi[1] * nb + i, 0))),
        out_shape=jax.ShapeDtypeStruct(gbuf.shape, F32),
        input_output_aliases={3: 0},
        compiler_params=_params(("parallel",)),
    )(place_idx, p, got, gbuf)


def join_halves(gbufs, *, name="rs_join_halves"):
    n = len(gbufs)

    def body(*refs):
        outs = refs[n:2 * n]
        send, recv = refs[2 * n:]
        x, y, c, _ = _place()
        sib = (x, y, 1 - c)
        cps = []
        for i in range(n):
            r2 = outs[i].shape[1] // 2
            mine = outs[i].at[:, pl.ds(c * r2, r2)]
            cp = _remote(mine, mine, send, recv, i, sib)
            cp.start()
            cps.append(cp)
        for i in range(n):
            r2 = outs[i].shape[1] // 2
            theirs = outs[i].at[:, pl.ds((1 - c) * r2, r2)]
            _remote(theirs, theirs, send, recv, i, sib).wait_recv()
        for cp in cps:
            cp.wait_send()

    return pl.pallas_call(
        body, name=name, in_specs=[ANY] * n, out_specs=[ANY] * n,
        out_shape=[jax.ShapeDtypeStruct(g.shape, F32) for g in gbufs],
        input_output_aliases={i: i for i in range(n)},
        scratch_shapes=[pltpu.SemaphoreType.DMA((n,)), pltpu.SemaphoreType.DMA((n,))],
        compiler_params=pltpu.CompilerParams(has_side_effects=True),
    )(*gbufs)


def all_gather_rows(v, *, name="all_gather_small"):
    m, ncol = v.shape

    def body(x_ref, out_ref, send, recv, lsem):
        x, y, c, chips = _place()
        me, sib = (x, y, c), (x, y, 1 - c)

        def rows(px, py, pc):
            return out_ref.at[pl.ds((4 * px + 2 * py + pc) * m, m), :]

        def copy(k, block, to, src=None):
            return _remote(rows(*block) if src is None else src, rows(*block), send, recv, k, to)

        mine = pltpu.make_async_copy(x_ref, rows(*me), lsem)
        mine.start()
        first = [copy(0, me, sib, src=x_ref)]
        first += [copy(1 + j, me, (*chip, c), src=x_ref) for j, chip in enumerate(chips)]
        for cp in first:
            cp.start()
        passed = [copy(4 + j, (*chip, c), sib) for j, chip in enumerate(chips)]
        for j, chip in enumerate(chips):
            copy(1 + j, (*chip, c), me).wait_recv()
            passed[j].start()
        copy(0, sib, me).wait_recv()
        for j, chip in enumerate(chips):
            copy(4 + j, (*chip, 1 - c), me).wait_recv()
        for cp in first + passed:
            cp.wait_send()
        mine.wait()

    return pl.pallas_call(
        body, name=name,
        in_specs=[pl.BlockSpec(memory_space=pltpu.VMEM)], out_specs=pl.BlockSpec(memory_space=pltpu.VMEM),
        out_shape=jax.ShapeDtypeStruct((8 * m, ncol), v.dtype),
        scratch_shapes=[pltpu.SemaphoreType.DMA((7,)), pltpu.SemaphoreType.DMA((7,)), pltpu.SemaphoreType.DMA],
        compiler_params=pltpu.CompilerParams(vmem_limit_bytes=VMEM_LIMIT_BYTES, has_side_effects=True),
    )(v)


def sum_blocks(g8, *, name="sum_blocks"):
    nb, m, ncol = g8.shape
    tr = _tile8(m, 512)

    def body(g_ref, o_ref):
        acc = g_ref[0]
        for k in range(1, nb):
            acc = acc + g_ref[k]
        o_ref[...] = acc

    return pl.pallas_call(
        body, name=name, grid=(m // tr,),
        in_specs=[_vspec((nb, tr, ncol), lambda i: (0, i, 0))], out_specs=_vspec((tr, ncol), lambda i: (i, 0)),
        out_shape=jax.ShapeDtypeStruct((m, ncol), F32), compiler_params=_params(("parallel",)),
    )(g8)


PACK_ROWS = 256


def _pack(arrs, mult):
    flat = jnp.concatenate([a.reshape(-1) for a in arrs])
    pad = (-flat.shape[0]) % (mult * LANES)
    return jnp.pad(flat, (0, pad)).reshape(-1, LANES)


def _unpack(packed, like):
    flat = packed.reshape(-1)
    out, off = [], 0
    for a in like:
        out.append(flat[off:off + a.size].reshape(a.shape))
        off += a.size
    return out


def _rows2d(a):
    return a.reshape(-1, a.shape[-1])


def kernel(x, mem, norm_mix, norm_xattn, norm_ffn, norm_final, norm_mem, rel_bias, ev_w_in, ev_conv_w, s5_a_re, s5_a_im, s5_log_dt, s5_b_re, s5_b_im, s5_c_re, s5_c_im, s5_d, s5_glu_w, ev_w_out, od_w_qkv, od_b_qkv, od_sinks, od_w_out, xa_w_q, xa_w_kv, xa_w_o, ff_w_gate, ff_w_up, ff_conv_w, ff_conv_b, ff_w_down, loss_target, m_norm_mix, m_norm_xattn, m_norm_ffn, m_norm_final, m_norm_mem, m_rel_bias, m_ev_w_in, m_ev_conv_w, m_s5_a_re, m_s5_a_im, m_s5_log_dt, m_s5_b_re, m_s5_b_im, m_s5_c_re, m_s5_c_im, m_s5_d, m_s5_glu_w, m_ev_w_out, m_od_w_qkv, m_od_b_qkv, m_od_sinks, m_od_w_out, m_xa_w_q, m_xa_w_kv, m_xa_w_o, m_ff_w_gate, m_ff_w_up, m_ff_conv_w, m_ff_conv_b, m_ff_w_down, v_norm_mix, v_norm_xattn, v_norm_ffn, v_norm_final, v_norm_mem, v_rel_bias, v_ev_w_in, v_ev_conv_w, v_s5_a_re, v_s5_a_im, v_s5_log_dt, v_s5_b_re, v_s5_b_im, v_s5_c_re, v_s5_c_im, v_s5_d, v_s5_glu_w, v_ev_w_out, v_od_w_qkv, v_od_b_qkv, v_od_sinks, v_od_w_out, v_xa_w_q, v_xa_w_kv, v_xa_w_o, v_ff_w_gate, v_ff_w_up, v_ff_conv_w, v_ff_conv_b, v_ff_w_down):
    names = ["norm_mix", "norm_xattn", "norm_ffn", "norm_final", "norm_mem", "rel_bias", "ev_w_in", "ev_conv_w",
             "s5_a_re", "s5_a_im", "s5_log_dt", "s5_b_re", "s5_b_im", "s5_c_re", "s5_c_im", "s5_d", "s5_glu_w",
             "ev_w_out", "od_w_qkv", "od_b_qkv", "od_sinks", "od_w_out", "xa_w_q", "xa_w_kv", "xa_w_o",
             "ff_w_gate", "ff_w_up", "ff_conv_w", "ff_conv_b", "ff_w_down"]
    env = dict(locals())
    W = {k: env[k] for k in names}
    Mo = {k: env["m_" + k] for k in names}
    Vo = {k: env["v_" + k] for k in names}

    h = x[0]
    target = loss_target[0]
    L, D = h.shape
    depth = norm_mix.shape[0]
    c_idx = lax.axis_index("c").astype(jnp.int32).reshape(1)
    me_idx = (2 * lax.axis_index("x") + lax.axis_index("y")).astype(jnp.int32).reshape(1)

    col_sharded = ["ev_w_in", "od_w_qkv", "xa_w_kv", "ff_w_gate", "ff_w_up"]
    row_sharded = ["ev_w_out", "od_w_out", "xa_w_q", "xa_w_o", "ff_w_down"]
    small_sharded = ["ev_conv_w", "od_b_qkv", "ff_conv_w"]
    big = col_sharded + row_sharded

    def layer_weights(l):
        mix = [("ev_w_in", l // 2), ("ev_w_out", l // 2)] if l % 2 == 0 else [("od_w_qkv", l // 2), ("od_w_out", l // 2)]
        return mix + [(k, l) for k in ("xa_w_q", "xa_w_kv", "xa_w_o", "ff_w_gate", "ff_w_up", "ff_w_down")]

    gjob = {kl: GatherJob(W[kl[0]][kl[1]].astype(BF16)) for l in range(depth) for kl in layer_weights(l)}
    small_jobs = [GatherJob(W[k]) for k in small_sharded]
    pending = [gjob[kl] for l in range(depth) for kl in layer_weights(l)]
    run_comm([pending.pop(0)] + small_jobs, name="gather_first")
    flushes = []

    def take(queue, host_cost):
        jobs, acc = [], 0.0
        while queue and acc + queue[0].cost <= 1.0 * host_cost:
            acc += queue[0].cost
            jobs.append(queue.pop(0))
        return jobs

    def fwd_host(fn, units, *args, **kw):
        return fn(*args, comm=take(pending, units), **kw)

    def weight(k, l):
        job = gjob[(k, l)]
        if job.result is None:
            n = pending.index(job) + 1
            run_comm(pending[:n], name="gather_flush_%d" % len(flushes))
            flushes.append(n)
            del pending[:n]
        return job.result[0]

    def wcol(k, l):
        return weight(k, l)

    def wrow(k, l):
        g = weight(k, l)
        return g.reshape(1, g.shape[0] * g.shape[1], g.shape[2])

    def fwd_mm(a, w3, **kw):
        return mm_nn(a, w3, comm=take(pending, w3.size * a.shape[0] / L), **kw)

    sg = [j.result[0] for j in small_jobs]
    ev_conv_w_f = sg[0].transpose(1, 2, 0, 3).reshape(ev_conv_w.shape[0], 3, -1)
    od_b_qkv_f = sg[1].transpose(1, 0, 2).reshape(od_b_qkv.shape[0], 1, -1)
    ff_conv_w_f = sg[2].transpose(1, 2, 0, 3).reshape(ff_conv_w.shape[0], 3, -1)

    buckets = _bucket_table()
    NQ = D // HEAD_DIM
    NKV = NQ // Q_PER_KV
    onehot = jnp.asarray((buckets[:, None] == np.arange(N_BUCKETS)[None, :]).astype(np.float32))
    bias_tab = jnp.dot(rel_bias.T, onehot.T, precision=lax.Precision.HIGHEST).reshape(NQ, WINDOW, 2 * WINDOW)
    bias2 = _masked_bias(bias_tab)

    mem_n = rms_fwd(mem[0], norm_mem.reshape(1, D), name="rms_fwd_mem")

    saved = []
    for l in range(depth):
        i = l // 2
        s = {"h0": h}
        hn = rms_fwd(h, norm_mix[l].reshape(1, D))
        s["hn"] = hn
        if l % 2 == 0:
            A = ev_conv_w_f.shape[-1]
            z = fwd_mm(hn, wcol("ev_w_in", i), name="mm_ev_in")
            ya = conv_mixer_fwd(z, ev_conv_w_f[i])
            prep = functools.partial(_s5_prep, seg_len=L // N_SEG)
            s5p = (s5_a_re[i], s5_a_im[i], s5_log_dt[i], s5_b_re[i], s5_b_im[i], s5_c_re[i], s5_c_im[i], s5_d[i], s5_glu_w[i])
            (a_l, apow_l, bm, cm, d_l, glu), prep_vjp = jax.vjp(prep, *s5p)
            bm16, cm16, glu16 = bm.astype(BF16), cm.astype(BF16), glu.astype(BF16)
            u_p = _to_segments(z[:, 3 * A:])
            fin = fwd_host(s5_scan_fwd, HOST_UNITS["s5_ends"] * u_p.size,
                           u_p, a_l, apow_l, bm16, cm16, d_l, glu16, None, name="s5_fwd_ends")
            ys_p, y_p, states, starts = fwd_host(s5_scan_fwd, HOST_UNITS["s5_fwd"] * u_p.size,
                                                 u_p, a_l, apow_l, bm16, cm16, d_l, glu16, fin, name="s5_fwd")
            ycat = jnp.concatenate([ya, _from_segments(ys_p)], axis=1)
            s.update(z=z, u_p=u_p, y_p=y_p, states=states, starts=starts, ycat=ycat, prep_vjp=prep_vjp,
                     s5ops=(a_l, apow_l, bm16, cm16, d_l, glu16))
            h = fwd_mm(ycat, wrow("ev_w_out", i), res=h, name="mm_ev_out")
        else:
            z = fwd_mm(hn, wcol("od_w_qkv", i), bias=od_b_qkv_f[i], out_dtype=BF16, name="mm_od_qkv")
            qT = z[:, :NQ * HEAD_DIM].reshape(L, NQ, HEAD_DIM).transpose(1, 0, 2)
            kT = z[:, NQ * HEAD_DIM:(NQ + NKV) * HEAD_DIM].reshape(L, NKV, HEAD_DIM).transpose(1, 0, 2)
            vT = z[:, (NQ + NKV) * HEAD_DIM:].reshape(L, NKV, HEAD_DIM).transpose(1, 0, 2)
            sink_col = jnp.repeat(od_sinks[i], WINDOW).reshape(NQ * WINDOW, 1)
            oT = fwd_host(swa_fwd, HOST_UNITS["swa_fwd"] * qT.size, qT, kT, vT, bias2, sink_col)
            o = oT.transpose(1, 0, 2).reshape(L, D)
            s.update(qT=qT, kT=kT, vT=vT, sink_col=sink_col, o=o)
            h = fwd_mm(o, wrow("od_w_out", i), res=h, name="mm_od_out")
        s["h1"] = h
        hn2 = rms_fwd(h, norm_xattn[l].reshape(1, D))
        q = fwd_mm(hn2, wrow("xa_w_q", l), out_dtype=BF16, name="mm_xa_q")
        kv = fwd_mm(mem_n, wcol("xa_w_kv", l), out_dtype=BF16, name="mm_xa_kv")
        ox = xattn_fwd(q, kv)
        s.update(hn2=hn2, q=q, kv=kv, ox=ox)
        h = fwd_mm(ox, wrow("xa_w_o", l), res=h, name="mm_xa_o")
        s["h2"] = h
        hn3 = rms_fwd(h, norm_ffn[l].reshape(1, D))
        gpre = fwd_mm(hn3, wcol("ff_w_gate", l), out_dtype=BF16, name="mm_ff_gate")
        up = fwd_mm(hn3, wcol("ff_w_up", l), out_dtype=BF16, name="mm_ff_up")
        act = fwd_host(ffn_act_fwd, HOST_UNITS["ffn_act_fwd"] * gpre.size,
                       gpre, up, ff_conv_w_f[l], ff_conv_b[l].reshape(1, -1))
        s.update(hn3=hn3, gpre=gpre, up=up, act=act)
        h = fwd_mm(act, wrow("ff_w_down", l), res=h, name="mm_ff_down")
        saved.append(s)

    loss11, dh, dg_final, dh16 = final_loss(h, norm_final.reshape(1, D), target)
    loss = lax.psum(loss11[0, 0], AXES)

    gs = {k: [None] * W[k].shape[0] for k in names if k not in big and W[k].ndim > 1 and k != "rel_bias"}
    dmem_n = None
    dbias_tab = jnp.zeros_like(bias_tab)
    place_idx = jnp.concatenate([me_idx, c_idx])
    gbufs, exchanges, scatters = {}, [], []

    def settle(jobs):
        for j in jobs:
            k, l = j.tag
            if isinstance(j, ExchangeJob):
                nxt = ScatterJob(add_half(j.ops[0], j.result[0], c_idx))
                nxt.tag = j.tag
                scatters.append(nxt)
            else:
                p = j.ops[0]
                if k not in gbufs:
                    gbufs[k] = jnp.zeros((W[k].shape[0], 2 * p.shape[1], p.shape[2]), F32)
                gbufs[k] = add_partials(p, j.result[0], place_idx, gbufs[k], l)

    def take_fit(queue, budget):
        jobs = []
        for j in list(queue):
            if j.cost <= budget:
                budget -= j.cost
                jobs.append(j)
                queue.remove(j)
        return jobs

    def bwd_host(fn, units, *args, **kw):
        jobs = take_fit(exchanges, 0.6 * units) + take_fit(scatters, 1.2 * units)
        out = fn(*args, comm=jobs, **kw)
        settle(jobs)
        return out

    def dx_mm(a, w3, **kw):
        return bwd_host(mm_nt, w3.size * a.shape[0] / L, a, w3, **kw)

    def dw_mm(k, l, xx, dy, S, **kw):
        g3 = bwd_host(mm_tn, xx.shape[1] * dy.shape[1] * xx.shape[0] / L, xx, dy, S, **kw)
        if S == 1:
            g3 = g3.reshape(N_CHIPS, g3.shape[1] // N_CHIPS, g3.shape[2])
        job = ExchangeJob(g3)
        job.tag = (k, l)
        exchanges.append(job)

    for l in reversed(range(depth)):
        i = l // 2
        s = saved[l]
        dact = dx_mm(dh16, wrow("ff_w_down", l), out_dtype=BF16, name="mm_ff_down_dx")
        dw_mm("ff_w_down", l, s["act"], dh16, 1, name="mm_ff_down_dw")
        dgpre, dup, dcw, dcb = bwd_host(ffn_act_bwd, HOST_UNITS["ffn_act_bwd"] * dact.size,
                                        dact, s["gpre"], s["up"], ff_conv_w_f[l], ff_conv_b[l].reshape(1, -1))
        gs["ff_conv_w"][l], gs["ff_conv_b"][l] = dcw, dcb[0]
        dhn3 = dx_mm(dgpre, wcol("ff_w_gate", l), name="mm_ff_gate_dx")
        dhn3 = dx_mm(dup, wcol("ff_w_up", l), res=dhn3, name="mm_ff_up_dx")
        dw_mm("ff_w_gate", l, s["hn3"], dgpre, N_CHIPS, name="mm_ff_gate_dw")
        dw_mm("ff_w_up", l, s["hn3"], dup, N_CHIPS, name="mm_ff_up_dw")
        dh, dg, dh16 = bwd_host(rms_bwd, HOST_UNITS["rms_bwd"] * dh.size, s["h2"], norm_ffn[l].reshape(1, D), dhn3, dh)
        gs["norm_ffn"][l] = dg[0]
        dox = dx_mm(dh16, wrow("xa_w_o", l), out_dtype=BF16, name="mm_xa_o_dx")
        dw_mm("xa_w_o", l, s["ox"], dh16, 1, name="mm_xa_o_dw")
        dq, dkv = xattn_bwd(s["q"], s["kv"], dox)
        dhn2 = dx_mm(dq, wrow("xa_w_q", l), name="mm_xa_q_dx")
        dw_mm("xa_w_q", l, s["hn2"], dq, 1, name="mm_xa_q_dw")
        dw_mm("xa_w_kv", l, mem_n, dkv, N_CHIPS, name="mm_xa_kv_dw")
        dmem_n = dx_mm(dkv, wcol("xa_w_kv", l), res=dmem_n, name="mm_xa_kv_dx")
        dh, dg, dh16 = bwd_host(rms_bwd, HOST_UNITS["rms_bwd"] * dh.size, s["h1"], norm_xattn[l].reshape(1, D), dhn2, dh)
        gs["norm_xattn"][l] = dg[0]
        if l % 2 == 0:
            A = ev_conv_w_f.shape[-1]
            dycat = dx_mm(dh16, wrow("ev_w_out", i), name="mm_ev_out_dx")
            dw_mm("ev_w_out", i, s["ycat"], dh16, 1, name="mm_ev_out_dw")
            dgb, dgc, dxa, dcw = conv_mixer_bwd(s["z"], ev_conv_w_f[i], dycat[:, :A])
            gs["ev_conv_w"][i] = dcw
            a_l, apow_l, bm16, cm16, d_l, glu16 = s["s5ops"]
            dys_p = _to_segments(dycat[:, A:])
            dy_p, dglu, dd = bwd_host(s5_out_bwd, HOST_UNITS["s5_out_bwd"] * dys_p.size, dys_p, s["y_p"], s["u_p"], glu16)
            cmt = cm16.transpose(0, 2, 1)
            bmt = bm16.transpose(0, 2, 1)
            gfin = bwd_host(s5_scan_bwd, HOST_UNITS["s5_ends"] * dy_p.size, dy_p, a_l, apow_l, cmt, None, name="s5_bwd_ends")
            du_p, da, dbm, dcm = bwd_host(s5_scan_bwd, HOST_UNITS["s5_bwd"] * dy_p.size, dy_p, a_l, apow_l, cmt, gfin,
                                          s["states"], s["starts"], s["u_p"], bmt, d_l, name="s5_bwd")
            dprm = s["prep_vjp"]((da, jnp.zeros_like(apow_l), dbm, dcm, dd, dglu))
            for k, g in zip(["s5_a_re", "s5_a_im", "s5_log_dt", "s5_b_re", "s5_b_im", "s5_c_re", "s5_c_im", "s5_d", "s5_glu_w"], dprm):
                gs[k][i] = g
            dz = jnp.concatenate([dgb, dgc, dxa, _from_segments(du_p)], axis=1)
            dhn = dx_mm(dz, wcol("ev_w_in", i), name="mm_ev_in_dx")
            dw_mm("ev_w_in", i, s["hn"], dz, N_CHIPS, name="mm_ev_in_dw")
        else:
            do = dx_mm(dh16, wrow("od_w_out", i), out_dtype=BF16, name="mm_od_out_dx")
            dw_mm("od_w_out", i, s["o"], dh16, 1, name="mm_od_out_dw")
            doT = do.reshape(L, NQ, HEAD_DIM).transpose(1, 0, 2)
            dqT, dkT, dvT, dbias, dsink = bwd_host(swa_bwd, HOST_UNITS["swa_bwd"] * doT.size,
                                                   s["qT"], s["kT"], s["vT"], bias2, s["sink_col"], doT)
            dbias_tab = dbias_tab + dbias
            gs["od_sinks"][i] = jnp.sum(dsink.reshape(NQ, WINDOW), axis=1)
            dz = jnp.concatenate([dqT.transpose(1, 0, 2).reshape(L, NQ * HEAD_DIM),
                                  dkT.astype(BF16).transpose(2, 0, 1).reshape(L, NKV * HEAD_DIM),
                                  dvT.astype(BF16).transpose(2, 0, 1).reshape(L, NKV * HEAD_DIM)], axis=1)
            gs["od_b_qkv"][i] = col_sum(dz)[0]
            dhn = dx_mm(dz, wcol("od_w_qkv", i), name="mm_od_qkv_dx")
            dw_mm("od_w_qkv", i, s["hn"], dz, N_CHIPS, name="mm_od_qkv_dw")
        dh, dg, dh16 = bwd_host(rms_bwd, HOST_UNITS["rms_bwd"] * dh.size, s["h0"], norm_mix[l].reshape(1, D), dhn, dh)
        gs["norm_mix"][l] = dg[0]

    grad_x = dh[None]
    _, dg_mem, _ = rms_bwd(mem[0], norm_mem.reshape(1, D), dmem_n, jnp.zeros_like(dmem_n), name="rms_bwd_mem")
    d_rel_bias = jnp.dot(dbias_tab.reshape(NQ, -1), onehot, precision=lax.Precision.HIGHEST).T

    small = [k for k in names if k not in big]
    local_small = {k: (jnp.stack(gs[k]) if k in gs else None) for k in small}
    local_small["norm_final"] = dg_final[0]
    local_small["norm_mem"] = dg_mem[0]
    local_small["rel_bias"] = d_rel_bias
    full_shape = {k: W[k].shape for k in small}
    for k in small_sharded:
        full_shape[k] = local_small[k].shape
    lst = [local_small[k].reshape(full_shape[k]).astype(F32) for k in small]
    packed = _pack(lst, PACK_ROWS)
    m_rows = packed.shape[0]
    summed = sum_blocks(all_gather_rows(packed).reshape(8, m_rows, LANES))
    gsum = dict(zip(small, _unpack(summed, lst)))
    for k in small_sharded:
        n4 = W[k].shape[-1]
        gsum[k] = lax.dynamic_slice_in_dim(gsum[k], me_idx[0] * n4, n4, axis=gsum[k].ndim - 1)

    for queue, name in ((exchanges, "rs_exchange_rest"), (scatters, "rs_scatter_rest")):
        jobs = queue[:]
        del queue[:]
        if jobs:
            run_comm(jobs, name=name)
            settle(jobs)
    joined = join_halves([gbufs[k] for k in big])
    gbig = {k: g.reshape(W[k].shape) for k, g in zip(big, joined)}

    grads = {**gsum, **gbig}
    delta, new_m, new_v = {}, {}, {}
    for k in big:
        d_, m_, v_ = adamw(_rows2d(W[k]), _rows2d(grads[k]), _rows2d(Mo[k]), _rows2d(Vo[k]), name="adamw_" + k)
        delta[k], new_m[k], new_v[k] = d_.reshape(W[k].shape), m_.reshape(W[k].shape), v_.reshape(W[k].shape)
    sw = [W[k] for k in small]
    d_, m_, v_ = adamw(_pack(sw, PACK_ROWS), _pack([grads[k] for k in small], PACK_ROWS),
                       _pack([Mo[k] for k in small], PACK_ROWS), _pack([Vo[k] for k in small], PACK_ROWS), name="adamw_small")
    for k, a, b, c_ in zip(small, _unpack(d_, sw), _unpack(m_, sw), _unpack(v_, sw)):
        delta[k], new_m[k], new_v[k] = a, b, c_

    return (loss, grad_x, *[grads[k] for k in names], *[delta[k] for k in names],
            *[new_m[k] for k in names], *[new_v[k] for k in names])
```

```python
import functools
import math

import numpy as np
import jax
import jax.numpy as jnp
from jax import lax
from jax.experimental import pallas as pl
from jax.experimental.pallas import tpu as pltpu

F32, BF16 = jnp.float32, jnp.bfloat16
MESH = pl.DeviceIdType.MESH
AXES = ("x", "y", "c")

VMEM_LIMIT_BYTES = 56 * 2**20
SUBLANES, LANES = 8, 128

RMS_EPS = 1e-5
S5_GROUP, S5_STATE = 16, 64
HEAD_DIM, Q_PER_KV, WINDOW = 64, 8, 128
N_BUCKETS, MAX_DISTANCE = 32, 128
X_HEADS = 4
NEG_INF = -1e30
ADAM_LR, ADAM_B1, ADAM_B2, ADAM_EPS, ADAM_WD, ADAM_STEP = 0.001, 0.9, 0.999, 1e-08, 0.01, 10
N_CHIPS = 4
N_SEG = 8
S5_STEPS = 32
HOST_UNITS = {"rms_bwd": 0.57, "ffn_act_fwd": 0.39, "ffn_act_bwd": 0.63, "swa_fwd": 1.2, "swa_bwd": 2.2,
              "s5_fwd": 3.8, "s5_ends": 2.3, "s5_bwd": 4.5, "s5_out_bwd": 1.4}


def _params(sem=None):
    return pltpu.CompilerParams(dimension_semantics=sem, vmem_limit_bytes=VMEM_LIMIT_BYTES)


def _vspec(shape, index_map):
    return pl.BlockSpec(shape, index_map)


ANY = pl.BlockSpec(memory_space=pl.ANY)


def _tile(n, pref):
    t = (min(pref, n) // LANES) * LANES
    while t >= LANES:
        if n % t == 0:
            return t
        t -= LANES
    return n


def _acc_matmul(nk, k, acc, partial, finish):
    if nk == 1:
        finish(partial())
        return

    @pl.when(k == 0)
    def _():
        acc[...] = partial()

    @pl.when(jnp.logical_and(k > 0, k < nk - 1))
    def _():
        acc[...] += partial()

    @pl.when(k == nk - 1)
    def _():
        finish(acc[...] + partial())


MM_MAX_K = 2048


def hosted_call(body, *, name, grid, in_specs, out_specs, out_shape, ops, scratch_shapes=(), semantics, comm=()):
    if not comm:
        return pl.pallas_call(body, name=name, grid=grid, in_specs=in_specs, out_specs=out_specs, out_shape=out_shape,
                              scratch_shapes=list(scratch_shapes), compiler_params=_params(semantics))(*ops)
    n_in, n_out, n_scr = len(ops), len(out_shape), len(scratch_shapes)
    cops = [a for j in comm for a in j.ops]
    couts = [s for j in comm for s in j.outs]
    nsem = sum(j.nsem for j in comm)

    def hosted(*refs):
        ins, refs = refs[:n_in], refs[n_in:]
        cin, refs = refs[:len(cops)], refs[len(cops):]
        outs, refs = refs[:n_out], refs[n_out:]
        cout, refs = refs[:len(couts)], refs[len(couts):]
        scr, (send, recv) = refs[:n_scr], refs[n_scr:]

        def each(phase):
            ii = io = base = 0
            for j in comm:
                getattr(j, phase)(cin[ii:ii + len(j.ops)], cout[io:io + len(j.outs)], send, recv, base)
                ii, io, base = ii + len(j.ops), io + len(j.outs), base + j.nsem

        pids = [pl.program_id(d) for d in range(len(grid))]
        first = functools.reduce(jnp.logical_and, [p == 0 for p in pids])
        last = functools.reduce(jnp.logical_and, [p == g - 1 for p, g in zip(pids, grid)])
        pl.when(first)(lambda: each("start"))
        body(*ins, *outs, *scr)
        pl.when(last)(lambda: each("finish"))

    res = pl.pallas_call(
        hosted, name=name, grid=grid, in_specs=list(in_specs) + [ANY] * len(cops),
        out_specs=list(out_specs) + [ANY] * len(couts), out_shape=list(out_shape) + couts,
        scratch_shapes=list(scratch_shapes) + [pltpu.SemaphoreType.DMA((nsem,)), pltpu.SemaphoreType.DMA((nsem,))],
        compiler_params=pltpu.CompilerParams(dimension_semantics=("arbitrary",) * len(grid),
                                             vmem_limit_bytes=VMEM_LIMIT_BYTES, has_side_effects=True),
    )(*ops, *cops)
    io = n_out
    for j in comm:
        j.result = list(res[io:io + len(j.outs)])
        io += len(j.outs)
    return list(res[:n_out])


def _pcall(body, *, name, grid, in_specs, out_specs, out_shape, scratch_shapes=(), sem, comm=()):
    single = not isinstance(out_shape, (list, tuple))

    def run(*ops):
        res = hosted_call(body, name=name, grid=grid, in_specs=list(in_specs),
                          out_specs=[out_specs] if single else list(out_specs),
                          out_shape=[out_shape] if single else list(out_shape), ops=list(ops),
                          scratch_shapes=scratch_shapes, semantics=sem, comm=comm)
        return res[0] if single else res

    return run


def _mm_call(core, grid, in_specs, ops, out_spec, out_shape, acc_shape, name, comm):
    n_in = len(ops)

    def body(*refs):
        core(refs[:n_in], refs[n_in], refs[n_in + 1])

    return hosted_call(body, name=name, grid=grid, in_specs=in_specs, out_specs=[out_spec], out_shape=[out_shape],
                       ops=ops, scratch_shapes=[pltpu.VMEM(acc_shape, F32)],
                       semantics=("parallel", "parallel", "arbitrary"), comm=comm)[0]


def mm_nn(a, w3, *, bias=None, res=None, out_dtype=F32, name, comm=()):
    M, K = a.shape
    S, K2, ns = w3.shape
    assert K == K2
    tm = _tile(M, 512)
    tk = K if K <= MM_MAX_K else _tile(K, 1536)
    nk = K // tk
    has_b, has_r = bias is not None, res is not None

    def core(ins, o_ref, acc):
        a_ref, w_ref = ins[0], ins[1]
        b_ref = ins[2] if has_b else None
        r_ref = ins[2 + has_b] if has_r else None

        def partial():
            return jnp.dot(a_ref[...].astype(BF16), w_ref[...], preferred_element_type=F32)

        def finish(r):
            if has_b:
                r = r + b_ref[...]
            if has_r:
                r = r + r_ref[...]
            o_ref[...] = r.astype(o_ref.dtype)

        _acc_matmul(nk, pl.program_id(2), acc, partial, finish)

    in_specs = [_vspec((tm, tk), lambda s, i, k: (i, k)), _vspec((None, tk, ns), lambda s, i, k: (s, k, 0))]
    ops = [a, w3]
    if has_b:
        in_specs.append(_vspec((1, ns), lambda s, i, k: (0, s)))
        ops.append(bias)
    if has_r:
        in_specs.append(_vspec((tm, ns), lambda s, i, k: (i, s)))
        ops.append(res)
    return _mm_call(core, (S, M // tm, nk), in_specs, ops, _vspec((tm, ns), lambda s, i, k: (i, s)),
                    jax.ShapeDtypeStruct((M, S * ns), out_dtype), (tm, ns) if nk > 1 else (SUBLANES, LANES), name, comm)


def mm_nt(a, w3, *, res=None, out_dtype=F32, name, comm=()):
    M, N = a.shape
    S, K, ns = w3.shape
    assert N == S * ns
    tko = K if K <= MM_MAX_K else _tile(K, 1024)
    tm = _tile(M, 512 if tko == K else 1024)
    tc = ns if ns <= MM_MAX_K else _tile(ns, 1024)
    ncs = ns // tc
    nc = S * ncs
    has_r = res is not None

    def core(ins, o_ref, acc):
        a_ref, w_ref = ins[0], ins[1]
        r_ref = ins[2] if has_r else None

        def partial():
            return lax.dot_general(a_ref[...].astype(BF16), w_ref[...], (((1,), (1,)), ((), ())),
                                   preferred_element_type=F32)

        def finish(r):
            if has_r:
                r = r + r_ref[...]
            o_ref[...] = r.astype(o_ref.dtype)

        _acc_matmul(nc, pl.program_id(2), acc, partial, finish)

    in_specs = [_vspec((tm, tc), lambda i, j, k: (i, k)),
                _vspec((None, tko, tc), lambda i, j, k: (k // ncs, j, k % ncs))]
    ops = [a, w3]
    if has_r:
        in_specs.append(_vspec((tm, tko), lambda i, j, k: (i, j)))
        ops.append(res)
    return _mm_call(core, (M // tm, K // tko, nc), in_specs, ops, _vspec((tm, tko), lambda i, j, k: (i, j)),
                    jax.ShapeDtypeStruct((M, K), out_dtype), (tm, tko) if nc > 1 else (SUBLANES, LANES), name, comm)


def mm_tn(x, dy, S, *, out_dtype=BF16, name, comm=()):
    M, K = x.shape
    M2, N = dy.shape
    assert M == M2 and N % S == 0
    ns = N // S
    tk = _tile(K, 1536) if (K > MM_MAX_K and dy.dtype == BF16) else _tile(K, 1024)
    tmc = _tile(M, 512 if tk > 1024 else 1024)
    nm = M // tmc

    def core(ins, o_ref, acc):
        x_ref, dy_ref = ins

        def partial():
            return lax.dot_general(x_ref[...].astype(BF16), dy_ref[...].astype(BF16), (((0,), (0,)), ((), ())),
                                   preferred_element_type=F32)

        def finish(r):
            o_ref[...] = r.astype(o_ref.dtype)

        _acc_matmul(nm, pl.program_id(2), acc, partial, finish)

    in_specs = [_vspec((tmc, tk), lambda s, i, m: (m, i)), _vspec((tmc, ns), lambda s, i, m: (m, s))]
    return _mm_call(core, (S, K // tk, nm), in_specs, [x, dy], _vspec((None, tk, ns), lambda s, i, m: (s, i, 0)),
                    jax.ShapeDtypeStruct((S, K, ns), out_dtype), (tk, ns) if nm > 1 else (SUBLANES, LANES), name, comm)


def rms_fwd(h, g, *, name="rms_fwd"):
    R, D = h.shape
    tr = _tile8(R, 256)

    def body(h_ref, g_ref, o_ref):
        x = h_ref[...]
        r = lax.rsqrt(jnp.mean(x * x, axis=-1, keepdims=True) + RMS_EPS)
        o_ref[...] = (x * r * g_ref[...]).astype(o_ref.dtype)

    return pl.pallas_call(
        body, name=name, grid=(R // tr,),
        in_specs=[_vspec((tr, D), lambda i: (i, 0)), _vspec((1, D), lambda i: (0, 0))],
        out_specs=_vspec((tr, D), lambda i: (i, 0)),
        out_shape=jax.ShapeDtypeStruct((R, D), BF16),
        compiler_params=_params(("parallel",)),
    )(h, g)


def _tile8(n, pref):
    t = (min(pref, n) // SUBLANES) * SUBLANES
    while t >= SUBLANES:
        if n % t == 0:
            return t
        t -= SUBLANES
    return n


def rms_bwd(h, g, dhn, dres, *, name="rms_bwd", comm=()):
    R, D = h.shape
    tr = _tile8(R, 256)

    def body(h_ref, g_ref, dhn_ref, dres_ref, dh_ref, dg_ref, dh16_ref):
        @pl.when(pl.program_id(0) == 0)
        def _():
            dg_ref[...] = jnp.zeros_like(dg_ref)

        x = h_ref[...]
        d = dhn_ref[...].astype(F32)
        r = lax.rsqrt(jnp.mean(x * x, axis=-1, keepdims=True) + RMS_EPS)
        xhat = x * r
        dg_ref[...] += jnp.sum(d * xhat, axis=0, keepdims=True)
        t = d * g_ref[...]
        dh = dres_ref[...] + r * (t - xhat * jnp.mean(t * xhat, axis=-1, keepdims=True))
        dh_ref[...] = dh
        dh16_ref[...] = dh.astype(BF16)

    return _pcall(
        body, name=name, grid=(R // tr,),
        in_specs=[_vspec((tr, D), lambda i: (i, 0)), _vspec((1, D), lambda i: (0, 0)),
                  _vspec((tr, D), lambda i: (i, 0)), _vspec((tr, D), lambda i: (i, 0))],
        out_specs=[_vspec((tr, D), lambda i: (i, 0)), _vspec((1, D), lambda i: (0, 0)), _vspec((tr, D), lambda i: (i, 0))],
        out_shape=[jax.ShapeDtypeStruct((R, D), F32), jax.ShapeDtypeStruct((1, D), F32), jax.ShapeDtypeStruct((R, D), BF16)],
        sem=("arbitrary",), comm=comm,
    )(h, g, dhn, dres)


def final_loss(h, g, target, *, name="final_loss"):
    R, D = h.shape
    tr = _tile8(R, 256)

    def body(h_ref, g_ref, t_ref, loss_ref, dh_ref, dg_ref, dh16_ref):
        @pl.when(pl.program_id(0) == 0)
        def _():
            dg_ref[...] = jnp.zeros_like(dg_ref)
            loss_ref[...] = jnp.zeros_like(loss_ref)

        x = h_ref[...]
        r = lax.rsqrt(jnp.mean(x * x, axis=-1, keepdims=True) + RMS_EPS)
        xhat = x * r
        err = xhat * g_ref[...] - t_ref[...]
        row = jnp.mean(err * err, axis=-1, keepdims=True)
        loss_ref[...] += 0.5 * jnp.sum(row, axis=0, keepdims=True)
        d = err * (1.0 / D)
        dg_ref[...] += jnp.sum(d * xhat, axis=0, keepdims=True)
        t = d * g_ref[...]
        dh = r * (t - xhat * jnp.mean(t * xhat, axis=-1, keepdims=True))
        dh_ref[...] = dh
        dh16_ref[...] = dh.astype(BF16)

    return pl.pallas_call(
        body, name=name, grid=(R // tr,),
        in_specs=[_vspec((tr, D), lambda i: (i, 0)), _vspec((1, D), lambda i: (0, 0)), _vspec((tr, D), lambda i: (i, 0))],
        out_specs=[_vspec((1, 1), lambda i: (0, 0)), _vspec((tr, D), lambda i: (i, 0)), _vspec((1, D), lambda i: (0, 0)),
                   _vspec((tr, D), lambda i: (i, 0))],
        out_shape=[jax.ShapeDtypeStruct((1, 1), F32), jax.ShapeDtypeStruct((R, D), F32), jax.ShapeDtypeStruct((1, D), F32),
                   jax.ShapeDtypeStruct((R, D), BF16)],
        compiler_params=_params(("arbitrary",)),
    )(h, g, target)


def _shift_down(v, k):
    rows = lax.broadcasted_iota(jnp.int32, v.shape, 0)
    return jnp.where(rows >= k, pltpu.roll(v, k, axis=0), 0.0)


def _shift_up(v, k):
    L = v.shape[0]
    rows = lax.broadcasted_iota(jnp.int32, v.shape, 0)
    return jnp.where(rows < L - k, pltpu.roll(v, L - k, axis=0), 0.0)


def _conv(v, w):
    return w[2:3, :] * v + w[1:2, :] * _shift_down(v, 1) + w[0:1, :] * _shift_down(v, 2)


def _conv_t(d, w):
    return w[2:3, :] * d + w[1:2, :] * _shift_up(d, 1) + w[0:1, :] * _shift_up(d, 2)


def _conv_dw(d, v):
    return jnp.concatenate([
        jnp.sum(d * _shift_down(v, 2), axis=0, keepdims=True),
        jnp.sum(d * _shift_down(v, 1), axis=0, keepdims=True),
        jnp.sum(d * v, axis=0, keepdims=True)], axis=0)


COL_BLOCK = 128


def conv_mixer_fwd(z, cw, *, name="conv_mixer_fwd"):
    L = z.shape[0]
    A = cw.shape[1]
    cb = _tile(A, COL_BLOCK)
    nb = A // cb

    def body(gb_ref, gc_ref, xa_ref, w_ref, o_ref):
        v = gc_ref[...] * xa_ref[...]
        o_ref[...] = (gb_ref[...] * _conv(v, w_ref[...])).astype(o_ref.dtype)

    return pl.pallas_call(
        body, name=name, grid=(nb,),
        in_specs=[_vspec((L, cb), lambda j: (0, j)), _vspec((L, cb), lambda j: (0, nb + j)),
                  _vspec((L, cb), lambda j: (0, 2 * nb + j)), _vspec((3, cb), lambda j: (0, j))],
        out_specs=_vspec((L, cb), lambda j: (0, j)),
        out_shape=jax.ShapeDtypeStruct((L, A), BF16),
        compiler_params=_params(("parallel",)),
    )(z, z, z, cw)


def conv_mixer_bwd(z, cw, dya, *, name="conv_mixer_bwd"):
    L = z.shape[0]
    A = cw.shape[1]
    cb = _tile(A, COL_BLOCK)
    nb = A // cb

    def body(gb_ref, gc_ref, xa_ref, w_ref, d_ref, dgb_ref, dgc_ref, dxa_ref, dw_ref):
        gc, xa, w, d = gc_ref[...], xa_ref[...], w_ref[...], d_ref[...]
        v = gc * xa
        dgb_ref[...] = (d * _conv(v, w)).astype(dgb_ref.dtype)
        dc = d * gb_ref[...]
        dw_ref[...] = _conv_dw(dc, v)
        dv = _conv_t(dc, w)
        dgc_ref[...] = (dv * xa).astype(dgc_ref.dtype)
        dxa_ref[...] = (dv * gc).astype(dxa_ref.dtype)

    col = lambda j: (0, j)
    outs = pl.pallas_call(
        body, name=name, grid=(nb,),
        in_specs=[_vspec((L, cb), col), _vspec((L, cb), lambda j: (0, nb + j)),
                  _vspec((L, cb), lambda j: (0, 2 * nb + j)), _vspec((3, cb), col), _vspec((L, cb), col)],
        out_specs=[_vspec((L, cb), col), _vspec((L, cb), col), _vspec((L, cb), col), _vspec((3, cb), col)],
        out_shape=[jax.ShapeDtypeStruct((L, A), BF16)] * 3 + [jax.ShapeDtypeStruct((3, A), F32)],
        compiler_params=_params(("parallel",)),
    )(z, z, z, cw, dya)
    return outs[0], outs[1], outs[2], outs[3]


def ffn_act_fwd(gpre, up, cw, cbias, *, name="ffn_act_fwd", comm=()):
    L, Fd = gpre.shape
    cb = _tile(Fd, COL_BLOCK)

    def body(g_ref, u_ref, w_ref, b_ref, o_ref):
        g = _conv(g_ref[...].astype(F32), w_ref[...]) + b_ref[...]
        o_ref[...] = (g * jax.nn.sigmoid(g) * u_ref[...].astype(F32)).astype(o_ref.dtype)

    col = lambda j: (0, j)
    return _pcall(
        body, name=name, grid=(Fd // cb,),
        in_specs=[_vspec((L, cb), col), _vspec((L, cb), col), _vspec((3, cb), col), _vspec((1, cb), col)],
        out_specs=_vspec((L, cb), col),
        out_shape=jax.ShapeDtypeStruct((L, Fd), BF16),
        sem=("parallel",), comm=comm,
    )(gpre, up, cw, cbias)


def ffn_act_bwd(dact, gpre, up, cw, cbias, *, name="ffn_act_bwd", comm=()):
    L, Fd = gpre.shape
    cb = _tile(Fd, COL_BLOCK)

    def body(d_ref, g_ref, u_ref, w_ref, b_ref, dg_ref, du_ref, dw_ref, db_ref):
        gp, w, d = g_ref[...].astype(F32), w_ref[...], d_ref[...].astype(F32)
        g = _conv(gp, w) + b_ref[...]
        sg = jax.nn.sigmoid(g)
        du_ref[...] = (d * (g * sg)).astype(du_ref.dtype)
        dg = d * u_ref[...].astype(F32) * (sg * (1.0 + g * (1.0 - sg)))
        db_ref[...] = jnp.sum(dg, axis=0, keepdims=True)
        dw_ref[...] = _conv_dw(dg, gp)
        dg_ref[...] = _conv_t(dg, w).astype(dg_ref.dtype)

    col = lambda j: (0, j)
    return _pcall(
        body, name=name, grid=(Fd // cb,),
        in_specs=[_vspec((L, cb), col)] * 3 + [_vspec((3, cb), col), _vspec((1, cb), col)],
        out_specs=[_vspec((L, cb), col), _vspec((L, cb), col), _vspec((3, cb), col), _vspec((1, cb), col)],
        out_shape=[jax.ShapeDtypeStruct((L, Fd), BF16), jax.ShapeDtypeStruct((L, Fd), BF16),
                   jax.ShapeDtypeStruct((3, Fd), F32), jax.ShapeDtypeStruct((1, Fd), F32)],
        sem=("parallel",), comm=comm,
    )(dact, gpre, up, cw, cbias)


def col_sum(x, *, name="col_sum"):
    R, C = x.shape
    tr = _tile8(R, 512)

    def body(x_ref, o_ref):
        @pl.when(pl.program_id(0) == 0)
        def _():
            o_ref[...] = jnp.zeros_like(o_ref)

        o_ref[...] += jnp.sum(x_ref[...].astype(F32), axis=0, keepdims=True)

    return pl.pallas_call(
        body, name=name, grid=(R // tr,),
        in_specs=[_vspec((tr, C), lambda i: (i, 0))], out_specs=_vspec((1, C), lambda i: (0, 0)),
        out_shape=jax.ShapeDtypeStruct((1, C), F32), compiler_params=_params(("arbitrary",)),
    )(x)


def _cmul_add(ar, ai, sr, si, br, bi):
    return ar * sr - ai * si + br, ar * si + ai * sr + bi


def _segment_starts(fin, pr, pi, reverse):
    H = fin.shape[1] // 2
    rows = lax.broadcasted_iota(jnp.int32, fin.shape, 0)
    cr = jnp.zeros((1, H), F32)
    ci = jnp.zeros((1, H), F32)
    out = jnp.zeros(fin.shape, F32)
    order = range(N_SEG - 1, -1, -1) if reverse else range(N_SEG)
    for k in order:
        out = jnp.where(rows == k, jnp.concatenate([cr, ci], axis=1), out)
        cr, ci = _cmul_add(pr, pi, cr, ci, fin[k:k + 1, :H], fin[k:k + 1, H:])
    return out


def _gelu(y):
    c0 = math.sqrt(2.0 / math.pi)
    t = jnp.tanh(c0 * (y + 0.044715 * y * y * y))
    return 0.5 * y * (1.0 + t), t


def s5_scan_fwd(u_p, a_l, apow_l, bm, cm, d_l, glu, fin, *, name, comm=()):
    L, C = u_p.shape
    NK, _, SW = bm.shape
    H = SW // 2
    RB = S5_STEPS * N_SEG
    NC = L // RB
    final_only = fin is None

    def scan_chunk(a_ref, buf, st):
        ar = jnp.broadcast_to(a_ref[:, :H], (N_SEG, H))
        ai = jnp.broadcast_to(a_ref[:, H:], (N_SEG, H))

        def step(j, carry):
            sr, si = carry
            rows = pl.ds(pl.multiple_of(j * N_SEG, N_SEG), N_SEG)
            sr, si = _cmul_add(ar, ai, sr, si, buf[rows, :H], buf[rows, H:])
            buf[rows, :H] = sr
            buf[rows, H:] = si
            return sr, si

        sr, si = lax.fori_loop(0, S5_STEPS, step, (st[:, :H], st[:, H:]), unroll=4)
        st[:, :H] = sr
        st[:, H:] = si

    if final_only:
        def body(u_ref, a_ref, bm_ref, fin_ref, buf, st):
            @pl.when(pl.program_id(1) == 0)
            def _():
                st[...] = jnp.zeros_like(st)

            buf[...] = jnp.dot(u_ref[...].astype(BF16), bm_ref[...], preferred_element_type=F32)
            scan_chunk(a_ref, buf, st)
            fin_ref[...] = st[...]

        return _pcall(
            body, name=name, grid=(NK, NC),
            in_specs=[_vspec((RB, LANES), lambda k, j: (j, k)), _vspec((None, 1, SW), lambda k, j: (k, 0, 0)),
                      _vspec((None, LANES, SW), lambda k, j: (k, 0, 0))],
            out_specs=_vspec((None, N_SEG, SW), lambda k, j: (k, 0, 0)),
            out_shape=jax.ShapeDtypeStruct((NK, N_SEG, SW), F32),
            scratch_shapes=[pltpu.VMEM((RB, SW), F32), pltpu.VMEM((N_SEG, SW), F32)],
            sem=("parallel", "arbitrary"), comm=comm,
        )(u_p, a_l, bm)

    def body(u_ref, a_ref, ap_ref, bm_ref, cm_ref, d_ref, glu_ref, fin_ref, o_ref, y_ref, s_ref, start_ref, buf, st):
        @pl.when(pl.program_id(1) == 0)
        def _():
            st[...] = _segment_starts(fin_ref[...], ap_ref[:, :H], ap_ref[:, H:], False)
            start_ref[...] = st[...]

        u = u_ref[...]
        buf[...] = jnp.dot(u.astype(BF16), bm_ref[...], preferred_element_type=F32)
        scan_chunk(a_ref, buf, st)
        states = buf[...]
        s_ref[...] = states
        y = jnp.dot(states.astype(BF16), cm_ref[...], preferred_element_type=F32) + d_ref[...] * u
        y_ref[...] = y
        yg, _ = _gelu(y)
        gate = jnp.dot(yg.astype(BF16), glu_ref[...], preferred_element_type=F32)
        o_ref[...] = (yg * jax.nn.sigmoid(gate)).astype(o_ref.dtype)

    blk = lambda k, j: (j, k)
    per_k = lambda k, j: (k, 0, 0)
    return _pcall(
        body, name=name, grid=(NK, NC),
        in_specs=[_vspec((RB, LANES), blk), _vspec((None, 1, SW), per_k), _vspec((None, 1, SW), per_k),
                  _vspec((None, LANES, SW), per_k), _vspec((None, SW, LANES), per_k), _vspec((1, LANES), lambda k, j: (0, k)),
                  _vspec((None, LANES, LANES), per_k), _vspec((None, N_SEG, SW), per_k)],
        out_specs=[_vspec((RB, LANES), blk), _vspec((RB, LANES), blk), _vspec((RB, SW), blk),
                   _vspec((None, N_SEG, SW), per_k)],
        out_shape=[jax.ShapeDtypeStruct((L, C), BF16), jax.ShapeDtypeStruct((L, C), F32),
                   jax.ShapeDtypeStruct((L, NK * SW), F32), jax.ShapeDtypeStruct((NK, N_SEG, SW), F32)],
        scratch_shapes=[pltpu.VMEM((RB, SW), F32), pltpu.VMEM((N_SEG, SW), F32)],
        sem=("parallel", "arbitrary"), comm=comm,
    )(u_p, a_l, apow_l, bm, cm, d_l, glu, fin)


def s5_out_bwd(dout_p, y_p, u_p, glu, *, name="s5_out_bwd", comm=()):
    L, C = y_p.shape
    NK = C // LANES
    tr = _tile8(L, 512)

    def body(do_ref, y_ref, u_ref, glu_ref, dy_ref, dglu_ref, dd_ref):
        @pl.when(pl.program_id(1) == 0)
        def _():
            dglu_ref[...] = jnp.zeros_like(dglu_ref)
            dd_ref[...] = jnp.zeros_like(dd_ref)

        y, do, w = y_ref[...], do_ref[...].astype(F32), glu_ref[...]
        yg, t = _gelu(y)
        sg = jax.nn.sigmoid(jnp.dot(yg.astype(BF16), w, preferred_element_type=F32))
        dgate = (do * yg * sg * (1.0 - sg)).astype(BF16)
        dyg = do * sg + lax.dot_general(dgate, w, (((1,), (1,)), ((), ())), preferred_element_type=F32)
        dglu_ref[...] += lax.dot_general(yg.astype(BF16), dgate, (((0,), (0,)), ((), ())), preferred_element_type=F32)
        c0 = math.sqrt(2.0 / math.pi)
        dgelu = 0.5 * (1.0 + t) + 0.5 * y * (1.0 - t * t) * c0 * (1.0 + 3.0 * 0.044715 * y * y)
        dy = dyg * dgelu
        dy_ref[...] = dy
        dd_ref[...] += jnp.sum(dy * u_ref[...], axis=0, keepdims=True)

    blk = lambda k, i: (i, k)
    return _pcall(
        body, name=name, grid=(NK, L // tr),
        in_specs=[_vspec((tr, LANES), blk)] * 3 + [_vspec((None, LANES, LANES), lambda k, i: (k, 0, 0))],
        out_specs=[_vspec((tr, LANES), blk), _vspec((None, LANES, LANES), lambda k, i: (k, 0, 0)),
                   _vspec((1, LANES), lambda k, i: (0, k))],
        out_shape=[jax.ShapeDtypeStruct((L, C), F32), jax.ShapeDtypeStruct((NK, LANES, LANES), F32),
                   jax.ShapeDtypeStruct((1, C), F32)],
        sem=("parallel", "arbitrary"), comm=comm,
    )(dout_p, y_p, u_p, glu)


def s5_scan_bwd(dy_p, a_l, apow_l, cmt, gfin, states=None, starts=None, u_p=None, bmt=None, d_l=None, *, name, comm=()):
    L, C = dy_p.shape
    NK, _, SW = cmt.shape
    H = SW // 2
    RB = S5_STEPS * N_SEG
    NC = L // RB
    final_only = gfin is None

    def scan_chunk(a_ref, buf, st):
        ar = jnp.broadcast_to(a_ref[:, :H], (N_SEG, H))
        ai = -jnp.broadcast_to(a_ref[:, H:], (N_SEG, H))

        def step(jj, carry):
            gr, gi = carry
            j = S5_STEPS - 1 - jj
            rows = pl.ds(pl.multiple_of(j * N_SEG, N_SEG), N_SEG)
            gr, gi = _cmul_add(ar, ai, gr, gi, buf[rows, :H], buf[rows, H:])
            buf[rows, :H] = gr
            buf[rows, H:] = gi
            return gr, gi

        gr, gi = lax.fori_loop(0, S5_STEPS, step, (st[:, :H], st[:, H:]), unroll=4)
        st[:, :H] = gr
        st[:, H:] = gi

    rblk = lambda k, j: (NC - 1 - j, k)
    per_k = lambda k, j: (k, 0, 0)

    if final_only:
        def body(dy_ref, a_ref, cmt_ref, fin_ref, buf, st):
            @pl.when(pl.program_id(1) == 0)
            def _():
                st[...] = jnp.zeros_like(st)

            buf[...] = jnp.dot(dy_ref[...].astype(BF16), cmt_ref[...], preferred_element_type=F32)
            scan_chunk(a_ref, buf, st)
            fin_ref[...] = st[...]

        return _pcall(
            body, name=name, grid=(NK, NC),
            in_specs=[_vspec((RB, LANES), rblk), _vspec((None, 1, SW), per_k), _vspec((None, LANES, SW), per_k)],
            out_specs=_vspec((None, N_SEG, SW), per_k),
            out_shape=jax.ShapeDtypeStruct((NK, N_SEG, SW), F32),
            scratch_shapes=[pltpu.VMEM((RB, SW), F32), pltpu.VMEM((N_SEG, SW), F32)],
            sem=("parallel", "arbitrary"), comm=comm,
        )(dy_p, a_l, cmt)

    def body(dy_ref, a_ref, ap_ref, cmt_ref, gfin_ref, s_ref, sprev_ref, start_ref, u_ref, bmt_ref, d_ref,
             du_ref, da_ref, dbm_ref, dcm_ref, buf, st):
        jc = pl.program_id(1)

        @pl.when(jc == 0)
        def _():
            st[...] = _segment_starts(gfin_ref[...], ap_ref[:, :H], -ap_ref[:, H:], True)
            da_ref[...] = jnp.zeros_like(da_ref)
            dbm_ref[...] = jnp.zeros_like(dbm_ref)
            dcm_ref[...] = jnp.zeros_like(dcm_ref)

        dy = dy_ref[...]
        dyb = dy.astype(BF16)
        buf[...] = jnp.dot(dyb, cmt_ref[...], preferred_element_type=F32)
        scan_chunk(a_ref, buf, st)
        g = buf[...]
        s = s_ref[...]
        first = jnp.where(jc == NC - 1, start_ref[...], sprev_ref[...])
        sp = jnp.concatenate([first, s[:RB - N_SEG, :]], axis=0)
        gr, gi, pr, pi = g[:, :H], g[:, H:], sp[:, :H], sp[:, H:]
        da_ref[...] += jnp.concatenate([jnp.sum(gr * pr + gi * pi, axis=0, keepdims=True),
                                        jnp.sum(gi * pr - gr * pi, axis=0, keepdims=True)], axis=1)
        gb = g.astype(BF16)
        u = u_ref[...]
        du_ref[...] = (jnp.dot(gb, bmt_ref[...], preferred_element_type=F32) + dy * d_ref[...]).astype(du_ref.dtype)
        dbm_ref[...] += lax.dot_general(u.astype(BF16), gb, (((0,), (0,)), ((), ())), preferred_element_type=F32)
        dcm_ref[...] += lax.dot_general(s.astype(BF16), dyb, (((0,), (0,)), ((), ())), preferred_element_type=F32)

    prev8 = lambda k, j: (jnp.maximum((NC - 1 - j) * S5_STEPS - 1, 0), k)
    return _pcall(
        body, name=name, grid=(NK, NC),
        in_specs=[_vspec((RB, LANES), rblk), _vspec((None, 1, SW), per_k), _vspec((None, 1, SW), per_k),
                  _vspec((None, LANES, SW), per_k), _vspec((None, N_SEG, SW), per_k), _vspec((RB, SW), rblk),
                  _vspec((N_SEG, SW), prev8), _vspec((None, N_SEG, SW), per_k), _vspec((RB, LANES), rblk),
                  _vspec((None, SW, LANES), per_k), _vspec((1, LANES), lambda k, j: (0, k))],
        out_specs=[_vspec((RB, LANES), rblk), _vspec((None, 1, SW), per_k), _vspec((None, LANES, SW), per_k),
                   _vspec((None, SW, LANES), per_k)],
        out_shape=[jax.ShapeDtypeStruct((L, C), BF16), jax.ShapeDtypeStruct((NK, 1, SW), F32),
                   jax.ShapeDtypeStruct((NK, LANES, SW), F32), jax.ShapeDtypeStruct((NK, SW, LANES), F32)],
        scratch_shapes=[pltpu.VMEM((RB, SW), F32), pltpu.VMEM((N_SEG, SW), F32)],
        sem=("parallel", "arbitrary"), comm=comm,
    )(dy_p, a_l, apow_l, cmt, gfin, states, states, starts, u_p, bmt, d_l)


def _s5_prep(a_re, a_im, log_dt, b_re, b_im, c_re, c_im, d, glu_w, seg_len):
    G, P = a_re.shape
    Hc = b_re.shape[-1]
    gl = LANES // Hc
    nk = G // gl
    dt = jnp.exp(log_dt)[:, None]
    er = jnp.exp(a_re * dt)
    ab_r, ab_i = er * jnp.cos(a_im * dt), er * jnp.sin(a_im * dt)
    den = a_re * a_re + a_im * a_im
    nr, ni = ab_r - 1.0, ab_i
    q_r, q_i = (nr * a_re + ni * a_im) / den, (ni * a_re - nr * a_im) / den
    bb_r = q_r[..., None] * b_re - q_i[..., None] * b_im
    bb_i = q_r[..., None] * b_im + q_i[..., None] * b_re
    ep = jnp.exp(a_re * dt * seg_len)
    ap_r, ap_i = ep * jnp.cos(a_im * dt * seg_len), ep * jnp.sin(a_im * dt * seg_len)
    eye = jnp.eye(gl, dtype=F32)

    def lanes(t):
        return t.reshape(nk, 1, gl * P)

    def b_mat(t):
        return jnp.einsum("kgph,gq->kghqp", t.reshape(nk, gl, P, Hc), eye).reshape(nk, gl * Hc, gl * P)

    def c_mat(t):
        return jnp.einsum("kghp,gq->kgpqh", t.reshape(nk, gl, Hc, P), eye).reshape(nk, gl * P, gl * Hc)

    a_l = jnp.concatenate([lanes(ab_r), lanes(ab_i)], axis=-1)
    apow_l = jnp.concatenate([lanes(ap_r), lanes(ap_i)], axis=-1)
    bm = jnp.concatenate([b_mat(bb_r), b_mat(bb_i)], axis=-1)
    cm = jnp.concatenate([c_mat(c_re), -c_mat(c_im)], axis=1)
    glu = jnp.einsum("kgho,gq->kghqo", glu_w.reshape(nk, gl, Hc, Hc), eye).reshape(nk, gl * Hc, gl * Hc)
    return a_l, apow_l, bm, cm, d.reshape(1, G * Hc), glu


def _to_segments(t):
    L, C = t.shape
    return t.reshape(N_SEG, L // N_SEG, C).transpose(1, 0, 2).reshape(L, C)


def _from_segments(t):
    L, C = t.shape
    return t.reshape(L // N_SEG, N_SEG, C).transpose(1, 0, 2).reshape(L, C)


def _swa_probs(q, kk, bias_t, sink):
    s = lax.dot_general(kk, q, (((1,), (1,)), ((), ())), preferred_element_type=F32) * (HEAD_DIM ** -0.5)
    s = s + bias_t
    m = jnp.maximum(jnp.max(s, axis=0, keepdims=True), sink)
    e = jnp.exp(s - m)
    es = jnp.exp(sink - m)
    inv = 1.0 / (jnp.sum(e, axis=0, keepdims=True) + es)
    return e * inv, es * inv


def _swa_blocks(q_ref, kp_ref, kc_ref, vp_ref, vc_ref, bias_ref, n2):
    W = WINDOW
    rows = Q_PER_KV * W
    k0, k1, v0, v1 = kc_ref[0:W, :], kc_ref[W:, :], vc_ref[0:W, :], vc_ref[W:, :]
    table = bias_ref[jnp.minimum(n2, 1)]
    return [(0, q_ref[:, 0:W, :].reshape(rows, HEAD_DIM), jnp.concatenate([kp_ref[...], k0], axis=0),
             jnp.concatenate([vp_ref[...], v0], axis=0), table),
            (1, q_ref[:, W:, :].reshape(rows, HEAD_DIM), jnp.concatenate([k0, k1], axis=0),
             jnp.concatenate([v0, v1], axis=0), bias_ref[1])]


def _swa_specs(nq):
    W = WINDOW
    qs = _vspec((Q_PER_KV, 2 * W, HEAD_DIM), lambda g, n: (g, n, 0))
    kprev = _vspec((None, W, HEAD_DIM), lambda g, n: (g, jnp.maximum(2 * n - 1, 0), 0))
    kcur = _vspec((None, 2 * W, HEAD_DIM), lambda g, n: (g, n, 0))
    bias = _vspec((2, None, 2 * W, Q_PER_KV * W), lambda g, n: (0, g, 0, 0))
    dbias = _vspec((None, 2 * W, Q_PER_KV * W), lambda g, n: (g, 0, 0))
    sink = _vspec((1, Q_PER_KV * W), lambda g, n: (0, g))
    outs = _vspec((None, 2, HEAD_DIM, Q_PER_KV * W), lambda g, n: (g, n, 0, 0))
    return qs, kprev, kcur, bias, dbias, sink, outs


def _from_head_lanes(t, L):
    nkv = t.shape[0]
    t = t.reshape(nkv, L // WINDOW, HEAD_DIM, Q_PER_KV, WINDOW)
    return t.transpose(1, 4, 0, 3, 2).reshape(L, nkv * Q_PER_KV * HEAD_DIM)


def _masked_bias(bias_tab):
    qi = np.arange(WINDOW)[:, None]
    kj = np.arange(2 * WINDOW)[None, :]
    valid = ((kj < WINDOW) & (kj > qi)) | ((kj >= WINDOW) & (kj - WINDOW <= qi))
    first = valid & (kj >= WINDOW)
    both = jnp.stack([jnp.where(first[None], bias_tab, NEG_INF), jnp.where(valid[None], bias_tab, NEG_INF)])
    nkv = bias_tab.shape[0] // Q_PER_KV
    both = both.reshape(2, nkv, Q_PER_KV, WINDOW, 2 * WINDOW).transpose(0, 1, 4, 2, 3)
    return both.reshape(2, nkv, 2 * WINDOW, Q_PER_KV * WINDOW)


def _unmasked_dbias(dbias_t):
    nkv = dbias_t.shape[0]
    t = dbias_t.reshape(nkv, 2 * WINDOW, Q_PER_KV, WINDOW).transpose(0, 2, 3, 1)
    return t.reshape(nkv * Q_PER_KV, WINDOW, 2 * WINDOW)


def swa_fwd(qT, kT, vT, bias2, sink_row, *, name="swa_fwd", comm=()):
    NQ, L, _ = qT.shape
    NKV = kT.shape[0]
    qs, kprev, kcur, bs, _, sk, outs = _swa_specs(NQ)

    def body(q_ref, kp_ref, kc_ref, vp_ref, vc_ref, bias_ref, sink_ref, o_ref):
        sink = sink_ref[...]
        for b, q, kk, vv, bias in _swa_blocks(q_ref, kp_ref, kc_ref, vp_ref, vc_ref, bias_ref, pl.program_id(1)):
            p, _ = _swa_probs(q, kk, bias, sink)
            o = lax.dot_general(vv, p.astype(BF16), (((0,), (0,)), ((), ())), preferred_element_type=F32)
            o_ref[b] = o.astype(o_ref.dtype)

    return _pcall(
        body, name=name, grid=(NKV, L // (2 * WINDOW)),
        in_specs=[qs, kprev, kcur, kprev, kcur, bs, sk], out_specs=outs,
        out_shape=jax.ShapeDtypeStruct((NKV, L // WINDOW, HEAD_DIM, Q_PER_KV * WINDOW), BF16),
        sem=("parallel", "arbitrary"), comm=comm,
    )(qT, kT, kT, vT, vT, bias2, sink_row)


def swa_bwd(qT, kT, vT, bias2, sink_row, doT, *, name="swa_bwd", comm=()):
    NQ, L, _ = qT.shape
    NKV = kT.shape[0]
    qs, kprev, kcur, bs, dbs, sk, outs = _swa_specs(NQ)
    W = WINDOW

    def body(q_ref, kp_ref, kc_ref, vp_ref, vc_ref, bias_ref, sink_ref, do_ref,
             dq_ref, dk_ref, dv_ref, dbias_ref, dsink_ref):
        n = pl.program_id(1)

        @pl.when(n == 0)
        def _():
            dk_ref[...] = jnp.zeros_like(dk_ref)
            dv_ref[...] = jnp.zeros_like(dv_ref)
            dbias_ref[...] = jnp.zeros_like(dbias_ref)
            dsink_ref[...] = jnp.zeros_like(dsink_ref)

        sink = sink_ref[...]
        scale = HEAD_DIM ** -0.5
        grads = []
        for b, q, kk, vv, bias in _swa_blocks(q_ref, kp_ref, kc_ref, vp_ref, vc_ref, bias_ref, n):
            p, ps = _swa_probs(q, kk, bias, sink)
            do = do_ref[:, b * W:(b + 1) * W, :].reshape(Q_PER_KV * W, HEAD_DIM)
            dp = lax.dot_general(vv, do, (((1,), (1,)), ((), ())), preferred_element_type=F32)
            delta = jnp.sum(p * dp, axis=0, keepdims=True)
            ds = p * (dp - delta)
            dsink_ref[...] += -ps * delta
            dbias_ref[...] += ds
            dsb = ds.astype(BF16)
            dq = lax.dot_general(kk, dsb, (((0,), (0,)), ((), ())), preferred_element_type=F32) * scale
            dq_ref[b] = dq.astype(dq_ref.dtype)
            grads.append((jnp.dot(dsb, q, preferred_element_type=F32) * scale,
                          jnp.dot(p.astype(BF16), do, preferred_element_type=F32)))
        (dk0, dv0), (dk1, dv1) = grads

        @pl.when(n == 0)
        def _():
            dk_ref[0:W, :] += dk0[W:, :]
            dv_ref[0:W, :] += dv0[W:, :]

        @pl.when(n > 0)
        def _():
            rows = pl.ds(pl.multiple_of((2 * n - 1) * W, W), 2 * W)
            dk_ref[rows, :] += dk0
            dv_ref[rows, :] += dv0

        rows = pl.ds(pl.multiple_of(2 * n * W, W), 2 * W)
        dk_ref[rows, :] += dk1
        dv_ref[rows, :] += dv1

    whole = _vspec((None, L, HEAD_DIM), lambda g, n: (g, 0, 0))
    return _pcall(
        body, name=name, grid=(NKV, L // (2 * W)),
        in_specs=[qs, kprev, kcur, kprev, kcur, bs, sk, qs],
        out_specs=[outs, whole, whole, dbs, sk],
        out_shape=[jax.ShapeDtypeStruct((NKV, L // W, HEAD_DIM, Q_PER_KV * W), BF16),
                   jax.ShapeDtypeStruct((NKV, L, HEAD_DIM), F32), jax.ShapeDtypeStruct((NKV, L, HEAD_DIM), F32),
                   jax.ShapeDtypeStruct((NKV, 2 * W, Q_PER_KV * W), F32), jax.ShapeDtypeStruct((1, NQ * W), F32)],
        sem=("parallel", "arbitrary"), comm=comm,
    )(qT, kT, kT, vT, vT, bias2, sink_row, doT)


def _bucket_table():
    qi = np.arange(WINDOW)[:, None]
    kj = np.arange(2 * WINDOW)[None, :]
    rel = qi + WINDOW - kj
    max_exact = N_BUCKETS // 2
    n = np.maximum(rel, 0)
    nf = np.maximum(n, max_exact).astype(np.float32)
    large = max_exact + (np.log(nf / max_exact) / math.log(MAX_DISTANCE / max_exact) * (N_BUCKETS - max_exact)).astype(np.int32)
    large = np.minimum(large, N_BUCKETS - 1)
    return np.where(n < max_exact, n, large).astype(np.int32).reshape(-1)


def _xa_probs(q, k):
    hd = q.shape[1]
    s = lax.dot_general(q, k, (((1,), (1,)), ((), ())), preferred_element_type=F32) * (hd ** -0.5)
    e = jnp.exp(s - jnp.max(s, axis=1, keepdims=True))
    return e / jnp.sum(e, axis=1, keepdims=True)


def xattn_fwd(q, kv, *, name="xattn_fwd"):
    L, D = q.shape
    Mm = kv.shape[0]
    hd = D // X_HEADS
    tq = _tile8(L, 512)

    def body(q_ref, kv_ref, o_ref):
        for h in range(X_HEADS):
            cols = slice(h * hd, (h + 1) * hd)
            p = _xa_probs(q_ref[:, cols], kv_ref[:, cols])
            o_ref[:, cols] = jnp.dot(p.astype(BF16), kv_ref[:, D + h * hd:D + (h + 1) * hd],
                                     preferred_element_type=F32).astype(o_ref.dtype)

    return pl.pallas_call(
        body, name=name, grid=(L // tq,),
        in_specs=[_vspec((tq, D), lambda i: (i, 0)), _vspec((Mm, 2 * D), lambda i: (0, 0))],
        out_specs=_vspec((tq, D), lambda i: (i, 0)),
        out_shape=jax.ShapeDtypeStruct((L, D), BF16),
        compiler_params=_params(("parallel",)),
    )(q, kv)


def xattn_bwd(q, kv, do, *, name="xattn_bwd"):
    L, D = q.shape
    Mm = kv.shape[0]
    hd = D // X_HEADS
    tq = _tile8(L, 512)

    def body(q_ref, kv_ref, do_ref, dq_ref, dkv_ref):
        @pl.when(pl.program_id(0) == 0)
        def _():
            dkv_ref[...] = jnp.zeros_like(dkv_ref)

        for h in range(X_HEADS):
            cols = slice(h * hd, (h + 1) * hd)
            vcols = slice(D + h * hd, D + (h + 1) * hd)
            qh, kh, vh, doh = q_ref[:, cols], kv_ref[:, cols], kv_ref[:, vcols], do_ref[:, cols]
            p = _xa_probs(qh, kh)
            dp = lax.dot_general(doh, vh, (((1,), (1,)), ((), ())), preferred_element_type=F32)
            ds = (p * (dp - jnp.sum(p * dp, axis=1, keepdims=True)) * (hd ** -0.5)).astype(BF16)
            dq_ref[:, cols] = jnp.dot(ds, kh, preferred_element_type=F32).astype(dq_ref.dtype)
            dkv_ref[:, cols] += lax.dot_general(ds, qh, (((0,), (0,)), ((), ())), preferred_element_type=F32)
            dkv_ref[:, vcols] += lax.dot_general(p.astype(BF16), doh, (((0,), (0,)), ((), ())), preferred_element_type=F32)

    return pl.pallas_call(
        body, name=name, grid=(L // tq,),
        in_specs=[_vspec((tq, D), lambda i: (i, 0)), _vspec((Mm, 2 * D), lambda i: (0, 0)), _vspec((tq, D), lambda i: (i, 0))],
        out_specs=[_vspec((tq, D), lambda i: (i, 0)), _vspec((Mm, 2 * D), lambda i: (0, 0))],
        out_shape=[jax.ShapeDtypeStruct((L, D), BF16), jax.ShapeDtypeStruct((Mm, 2 * D), F32)],
        compiler_params=_params(("arbitrary",)),
    )(q, kv, do)


def adamw(w, g, m, v, *, name="adamw"):
    R, C = w.shape
    tr = _tile8(R, max(SUBLANES, (256 * 1024) // C // SUBLANES * SUBLANES))

    def body(w_ref, g_ref, m_ref, v_ref, d_ref, nm_ref, nv_ref):
        g_ = g_ref[...]
        nm = ADAM_B1 * m_ref[...] + (1.0 - ADAM_B1) * g_
        nv = ADAM_B2 * v_ref[...] + (1.0 - ADAM_B2) * (g_ * g_)
        m_hat = nm / (1.0 - ADAM_B1 ** ADAM_STEP)
        v_hat = nv / (1.0 - ADAM_B2 ** ADAM_STEP)
        d_ref[...] = -ADAM_LR * (m_hat / (jnp.sqrt(v_hat) + ADAM_EPS) + ADAM_WD * w_ref[...])
        nm_ref[...] = nm
        nv_ref[...] = nv

    spec = _vspec((tr, C), lambda i: (i, 0))
    return pl.pallas_call(
        body, name=name, grid=(R // tr,), in_specs=[spec] * 4, out_specs=[spec] * 3,
        out_shape=[jax.ShapeDtypeStruct((R, C), F32)] * 3, compiler_params=_params(("parallel",)),
    )(w, g, m, v)


def _place():
    x, y, c = lax.axis_index("x"), lax.axis_index("y"), lax.axis_index("c")
    chips = [(1 - x, y), (x, 1 - y), (1 - x, 1 - y)]
    return x, y, c, chips


def _remote(src, dst, send, recv, k, to):
    return pltpu.make_async_remote_copy(src_ref=src, dst_ref=dst, send_sem=send.at[k], recv_sem=recv.at[k],
                                        device_id=to, device_id_type=MESH)


class _Job:
    result = None

    def start(self, ins, outs, send, recv, base):
        for cp in self.copies(ins, outs, send, recv, base)[0]:
            cp.start()


class GatherJob(_Job):
    nsem = 7

    def __init__(self, w):
        self.ops = [w]
        self.outs = [jax.ShapeDtypeStruct((N_CHIPS,) + w.shape, w.dtype)]
        self.cost = N_CHIPS * w.size

    def copies(self, ins, outs, send, recv, base, first_only=True):
        w, out = ins[0], outs[0]
        x, y, c, chips = _place()
        me, sib = 2 * x + y, (x, y, 1 - c)
        h = w.shape[0] // 2
        mine, theirs = pl.ds(c * h, h), pl.ds((1 - c) * h, h)
        first = [_remote(w.at[mine], out.at[me, mine], send, recv, base + j, (px, py, c))
                 for j, (px, py) in enumerate(chips)]
        first.append(_remote(w, out.at[me], send, recv, base + 6, sib))
        if first_only:
            return first,
        landed = [out.at[2 * px + py, mine] for px, py in chips]
        lands = [_remote(w.at[mine], landed[j], send, recv, base + j, (px, py, c)) for j, (px, py) in enumerate(chips)]
        passes = [_remote(landed[j], landed[j], send, recv, base + 3 + j, sib) for j in range(3)]
        arrives = [_remote(w.at[theirs], out.at[2 * px + py, theirs], send, recv, base + 3 + j, sib)
                   for j, (px, py) in enumerate(chips)]
        return first, lands, passes, arrives

    def finish(self, ins, outs, send, recv, base):
        first, lands, passes, arrives = self.copies(ins, outs, send, recv, base, first_only=False)
        for land, fwd in zip(lands, passes):
            land.wait_recv()
            fwd.start()
        first[3].wait_recv()
        for cp in arrives:
            cp.wait_recv()
        for cp in first + passes:
            cp.wait_send()


class ExchangeJob(_Job):
    nsem = 1

    def __init__(self, g):
        self.ops = [g]
        self.outs = [jax.ShapeDtypeStruct((g.shape[0], g.shape[1] // 2, g.shape[2]), g.dtype)]
        self.cost = 0.15 * g.size

    def copies(self, ins, outs, send, recv, base):
        x, y, c, _ = _place()
        r2 = ins[0].shape[1] // 2
        return [_remote(ins[0].at[:, pl.ds((1 - c) * r2, r2)], outs[0], send, recv, base, (x, y, 1 - c))],

    def finish(self, ins, outs, send, recv, base):
        self.copies(ins, outs, send, recv, base)[0][0].wait()


class ScatterJob(_Job):
    nsem = 3

    def __init__(self, p):
        self.ops = [p]
        self.outs = [jax.ShapeDtypeStruct((N_CHIPS - 1,) + p.shape[1:], p.dtype)]
        self.cost = 2 * p.size

    def copies(self, ins, outs, send, recv, base):
        x, y, c, chips = _place()
        return [_remote(ins[0].at[2 * px + py], outs[0].at[j], send, recv, base + j, (px, py, c))
                for j, (px, py) in enumerate(chips)],

    def finish(self, ins, outs, send, recv, base):
        for cp in self.copies(ins, outs, send, recv, base)[0]:
            cp.wait()


def run_comm(jobs, *, name):
    cops = [a for j in jobs for a in j.ops]
    couts = [s for j in jobs for s in j.outs]
    nsem = sum(j.nsem for j in jobs)

    def body(*refs):
        cin, cout = refs[:len(cops)], refs[len(cops):len(cops) + len(couts)]
        send, recv = refs[len(cops) + len(couts):]
        for phase in ("start", "finish"):
            ii = io = base = 0
            for j in jobs:
                getattr(j, phase)(cin[ii:ii + len(j.ops)], cout[io:io + len(j.outs)], send, recv, base)
                ii, io, base = ii + len(j.ops), io + len(j.outs), base + j.nsem

    res = pl.pallas_call(
        body, name=name, in_specs=[ANY] * len(cops), out_specs=[ANY] * len(couts), out_shape=couts,
        scratch_shapes=[pltpu.SemaphoreType.DMA((nsem,)), pltpu.SemaphoreType.DMA((nsem,))],
        compiler_params=pltpu.CompilerParams(has_side_effects=True),
    )(*cops)
    io = 0
    for j in jobs:
        j.result = list(res[io:io + len(j.outs)])
        io += len(j.outs)


def add_half(g, other, c_idx, *, name="rs_add_half"):
    S, R, C = g.shape
    r2 = R // 2
    tr = _tile8(r2, max(SUBLANES, (512 * 1024) // C // SUBLANES * SUBLANES))
    nb = r2 // tr

    def body(c_ref, g_ref, o_ref, out_ref):
        out_ref[...] = (g_ref[...].astype(F32) + o_ref[...].astype(F32)).astype(out_ref.dtype)

    return pl.pallas_call(
        body, name=name,
        grid_spec=pltpu.PrefetchScalarGridSpec(
            num_scalar_prefetch=1, grid=(S, nb),
            in_specs=[pl.BlockSpec((None, tr, C), lambda s, i, c_ref: (s, c_ref[0] * nb + i, 0)),
                      pl.BlockSpec((None, tr, C), lambda s, i, c_ref: (s, i, 0))],
            out_specs=pl.BlockSpec((None, tr, C), lambda s, i, c_ref: (s, i, 0))),
        out_shape=jax.ShapeDtypeStruct((S, r2, C), BF16),
        compiler_params=_params(("parallel", "parallel")),
    )(c_idx, g, other)


def add_partials(p, got, place_idx, gbuf, layer, *, name="rs_add_partials"):
    S, r2, C = p.shape
    tr = _tile8(r2, max(SUBLANES, (512 * 1024) // C // SUBLANES * SUBLANES))
    nb = r2 // tr

    def body(pi_ref, p_ref, g_ref, buf_ref, out_ref):
        out_ref[...] = ((p_ref[...].astype(F32) + g_ref[0].astype(F32)) + g_ref[1].astype(F32)) + g_ref[2].astype(F32)

    return pl.pallas_call(
        body, name=name,
        grid_spec=pltpu.PrefetchScalarGridSpec(
            num_scalar_prefetch=1, grid=(nb,),
            in_specs=[pl.BlockSpec((None, tr, C), lambda i, pi: (pi[0], i, 0)),
                      pl.BlockSpec((N_CHIPS - 1, tr, C), lambda i, pi: (0, i, 0)),
                      ANY],
            out_specs=pl.BlockSpec((None, tr, C), lambda i, pi: (layer, pi[1] * nb + i, 0))),
        out_shape=jax.ShapeDtypeStruct(gbuf.shape, F32),
        input_output_aliases={3: 0},
        compiler_params=_params(("parallel",)),
    )(place_idx, p, got, gbuf)


def join_halves(gbufs, *, name="rs_join_halves"):
    n = len(gbufs)

    def body(*refs):
        outs = refs[n:2 * n]
        send, recv = refs[2 * n:]
        x, y, c, _ = _place()
        sib = (x, y, 1 - c)
        cps = []
        for i in range(n):
            r2 = outs[i].shape[1] // 2
            mine = outs[i].at[:, pl.ds(c * r2, r2)]
            cp = _remote(mine, mine, send, recv, i, sib)
            cp.start()
            cps.append(cp)
        for i in range(n):
            r2 = outs[i].shape[1] // 2
            theirs = outs[i].at[:, pl.ds((1 - c) * r2, r2)]
            _remote(theirs, theirs, send, recv, i, sib).wait_recv()
        for cp in cps:
            cp.wait_send()

    return pl.pallas_call(
        body, name=name, in_specs=[ANY] * n, out_specs=[ANY] * n,
        out_shape=[jax.ShapeDtypeStruct(g.shape, F32) for g in gbufs],
        input_output_aliases={i: i for i in range(n)},
        scratch_shapes=[pltpu.SemaphoreType.DMA((n,)), pltpu.SemaphoreType.DMA((n,))],
        compiler_params=pltpu.CompilerParams(has_side_effects=True),
    )(*gbufs)


def all_gather_rows(v, *, name="all_gather_small"):
    m, ncol = v.shape

    def body(x_ref, out_ref, send, recv, lsem):
        x, y, c, chips = _place()
        me, sib = (x, y, c), (x, y, 1 - c)

        def rows(px, py, pc):
            return out_ref.at[pl.ds((4 * px + 2 * py + pc) * m, m), :]

        def copy(k, block, to, src=None):
            return _remote(rows(*block) if src is None else src, rows(*block), send, recv, k, to)

        mine = pltpu.make_async_copy(x_ref, rows(*me), lsem)
        mine.start()
        first = [copy(0, me, sib, src=x_ref)]
        first += [copy(1 + j, me, (*chip, c), src=x_ref) for j, chip in enumerate(chips)]
        for cp in first:
            cp.start()
        passed = [copy(4 + j, (*chip, c), sib) for j, chip in enumerate(chips)]
        for j, chip in enumerate(chips):
            copy(1 + j, (*chip, c), me).wait_recv()
            passed[j].start()
        copy(0, sib, me).wait_recv()
        for j, chip in enumerate(chips):
            copy(4 + j, (*chip, 1 - c), me).wait_recv()
        for cp in first + passed:
            cp.wait_send()
        mine.wait()

    return pl.pallas_call(
        body, name=name,
        in_specs=[pl.BlockSpec(memory_space=pltpu.VMEM)], out_specs=pl.BlockSpec(memory_space=pltpu.VMEM),
        out_shape=jax.ShapeDtypeStruct((8 * m, ncol), v.dtype),
        scratch_shapes=[pltpu.SemaphoreType.DMA((7,)), pltpu.SemaphoreType.DMA((7,)), pltpu.SemaphoreType.DMA],
        compiler_params=pltpu.CompilerParams(vmem_limit_bytes=VMEM_LIMIT_BYTES, has_side_effects=True),
    )(v)


def sum_blocks(g8, *, name="sum_blocks"):
    nb, m, ncol = g8.shape
    tr = _tile8(m, 512)

    def body(g_ref, o_ref):
        acc = g_ref[0]
        for k in range(1, nb):
            acc = acc + g_ref[k]
        o_ref[...] = acc

    return pl.pallas_call(
        body, name=name, grid=(m // tr,),
        in_specs=[_vspec((nb, tr, ncol), lambda i: (0, i, 0))], out_specs=_vspec((tr, ncol), lambda i: (i, 0)),
        out_shape=jax.ShapeDtypeStruct((m, ncol), F32), compiler_params=_params(("parallel",)),
    )(g8)


PACK_ROWS = 256


def _pack(arrs, mult):
    flat = jnp.concatenate([a.reshape(-1) for a in arrs])
    pad = (-flat.shape[0]) % (mult * LANES)
    return jnp.pad(flat, (0, pad)).reshape(-1, LANES)


def _unpack(packed, like):
    flat = packed.reshape(-1)
    out, off = [], 0
    for a in like:
        out.append(flat[off:off + a.size].reshape(a.shape))
        off += a.size
    return out


def _rows2d(a):
    return a.reshape(-1, a.shape[-1])


def kernel(x, mem, norm_mix, norm_xattn, norm_ffn, norm_final, norm_mem, rel_bias, ev_w_in, ev_conv_w, s5_a_re, s5_a_im, s5_log_dt, s5_b_re, s5_b_im, s5_c_re, s5_c_im, s5_d, s5_glu_w, ev_w_out, od_w_qkv, od_b_qkv, od_sinks, od_w_out, xa_w_q, xa_w_kv, xa_w_o, ff_w_gate, ff_w_up, ff_conv_w, ff_conv_b, ff_w_down, loss_target, m_norm_mix, m_norm_xattn, m_norm_ffn, m_norm_final, m_norm_mem, m_rel_bias, m_ev_w_in, m_ev_conv_w, m_s5_a_re, m_s5_a_im, m_s5_log_dt, m_s5_b_re, m_s5_b_im, m_s5_c_re, m_s5_c_im, m_s5_d, m_s5_glu_w, m_ev_w_out, m_od_w_qkv, m_od_b_qkv, m_od_sinks, m_od_w_out, m_xa_w_q, m_xa_w_kv, m_xa_w_o, m_ff_w_gate, m_ff_w_up, m_ff_conv_w, m_ff_conv_b, m_ff_w_down, v_norm_mix, v_norm_xattn, v_norm_ffn, v_norm_final, v_norm_mem, v_rel_bias, v_ev_w_in, v_ev_conv_w, v_s5_a_re, v_s5_a_im, v_s5_log_dt, v_s5_b_re, v_s5_b_im, v_s5_c_re, v_s5_c_im, v_s5_d, v_s5_glu_w, v_ev_w_out, v_od_w_qkv, v_od_b_qkv, v_od_sinks, v_od_w_out, v_xa_w_q, v_xa_w_kv, v_xa_w_o, v_ff_w_gate, v_ff_w_up, v_ff_conv_w, v_ff_conv_b, v_ff_w_down):
    names = ["norm_mix", "norm_xattn", "norm_ffn", "norm_final", "norm_mem", "rel_bias", "ev_w_in", "ev_conv_w",
             "s5_a_re", "s5_a_im", "s5_log_dt", "s5_b_re", "s5_b_im", "s5_c_re", "s5_c_im", "s5_d", "s5_glu_w",
             "ev_w_out", "od_w_qkv", "od_b_qkv", "od_sinks", "od_w_out", "xa_w_q", "xa_w_kv", "xa_w_o",
             "ff_w_gate", "ff_w_up", "ff_conv_w", "ff_conv_b", "ff_w_down"]
    env = dict(locals())
    W = {k: env[k] for k in names}
    Mo = {k: env["m_" + k] for k in names}
    Vo = {k: env["v_" + k] for k in names}

    h = x[0]
    target = loss_target[0]
    L, D = h.shape
    depth = norm_mix.shape[0]
    c_idx = lax.axis_index("c").astype(jnp.int32).reshape(1)
    me_idx = (2 * lax.axis_index("x") + lax.axis_index("y")).astype(jnp.int32).reshape(1)

    col_sharded = ["ev_w_in", "od_w_qkv", "xa_w_kv", "ff_w_gate", "ff_w_up"]
    row_sharded = ["ev_w_out", "od_w_out", "xa_w_q", "xa_w_o", "ff_w_down"]
    small_sharded = ["ev_conv_w", "od_b_qkv", "ff_conv_w"]
    big = col_sharded + row_sharded

    def layer_weights(l):
        mix = [("ev_w_in", l // 2), ("ev_w_out", l // 2)] if l % 2 == 0 else [("od_w_qkv", l // 2), ("od_w_out", l // 2)]
        return mix + [(k, l) for k in ("xa_w_q", "xa_w_kv", "xa_w_o", "ff_w_gate", "ff_w_up", "ff_w_down")]

    gjob = {kl: GatherJob(W[kl[0]][kl[1]].astype(BF16)) for l in range(depth) for kl in layer_weights(l)}
    small_jobs = [GatherJob(W[k]) for k in small_sharded]
    pending = [gjob[kl] for l in range(depth) for kl in layer_weights(l)]
    run_comm([pending.pop(0)] + small_jobs, name="gather_first")
    flushes = []

    def take(queue, host_cost):
        jobs, acc = [], 0.0
        while queue and acc + queue[0].cost <= 1.25 * host_cost:
            acc += queue[0].cost
            jobs.append(queue.pop(0))
        return jobs

    def fwd_host(fn, units, *args, **kw):
        return fn(*args, comm=take(pending, units), **kw)

    def weight(k, l):
        job = gjob[(k, l)]
        if job.result is None:
            n = pending.index(job) + 1
            run_comm(pending[:n], name="gather_flush_%d" % len(flushes))
            flushes.append(n)
            del pending[:n]
        return job.result[0]

    def wcol(k, l):
        return weight(k, l)

    def wrow(k, l):
        g = weight(k, l)
        return g.reshape(1, g.shape[0] * g.shape[1], g.shape[2])

    def fwd_mm(a, w3, **kw):
        return mm_nn(a, w3, comm=take(pending, w3.size * a.shape[0] / L), **kw)

    sg = [j.result[0] for j in small_jobs]
    ev_conv_w_f = sg[0].transpose(1, 2, 0, 3).reshape(ev_conv_w.shape[0], 3, -1)
    od_b_qkv_f = sg[1].transpose(1, 0, 2).reshape(od_b_qkv.shape[0], 1, -1)
    ff_conv_w_f = sg[2].transpose(1, 2, 0, 3).reshape(ff_conv_w.shape[0], 3, -1)

    buckets = _bucket_table()
    NQ = D // HEAD_DIM
    NKV = NQ // Q_PER_KV
    onehot = jnp.asarray((buckets[:, None] == np.arange(N_BUCKETS)[None, :]).astype(np.float32))
    bias_tab = jnp.dot(rel_bias.T, onehot.T, precision=lax.Precision.HIGHEST).reshape(NQ, WINDOW, 2 * WINDOW)
    bias2 = _masked_bias(bias_tab)

    mem_n = rms_fwd(mem[0], norm_mem.reshape(1, D), name="rms_fwd_mem")

    saved = []
    for l in range(depth):
        i = l // 2
        s = {"h0": h}
        hn = rms_fwd(h, norm_mix[l].reshape(1, D))
        s["hn"] = hn
        if l % 2 == 0:
            A = ev_conv_w_f.shape[-1]
            z = fwd_mm(hn, wcol("ev_w_in", i), name="mm_ev_in")
            ya = conv_mixer_fwd(z, ev_conv_w_f[i])
            prep = functools.partial(_s5_prep, seg_len=L // N_SEG)
            s5p = (s5_a_re[i], s5_a_im[i], s5_log_dt[i], s5_b_re[i], s5_b_im[i], s5_c_re[i], s5_c_im[i], s5_d[i], s5_glu_w[i])
            (a_l, apow_l, bm, cm, d_l, glu), prep_vjp = jax.vjp(prep, *s5p)
            bm16, cm16, glu16 = bm.astype(BF16), cm.astype(BF16), glu.astype(BF16)
            u_p = _to_segments(z[:, 3 * A:])
            fin = fwd_host(s5_scan_fwd, HOST_UNITS["s5_ends"] * u_p.size,
                           u_p, a_l, apow_l, bm16, cm16, d_l, glu16, None, name="s5_fwd_ends")
            ys_p, y_p, states, starts = fwd_host(s5_scan_fwd, HOST_UNITS["s5_fwd"] * u_p.size,
                                                 u_p, a_l, apow_l, bm16, cm16, d_l, glu16, fin, name="s5_fwd")
            ycat = jnp.concatenate([ya, _from_segments(ys_p)], axis=1)
            s.update(z=z, u_p=u_p, y_p=y_p, states=states, starts=starts, ycat=ycat, prep_vjp=prep_vjp,
                     s5ops=(a_l, apow_l, bm16, cm16, d_l, glu16))
            h = fwd_mm(ycat, wrow("ev_w_out", i), res=h, name="mm_ev_out")
        else:
            z = fwd_mm(hn, wcol("od_w_qkv", i), bias=od_b_qkv_f[i], out_dtype=BF16, name="mm_od_qkv")
            qT = z[:, :NQ * HEAD_DIM].reshape(L, NQ, HEAD_DIM).transpose(1, 0, 2)
            kT = z[:, NQ * HEAD_DIM:(NQ + NKV) * HEAD_DIM].reshape(L, NKV, HEAD_DIM).transpose(1, 0, 2)
            vT = z[:, (NQ + NKV) * HEAD_DIM:].reshape(L, NKV, HEAD_DIM).transpose(1, 0, 2)
            sink_row = jnp.repeat(od_sinks[i], WINDOW).reshape(1, NQ * WINDOW)
            o = _from_head_lanes(fwd_host(swa_fwd, HOST_UNITS["swa_fwd"] * qT.size, qT, kT, vT, bias2, sink_row), L)
            s.update(qT=qT, kT=kT, vT=vT, sink_row=sink_row, o=o)
            h = fwd_mm(o, wrow("od_w_out", i), res=h, name="mm_od_out")
        s["h1"] = h
        hn2 = rms_fwd(h, norm_xattn[l].reshape(1, D))
        q = fwd_mm(hn2, wrow("xa_w_q", l), out_dtype=BF16, name="mm_xa_q")
        kv = fwd_mm(mem_n, wcol("xa_w_kv", l), out_dtype=BF16, name="mm_xa_kv")
        ox = xattn_fwd(q, kv)
        s.update(hn2=hn2, q=q, kv=kv, ox=ox)
        h = fwd_mm(ox, wrow("xa_w_o", l), res=h, name="mm_xa_o")
        s["h2"] = h
        hn3 = rms_fwd(h, norm_ffn[l].reshape(1, D))
        gpre = fwd_mm(hn3, wcol("ff_w_gate", l), out_dtype=BF16, name="mm_ff_gate")
        up = fwd_mm(hn3, wcol("ff_w_up", l), out_dtype=BF16, name="mm_ff_up")
        act = fwd_host(ffn_act_fwd, HOST_UNITS["ffn_act_fwd"] * gpre.size,
                       gpre, up, ff_conv_w_f[l], ff_conv_b[l].reshape(1, -1))
        s.update(hn3=hn3, gpre=gpre, up=up, act=act)
        h = fwd_mm(act, wrow("ff_w_down", l), res=h, name="mm_ff_down")
        saved.append(s)

    loss11, dh, dg_final, dh16 = final_loss(h, norm_final.reshape(1, D), target)
    loss = lax.psum(loss11[0, 0], AXES)

    gs = {k: [None] * W[k].shape[0] for k in names if k not in big and W[k].ndim > 1 and k != "rel_bias"}
    dmem_n = None
    dbias_tab = jnp.zeros_like(bias_tab)
    place_idx = jnp.concatenate([me_idx, c_idx])
    gbufs, exchanges, scatters = {}, [], []

    def settle(jobs):
        for j in jobs:
            k, l = j.tag
            if isinstance(j, ExchangeJob):
                nxt = ScatterJob(add_half(j.ops[0], j.result[0], c_idx))
                nxt.tag = j.tag
                scatters.append(nxt)
            else:
                p = j.ops[0]
                if k not in gbufs:
                    gbufs[k] = jnp.zeros((W[k].shape[0], 2 * p.shape[1], p.shape[2]), F32)
                gbufs[k] = add_partials(p, j.result[0], place_idx, gbufs[k], l)

    def take_fit(queue, budget):
        jobs = []
        for j in list(queue):
            if j.cost <= budget:
                budget -= j.cost
                jobs.append(j)
                queue.remove(j)
        return jobs

    def bwd_host(fn, units, *args, **kw):
        jobs = take_fit(exchanges, 0.6 * units) + take_fit(scatters, 1.2 * units)
        out = fn(*args, comm=jobs, **kw)
        settle(jobs)
        return out

    def dx_mm(a, w3, **kw):
        return bwd_host(mm_nt, w3.size * a.shape[0] / L, a, w3, **kw)

    def dw_mm(k, l, xx, dy, S, **kw):
        g3 = bwd_host(mm_tn, xx.shape[1] * dy.shape[1] * xx.shape[0] / L, xx, dy, S, **kw)
        if S == 1:
            g3 = g3.reshape(N_CHIPS, g3.shape[1] // N_CHIPS, g3.shape[2])
        job = ExchangeJob(g3)
        job.tag = (k, l)
        exchanges.append(job)

    for l in reversed(range(depth)):
        i = l // 2
        s = saved[l]
        dact = dx_mm(dh16, wrow("ff_w_down", l), out_dtype=BF16, name="mm_ff_down_dx")
        dw_mm("ff_w_down", l, s["act"], dh16, 1, name="mm_ff_down_dw")
        dgpre, dup, dcw, dcb = bwd_host(ffn_act_bwd, HOST_UNITS["ffn_act_bwd"] * dact.size,
                                        dact, s["gpre"], s["up"], ff_conv_w_f[l], ff_conv_b[l].reshape(1, -1))
        gs["ff_conv_w"][l], gs["ff_conv_b"][l] = dcw, dcb[0]
        dhn3 = dx_mm(dgpre, wcol("ff_w_gate", l), name="mm_ff_gate_dx")
        dhn3 = dx_mm(dup, wcol("ff_w_up", l), res=dhn3, name="mm_ff_up_dx")
        dw_mm("ff_w_gate", l, s["hn3"], dgpre, N_CHIPS, name="mm_ff_gate_dw")
        dw_mm("ff_w_up", l, s["hn3"], dup, N_CHIPS, name="mm_ff_up_dw")
        dh, dg, dh16 = bwd_host(rms_bwd, HOST_UNITS["rms_bwd"] * dh.size, s["h2"], norm_ffn[l].reshape(1, D), dhn3, dh)
        gs["norm_ffn"][l] = dg[0]
        dox = dx_mm(dh16, wrow("xa_w_o", l), out_dtype=BF16, name="mm_xa_o_dx")
        dw_mm("xa_w_o", l, s["ox"], dh16, 1, name="mm_xa_o_dw")
        dq, dkv = xattn_bwd(s["q"], s["kv"], dox)
        dhn2 = dx_mm(dq, wrow("xa_w_q", l), name="mm_xa_q_dx")
        dw_mm("xa_w_q", l, s["hn2"], dq, 1, name="mm_xa_q_dw")
        dw_mm("xa_w_kv", l, mem_n, dkv, N_CHIPS, name="mm_xa_kv_dw")
        dmem_n = dx_mm(dkv, wcol("xa_w_kv", l), res=dmem_n, name="mm_xa_kv_dx")
        dh, dg, dh16 = bwd_host(rms_bwd, HOST_UNITS["rms_bwd"] * dh.size, s["h1"], norm_xattn[l].reshape(1, D), dhn2, dh)
        gs["norm_xattn"][l] = dg[0]
        if l % 2 == 0:
            A = ev_conv_w_f.shape[-1]
            dycat = dx_mm(dh16, wrow("ev_w_out", i), name="mm_ev_out_dx")
            dw_mm("ev_w_out", i, s["ycat"], dh16, 1, name="mm_ev_out_dw")
            dgb, dgc, dxa, dcw = conv_mixer_bwd(s["z"], ev_conv_w_f[i], dycat[:, :A])
            gs["ev_conv_w"][i] = dcw
            a_l, apow_l, bm16, cm16, d_l, glu16 = s["s5ops"]
            dys_p = _to_segments(dycat[:, A:])
            dy_p, dglu, dd = bwd_host(s5_out_bwd, HOST_UNITS["s5_out_bwd"] * dys_p.size, dys_p, s["y_p"], s["u_p"], glu16)
            cmt = cm16.transpose(0, 2, 1)
            bmt = bm16.transpose(0, 2, 1)
            gfin = bwd_host(s5_scan_bwd, HOST_UNITS["s5_ends"] * dy_p.size, dy_p, a_l, apow_l, cmt, None, name="s5_bwd_ends")
            du_p, da, dbm, dcm = bwd_host(s5_scan_bwd, HOST_UNITS["s5_bwd"] * dy_p.size, dy_p, a_l, apow_l, cmt, gfin,
                                          s["states"], s["starts"], s["u_p"], bmt, d_l, name="s5_bwd")
            dprm = s["prep_vjp"]((da, jnp.zeros_like(apow_l), dbm, dcm, dd, dglu))
            for k, g in zip(["s5_a_re", "s5_a_im", "s5_log_dt", "s5_b_re", "s5_b_im", "s5_c_re", "s5_c_im", "s5_d", "s5_glu_w"], dprm):
                gs[k][i] = g
            dz = jnp.concatenate([dgb, dgc, dxa, _from_segments(du_p)], axis=1)
            dhn = dx_mm(dz, wcol("ev_w_in", i), name="mm_ev_in_dx")
            dw_mm("ev_w_in", i, s["hn"], dz, N_CHIPS, name="mm_ev_in_dw")
        else:
            do = dx_mm(dh16, wrow("od_w_out", i), out_dtype=BF16, name="mm_od_out_dx")
            dw_mm("od_w_out", i, s["o"], dh16, 1, name="mm_od_out_dw")
            doT = do.reshape(L, NQ, HEAD_DIM).transpose(1, 0, 2)
            dqT, dkT, dvT, dbias, dsink = bwd_host(swa_bwd, HOST_UNITS["swa_bwd"] * doT.size,
                                                   s["qT"], s["kT"], s["vT"], bias2, s["sink_row"], doT)
            dbias_tab = dbias_tab + _unmasked_dbias(dbias)
            gs["od_sinks"][i] = jnp.sum(dsink.reshape(NQ, WINDOW), axis=1)
            dz = jnp.concatenate([_from_head_lanes(dqT, L),
                                  dkT.astype(BF16).transpose(1, 0, 2).reshape(L, NKV * HEAD_DIM),
                                  dvT.astype(BF16).transpose(1, 0, 2).reshape(L, NKV * HEAD_DIM)], axis=1)
            gs["od_b_qkv"][i] = col_sum(dz)[0]
            dhn = dx_mm(dz, wcol("od_w_qkv", i), name="mm_od_qkv_dx")
            dw_mm("od_w_qkv", i, s["hn"], dz, N_CHIPS, name="mm_od_qkv_dw")
        dh, dg, dh16 = bwd_host(rms_bwd, HOST_UNITS["rms_bwd"] * dh.size, s["h0"], norm_mix[l].reshape(1, D), dhn, dh)
        gs["norm_mix"][l] = dg[0]

    grad_x = dh[None]
    _, dg_mem, _ = rms_bwd(mem[0], norm_mem.reshape(1, D), dmem_n, jnp.zeros_like(dmem_n), name="rms_bwd_mem")
    d_rel_bias = jnp.dot(dbias_tab.reshape(NQ, -1), onehot, precision=lax.Precision.HIGHEST).T

    small = [k for k in names if k not in big]
    local_small = {k: (jnp.stack(gs[k]) if k in gs else None) for k in small}
    local_small["norm_final"] = dg_final[0]
    local_small["norm_mem"] = dg_mem[0]
    local_small["rel_bias"] = d_rel_bias
    full_shape = {k: W[k].shape for k in small}
    for k in small_sharded:
        full_shape[k] = local_small[k].shape
    lst = [local_small[k].reshape(full_shape[k]).astype(F32) for k in small]
    packed = _pack(lst, PACK_ROWS)
    m_rows = packed.shape[0]
    summed = sum_blocks(all_gather_rows(packed).reshape(8, m_rows, LANES))
    gsum = dict(zip(small, _unpack(summed, lst)))
    for k in small_sharded:
        n4 = W[k].shape[-1]
        gsum[k] = lax.dynamic_slice_in_dim(gsum[k], me_idx[0] * n4, n4, axis=gsum[k].ndim - 1)

    for queue, name in ((exchanges, "rs_exchange_rest"), (scatters, "rs_scatter_rest")):
        jobs = queue[:]
        del queue[:]
        if jobs:
            run_comm(jobs, name=name)
            settle(jobs)
    joined = join_halves([gbufs[k] for k in big])
    gbig = {k: g.reshape(W[k].shape) for k, g in zip(big, joined)}

    grads = {**gsum, **gbig}
    delta, new_m, new_v = {}, {}, {}
    for k in big:
        d_, m_, v_ = adamw(_rows2d(W[k]), _rows2d(grads[k]), _rows2d(Mo[k]), _rows2d(Vo[k]), name="adamw_" + k)
        delta[k], new_m[k], new_v[k] = d_.reshape(W[k].shape), m_.reshape(W[k].shape), v_.reshape(W[k].shape)
    sw = [W[k] for k in small]
    d_, m_, v_ = adamw(_pack(sw, PACK_ROWS), _pack([grads[k] for k in small], PACK_ROWS),
                       _pack([Mo[k] for k in small], PACK_ROWS), _pack([Vo[k] for k in small], PACK_ROWS), name="adamw_small")
    for k, a, b, c_ in zip(small, _unpack(d_, sw), _unpack(m_, sw), _unpack(v_, sw)):
        delta[k], new_m[k], new_v[k] = a, b, c_

    return (loss, grad_x, *[grads[k] for k in names], *[delta[k] for k in names],
            *[new_m[k] for k in names], *[new_v[k] for k in names])
```

```python
import functools
import math

import numpy as np
import jax
import jax.numpy as jnp
from jax import lax
from jax.experimental import pallas as pl
from jax.experimental.pallas import tpu as pltpu

F32, BF16 = jnp.float32, jnp.bfloat16
MESH = pl.DeviceIdType.MESH
AXES = ("x", "y", "c")

VMEM_LIMIT_BYTES = 56 * 2**20
SUBLANES, LANES = 8, 128

RMS_EPS = 1e-5
S5_GROUP, S5_STATE = 16, 64
HEAD_DIM, Q_PER_KV, WINDOW = 64, 8, 128
N_BUCKETS, MAX_DISTANCE = 32, 128
X_HEADS = 4
NEG_INF = -1e30
ADAM_LR, ADAM_B1, ADAM_B2, ADAM_EPS, ADAM_WD, ADAM_STEP = 0.001, 0.9, 0.999, 1e-08, 0.01, 10
N_CHIPS = 4
N_SEG = 8
S5_STEPS = 32
HOST_UNITS = {"rms_bwd": 0.57, "ffn_act_fwd": 0.39, "ffn_act_bwd": 0.63, "swa_fwd": 1.2, "swa_bwd": 2.2,
              "s5_fwd": 3.8, "s5_ends": 2.3, "s5_bwd": 4.5, "s5_out_bwd": 1.4}


def _params(sem=None):
    return pltpu.CompilerParams(dimension_semantics=sem, vmem_limit_bytes=VMEM_LIMIT_BYTES)


def _vspec(shape, index_map):
    return pl.BlockSpec(shape, index_map)


ANY = pl.BlockSpec(memory_space=pl.ANY)


def _tile(n, pref):
    t = (min(pref, n) // LANES) * LANES
    while t >= LANES:
        if n % t == 0:
            return t
        t -= LANES
    return n


def _acc_matmul(nk, k, acc, partial, finish):
    if nk == 1:
        finish(partial())
        return

    @pl.when(k == 0)
    def _():
        acc[...] = partial()

    @pl.when(jnp.logical_and(k > 0, k < nk - 1))
    def _():
        acc[...] += partial()

    @pl.when(k == nk - 1)
    def _():
        finish(acc[...] + partial())


MM_MAX_K = 2048


def _job_aliases(jobs, n_in, n_out):
    aliases, ii, io = {}, n_in, n_out
    for j in jobs:
        if j.in_place:
            aliases.update({ii + t: io + t for t in range(len(j.ops))})
        ii, io = ii + len(j.ops), io + len(j.outs)
    return aliases


def hosted_call(body, *, name, grid, in_specs, out_specs, out_shape, ops, scratch_shapes=(), semantics, comm=()):
    if not comm:
        return pl.pallas_call(body, name=name, grid=grid, in_specs=in_specs, out_specs=out_specs, out_shape=out_shape,
                              scratch_shapes=list(scratch_shapes), compiler_params=_params(semantics))(*ops)
    n_in, n_out, n_scr = len(ops), len(out_shape), len(scratch_shapes)
    cops = [a for j in comm for a in j.ops]
    couts = [s for j in comm for s in j.outs]
    nsem = sum(j.nsem for j in comm)

    def hosted(*refs):
        ins, refs = refs[:n_in], refs[n_in:]
        cin, refs = refs[:len(cops)], refs[len(cops):]
        outs, refs = refs[:n_out], refs[n_out:]
        cout, refs = refs[:len(couts)], refs[len(couts):]
        scr, (send, recv) = refs[:n_scr], refs[n_scr:]

        def each(phase):
            ii = io = base = 0
            for j in comm:
                getattr(j, phase)(cin[ii:ii + len(j.ops)], cout[io:io + len(j.outs)], send, recv, base)
                ii, io, base = ii + len(j.ops), io + len(j.outs), base + j.nsem

        pids = [pl.program_id(d) for d in range(len(grid))]
        first = functools.reduce(jnp.logical_and, [p == 0 for p in pids])
        last = functools.reduce(jnp.logical_and, [p == g - 1 for p, g in zip(pids, grid)])
        pl.when(first)(lambda: each("start"))
        body(*ins, *outs, *scr)
        pl.when(last)(lambda: each("finish"))

    res = pl.pallas_call(
        hosted, name=name, grid=grid, in_specs=list(in_specs) + [ANY] * len(cops),
        out_specs=list(out_specs) + [ANY] * len(couts), out_shape=list(out_shape) + couts,
        input_output_aliases=_job_aliases(comm, n_in, n_out),
        scratch_shapes=list(scratch_shapes) + [pltpu.SemaphoreType.DMA((nsem,)), pltpu.SemaphoreType.DMA((nsem,))],
        compiler_params=pltpu.CompilerParams(dimension_semantics=("arbitrary",) * len(grid),
                                             vmem_limit_bytes=VMEM_LIMIT_BYTES, has_side_effects=True),
    )(*ops, *cops)
    io = n_out
    for j in comm:
        j.result = list(res[io:io + len(j.outs)])
        io += len(j.outs)
    return list(res[:n_out])


def _pcall(body, *, name, grid, in_specs, out_specs, out_shape, scratch_shapes=(), sem, comm=()):
    single = not isinstance(out_shape, (list, tuple))

    def run(*ops):
        res = hosted_call(body, name=name, grid=grid, in_specs=list(in_specs),
                          out_specs=[out_specs] if single else list(out_specs),
                          out_shape=[out_shape] if single else list(out_shape), ops=list(ops),
                          scratch_shapes=scratch_shapes, semantics=sem, comm=comm)
        return res[0] if single else res

    return run


def _mm_call(core, grid, in_specs, ops, out_spec, out_shape, acc_shape, name, comm):
    n_in = len(ops)

    def body(*refs):
        core(refs[:n_in], refs[n_in], refs[n_in + 1])

    return hosted_call(body, name=name, grid=grid, in_specs=in_specs, out_specs=[out_spec], out_shape=[out_shape],
                       ops=ops, scratch_shapes=[pltpu.VMEM(acc_shape, F32)],
                       semantics=("parallel", "parallel", "arbitrary"), comm=comm)[0]


def mm_nn(a, w3, *, bias=None, res=None, out_dtype=F32, name, comm=()):
    M, K = a.shape
    S, K2, ns = w3.shape
    assert K == K2
    tm = _tile(M, 512)
    tk = K if K <= MM_MAX_K else _tile(K, 1536)
    nk = K // tk
    has_b, has_r = bias is not None, res is not None

    def core(ins, o_ref, acc):
        a_ref, w_ref = ins[0], ins[1]
        b_ref = ins[2] if has_b else None
        r_ref = ins[2 + has_b] if has_r else None

        def partial():
            return jnp.dot(a_ref[...].astype(BF16), w_ref[...], preferred_element_type=F32)

        def finish(r):
            if has_b:
                r = r + b_ref[...]
            if has_r:
                r = r + r_ref[...]
            o_ref[...] = r.astype(o_ref.dtype)

        _acc_matmul(nk, pl.program_id(2), acc, partial, finish)

    in_specs = [_vspec((tm, tk), lambda s, i, k: (i, k)), _vspec((None, tk, ns), lambda s, i, k: (s, k, 0))]
    ops = [a, w3]
    if has_b:
        in_specs.append(_vspec((1, ns), lambda s, i, k: (0, s)))
        ops.append(bias)
    if has_r:
        in_specs.append(_vspec((tm, ns), lambda s, i, k: (i, s)))
        ops.append(res)
    return _mm_call(core, (S, M // tm, nk), in_specs, ops, _vspec((tm, ns), lambda s, i, k: (i, s)),
                    jax.ShapeDtypeStruct((M, S * ns), out_dtype), (tm, ns) if nk > 1 else (SUBLANES, LANES), name, comm)


def mm_nt(a, w3, *, res=None, out_dtype=F32, name, comm=()):
    M, N = a.shape
    S, K, ns = w3.shape
    assert N == S * ns
    tko = K if K <= MM_MAX_K else _tile(K, 1024)
    tm = _tile(M, 512 if tko == K else 1024)
    tc = ns if ns <= MM_MAX_K else _tile(ns, 1024)
    ncs = ns // tc
    nc = S * ncs
    has_r = res is not None

    def core(ins, o_ref, acc):
        a_ref, w_ref = ins[0], ins[1]
        r_ref = ins[2] if has_r else None

        def partial():
            return lax.dot_general(a_ref[...].astype(BF16), w_ref[...], (((1,), (1,)), ((), ())),
                                   preferred_element_type=F32)

        def finish(r):
            if has_r:
                r = r + r_ref[...]
            o_ref[...] = r.astype(o_ref.dtype)

        _acc_matmul(nc, pl.program_id(2), acc, partial, finish)

    in_specs = [_vspec((tm, tc), lambda i, j, k: (i, k)),
                _vspec((None, tko, tc), lambda i, j, k: (k // ncs, j, k % ncs))]
    ops = [a, w3]
    if has_r:
        in_specs.append(_vspec((tm, tko), lambda i, j, k: (i, j)))
        ops.append(res)
    return _mm_call(core, (M // tm, K // tko, nc), in_specs, ops, _vspec((tm, tko), lambda i, j, k: (i, j)),
                    jax.ShapeDtypeStruct((M, K), out_dtype), (tm, tko) if nc > 1 else (SUBLANES, LANES), name, comm)


def mm_tn(x, dy, S, *, out_dtype=BF16, name, comm=()):
    M, K = x.shape
    M2, N = dy.shape
    assert M == M2 and N % S == 0
    ns = N // S
    tk = _tile(K, 1536) if (K > MM_MAX_K and dy.dtype == BF16) else _tile(K, 1024)
    tmc = _tile(M, 512 if tk > 1024 else 1024)
    nm = M // tmc

    def core(ins, o_ref, acc):
        x_ref, dy_ref = ins

        def partial():
            return lax.dot_general(x_ref[...].astype(BF16), dy_ref[...].astype(BF16), (((0,), (0,)), ((), ())),
                                   preferred_element_type=F32)

        def finish(r):
            o_ref[...] = r.astype(o_ref.dtype)

        _acc_matmul(nm, pl.program_id(2), acc, partial, finish)

    in_specs = [_vspec((tmc, tk), lambda s, i, m: (m, i)), _vspec((tmc, ns), lambda s, i, m: (m, s))]
    return _mm_call(core, (S, K // tk, nm), in_specs, [x, dy], _vspec((None, tk, ns), lambda s, i, m: (s, i, 0)),
                    jax.ShapeDtypeStruct((S, K, ns), out_dtype), (tk, ns) if nm > 1 else (SUBLANES, LANES), name, comm)


def rms_fwd(h, g, *, name="rms_fwd"):
    R, D = h.shape
    tr = _tile8(R, 256)

    def body(h_ref, g_ref, o_ref):
        x = h_ref[...]
        r = lax.rsqrt(jnp.mean(x * x, axis=-1, keepdims=True) + RMS_EPS)
        o_ref[...] = (x * r * g_ref[...]).astype(o_ref.dtype)

    return pl.pallas_call(
        body, name=name, grid=(R // tr,),
        in_specs=[_vspec((tr, D), lambda i: (i, 0)), _vspec((1, D), lambda i: (0, 0))],
        out_specs=_vspec((tr, D), lambda i: (i, 0)),
        out_shape=jax.ShapeDtypeStruct((R, D), BF16),
        compiler_params=_params(("parallel",)),
    )(h, g)


def _tile8(n, pref):
    t = (min(pref, n) // SUBLANES) * SUBLANES
    while t >= SUBLANES:
        if n % t == 0:
            return t
        t -= SUBLANES
    return n


def rms_bwd(h, g, dhn, dres, *, name="rms_bwd", comm=()):
    R, D = h.shape
    tr = _tile8(R, 256)

    def body(h_ref, g_ref, dhn_ref, dres_ref, dh_ref, dg_ref, dh16_ref):
        @pl.when(pl.program_id(0) == 0)
        def _():
            dg_ref[...] = jnp.zeros_like(dg_ref)

        x = h_ref[...]
        d = dhn_ref[...].astype(F32)
        r = lax.rsqrt(jnp.mean(x * x, axis=-1, keepdims=True) + RMS_EPS)
        xhat = x * r
        dg_ref[...] += jnp.sum(d * xhat, axis=0, keepdims=True)
        t = d * g_ref[...]
        dh = dres_ref[...] + r * (t - xhat * jnp.mean(t * xhat, axis=-1, keepdims=True))
        dh_ref[...] = dh
        dh16_ref[...] = dh.astype(BF16)

    return _pcall(
        body, name=name, grid=(R // tr,),
        in_specs=[_vspec((tr, D), lambda i: (i, 0)), _vspec((1, D), lambda i: (0, 0)),
                  _vspec((tr, D), lambda i: (i, 0)), _vspec((tr, D), lambda i: (i, 0))],
        out_specs=[_vspec((tr, D), lambda i: (i, 0)), _vspec((1, D), lambda i: (0, 0)), _vspec((tr, D), lambda i: (i, 0))],
        out_shape=[jax.ShapeDtypeStruct((R, D), F32), jax.ShapeDtypeStruct((1, D), F32), jax.ShapeDtypeStruct((R, D), BF16)],
        sem=("arbitrary",), comm=comm,
    )(h, g, dhn, dres)


def final_loss(h, g, target, *, name="final_loss"):
    R, D = h.shape
    tr = _tile8(R, 256)

    def body(h_ref, g_ref, t_ref, loss_ref, dh_ref, dg_ref, dh16_ref):
        @pl.when(pl.program_id(0) == 0)
        def _():
            dg_ref[...] = jnp.zeros_like(dg_ref)
            loss_ref[...] = jnp.zeros_like(loss_ref)

        x = h_ref[...]
        r = lax.rsqrt(jnp.mean(x * x, axis=-1, keepdims=True) + RMS_EPS)
        xhat = x * r
        err = xhat * g_ref[...] - t_ref[...]
        row = jnp.mean(err * err, axis=-1, keepdims=True)
        loss_ref[...] += 0.5 * jnp.sum(row, axis=0, keepdims=True)
        d = err * (1.0 / D)
        dg_ref[...] += jnp.sum(d * xhat, axis=0, keepdims=True)
        t = d * g_ref[...]
        dh = r * (t - xhat * jnp.mean(t * xhat, axis=-1, keepdims=True))
        dh_ref[...] = dh
        dh16_ref[...] = dh.astype(BF16)

    return pl.pallas_call(
        body, name=name, grid=(R // tr,),
        in_specs=[_vspec((tr, D), lambda i: (i, 0)), _vspec((1, D), lambda i: (0, 0)), _vspec((tr, D), lambda i: (i, 0))],
        out_specs=[_vspec((1, 1), lambda i: (0, 0)), _vspec((tr, D), lambda i: (i, 0)), _vspec((1, D), lambda i: (0, 0)),
                   _vspec((tr, D), lambda i: (i, 0))],
        out_shape=[jax.ShapeDtypeStruct((1, 1), F32), jax.ShapeDtypeStruct((R, D), F32), jax.ShapeDtypeStruct((1, D), F32),
                   jax.ShapeDtypeStruct((R, D), BF16)],
        compiler_params=_params(("arbitrary",)),
    )(h, g, target)


def _shift_down(v, k):
    rows = lax.broadcasted_iota(jnp.int32, v.shape, 0)
    return jnp.where(rows >= k, pltpu.roll(v, k, axis=0), 0.0)


def _shift_up(v, k):
    L = v.shape[0]
    rows = lax.broadcasted_iota(jnp.int32, v.shape, 0)
    return jnp.where(rows < L - k, pltpu.roll(v, L - k, axis=0), 0.0)


def _conv(v, w):
    return w[2:3, :] * v + w[1:2, :] * _shift_down(v, 1) + w[0:1, :] * _shift_down(v, 2)


def _conv_t(d, w):
    return w[2:3, :] * d + w[1:2, :] * _shift_up(d, 1) + w[0:1, :] * _shift_up(d, 2)


def _conv_dw(d, v):
    return jnp.concatenate([
        jnp.sum(d * _shift_down(v, 2), axis=0, keepdims=True),
        jnp.sum(d * _shift_down(v, 1), axis=0, keepdims=True),
        jnp.sum(d * v, axis=0, keepdims=True)], axis=0)


COL_BLOCK = 128


def conv_mixer_fwd(z, cw, *, name="conv_mixer_fwd"):
    L = z.shape[0]
    A = cw.shape[1]
    cb = _tile(A, COL_BLOCK)
    nb = A // cb

    def body(gb_ref, gc_ref, xa_ref, w_ref, o_ref):
        v = gc_ref[...] * xa_ref[...]
        o_ref[...] = (gb_ref[...] * _conv(v, w_ref[...])).astype(o_ref.dtype)

    return pl.pallas_call(
        body, name=name, grid=(nb,),
        in_specs=[_vspec((L, cb), lambda j: (0, j)), _vspec((L, cb), lambda j: (0, nb + j)),
                  _vspec((L, cb), lambda j: (0, 2 * nb + j)), _vspec((3, cb), lambda j: (0, j))],
        out_specs=_vspec((L, cb), lambda j: (0, j)),
        out_shape=jax.ShapeDtypeStruct((L, A), BF16),
        compiler_params=_params(("parallel",)),
    )(z, z, z, cw)


def conv_mixer_bwd(z, cw, dya, *, name="conv_mixer_bwd"):
    L = z.shape[0]
    A = cw.shape[1]
    cb = _tile(A, COL_BLOCK)
    nb = A // cb

    def body(gb_ref, gc_ref, xa_ref, w_ref, d_ref, dgb_ref, dgc_ref, dxa_ref, dw_ref):
        gc, xa, w, d = gc_ref[...], xa_ref[...], w_ref[...], d_ref[...]
        v = gc * xa
        dgb_ref[...] = (d * _conv(v, w)).astype(dgb_ref.dtype)
        dc = d * gb_ref[...]
        dw_ref[...] = _conv_dw(dc, v)
        dv = _conv_t(dc, w)
        dgc_ref[...] = (dv * xa).astype(dgc_ref.dtype)
        dxa_ref[...] = (dv * gc).astype(dxa_ref.dtype)

    col = lambda j: (0, j)
    outs = pl.pallas_call(
        body, name=name, grid=(nb,),
        in_specs=[_vspec((L, cb), col), _vspec((L, cb), lambda j: (0, nb + j)),
                  _vspec((L, cb), lambda j: (0, 2 * nb + j)), _vspec((3, cb), col), _vspec((L, cb), col)],
        out_specs=[_vspec((L, cb), col), _vspec((L, cb), col), _vspec((L, cb), col), _vspec((3, cb), col)],
        out_shape=[jax.ShapeDtypeStruct((L, A), BF16)] * 3 + [jax.ShapeDtypeStruct((3, A), F32)],
        compiler_params=_params(("parallel",)),
    )(z, z, z, cw, dya)
    return outs[0], outs[1], outs[2], outs[3]


def ffn_act_fwd(gpre, up, cw, cbias, *, name="ffn_act_fwd", comm=()):
    L, Fd = gpre.shape
    cb = _tile(Fd, COL_BLOCK)

    def body(g_ref, u_ref, w_ref, b_ref, o_ref):
        g = _conv(g_ref[...].astype(F32), w_ref[...]) + b_ref[...]
        o_ref[...] = (g * jax.nn.sigmoid(g) * u_ref[...].astype(F32)).astype(o_ref.dtype)

    col = lambda j: (0, j)
    return _pcall(
        body, name=name, grid=(Fd // cb,),
        in_specs=[_vspec((L, cb), col), _vspec((L, cb), col), _vspec((3, cb), col), _vspec((1, cb), col)],
        out_specs=_vspec((L, cb), col),
        out_shape=jax.ShapeDtypeStruct((L, Fd), BF16),
        sem=("parallel",), comm=comm,
    )(gpre, up, cw, cbias)


def ffn_act_bwd(dact, gpre, up, cw, cbias, *, name="ffn_act_bwd", comm=()):
    L, Fd = gpre.shape
    cb = _tile(Fd, COL_BLOCK)

    def body(d_ref, g_ref, u_ref, w_ref, b_ref, dg_ref, du_ref, dw_ref, db_ref):
        gp, w, d = g_ref[...].astype(F32), w_ref[...], d_ref[...].astype(F32)
        g = _conv(gp, w) + b_ref[...]
        sg = jax.nn.sigmoid(g)
        du_ref[...] = (d * (g * sg)).astype(du_ref.dtype)
        dg = d * u_ref[...].astype(F32) * (sg * (1.0 + g * (1.0 - sg)))
        db_ref[...] = jnp.sum(dg, axis=0, keepdims=True)
        dw_ref[...] = _conv_dw(dg, gp)
        dg_ref[...] = _conv_t(dg, w).astype(dg_ref.dtype)

    col = lambda j: (0, j)
    return _pcall(
        body, name=name, grid=(Fd // cb,),
        in_specs=[_vspec((L, cb), col)] * 3 + [_vspec((3, cb), col), _vspec((1, cb), col)],
        out_specs=[_vspec((L, cb), col), _vspec((L, cb), col), _vspec((3, cb), col), _vspec((1, cb), col)],
        out_shape=[jax.ShapeDtypeStruct((L, Fd), BF16), jax.ShapeDtypeStruct((L, Fd), BF16),
                   jax.ShapeDtypeStruct((3, Fd), F32), jax.ShapeDtypeStruct((1, Fd), F32)],
        sem=("parallel",), comm=comm,
    )(dact, gpre, up, cw, cbias)


def col_sum(x, *, name="col_sum"):
    R, C = x.shape
    tr = _tile8(R, 512)

    def body(x_ref, o_ref):
        @pl.when(pl.program_id(0) == 0)
        def _():
            o_ref[...] = jnp.zeros_like(o_ref)

        o_ref[...] += jnp.sum(x_ref[...].astype(F32), axis=0, keepdims=True)

    return pl.pallas_call(
        body, name=name, grid=(R // tr,),
        in_specs=[_vspec((tr, C), lambda i: (i, 0))], out_specs=_vspec((1, C), lambda i: (0, 0)),
        out_shape=jax.ShapeDtypeStruct((1, C), F32), compiler_params=_params(("arbitrary",)),
    )(x)


def _cmul_add(ar, ai, sr, si, br, bi):
    return ar * sr - ai * si + br, ar * si + ai * sr + bi


def _segment_starts(fin, pr, pi, reverse):
    H = fin.shape[1] // 2
    rows = lax.broadcasted_iota(jnp.int32, fin.shape, 0)
    cr = jnp.zeros((1, H), F32)
    ci = jnp.zeros((1, H), F32)
    out = jnp.zeros(fin.shape, F32)
    order = range(N_SEG - 1, -1, -1) if reverse else range(N_SEG)
    for k in order:
        out = jnp.where(rows == k, jnp.concatenate([cr, ci], axis=1), out)
        cr, ci = _cmul_add(pr, pi, cr, ci, fin[k:k + 1, :H], fin[k:k + 1, H:])
    return out


def _gelu(y):
    c0 = math.sqrt(2.0 / math.pi)
    t = jnp.tanh(c0 * (y + 0.044715 * y * y * y))
    return 0.5 * y * (1.0 + t), t


def s5_scan_fwd(u_p, a_l, apow_l, bm, cm, d_l, glu, fin, *, name, comm=()):
    L, C = u_p.shape
    NK, _, SW = bm.shape
    H = SW // 2
    RB = S5_STEPS * N_SEG
    NC = L // RB
    final_only = fin is None

    def scan_chunk(a_ref, buf, st):
        ar = jnp.broadcast_to(a_ref[:, :H], (N_SEG, H))
        ai = jnp.broadcast_to(a_ref[:, H:], (N_SEG, H))

        def step(j, carry):
            sr, si = carry
            rows = pl.ds(pl.multiple_of(j * N_SEG, N_SEG), N_SEG)
            sr, si = _cmul_add(ar, ai, sr, si, buf[rows, :H], buf[rows, H:])
            buf[rows, :H] = sr
            buf[rows, H:] = si
            return sr, si

        sr, si = lax.fori_loop(0, S5_STEPS, step, (st[:, :H], st[:, H:]), unroll=4)
        st[:, :H] = sr
        st[:, H:] = si

    if final_only:
        def body(u_ref, a_ref, bm_ref, fin_ref, buf, st):
            @pl.when(pl.program_id(1) == 0)
            def _():
                st[...] = jnp.zeros_like(st)

            buf[...] = jnp.dot(u_ref[...].astype(BF16), bm_ref[...], preferred_element_type=F32)
            scan_chunk(a_ref, buf, st)
            fin_ref[...] = st[...]

        return _pcall(
            body, name=name, grid=(NK, NC),
            in_specs=[_vspec((RB, LANES), lambda k, j: (j, k)), _vspec((None, 1, SW), lambda k, j: (k, 0, 0)),
                      _vspec((None, LANES, SW), lambda k, j: (k, 0, 0))],
            out_specs=_vspec((None, N_SEG, SW), lambda k, j: (k, 0, 0)),
            out_shape=jax.ShapeDtypeStruct((NK, N_SEG, SW), F32),
            scratch_shapes=[pltpu.VMEM((RB, SW), F32), pltpu.VMEM((N_SEG, SW), F32)],
            sem=("parallel", "arbitrary"), comm=comm,
        )(u_p, a_l, bm)

    def body(u_ref, a_ref, ap_ref, bm_ref, cm_ref, d_ref, glu_ref, fin_ref, o_ref, y_ref, s_ref, start_ref, buf, st):
        @pl.when(pl.program_id(1) == 0)
        def _():
            st[...] = _segment_starts(fin_ref[...], ap_ref[:, :H], ap_ref[:, H:], False)
            start_ref[...] = st[...]

        u = u_ref[...]
        buf[...] = jnp.dot(u.astype(BF16), bm_ref[...], preferred_element_type=F32)
        scan_chunk(a_ref, buf, st)
        states = buf[...]
        s_ref[...] = states
        y = jnp.dot(states.astype(BF16), cm_ref[...], preferred_element_type=F32) + d_ref[...] * u
        y_ref[...] = y
        yg, _ = _gelu(y)
        gate = jnp.dot(yg.astype(BF16), glu_ref[...], preferred_element_type=F32)
        o_ref[...] = (yg * jax.nn.sigmoid(gate)).astype(o_ref.dtype)

    blk = lambda k, j: (j, k)
    per_k = lambda k, j: (k, 0, 0)
    return _pcall(
        body, name=name, grid=(NK, NC),
        in_specs=[_vspec((RB, LANES), blk), _vspec((None, 1, SW), per_k), _vspec((None, 1, SW), per_k),
                  _vspec((None, LANES, SW), per_k), _vspec((None, SW, LANES), per_k), _vspec((1, LANES), lambda k, j: (0, k)),
                  _vspec((None, LANES, LANES), per_k), _vspec((None, N_SEG, SW), per_k)],
        out_specs=[_vspec((RB, LANES), blk), _vspec((RB, LANES), blk), _vspec((RB, SW), blk),
                   _vspec((None, N_SEG, SW), per_k)],
        out_shape=[jax.ShapeDtypeStruct((L, C), BF16), jax.ShapeDtypeStruct((L, C), F32),
                   jax.ShapeDtypeStruct((L, NK * SW), F32), jax.ShapeDtypeStruct((NK, N_SEG, SW), F32)],
        scratch_shapes=[pltpu.VMEM((RB, SW), F32), pltpu.VMEM((N_SEG, SW), F32)],
        sem=("parallel", "arbitrary"), comm=comm,
    )(u_p, a_l, apow_l, bm, cm, d_l, glu, fin)


def s5_out_bwd(dout_p, y_p, u_p, glu, *, name="s5_out_bwd", comm=()):
    L, C = y_p.shape
    NK = C // LANES
    tr = _tile8(L, 512)

    def body(do_ref, y_ref, u_ref, glu_ref, dy_ref, dglu_ref, dd_ref):
        @pl.when(pl.program_id(1) == 0)
        def _():
            dglu_ref[...] = jnp.zeros_like(dglu_ref)
            dd_ref[...] = jnp.zeros_like(dd_ref)

        y, do, w = y_ref[...], do_ref[...].astype(F32), glu_ref[...]
        yg, t = _gelu(y)
        sg = jax.nn.sigmoid(jnp.dot(yg.astype(BF16), w, preferred_element_type=F32))
        dgate = (do * yg * sg * (1.0 - sg)).astype(BF16)
        dyg = do * sg + lax.dot_general(dgate, w, (((1,), (1,)), ((), ())), preferred_element_type=F32)
        dglu_ref[...] += lax.dot_general(yg.astype(BF16), dgate, (((0,), (0,)), ((), ())), preferred_element_type=F32)
        c0 = math.sqrt(2.0 / math.pi)
        dgelu = 0.5 * (1.0 + t) + 0.5 * y * (1.0 - t * t) * c0 * (1.0 + 3.0 * 0.044715 * y * y)
        dy = dyg * dgelu
        dy_ref[...] = dy
        dd_ref[...] += jnp.sum(dy * u_ref[...], axis=0, keepdims=True)

    blk = lambda k, i: (i, k)
    return _pcall(
        body, name=name, grid=(NK, L // tr),
        in_specs=[_vspec((tr, LANES), blk)] * 3 + [_vspec((None, LANES, LANES), lambda k, i: (k, 0, 0))],
        out_specs=[_vspec((tr, LANES), blk), _vspec((None, LANES, LANES), lambda k, i: (k, 0, 0)),
                   _vspec((1, LANES), lambda k, i: (0, k))],
        out_shape=[jax.ShapeDtypeStruct((L, C), F32), jax.ShapeDtypeStruct((NK, LANES, LANES), F32),
                   jax.ShapeDtypeStruct((1, C), F32)],
        sem=("parallel", "arbitrary"), comm=comm,
    )(dout_p, y_p, u_p, glu)


def s5_scan_bwd(dy_p, a_l, apow_l, cmt, gfin, states=None, starts=None, u_p=None, bmt=None, d_l=None, *, name, comm=()):
    L, C = dy_p.shape
    NK, _, SW = cmt.shape
    H = SW // 2
    RB = S5_STEPS * N_SEG
    NC = L // RB
    final_only = gfin is None

    def scan_chunk(a_ref, buf, st):
        ar = jnp.broadcast_to(a_ref[:, :H], (N_SEG, H))
        ai = -jnp.broadcast_to(a_ref[:, H:], (N_SEG, H))

        def step(jj, carry):
            gr, gi = carry
            j = S5_STEPS - 1 - jj
            rows = pl.ds(pl.multiple_of(j * N_SEG, N_SEG), N_SEG)
            gr, gi = _cmul_add(ar, ai, gr, gi, buf[rows, :H], buf[rows, H:])
            buf[rows, :H] = gr
            buf[rows, H:] = gi
            return gr, gi

        gr, gi = lax.fori_loop(0, S5_STEPS, step, (st[:, :H], st[:, H:]), unroll=4)
        st[:, :H] = gr
        st[:, H:] = gi

    rblk = lambda k, j: (NC - 1 - j, k)
    per_k = lambda k, j: (k, 0, 0)

    if final_only:
        def body(dy_ref, a_ref, cmt_ref, fin_ref, buf, st):
            @pl.when(pl.program_id(1) == 0)
            def _():
                st[...] = jnp.zeros_like(st)

            buf[...] = jnp.dot(dy_ref[...].astype(BF16), cmt_ref[...], preferred_element_type=F32)
            scan_chunk(a_ref, buf, st)
            fin_ref[...] = st[...]

        return _pcall(
            body, name=name, grid=(NK, NC),
            in_specs=[_vspec((RB, LANES), rblk), _vspec((None, 1, SW), per_k), _vspec((None, LANES, SW), per_k)],
            out_specs=_vspec((None, N_SEG, SW), per_k),
            out_shape=jax.ShapeDtypeStruct((NK, N_SEG, SW), F32),
            scratch_shapes=[pltpu.VMEM((RB, SW), F32), pltpu.VMEM((N_SEG, SW), F32)],
            sem=("parallel", "arbitrary"), comm=comm,
        )(dy_p, a_l, cmt)

    def body(dy_ref, a_ref, ap_ref, cmt_ref, gfin_ref, s_ref, sprev_ref, start_ref, u_ref, bmt_ref, d_ref,
             du_ref, da_ref, dbm_ref, dcm_ref, buf, st):
        jc = pl.program_id(1)

        @pl.when(jc == 0)
        def _():
            st[...] = _segment_starts(gfin_ref[...], ap_ref[:, :H], -ap_ref[:, H:], True)
            da_ref[...] = jnp.zeros_like(da_ref)
            dbm_ref[...] = jnp.zeros_like(dbm_ref)
            dcm_ref[...] = jnp.zeros_like(dcm_ref)

        dy = dy_ref[...]
        dyb = dy.astype(BF16)
        buf[...] = jnp.dot(dyb, cmt_ref[...], preferred_element_type=F32)
        scan_chunk(a_ref, buf, st)
        g = buf[...]
        s = s_ref[...]
        first = jnp.where(jc == NC - 1, start_ref[...], sprev_ref[...])
        sp = jnp.concatenate([first, s[:RB - N_SEG, :]], axis=0)
        gr, gi, pr, pi = g[:, :H], g[:, H:], sp[:, :H], sp[:, H:]
        da_ref[...] += jnp.concatenate([jnp.sum(gr * pr + gi * pi, axis=0, keepdims=True),
                                        jnp.sum(gi * pr - gr * pi, axis=0, keepdims=True)], axis=1)
        gb = g.astype(BF16)
        u = u_ref[...]
        du_ref[...] = (jnp.dot(gb, bmt_ref[...], preferred_element_type=F32) + dy * d_ref[...]).astype(du_ref.dtype)
        dbm_ref[...] += lax.dot_general(u.astype(BF16), gb, (((0,), (0,)), ((), ())), preferred_element_type=F32)
        dcm_ref[...] += lax.dot_general(s.astype(BF16), dyb, (((0,), (0,)), ((), ())), preferred_element_type=F32)

    prev8 = lambda k, j: (jnp.maximum((NC - 1 - j) * S5_STEPS - 1, 0), k)
    return _pcall(
        body, name=name, grid=(NK, NC),
        in_specs=[_vspec((RB, LANES), rblk), _vspec((None, 1, SW), per_k), _vspec((None, 1, SW), per_k),
                  _vspec((None, LANES, SW), per_k), _vspec((None, N_SEG, SW), per_k), _vspec((RB, SW), rblk),
                  _vspec((N_SEG, SW), prev8), _vspec((None, N_SEG, SW), per_k), _vspec((RB, LANES), rblk),
                  _vspec((None, SW, LANES), per_k), _vspec((1, LANES), lambda k, j: (0, k))],
        out_specs=[_vspec((RB, LANES), rblk), _vspec((None, 1, SW), per_k), _vspec((None, LANES, SW), per_k),
                   _vspec((None, SW, LANES), per_k)],
        out_shape=[jax.ShapeDtypeStruct((L, C), BF16), jax.ShapeDtypeStruct((NK, 1, SW), F32),
                   jax.ShapeDtypeStruct((NK, LANES, SW), F32), jax.ShapeDtypeStruct((NK, SW, LANES), F32)],
        scratch_shapes=[pltpu.VMEM((RB, SW), F32), pltpu.VMEM((N_SEG, SW), F32)],
        sem=("parallel", "arbitrary"), comm=comm,
    )(dy_p, a_l, apow_l, cmt, gfin, states, states, starts, u_p, bmt, d_l)


def _s5_prep(a_re, a_im, log_dt, b_re, b_im, c_re, c_im, d, glu_w, seg_len):
    G, P = a_re.shape
    Hc = b_re.shape[-1]
    gl = LANES // Hc
    nk = G // gl
    dt = jnp.exp(log_dt)[:, None]
    er = jnp.exp(a_re * dt)
    ab_r, ab_i = er * jnp.cos(a_im * dt), er * jnp.sin(a_im * dt)
    den = a_re * a_re + a_im * a_im
    nr, ni = ab_r - 1.0, ab_i
    q_r, q_i = (nr * a_re + ni * a_im) / den, (ni * a_re - nr * a_im) / den
    bb_r = q_r[..., None] * b_re - q_i[..., None] * b_im
    bb_i = q_r[..., None] * b_im + q_i[..., None] * b_re
    ep = jnp.exp(a_re * dt * seg_len)
    ap_r, ap_i = ep * jnp.cos(a_im * dt * seg_len), ep * jnp.sin(a_im * dt * seg_len)
    eye = jnp.eye(gl, dtype=F32)

    def lanes(t):
        return t.reshape(nk, 1, gl * P)

    def b_mat(t):
        return jnp.einsum("kgph,gq->kghqp", t.reshape(nk, gl, P, Hc), eye).reshape(nk, gl * Hc, gl * P)

    def c_mat(t):
        return jnp.einsum("kghp,gq->kgpqh", t.reshape(nk, gl, Hc, P), eye).reshape(nk, gl * P, gl * Hc)

    a_l = jnp.concatenate([lanes(ab_r), lanes(ab_i)], axis=-1)
    apow_l = jnp.concatenate([lanes(ap_r), lanes(ap_i)], axis=-1)
    bm = jnp.concatenate([b_mat(bb_r), b_mat(bb_i)], axis=-1)
    cm = jnp.concatenate([c_mat(c_re), -c_mat(c_im)], axis=1)
    glu = jnp.einsum("kgho,gq->kghqo", glu_w.reshape(nk, gl, Hc, Hc), eye).reshape(nk, gl * Hc, gl * Hc)
    return a_l, apow_l, bm, cm, d.reshape(1, G * Hc), glu


def _to_segments(t):
    L, C = t.shape
    return t.reshape(N_SEG, L // N_SEG, C).transpose(1, 0, 2).reshape(L, C)


def _from_segments(t):
    L, C = t.shape
    return t.reshape(L // N_SEG, N_SEG, C).transpose(1, 0, 2).reshape(L, C)


def _swa_probs(q, kk, bias_t, sink):
    s = lax.dot_general(kk, q, (((1,), (1,)), ((), ())), preferred_element_type=F32) * (HEAD_DIM ** -0.5)
    s = s + bias_t
    m = jnp.maximum(jnp.max(s, axis=0, keepdims=True), sink)
    e = jnp.exp(s - m)
    es = jnp.exp(sink - m)
    inv = 1.0 / (jnp.sum(e, axis=0, keepdims=True) + es)
    return e * inv, es * inv


def _swa_blocks(q_ref, kp_ref, kc_ref, vp_ref, vc_ref, bias_ref, n2):
    W = WINDOW
    rows = Q_PER_KV * W
    k0, k1, v0, v1 = kc_ref[0:W, :], kc_ref[W:, :], vc_ref[0:W, :], vc_ref[W:, :]
    table = bias_ref[jnp.minimum(n2, 1)]
    return [(0, q_ref[:, 0:W, :].reshape(rows, HEAD_DIM), jnp.concatenate([kp_ref[...], k0], axis=0),
             jnp.concatenate([vp_ref[...], v0], axis=0), table),
            (1, q_ref[:, W:, :].reshape(rows, HEAD_DIM), jnp.concatenate([k0, k1], axis=0),
             jnp.concatenate([v0, v1], axis=0), bias_ref[1])]


def _swa_specs(nq):
    W = WINDOW
    qs = _vspec((Q_PER_KV, 2 * W, HEAD_DIM), lambda g, n: (g, n, 0))
    kprev = _vspec((None, W, HEAD_DIM), lambda g, n: (g, jnp.maximum(2 * n - 1, 0), 0))
    kcur = _vspec((None, 2 * W, HEAD_DIM), lambda g, n: (g, n, 0))
    bias = _vspec((2, None, 2 * W, Q_PER_KV * W), lambda g, n: (0, g, 0, 0))
    dbias = _vspec((None, 2 * W, Q_PER_KV * W), lambda g, n: (g, 0, 0))
    sink = _vspec((1, Q_PER_KV * W), lambda g, n: (0, g))
    outs = _vspec((None, 2, HEAD_DIM, Q_PER_KV * W), lambda g, n: (g, n, 0, 0))
    return qs, kprev, kcur, bias, dbias, sink, outs


def _from_head_lanes(t, L):
    nkv = t.shape[0]
    t = t.reshape(nkv, L // WINDOW, HEAD_DIM, Q_PER_KV, WINDOW)
    return t.transpose(1, 4, 0, 3, 2).reshape(L, nkv * Q_PER_KV * HEAD_DIM)


def _masked_bias(bias_tab):
    qi = np.arange(WINDOW)[:, None]
    kj = np.arange(2 * WINDOW)[None, :]
    valid = ((kj < WINDOW) & (kj > qi)) | ((kj >= WINDOW) & (kj - WINDOW <= qi))
    first = valid & (kj >= WINDOW)
    both = jnp.stack([jnp.where(first[None], bias_tab, NEG_INF), jnp.where(valid[None], bias_tab, NEG_INF)])
    nkv = bias_tab.shape[0] // Q_PER_KV
    both = both.reshape(2, nkv, Q_PER_KV, WINDOW, 2 * WINDOW).transpose(0, 1, 4, 2, 3)
    return both.reshape(2, nkv, 2 * WINDOW, Q_PER_KV * WINDOW)


def _unmasked_dbias(dbias_t):
    nkv = dbias_t.shape[0]
    t = dbias_t.reshape(nkv, 2 * WINDOW, Q_PER_KV, WINDOW).transpose(0, 2, 3, 1)
    return t.reshape(nkv * Q_PER_KV, WINDOW, 2 * WINDOW)


def swa_fwd(qT, kT, vT, bias2, sink_row, *, name="swa_fwd", comm=()):
    NQ, L, _ = qT.shape
    NKV = kT.shape[0]
    qs, kprev, kcur, bs, _, sk, outs = _swa_specs(NQ)

    def body(q_ref, kp_ref, kc_ref, vp_ref, vc_ref, bias_ref, sink_ref, o_ref):
        sink = sink_ref[...]
        for b, q, kk, vv, bias in _swa_blocks(q_ref, kp_ref, kc_ref, vp_ref, vc_ref, bias_ref, pl.program_id(1)):
            p, _ = _swa_probs(q, kk, bias, sink)
            o = lax.dot_general(vv, p.astype(BF16), (((0,), (0,)), ((), ())), preferred_element_type=F32)
            o_ref[b] = o.astype(o_ref.dtype)

    return _pcall(
        body, name=name, grid=(NKV, L // (2 * WINDOW)),
        in_specs=[qs, kprev, kcur, kprev, kcur, bs, sk], out_specs=outs,
        out_shape=jax.ShapeDtypeStruct((NKV, L // WINDOW, HEAD_DIM, Q_PER_KV * WINDOW), BF16),
        sem=("parallel", "arbitrary"), comm=comm,
    )(qT, kT, kT, vT, vT, bias2, sink_row)


def swa_bwd(qT, kT, vT, bias2, sink_row, doT, *, name="swa_bwd", comm=()):
    NQ, L, _ = qT.shape
    NKV = kT.shape[0]
    qs, kprev, kcur, bs, dbs, sk, outs = _swa_specs(NQ)
    W = WINDOW

    def body(q_ref, kp_ref, kc_ref, vp_ref, vc_ref, bias_ref, sink_ref, do_ref,
             dq_ref, dk_ref, dv_ref, dbias_ref, dsink_ref):
        n = pl.program_id(1)

        @pl.when(n == 0)
        def _():
            dk_ref[...] = jnp.zeros_like(dk_ref)
            dv_ref[...] = jnp.zeros_like(dv_ref)
            dbias_ref[...] = jnp.zeros_like(dbias_ref)
            dsink_ref[...] = jnp.zeros_like(dsink_ref)

        sink = sink_ref[...]
        scale = HEAD_DIM ** -0.5
        grads = []
        for b, q, kk, vv, bias in _swa_blocks(q_ref, kp_ref, kc_ref, vp_ref, vc_ref, bias_ref, n):
            p, ps = _swa_probs(q, kk, bias, sink)
            do = do_ref[:, b * W:(b + 1) * W, :].reshape(Q_PER_KV * W, HEAD_DIM)
            dp = lax.dot_general(vv, do, (((1,), (1,)), ((), ())), preferred_element_type=F32)
            delta = jnp.sum(p * dp, axis=0, keepdims=True)
            ds = p * (dp - delta)
            dsink_ref[...] += -ps * delta
            dbias_ref[...] += ds
            dsb = ds.astype(BF16)
            dq = lax.dot_general(kk, dsb, (((0,), (0,)), ((), ())), preferred_element_type=F32) * scale
            dq_ref[b] = dq.astype(dq_ref.dtype)
            grads.append((jnp.dot(dsb, q, preferred_element_type=F32) * scale,
                          jnp.dot(p.astype(BF16), do, preferred_element_type=F32)))
        (dk0, dv0), (dk1, dv1) = grads

        @pl.when(n == 0)
        def _():
            dk_ref[0:W, :] += dk0[W:, :]
            dv_ref[0:W, :] += dv0[W:, :]

        @pl.when(n > 0)
        def _():
            rows = pl.ds(pl.multiple_of((2 * n - 1) * W, W), 2 * W)
            dk_ref[rows, :] += dk0
            dv_ref[rows, :] += dv0

        rows = pl.ds(pl.multiple_of(2 * n * W, W), 2 * W)
        dk_ref[rows, :] += dk1
        dv_ref[rows, :] += dv1

    whole = _vspec((None, L, HEAD_DIM), lambda g, n: (g, 0, 0))
    return _pcall(
        body, name=name, grid=(NKV, L // (2 * W)),
        in_specs=[qs, kprev, kcur, kprev, kcur, bs, sk, qs],
        out_specs=[outs, whole, whole, dbs, sk],
        out_shape=[jax.ShapeDtypeStruct((NKV, L // W, HEAD_DIM, Q_PER_KV * W), BF16),
                   jax.ShapeDtypeStruct((NKV, L, HEAD_DIM), F32), jax.ShapeDtypeStruct((NKV, L, HEAD_DIM), F32),
                   jax.ShapeDtypeStruct((NKV, 2 * W, Q_PER_KV * W), F32), jax.ShapeDtypeStruct((1, NQ * W), F32)],
        sem=("parallel", "arbitrary"), comm=comm,
    )(qT, kT, kT, vT, vT, bias2, sink_row, doT)


def _bucket_table():
    qi = np.arange(WINDOW)[:, None]
    kj = np.arange(2 * WINDOW)[None, :]
    rel = qi + WINDOW - kj
    max_exact = N_BUCKETS // 2
    n = np.maximum(rel, 0)
    nf = np.maximum(n, max_exact).astype(np.float32)
    large = max_exact + (np.log(nf / max_exact) / math.log(MAX_DISTANCE / max_exact) * (N_BUCKETS - max_exact)).astype(np.int32)
    large = np.minimum(large, N_BUCKETS - 1)
    return np.where(n < max_exact, n, large).astype(np.int32).reshape(-1)


def _xa_probs(q, k):
    hd = q.shape[1]
    s = lax.dot_general(q, k, (((1,), (1,)), ((), ())), preferred_element_type=F32) * (hd ** -0.5)
    e = jnp.exp(s - jnp.max(s, axis=1, keepdims=True))
    return e / jnp.sum(e, axis=1, keepdims=True)


def xattn_fwd(q, kv, *, name="xattn_fwd"):
    L, D = q.shape
    Mm = kv.shape[0]
    hd = D // X_HEADS
    tq = _tile8(L, 512)

    def body(q_ref, kv_ref, o_ref):
        for h in range(X_HEADS):
            cols = slice(h * hd, (h + 1) * hd)
            p = _xa_probs(q_ref[:, cols], kv_ref[:, cols])
            o_ref[:, cols] = jnp.dot(p.astype(BF16), kv_ref[:, D + h * hd:D + (h + 1) * hd],
                                     preferred_element_type=F32).astype(o_ref.dtype)

    return pl.pallas_call(
        body, name=name, grid=(L // tq,),
        in_specs=[_vspec((tq, D), lambda i: (i, 0)), _vspec((Mm, 2 * D), lambda i: (0, 0))],
        out_specs=_vspec((tq, D), lambda i: (i, 0)),
        out_shape=jax.ShapeDtypeStruct((L, D), BF16),
        compiler_params=_params(("parallel",)),
    )(q, kv)


def xattn_bwd(q, kv, do, *, name="xattn_bwd"):
    L, D = q.shape
    Mm = kv.shape[0]
    hd = D // X_HEADS
    tq = _tile8(L, 512)

    def body(q_ref, kv_ref, do_ref, dq_ref, dkv_ref):
        @pl.when(pl.program_id(0) == 0)
        def _():
            dkv_ref[...] = jnp.zeros_like(dkv_ref)

        for h in range(X_HEADS):
            cols = slice(h * hd, (h + 1) * hd)
            vcols = slice(D + h * hd, D + (h + 1) * hd)
            qh, kh, vh, doh = q_ref[:, cols], kv_ref[:, cols], kv_ref[:, vcols], do_ref[:, cols]
            p = _xa_probs(qh, kh)
            dp = lax.dot_general(doh, vh, (((1,), (1,)), ((), ())), preferred_element_type=F32)
            ds = (p * (dp - jnp.sum(p * dp, axis=1, keepdims=True)) * (hd ** -0.5)).astype(BF16)
            dq_ref[:, cols] = jnp.dot(ds, kh, preferred_element_type=F32).astype(dq_ref.dtype)
            dkv_ref[:, cols] += lax.dot_general(ds, qh, (((0,), (0,)), ((), ())), preferred_element_type=F32)
            dkv_ref[:, vcols] += lax.dot_general(p.astype(BF16), doh, (((0,), (0,)), ((), ())), preferred_element_type=F32)

    return pl.pallas_call(
        body, name=name, grid=(L // tq,),
        in_specs=[_vspec((tq, D), lambda i: (i, 0)), _vspec((Mm, 2 * D), lambda i: (0, 0)), _vspec((tq, D), lambda i: (i, 0))],
        out_specs=[_vspec((tq, D), lambda i: (i, 0)), _vspec((Mm, 2 * D), lambda i: (0, 0))],
        out_shape=[jax.ShapeDtypeStruct((L, D), BF16), jax.ShapeDtypeStruct((Mm, 2 * D), F32)],
        compiler_params=_params(("arbitrary",)),
    )(q, kv, do)


def adamw(w, g, m, v, *, name="adamw", comm=()):
    R, C = w.shape
    tr = _tile8(R, max(SUBLANES, (256 * 1024) // C // SUBLANES * SUBLANES))

    def body(w_ref, g_ref, m_ref, v_ref, d_ref, nm_ref, nv_ref):
        g_ = g_ref[...]
        nm = ADAM_B1 * m_ref[...] + (1.0 - ADAM_B1) * g_
        nv = ADAM_B2 * v_ref[...] + (1.0 - ADAM_B2) * (g_ * g_)
        m_hat = nm / (1.0 - ADAM_B1 ** ADAM_STEP)
        v_hat = nv / (1.0 - ADAM_B2 ** ADAM_STEP)
        d_ref[...] = -ADAM_LR * (m_hat / (jnp.sqrt(v_hat) + ADAM_EPS) + ADAM_WD * w_ref[...])
        nm_ref[...] = nm
        nv_ref[...] = nv

    spec = _vspec((tr, C), lambda i: (i, 0))
    return _pcall(
        body, name=name, grid=(R // tr,), in_specs=[spec] * 4, out_specs=[spec] * 3,
        out_shape=[jax.ShapeDtypeStruct((R, C), F32)] * 3, sem=("parallel",), comm=comm,
    )(w, g, m, v)


def _place():
    x, y, c = lax.axis_index("x"), lax.axis_index("y"), lax.axis_index("c")
    chips = [(1 - x, y), (x, 1 - y), (1 - x, 1 - y)]
    return x, y, c, chips


def _remote(src, dst, send, recv, k, to):
    return pltpu.make_async_remote_copy(src_ref=src, dst_ref=dst, send_sem=send.at[k], recv_sem=recv.at[k],
                                        device_id=to, device_id_type=MESH)


class _Job:
    result = None
    in_place = False

    def start(self, ins, outs, send, recv, base):
        for cp in self.copies(ins, outs, send, recv, base)[0]:
            cp.start()


class GatherJob(_Job):
    nsem = 7

    def __init__(self, w):
        self.ops = [w]
        self.outs = [jax.ShapeDtypeStruct((N_CHIPS,) + w.shape, w.dtype)]
        self.cost = N_CHIPS * w.size

    def copies(self, ins, outs, send, recv, base, first_only=True):
        w, out = ins[0], outs[0]
        x, y, c, chips = _place()
        me, sib = 2 * x + y, (x, y, 1 - c)
        h = w.shape[0] // 2
        mine, theirs = pl.ds(c * h, h), pl.ds((1 - c) * h, h)
        first = [_remote(w.at[mine], out.at[me, mine], send, recv, base + j, (px, py, c))
                 for j, (px, py) in enumerate(chips)]
        first.append(_remote(w, out.at[me], send, recv, base + 6, sib))
        if first_only:
            return first,
        landed = [out.at[2 * px + py, mine] for px, py in chips]
        lands = [_remote(w.at[mine], landed[j], send, recv, base + j, (px, py, c)) for j, (px, py) in enumerate(chips)]
        passes = [_remote(landed[j], landed[j], send, recv, base + 3 + j, sib) for j in range(3)]
        arrives = [_remote(w.at[theirs], out.at[2 * px + py, theirs], send, recv, base + 3 + j, sib)
                   for j, (px, py) in enumerate(chips)]
        return first, lands, passes, arrives

    def finish(self, ins, outs, send, recv, base):
        first, lands, passes, arrives = self.copies(ins, outs, send, recv, base, first_only=False)
        for land, fwd in zip(lands, passes):
            land.wait_recv()
            fwd.start()
        first[3].wait_recv()
        for cp in arrives:
            cp.wait_recv()
        for cp in first + passes:
            cp.wait_send()


class ExchangeJob(_Job):
    nsem = 1

    def __init__(self, g):
        self.ops = [g]
        self.outs = [jax.ShapeDtypeStruct((g.shape[0], g.shape[1] // 2, g.shape[2]), g.dtype)]
        self.cost = 0.15 * g.size

    def copies(self, ins, outs, send, recv, base):
        x, y, c, _ = _place()
        r2 = ins[0].shape[1] // 2
        return [_remote(ins[0].at[:, pl.ds((1 - c) * r2, r2)], outs[0], send, recv, base, (x, y, 1 - c))],

    def finish(self, ins, outs, send, recv, base):
        self.copies(ins, outs, send, recv, base)[0][0].wait()


class ScatterJob(_Job):
    nsem = 3

    def __init__(self, p):
        self.ops = [p]
        self.outs = [jax.ShapeDtypeStruct((N_CHIPS - 1,) + p.shape[1:], p.dtype)]
        self.cost = 2 * p.size

    def copies(self, ins, outs, send, recv, base):
        x, y, c, chips = _place()
        return [_remote(ins[0].at[2 * px + py], outs[0].at[j], send, recv, base + j, (px, py, c))
                for j, (px, py) in enumerate(chips)],

    def finish(self, ins, outs, send, recv, base):
        for cp in self.copies(ins, outs, send, recv, base)[0]:
            cp.wait()


class JoinJob(_Job):
    nsem = 1
    in_place = True

    def __init__(self, gbuf):
        self.ops = [gbuf]
        self.outs = [jax.ShapeDtypeStruct(gbuf.shape, gbuf.dtype)]
        self.cost = 0.3 * gbuf.size

    def copies(self, ins, outs, send, recv, base):
        x, y, c, _ = _place()
        r2 = outs[0].shape[1] // 2
        mine = outs[0].at[:, pl.ds(c * r2, r2)]
        return [_remote(mine, mine, send, recv, base, (x, y, 1 - c))],

    def finish(self, ins, outs, send, recv, base):
        x, y, c, _ = _place()
        r2 = outs[0].shape[1] // 2
        theirs = outs[0].at[:, pl.ds((1 - c) * r2, r2)]
        _remote(theirs, theirs, send, recv, base, (x, y, 1 - c)).wait_recv()
        self.copies(ins, outs, send, recv, base)[0][0].wait_send()


def run_comm(jobs, *, name):
    cops = [a for j in jobs for a in j.ops]
    couts = [s for j in jobs for s in j.outs]
    nsem = sum(j.nsem for j in jobs)

    def body(*refs):
        cin, cout = refs[:len(cops)], refs[len(cops):len(cops) + len(couts)]
        send, recv = refs[len(cops) + len(couts):]
        for phase in ("start", "finish"):
            ii = io = base = 0
            for j in jobs:
                getattr(j, phase)(cin[ii:ii + len(j.ops)], cout[io:io + len(j.outs)], send, recv, base)
                ii, io, base = ii + len(j.ops), io + len(j.outs), base + j.nsem

    res = pl.pallas_call(
        body, name=name, in_specs=[ANY] * len(cops), out_specs=[ANY] * len(couts), out_shape=couts,
        input_output_aliases=_job_aliases(jobs, 0, 0),
        scratch_shapes=[pltpu.SemaphoreType.DMA((nsem,)), pltpu.SemaphoreType.DMA((nsem,))],
        compiler_params=pltpu.CompilerParams(has_side_effects=True),
    )(*cops)
    io = 0
    for j in jobs:
        j.result = list(res[io:io + len(j.outs)])
        io += len(j.outs)


def add_half(g, other, c_idx, *, name="rs_add_half"):
    S, R, C = g.shape
    r2 = R // 2
    tr = _tile8(r2, max(SUBLANES, (512 * 1024) // C // SUBLANES * SUBLANES))
    nb = r2 // tr

    def body(c_ref, g_ref, o_ref, out_ref):
        out_ref[...] = (g_ref[...].astype(F32) + o_ref[...].astype(F32)).astype(out_ref.dtype)

    return pl.pallas_call(
        body, name=name,
        grid_spec=pltpu.PrefetchScalarGridSpec(
            num_scalar_prefetch=1, grid=(S, nb),
            in_specs=[pl.BlockSpec((None, tr, C), lambda s, i, c_ref: (s, c_ref[0] * nb + i, 0)),
                      pl.BlockSpec((None, tr, C), lambda s, i, c_ref: (s, i, 0))],
            out_specs=pl.BlockSpec((None, tr, C), lambda s, i, c_ref: (s, i, 0))),
        out_shape=jax.ShapeDtypeStruct((S, r2, C), BF16),
        compiler_params=_params(("parallel", "parallel")),
    )(c_idx, g, other)


def add_partials(p, got, place_idx, gbuf, layer, *, name="rs_add_partials"):
    S, r2, C = p.shape
    tr = _tile8(r2, max(SUBLANES, (512 * 1024) // C // SUBLANES * SUBLANES))
    nb = r2 // tr

    def body(pi_ref, p_ref, g_ref, buf_ref, out_ref):
        out_ref[...] = ((p_ref[...].astype(F32) + g_ref[0].astype(F32)) + g_ref[1].astype(F32)) + g_ref[2].astype(F32)

    return pl.pallas_call(
        body, name=name,
        grid_spec=pltpu.PrefetchScalarGridSpec(
            num_scalar_prefetch=1, grid=(nb,),
            in_specs=[pl.BlockSpec((None, tr, C), lambda i, pi: (pi[0], i, 0)),
                      pl.BlockSpec((N_CHIPS - 1, tr, C), lambda i, pi: (0, i, 0)),
                      ANY],
            out_specs=pl.BlockSpec((None, tr, C), lambda i, pi: (layer, pi[1] * nb + i, 0))),
        out_shape=jax.ShapeDtypeStruct(gbuf.shape, F32),
        input_output_aliases={3: 0},
        compiler_params=_params(("parallel",)),
    )(place_idx, p, got, gbuf)


def all_gather_rows(v, *, name="all_gather_small"):
    m, ncol = v.shape

    def body(x_ref, out_ref, send, recv, lsem):
        x, y, c, chips = _place()
        me, sib = (x, y, c), (x, y, 1 - c)

        def rows(px, py, pc):
            return out_ref.at[pl.ds((4 * px + 2 * py + pc) * m, m), :]

        def copy(k, block, to, src=None):
            return _remote(rows(*block) if src is None else src, rows(*block), send, recv, k, to)

        mine = pltpu.make_async_copy(x_ref, rows(*me), lsem)
        mine.start()
        first = [copy(0, me, sib, src=x_ref)]
        first += [copy(1 + j, me, (*chip, c), src=x_ref) for j, chip in enumerate(chips)]
        for cp in first:
            cp.start()
        passed = [copy(4 + j, (*chip, c), sib) for j, chip in enumerate(chips)]
        for j, chip in enumerate(chips):
            copy(1 + j, (*chip, c), me).wait_recv()
            passed[j].start()
        copy(0, sib, me).wait_recv()
        for j, chip in enumerate(chips):
            copy(4 + j, (*chip, 1 - c), me).wait_recv()
        for cp in first + passed:
            cp.wait_send()
        mine.wait()

    return pl.pallas_call(
        body, name=name,
        in_specs=[pl.BlockSpec(memory_space=pltpu.VMEM)], out_specs=pl.BlockSpec(memory_space=pltpu.VMEM),
        out_shape=jax.ShapeDtypeStruct((8 * m, ncol), v.dtype),
        scratch_shapes=[pltpu.SemaphoreType.DMA((7,)), pltpu.SemaphoreType.DMA((7,)), pltpu.SemaphoreType.DMA],
        compiler_params=pltpu.CompilerParams(vmem_limit_bytes=VMEM_LIMIT_BYTES, has_side_effects=True),
    )(v)


def sum_blocks(g8, *, name="sum_blocks"):
    nb, m, ncol = g8.shape
    tr = _tile8(m, 512)

    def body(g_ref, o_ref):
        acc = g_ref[0]
        for k in range(1, nb):
            acc = acc + g_ref[k]
        o_ref[...] = acc

    return pl.pallas_call(
        body, name=name, grid=(m // tr,),
        in_specs=[_vspec((nb, tr, ncol), lambda i: (0, i, 0))], out_specs=_vspec((tr, ncol), lambda i: (i, 0)),
        out_shape=jax.ShapeDtypeStruct((m, ncol), F32), compiler_params=_params(("parallel",)),
    )(g8)


PACK_ROWS = 256


def _pack(arrs, mult):
    flat = jnp.concatenate([a.reshape(-1) for a in arrs])
    pad = (-flat.shape[0]) % (mult * LANES)
    return jnp.pad(flat, (0, pad)).reshape(-1, LANES)


def _unpack(packed, like):
    flat = packed.reshape(-1)
    out, off = [], 0
    for a in like:
        out.append(flat[off:off + a.size].reshape(a.shape))
        off += a.size
    return out


def _rows2d(a):
    return a.reshape(-1, a.shape[-1])


def kernel(x, mem, norm_mix, norm_xattn, norm_ffn, norm_final, norm_mem, rel_bias, ev_w_in, ev_conv_w, s5_a_re, s5_a_im, s5_log_dt, s5_b_re, s5_b_im, s5_c_re, s5_c_im, s5_d, s5_glu_w, ev_w_out, od_w_qkv, od_b_qkv, od_sinks, od_w_out, xa_w_q, xa_w_kv, xa_w_o, ff_w_gate, ff_w_up, ff_conv_w, ff_conv_b, ff_w_down, loss_target, m_norm_mix, m_norm_xattn, m_norm_ffn, m_norm_final, m_norm_mem, m_rel_bias, m_ev_w_in, m_ev_conv_w, m_s5_a_re, m_s5_a_im, m_s5_log_dt, m_s5_b_re, m_s5_b_im, m_s5_c_re, m_s5_c_im, m_s5_d, m_s5_glu_w, m_ev_w_out, m_od_w_qkv, m_od_b_qkv, m_od_sinks, m_od_w_out, m_xa_w_q, m_xa_w_kv, m_xa_w_o, m_ff_w_gate, m_ff_w_up, m_ff_conv_w, m_ff_conv_b, m_ff_w_down, v_norm_mix, v_norm_xattn, v_norm_ffn, v_norm_final, v_norm_mem, v_rel_bias, v_ev_w_in, v_ev_conv_w, v_s5_a_re, v_s5_a_im, v_s5_log_dt, v_s5_b_re, v_s5_b_im, v_s5_c_re, v_s5_c_im, v_s5_d, v_s5_glu_w, v_ev_w_out, v_od_w_qkv, v_od_b_qkv, v_od_sinks, v_od_w_out, v_xa_w_q, v_xa_w_kv, v_xa_w_o, v_ff_w_gate, v_ff_w_up, v_ff_conv_w, v_ff_conv_b, v_ff_w_down):
    names = ["norm_mix", "norm_xattn", "norm_ffn", "norm_final", "norm_mem", "rel_bias", "ev_w_in", "ev_conv_w",
             "s5_a_re", "s5_a_im", "s5_log_dt", "s5_b_re", "s5_b_im", "s5_c_re", "s5_c_im", "s5_d", "s5_glu_w",
             "ev_w_out", "od_w_qkv", "od_b_qkv", "od_sinks", "od_w_out", "xa_w_q", "xa_w_kv", "xa_w_o",
             "ff_w_gate", "ff_w_up", "ff_conv_w", "ff_conv_b", "ff_w_down"]
    env = dict(locals())
    W = {k: env[k] for k in names}
    Mo = {k: env["m_" + k] for k in names}
    Vo = {k: env["v_" + k] for k in names}

    h = x[0]
    target = loss_target[0]
    L, D = h.shape
    depth = norm_mix.shape[0]
    c_idx = lax.axis_index("c").astype(jnp.int32).reshape(1)
    me_idx = (2 * lax.axis_index("x") + lax.axis_index("y")).astype(jnp.int32).reshape(1)

    col_sharded = ["ev_w_in", "od_w_qkv", "xa_w_kv", "ff_w_gate", "ff_w_up"]
    row_sharded = ["ev_w_out", "od_w_out", "xa_w_q", "xa_w_o", "ff_w_down"]
    small_sharded = ["ev_conv_w", "od_b_qkv", "ff_conv_w"]
    big = col_sharded + row_sharded

    def layer_weights(l):
        mix = [("ev_w_in", l // 2), ("ev_w_out", l // 2)] if l % 2 == 0 else [("od_w_qkv", l // 2), ("od_w_out", l // 2)]
        return mix + [(k, l) for k in ("xa_w_q", "xa_w_kv", "xa_w_o", "ff_w_gate", "ff_w_up", "ff_w_down")]

    gjob = {kl: GatherJob(W[kl[0]][kl[1]].astype(BF16)) for l in range(depth) for kl in layer_weights(l)}
    small_jobs = [GatherJob(W[k]) for k in small_sharded]
    pending = [gjob[kl] for l in range(depth) for kl in layer_weights(l)]
    run_comm([pending.pop(0)] + small_jobs, name="gather_first")
    flushes = []

    def take(queue, host_cost):
        jobs, acc = [], 0.0
        while queue and acc + queue[0].cost <= 1.25 * host_cost:
            acc += queue[0].cost
            jobs.append(queue.pop(0))
        return jobs

    def fwd_host(fn, units, *args, **kw):
        return fn(*args, comm=take(pending, units), **kw)

    def weight(k, l):
        job = gjob[(k, l)]
        if job.result is None:
            n = pending.index(job) + 1
            run_comm(pending[:n], name="gather_flush_%d" % len(flushes))
            flushes.append(n)
            del pending[:n]
        return job.result[0]

    def wcol(k, l):
        return weight(k, l)

    def wrow(k, l):
        g = weight(k, l)
        return g.reshape(1, g.shape[0] * g.shape[1], g.shape[2])

    def fwd_mm(a, w3, **kw):
        return mm_nn(a, w3, comm=take(pending, w3.size * a.shape[0] / L), **kw)

    sg = [j.result[0] for j in small_jobs]
    ev_conv_w_f = sg[0].transpose(1, 2, 0, 3).reshape(ev_conv_w.shape[0], 3, -1)
    od_b_qkv_f = sg[1].transpose(1, 0, 2).reshape(od_b_qkv.shape[0], 1, -1)
    ff_conv_w_f = sg[2].transpose(1, 2, 0, 3).reshape(ff_conv_w.shape[0], 3, -1)

    buckets = _bucket_table()
    NQ = D // HEAD_DIM
    NKV = NQ // Q_PER_KV
    onehot = jnp.asarray((buckets[:, None] == np.arange(N_BUCKETS)[None, :]).astype(np.float32))
    bias_tab = jnp.dot(rel_bias.T, onehot.T, precision=lax.Precision.HIGHEST).reshape(NQ, WINDOW, 2 * WINDOW)
    bias2 = _masked_bias(bias_tab)

    mem_n = rms_fwd(mem[0], norm_mem.reshape(1, D), name="rms_fwd_mem")

    saved = []
    for l in range(depth):
        i = l // 2
        s = {"h0": h}
        hn = rms_fwd(h, norm_mix[l].reshape(1, D))
        s["hn"] = hn
        if l % 2 == 0:
            A = ev_conv_w_f.shape[-1]
            z = fwd_mm(hn, wcol("ev_w_in", i), name="mm_ev_in")
            ya = conv_mixer_fwd(z, ev_conv_w_f[i])
            prep = functools.partial(_s5_prep, seg_len=L // N_SEG)
            s5p = (s5_a_re[i], s5_a_im[i], s5_log_dt[i], s5_b_re[i], s5_b_im[i], s5_c_re[i], s5_c_im[i], s5_d[i], s5_glu_w[i])
            (a_l, apow_l, bm, cm, d_l, glu), prep_vjp = jax.vjp(prep, *s5p)
            bm16, cm16, glu16 = bm.astype(BF16), cm.astype(BF16), glu.astype(BF16)
            u_p = _to_segments(z[:, 3 * A:])
            fin = fwd_host(s5_scan_fwd, HOST_UNITS["s5_ends"] * u_p.size,
                           u_p, a_l, apow_l, bm16, cm16, d_l, glu16, None, name="s5_fwd_ends")
            ys_p, y_p, states, starts = fwd_host(s5_scan_fwd, HOST_UNITS["s5_fwd"] * u_p.size,
                                                 u_p, a_l, apow_l, bm16, cm16, d_l, glu16, fin, name="s5_fwd")
            ycat = jnp.concatenate([ya, _from_segments(ys_p)], axis=1)
            s.update(z=z, u_p=u_p, y_p=y_p, states=states, starts=starts, ycat=ycat, prep_vjp=prep_vjp,
                     s5ops=(a_l, apow_l, bm16, cm16, d_l, glu16))
            h = fwd_mm(ycat, wrow("ev_w_out", i), res=h, name="mm_ev_out")
        else:
            z = fwd_mm(hn, wcol("od_w_qkv", i), bias=od_b_qkv_f[i], out_dtype=BF16, name="mm_od_qkv")
            qT = z[:, :NQ * HEAD_DIM].reshape(L, NQ, HEAD_DIM).transpose(1, 0, 2)
            kT = z[:, NQ * HEAD_DIM:(NQ + NKV) * HEAD_DIM].reshape(L, NKV, HEAD_DIM).transpose(1, 0, 2)
            vT = z[:, (NQ + NKV) * HEAD_DIM:].reshape(L, NKV, HEAD_DIM).transpose(1, 0, 2)
            sink_row = jnp.repeat(od_sinks[i], WINDOW).reshape(1, NQ * WINDOW)
            o = _from_head_lanes(fwd_host(swa_fwd, HOST_UNITS["swa_fwd"] * qT.size, qT, kT, vT, bias2, sink_row), L)
            s.update(qT=qT, kT=kT, vT=vT, sink_row=sink_row, o=o)
            h = fwd_mm(o, wrow("od_w_out", i), res=h, name="mm_od_out")
        s["h1"] = h
        hn2 = rms_fwd(h, norm_xattn[l].reshape(1, D))
        q = fwd_mm(hn2, wrow("xa_w_q", l), out_dtype=BF16, name="mm_xa_q")
        kv = fwd_mm(mem_n, wcol("xa_w_kv", l), out_dtype=BF16, name="mm_xa_kv")
        ox = xattn_fwd(q, kv)
        s.update(hn2=hn2, q=q, kv=kv, ox=ox)
        h = fwd_mm(ox, wrow("xa_w_o", l), res=h, name="mm_xa_o")
        s["h2"] = h
        hn3 = rms_fwd(h, norm_ffn[l].reshape(1, D))
        gpre = fwd_mm(hn3, wcol("ff_w_gate", l), out_dtype=BF16, name="mm_ff_gate")
        up = fwd_mm(hn3, wcol("ff_w_up", l), out_dtype=BF16, name="mm_ff_up")
        act = fwd_host(ffn_act_fwd, HOST_UNITS["ffn_act_fwd"] * gpre.size,
                       gpre, up, ff_conv_w_f[l], ff_conv_b[l].reshape(1, -1))
        s.update(hn3=hn3, gpre=gpre, up=up, act=act)
        h = fwd_mm(act, wrow("ff_w_down", l), res=h, name="mm_ff_down")
        saved.append(s)

    loss11, dh, dg_final, dh16 = final_loss(h, norm_final.reshape(1, D), target)
    loss = lax.psum(loss11[0, 0], AXES)

    gs = {k: [None] * W[k].shape[0] for k in names if k not in big and W[k].ndim > 1 and k != "rel_bias"}
    dmem_n = None
    dbias_tab = jnp.zeros_like(bias_tab)
    place_idx = jnp.concatenate([me_idx, c_idx])
    gbufs, exchanges, scatters = {}, [], []

    def settle(jobs):
        for j in jobs:
            k, l = j.tag
            if isinstance(j, ExchangeJob):
                nxt = ScatterJob(add_half(j.ops[0], j.result[0], c_idx))
                nxt.tag = j.tag
                scatters.append(nxt)
            else:
                p = j.ops[0]
                if k not in gbufs:
                    gbufs[k] = jnp.zeros((W[k].shape[0], 2 * p.shape[1], p.shape[2]), F32)
                gbufs[k] = add_partials(p, j.result[0], place_idx, gbufs[k], l)

    def take_fit(queue, budget):
        jobs = []
        for j in list(queue):
            if j.cost <= budget:
                budget -= j.cost
                jobs.append(j)
                queue.remove(j)
        return jobs

    def bwd_host(fn, units, *args, **kw):
        jobs = take_fit(exchanges, 0.6 * units) + take_fit(scatters, 1.2 * units)
        out = fn(*args, comm=jobs, **kw)
        settle(jobs)
        return out

    def dx_mm(a, w3, **kw):
        return bwd_host(mm_nt, w3.size * a.shape[0] / L, a, w3, **kw)

    def dw_mm(k, l, xx, dy, S, **kw):
        g3 = bwd_host(mm_tn, xx.shape[1] * dy.shape[1] * xx.shape[0] / L, xx, dy, S, **kw)
        if S == 1:
            g3 = g3.reshape(N_CHIPS, g3.shape[1] // N_CHIPS, g3.shape[2])
        job = ExchangeJob(g3)
        job.tag = (k, l)
        exchanges.append(job)

    for l in reversed(range(depth)):
        i = l // 2
        s = saved[l]
        dact = dx_mm(dh16, wrow("ff_w_down", l), out_dtype=BF16, name="mm_ff_down_dx")
        dw_mm("ff_w_down", l, s["act"], dh16, 1, name="mm_ff_down_dw")
        dgpre, dup, dcw, dcb = bwd_host(ffn_act_bwd, HOST_UNITS["ffn_act_bwd"] * dact.size,
                                        dact, s["gpre"], s["up"], ff_conv_w_f[l], ff_conv_b[l].reshape(1, -1))
        gs["ff_conv_w"][l], gs["ff_conv_b"][l] = dcw, dcb[0]
        dhn3 = dx_mm(dgpre, wcol("ff_w_gate", l), name="mm_ff_gate_dx")
        dhn3 = dx_mm(dup, wcol("ff_w_up", l), res=dhn3, name="mm_ff_up_dx")
        dw_mm("ff_w_gate", l, s["hn3"], dgpre, N_CHIPS, name="mm_ff_gate_dw")
        dw_mm("ff_w_up", l, s["hn3"], dup, N_CHIPS, name="mm_ff_up_dw")
        dh, dg, dh16 = bwd_host(rms_bwd, HOST_UNITS["rms_bwd"] * dh.size, s["h2"], norm_ffn[l].reshape(1, D), dhn3, dh)
        gs["norm_ffn"][l] = dg[0]
        dox = dx_mm(dh16, wrow("xa_w_o", l), out_dtype=BF16, name="mm_xa_o_dx")
        dw_mm("xa_w_o", l, s["ox"], dh16, 1, name="mm_xa_o_dw")
        dq, dkv = xattn_bwd(s["q"], s["kv"], dox)
        dhn2 = dx_mm(dq, wrow("xa_w_q", l), name="mm_xa_q_dx")
        dw_mm("xa_w_q", l, s["hn2"], dq, 1, name="mm_xa_q_dw")
        dw_mm("xa_w_kv", l, mem_n, dkv, N_CHIPS, name="mm_xa_kv_dw")
        dmem_n = dx_mm(dkv, wcol("xa_w_kv", l), res=dmem_n, name="mm_xa_kv_dx")
        dh, dg, dh16 = bwd_host(rms_bwd, HOST_UNITS["rms_bwd"] * dh.size, s["h1"], norm_xattn[l].reshape(1, D), dhn2, dh)
        gs["norm_xattn"][l] = dg[0]
        if l % 2 == 0:
            A = ev_conv_w_f.shape[-1]
            dycat = dx_mm(dh16, wrow("ev_w_out", i), name="mm_ev_out_dx")
            dw_mm("ev_w_out", i, s["ycat"], dh16, 1, name="mm_ev_out_dw")
            dgb, dgc, dxa, dcw = conv_mixer_bwd(s["z"], ev_conv_w_f[i], dycat[:, :A])
            gs["ev_conv_w"][i] = dcw
            a_l, apow_l, bm16, cm16, d_l, glu16 = s["s5ops"]
            dys_p = _to_segments(dycat[:, A:])
            dy_p, dglu, dd = bwd_host(s5_out_bwd, HOST_UNITS["s5_out_bwd"] * dys_p.size, dys_p, s["y_p"], s["u_p"], glu16)
            cmt = cm16.transpose(0, 2, 1)
            bmt = bm16.transpose(0, 2, 1)
            gfin = bwd_host(s5_scan_bwd, HOST_UNITS["s5_ends"] * dy_p.size, dy_p, a_l, apow_l, cmt, None, name="s5_bwd_ends")
            du_p, da, dbm, dcm = bwd_host(s5_scan_bwd, HOST_UNITS["s5_bwd"] * dy_p.size, dy_p, a_l, apow_l, cmt, gfin,
                                          s["states"], s["starts"], s["u_p"], bmt, d_l, name="s5_bwd")
            dprm = s["prep_vjp"]((da, jnp.zeros_like(apow_l), dbm, dcm, dd, dglu))
            for k, g in zip(["s5_a_re", "s5_a_im", "s5_log_dt", "s5_b_re", "s5_b_im", "s5_c_re", "s5_c_im", "s5_d", "s5_glu_w"], dprm):
                gs[k][i] = g
            dz = jnp.concatenate([dgb, dgc, dxa, _from_segments(du_p)], axis=1)
            dhn = dx_mm(dz, wcol("ev_w_in", i), name="mm_ev_in_dx")
            dw_mm("ev_w_in", i, s["hn"], dz, N_CHIPS, name="mm_ev_in_dw")
        else:
            do = dx_mm(dh16, wrow("od_w_out", i), out_dtype=BF16, name="mm_od_out_dx")
            dw_mm("od_w_out", i, s["o"], dh16, 1, name="mm_od_out_dw")
            doT = do.reshape(L, NQ, HEAD_DIM).transpose(1, 0, 2)
            dqT, dkT, dvT, dbias, dsink = bwd_host(swa_bwd, HOST_UNITS["swa_bwd"] * doT.size,
                                                   s["qT"], s["kT"], s["vT"], bias2, s["sink_row"], doT)
            dbias_tab = dbias_tab + _unmasked_dbias(dbias)
            gs["od_sinks"][i] = jnp.sum(dsink.reshape(NQ, WINDOW), axis=1)
            dz = jnp.concatenate([_from_head_lanes(dqT, L),
                                  dkT.astype(BF16).transpose(1, 0, 2).reshape(L, NKV * HEAD_DIM),
                                  dvT.astype(BF16).transpose(1, 0, 2).reshape(L, NKV * HEAD_DIM)], axis=1)
            gs["od_b_qkv"][i] = col_sum(dz)[0]
            dhn = dx_mm(dz, wcol("od_w_qkv", i), name="mm_od_qkv_dx")
            dw_mm("od_w_qkv", i, s["hn"], dz, N_CHIPS, name="mm_od_qkv_dw")
        dh, dg, dh16 = bwd_host(rms_bwd, HOST_UNITS["rms_bwd"] * dh.size, s["h0"], norm_mix[l].reshape(1, D), dhn, dh)
        gs["norm_mix"][l] = dg[0]

    grad_x = dh[None]
    _, dg_mem, _ = rms_bwd(mem[0], norm_mem.reshape(1, D), dmem_n, jnp.zeros_like(dmem_n), name="rms_bwd_mem")
    d_rel_bias = jnp.dot(dbias_tab.reshape(NQ, -1), onehot, precision=lax.Precision.HIGHEST).T

    small = [k for k in names if k not in big]
    local_small = {k: (jnp.stack(gs[k]) if k in gs else None) for k in small}
    local_small["norm_final"] = dg_final[0]
    local_small["norm_mem"] = dg_mem[0]
    local_small["rel_bias"] = d_rel_bias
    full_shape = {k: W[k].shape for k in small}
    for k in small_sharded:
        full_shape[k] = local_small[k].shape
    lst = [local_small[k].reshape(full_shape[k]).astype(F32) for k in small]
    packed = _pack(lst, PACK_ROWS)
    m_rows = packed.shape[0]
    summed = sum_blocks(all_gather_rows(packed).reshape(8, m_rows, LANES))
    gsum = dict(zip(small, _unpack(summed, lst)))
    for k in small_sharded:
        n4 = W[k].shape[-1]
        gsum[k] = lax.dynamic_slice_in_dim(gsum[k], me_idx[0] * n4, n4, axis=gsum[k].ndim - 1)

    for queue, name in ((exchanges, "rs_exchange_rest"), (scatters, "rs_scatter_rest")):
        jobs = queue[:]
        del queue[:]
        if jobs:
            run_comm(jobs, name=name)
            settle(jobs)
    joins = [JoinJob(gbufs[k]) for k in big]
    run_comm(joins[:1], name="rs_join_first")
    grads = dict(gsum)
    delta, new_m, new_v = {}, {}, {}
    for n, k in enumerate(big):
        grads[k] = joins[n].result[0].reshape(W[k].shape)
        d_, m_, v_ = adamw(_rows2d(W[k]), _rows2d(grads[k]), _rows2d(Mo[k]), _rows2d(Vo[k]), name="adamw_" + k,
                           comm=joins[n + 1:n + 2])
        delta[k], new_m[k], new_v[k] = d_.reshape(W[k].shape), m_.reshape(W[k].shape), v_.reshape(W[k].shape)
    sw = [W[k] for k in small]
    d_, m_, v_ = adamw(_pack(sw, PACK_ROWS), _pack([grads[k] for k in small], PACK_ROWS),
                       _pack([Mo[k] for k in small], PACK_ROWS), _pack([Vo[k] for k in small], PACK_ROWS), name="adamw_small")
    for k, a, b, c_ in zip(small, _unpack(d_, sw), _unpack(m_, sw), _unpack(v_, sw)):
        delta[k], new_m[k], new_v[k] = a, b, c_

    return (loss, grad_x, *[grads[k] for k in names], *[delta[k] for k in names],
            *[new_m[k] for k in names], *[new_v[k] for k in names])
```

```python
import functools
import math

import numpy as np
import jax
import jax.numpy as jnp
from jax import lax
from jax.experimental import pallas as pl
from jax.experimental.pallas import tpu as pltpu

F32, BF16 = jnp.float32, jnp.bfloat16
MESH = pl.DeviceIdType.MESH
AXES = ("x", "y", "c")

VMEM_LIMIT_BYTES = 56 * 2**20
SUBLANES, LANES = 8, 128

RMS_EPS = 1e-5
S5_GROUP, S5_STATE = 16, 64
HEAD_DIM, Q_PER_KV, WINDOW = 64, 8, 128
N_BUCKETS, MAX_DISTANCE = 32, 128
X_HEADS = 4
NEG_INF = -1e30
ADAM_LR, ADAM_B1, ADAM_B2, ADAM_EPS, ADAM_WD, ADAM_STEP = 0.001, 0.9, 0.999, 1e-08, 0.01, 10
N_CHIPS = 4
N_SEG = 8
S5_STEPS = 64
HOST_UNITS = {"rms_bwd": 0.57, "ffn_act_fwd": 0.39, "ffn_act_bwd": 0.63, "swa_fwd": 1.2, "swa_bwd": 2.2,
              "s5_fwd": 3.8, "s5_ends": 2.3, "s5_bwd": 4.5, "s5_out_bwd": 1.4}


def _params(sem=None):
    return pltpu.CompilerParams(dimension_semantics=sem, vmem_limit_bytes=VMEM_LIMIT_BYTES)


def _vspec(shape, index_map):
    return pl.BlockSpec(shape, index_map)


ANY = pl.BlockSpec(memory_space=pl.ANY)


def _tile(n, pref):
    t = (min(pref, n) // LANES) * LANES
    while t >= LANES:
        if n % t == 0:
            return t
        t -= LANES
    return n


def _acc_matmul(nk, k, acc, partial, finish):
    if nk == 1:
        finish(partial())
        return

    @pl.when(k == 0)
    def _():
        acc[...] = partial()

    @pl.when(jnp.logical_and(k > 0, k < nk - 1))
    def _():
        acc[...] += partial()

    @pl.when(k == nk - 1)
    def _():
        finish(acc[...] + partial())


MM_MAX_K = 2048


def hosted_call(body, *, name, grid, in_specs, out_specs, out_shape, ops, scratch_shapes=(), semantics, comm=()):
    if not comm:
        return pl.pallas_call(body, name=name, grid=grid, in_specs=in_specs, out_specs=out_specs, out_shape=out_shape,
                              scratch_shapes=list(scratch_shapes), compiler_params=_params(semantics))(*ops)
    n_in, n_out, n_scr = len(ops), len(out_shape), len(scratch_shapes)
    cops = [a for j in comm for a in j.ops]
    couts = [s for j in comm for s in j.outs]
    nsem = sum(j.nsem for j in comm)

    def hosted(*refs):
        ins, refs = refs[:n_in], refs[n_in:]
        cin, refs = refs[:len(cops)], refs[len(cops):]
        outs, refs = refs[:n_out], refs[n_out:]
        cout, refs = refs[:len(couts)], refs[len(couts):]
        scr, (send, recv) = refs[:n_scr], refs[n_scr:]

        def each(phase):
            ii = io = base = 0
            for j in comm:
                getattr(j, phase)(cin[ii:ii + len(j.ops)], cout[io:io + len(j.outs)], send, recv, base)
                ii, io, base = ii + len(j.ops), io + len(j.outs), base + j.nsem

        pids = [pl.program_id(d) for d in range(len(grid))]
        first = functools.reduce(jnp.logical_and, [p == 0 for p in pids])
        last = functools.reduce(jnp.logical_and, [p == g - 1 for p, g in zip(pids, grid)])
        pl.when(first)(lambda: each("start"))
        body(*ins, *outs, *scr)
        pl.when(last)(lambda: each("finish"))

    res = pl.pallas_call(
        hosted, name=name, grid=grid, in_specs=list(in_specs) + [ANY] * len(cops),
        out_specs=list(out_specs) + [ANY] * len(couts), out_shape=list(out_shape) + couts,
        scratch_shapes=list(scratch_shapes) + [pltpu.SemaphoreType.DMA((nsem,)), pltpu.SemaphoreType.DMA((nsem,))],
        compiler_params=pltpu.CompilerParams(dimension_semantics=("arbitrary",) * len(grid),
                                             vmem_limit_bytes=VMEM_LIMIT_BYTES, has_side_effects=True),
    )(*ops, *cops)
    io = n_out
    for j in comm:
        j.result = list(res[io:io + len(j.outs)])
        io += len(j.outs)
    return list(res[:n_out])


def _pcall(body, *, name, grid, in_specs, out_specs, out_shape, scratch_shapes=(), sem, comm=()):
    single = not isinstance(out_shape, (list, tuple))

    def run(*ops):
        res = hosted_call(body, name=name, grid=grid, in_specs=list(in_specs),
                          out_specs=[out_specs] if single else list(out_specs),
                          out_shape=[out_shape] if single else list(out_shape), ops=list(ops),
                          scratch_shapes=scratch_shapes, semantics=sem, comm=comm)
        return res[0] if single else res

    return run


def _mm_call(core, grid, in_specs, ops, out_spec, out_shape, acc_shape, name, comm):
    n_in = len(ops)

    def body(*refs):
        core(refs[:n_in], refs[n_in], refs[n_in + 1])

    return hosted_call(body, name=name, grid=grid, in_specs=in_specs, out_specs=[out_spec], out_shape=[out_shape],
                       ops=ops, scratch_shapes=[pltpu.VMEM(acc_shape, F32)],
                       semantics=("parallel", "parallel", "arbitrary"), comm=comm)[0]


def mm_nn(a, w3, *, bias=None, res=None, out_dtype=F32, name, comm=()):
    M, K = a.shape
    S, K2, ns = w3.shape
    assert K == K2
    tm = _tile(M, 512)
    tk = K if K <= MM_MAX_K else _tile(K, 1536)
    nk = K // tk
    has_b, has_r = bias is not None, res is not None

    def core(ins, o_ref, acc):
        a_ref, w_ref = ins[0], ins[1]
        b_ref = ins[2] if has_b else None
        r_ref = ins[2 + has_b] if has_r else None

        def partial():
            return jnp.dot(a_ref[...].astype(BF16), w_ref[...], preferred_element_type=F32)

        def finish(r):
            if has_b:
                r = r + b_ref[...]
            if has_r:
                r = r + r_ref[...]
            o_ref[...] = r.astype(o_ref.dtype)

        _acc_matmul(nk, pl.program_id(2), acc, partial, finish)

    in_specs = [_vspec((tm, tk), lambda s, i, k: (i, k)), _vspec((None, tk, ns), lambda s, i, k: (s, k, 0))]
    ops = [a, w3]
    if has_b:
        in_specs.append(_vspec((1, ns), lambda s, i, k: (0, s)))
        ops.append(bias)
    if has_r:
        in_specs.append(_vspec((tm, ns), lambda s, i, k: (i, s)))
        ops.append(res)
    return _mm_call(core, (S, M // tm, nk), in_specs, ops, _vspec((tm, ns), lambda s, i, k: (i, s)),
                    jax.ShapeDtypeStruct((M, S * ns), out_dtype), (tm, ns) if nk > 1 else (SUBLANES, LANES), name, comm)


def mm_nt(a, w3, *, res=None, out_dtype=F32, name, comm=()):
    M, N = a.shape
    S, K, ns = w3.shape
    assert N == S * ns
    tko = K if K <= MM_MAX_K else _tile(K, 1024)
    tm = _tile(M, 512 if tko == K else 1024)
    tc = ns if ns <= MM_MAX_K else _tile(ns, 1024)
    ncs = ns // tc
    nc = S * ncs
    has_r = res is not None

    def core(ins, o_ref, acc):
        a_ref, w_ref = ins[0], ins[1]
        r_ref = ins[2] if has_r else None

        def partial():
            return lax.dot_general(a_ref[...].astype(BF16), w_ref[...], (((1,), (1,)), ((), ())),
                                   preferred_element_type=F32)

        def finish(r):
            if has_r:
                r = r + r_ref[...]
            o_ref[...] = r.astype(o_ref.dtype)

        _acc_matmul(nc, pl.program_id(2), acc, partial, finish)

    in_specs = [_vspec((tm, tc), lambda i, j, k: (i, k)),
                _vspec((None, tko, tc), lambda i, j, k: (k // ncs, j, k % ncs))]
    ops = [a, w3]
    if has_r:
        in_specs.append(_vspec((tm, tko), lambda i, j, k: (i, j)))
        ops.append(res)
    return _mm_call(core, (M // tm, K // tko, nc), in_specs, ops, _vspec((tm, tko), lambda i, j, k: (i, j)),
                    jax.ShapeDtypeStruct((M, K), out_dtype), (tm, tko) if nc > 1 else (SUBLANES, LANES), name, comm)


def mm_tn(x, dy, S, *, out_dtype=BF16, name, comm=()):
    M, K = x.shape
    M2, N = dy.shape
    assert M == M2 and N % S == 0
    ns = N // S
    tk = _tile(K, 1536) if (K > MM_MAX_K and dy.dtype == BF16) else _tile(K, 1024)
    tmc = _tile(M, 512 if tk > 1024 else 1024)
    nm = M // tmc

    def core(ins, o_ref, acc):
        x_ref, dy_ref = ins

        def partial():
            return lax.dot_general(x_ref[...].astype(BF16), dy_ref[...].astype(BF16), (((0,), (0,)), ((), ())),
                                   preferred_element_type=F32)

        def finish(r):
            o_ref[...] = r.astype(o_ref.dtype)

        _acc_matmul(nm, pl.program_id(2), acc, partial, finish)

    in_specs = [_vspec((tmc, tk), lambda s, i, m: (m, i)), _vspec((tmc, ns), lambda s, i, m: (m, s))]
    return _mm_call(core, (S, K // tk, nm), in_specs, [x, dy], _vspec((None, tk, ns), lambda s, i, m: (s, i, 0)),
                    jax.ShapeDtypeStruct((S, K, ns), out_dtype), (tk, ns) if nm > 1 else (SUBLANES, LANES), name, comm)


def rms_fwd(h, g, *, name="rms_fwd"):
    R, D = h.shape
    tr = _tile8(R, 256)

    def body(h_ref, g_ref, o_ref):
        x = h_ref[...]
        r = lax.rsqrt(jnp.mean(x * x, axis=-1, keepdims=True) + RMS_EPS)
        o_ref[...] = (x * r * g_ref[...]).astype(o_ref.dtype)

    return pl.pallas_call(
        body, name=name, grid=(R // tr,),
        in_specs=[_vspec((tr, D), lambda i: (i, 0)), _vspec((1, D), lambda i: (0, 0))],
        out_specs=_vspec((tr, D), lambda i: (i, 0)),
        out_shape=jax.ShapeDtypeStruct((R, D), BF16),
        compiler_params=_params(("parallel",)),
    )(h, g)


def _tile8(n, pref):
    t = (min(pref, n) // SUBLANES) * SUBLANES
    while t >= SUBLANES:
        if n % t == 0:
            return t
        t -= SUBLANES
    return n


def rms_bwd(h, g, dhn, dres, *, name="rms_bwd", comm=()):
    R, D = h.shape
    tr = _tile8(R, 256)

    def body(h_ref, g_ref, dhn_ref, dres_ref, dh_ref, dg_ref, dh16_ref):
        @pl.when(pl.program_id(0) == 0)
        def _():
            dg_ref[...] = jnp.zeros_like(dg_ref)

        x = h_ref[...]
        d = dhn_ref[...].astype(F32)
        r = lax.rsqrt(jnp.mean(x * x, axis=-1, keepdims=True) + RMS_EPS)
        xhat = x * r
        dg_ref[...] += jnp.sum(d * xhat, axis=0, keepdims=True)
        t = d * g_ref[...]
        dh = dres_ref[...] + r * (t - xhat * jnp.mean(t * xhat, axis=-1, keepdims=True))
        dh_ref[...] = dh
        dh16_ref[...] = dh.astype(BF16)

    return _pcall(
        body, name=name, grid=(R // tr,),
        in_specs=[_vspec((tr, D), lambda i: (i, 0)), _vspec((1, D), lambda i: (0, 0)),
                  _vspec((tr, D), lambda i: (i, 0)), _vspec((tr, D), lambda i: (i, 0))],
        out_specs=[_vspec((tr, D), lambda i: (i, 0)), _vspec((1, D), lambda i: (0, 0)), _vspec((tr, D), lambda i: (i, 0))],
        out_shape=[jax.ShapeDtypeStruct((R, D), F32), jax.ShapeDtypeStruct((1, D), F32), jax.ShapeDtypeStruct((R, D), BF16)],
        sem=("arbitrary",), comm=comm,
    )(h, g, dhn, dres)


def final_loss(h, g, target, *, name="final_loss"):
    R, D = h.shape
    tr = _tile8(R, 256)

    def body(h_ref, g_ref, t_ref, loss_ref, dh_ref, dg_ref, dh16_ref):
        @pl.when(pl.program_id(0) == 0)
        def _():
            dg_ref[...] = jnp.zeros_like(dg_ref)
            loss_ref[...] = jnp.zeros_like(loss_ref)

        x = h_ref[...]
        r = lax.rsqrt(jnp.mean(x * x, axis=-1, keepdims=True) + RMS_EPS)
        xhat = x * r
        err = xhat * g_ref[...] - t_ref[...]
        row = jnp.mean(err * err, axis=-1, keepdims=True)
        loss_ref[...] += 0.5 * jnp.sum(row, axis=0, keepdims=True)
        d = err * (1.0 / D)
        dg_ref[...] += jnp.sum(d * xhat, axis=0, keepdims=True)
        t = d * g_ref[...]
        dh = r * (t - xhat * jnp.mean(t * xhat, axis=-1, keepdims=True))
        dh_ref[...] = dh
        dh16_ref[...] = dh.astype(BF16)

    return pl.pallas_call(
        body, name=name, grid=(R // tr,),
        in_specs=[_vspec((tr, D), lambda i: (i, 0)), _vspec((1, D), lambda i: (0, 0)), _vspec((tr, D), lambda i: (i, 0))],
        out_specs=[_vspec((1, 1), lambda i: (0, 0)), _vspec((tr, D), lambda i: (i, 0)), _vspec((1, D), lambda i: (0, 0)),
                   _vspec((tr, D), lambda i: (i, 0))],
        out_shape=[jax.ShapeDtypeStruct((1, 1), F32), jax.ShapeDtypeStruct((R, D), F32), jax.ShapeDtypeStruct((1, D), F32),
                   jax.ShapeDtypeStruct((R, D), BF16)],
        compiler_params=_params(("arbitrary",)),
    )(h, g, target)


def _shift_down(v, k):
    rows = lax.broadcasted_iota(jnp.int32, v.shape, 0)
    return jnp.where(rows >= k, pltpu.roll(v, k, axis=0), 0.0)


def _shift_up(v, k):
    L = v.shape[0]
    rows = lax.broadcasted_iota(jnp.int32, v.shape, 0)
    return jnp.where(rows < L - k, pltpu.roll(v, L - k, axis=0), 0.0)


def _conv(v, w):
    return w[2:3, :] * v + w[1:2, :] * _shift_down(v, 1) + w[0:1, :] * _shift_down(v, 2)


def _conv_t(d, w):
    return w[2:3, :] * d + w[1:2, :] * _shift_up(d, 1) + w[0:1, :] * _shift_up(d, 2)


def _conv_dw(d, v):
    return jnp.concatenate([
        jnp.sum(d * _shift_down(v, 2), axis=0, keepdims=True),
        jnp.sum(d * _shift_down(v, 1), axis=0, keepdims=True),
        jnp.sum(d * v, axis=0, keepdims=True)], axis=0)


COL_BLOCK = 128


def conv_mixer_fwd(z, cw, *, name="conv_mixer_fwd"):
    L = z.shape[0]
    A = cw.shape[1]
    cb = _tile(A, COL_BLOCK)
    nb = A // cb

    def body(gb_ref, gc_ref, xa_ref, w_ref, o_ref):
        v = gc_ref[...] * xa_ref[...]
        o_ref[...] = (gb_ref[...] * _conv(v, w_ref[...])).astype(o_ref.dtype)

    return pl.pallas_call(
        body, name=name, grid=(nb,),
        in_specs=[_vspec((L, cb), lambda j: (0, j)), _vspec((L, cb), lambda j: (0, nb + j)),
                  _vspec((L, cb), lambda j: (0, 2 * nb + j)), _vspec((3, cb), lambda j: (0, j))],
        out_specs=_vspec((L, cb), lambda j: (0, j)),
        out_shape=jax.ShapeDtypeStruct((L, A), BF16),
        compiler_params=_params(("parallel",)),
    )(z, z, z, cw)


def conv_mixer_bwd(z, cw, dya, *, name="conv_mixer_bwd"):
    L = z.shape[0]
    A = cw.shape[1]
    cb = _tile(A, COL_BLOCK)
    nb = A // cb

    def body(gb_ref, gc_ref, xa_ref, w_ref, d_ref, dgb_ref, dgc_ref, dxa_ref, dw_ref):
        gc, xa, w, d = gc_ref[...], xa_ref[...], w_ref[...], d_ref[...]
        v = gc * xa
        dgb_ref[...] = (d * _conv(v, w)).astype(dgb_ref.dtype)
        dc = d * gb_ref[...]
        dw_ref[...] = _conv_dw(dc, v)
        dv = _conv_t(dc, w)
        dgc_ref[...] = (dv * xa).astype(dgc_ref.dtype)
        dxa_ref[...] = (dv * gc).astype(dxa_ref.dtype)

    col = lambda j: (0, j)
    outs = pl.pallas_call(
        body, name=name, grid=(nb,),
        in_specs=[_vspec((L, cb), col), _vspec((L, cb), lambda j: (0, nb + j)),
                  _vspec((L, cb), lambda j: (0, 2 * nb + j)), _vspec((3, cb), col), _vspec((L, cb), col)],
        out_specs=[_vspec((L, cb), col), _vspec((L, cb), col), _vspec((L, cb), col), _vspec((3, cb), col)],
        out_shape=[jax.ShapeDtypeStruct((L, A), BF16)] * 3 + [jax.ShapeDtypeStruct((3, A), F32)],
        compiler_params=_params(("parallel",)),
    )(z, z, z, cw, dya)
    return outs[0], outs[1], outs[2], outs[3]


def ffn_act_fwd(gpre, up, cw, cbias, *, name="ffn_act_fwd", comm=()):
    L, Fd = gpre.shape
    cb = _tile(Fd, COL_BLOCK)

    def body(g_ref, u_ref, w_ref, b_ref, o_ref):
        g = _conv(g_ref[...].astype(F32), w_ref[...]) + b_ref[...]
        o_ref[...] = (g * jax.nn.sigmoid(g) * u_ref[...].astype(F32)).astype(o_ref.dtype)

    col = lambda j: (0, j)
    return _pcall(
        body, name=name, grid=(Fd // cb,),
        in_specs=[_vspec((L, cb), col), _vspec((L, cb), col), _vspec((3, cb), col), _vspec((1, cb), col)],
        out_specs=_vspec((L, cb), col),
        out_shape=jax.ShapeDtypeStruct((L, Fd), BF16),
        sem=("parallel",), comm=comm,
    )(gpre, up, cw, cbias)


def ffn_act_bwd(dact, gpre, up, cw, cbias, *, name="ffn_act_bwd", comm=()):
    L, Fd = gpre.shape
    cb = _tile(Fd, COL_BLOCK)

    def body(d_ref, g_ref, u_ref, w_ref, b_ref, dg_ref, du_ref, dw_ref, db_ref):
        gp, w, d = g_ref[...].astype(F32), w_ref[...], d_ref[...].astype(F32)
        g = _conv(gp, w) + b_ref[...]
        sg = jax.nn.sigmoid(g)
        du_ref[...] = (d * (g * sg)).astype(du_ref.dtype)
        dg = d * u_ref[...].astype(F32) * (sg * (1.0 + g * (1.0 - sg)))
        db_ref[...] = jnp.sum(dg, axis=0, keepdims=True)
        dw_ref[...] = _conv_dw(dg, gp)
        dg_ref[...] = _conv_t(dg, w).astype(dg_ref.dtype)

    col = lambda j: (0, j)
    return _pcall(
        body, name=name, grid=(Fd // cb,),
        in_specs=[_vspec((L, cb), col)] * 3 + [_vspec((3, cb), col), _vspec((1, cb), col)],
        out_specs=[_vspec((L, cb), col), _vspec((L, cb), col), _vspec((3, cb), col), _vspec((1, cb), col)],
        out_shape=[jax.ShapeDtypeStruct((L, Fd), BF16), jax.ShapeDtypeStruct((L, Fd), BF16),
                   jax.ShapeDtypeStruct((3, Fd), F32), jax.ShapeDtypeStruct((1, Fd), F32)],
        sem=("parallel",), comm=comm,
    )(dact, gpre, up, cw, cbias)


def col_sum(x, *, name="col_sum"):
    R, C = x.shape
    tr = _tile8(R, 512)

    def body(x_ref, o_ref):
        @pl.when(pl.program_id(0) == 0)
        def _():
            o_ref[...] = jnp.zeros_like(o_ref)

        o_ref[...] += jnp.sum(x_ref[...].astype(F32), axis=0, keepdims=True)

    return pl.pallas_call(
        body, name=name, grid=(R // tr,),
        in_specs=[_vspec((tr, C), lambda i: (i, 0))], out_specs=_vspec((1, C), lambda i: (0, 0)),
        out_shape=jax.ShapeDtypeStruct((1, C), F32), compiler_params=_params(("arbitrary",)),
    )(x)


def _cmul_add(ar, ai, sr, si, br, bi):
    return ar * sr - ai * si + br, ar * si + ai * sr + bi


def _segment_starts(fin, pr, pi, reverse):
    H = fin.shape[1] // 2
    rows = lax.broadcasted_iota(jnp.int32, fin.shape, 0)
    cr = jnp.zeros((1, H), F32)
    ci = jnp.zeros((1, H), F32)
    out = jnp.zeros(fin.shape, F32)
    order = range(N_SEG - 1, -1, -1) if reverse else range(N_SEG)
    for k in order:
        out = jnp.where(rows == k, jnp.concatenate([cr, ci], axis=1), out)
        cr, ci = _cmul_add(pr, pi, cr, ci, fin[k:k + 1, :H], fin[k:k + 1, H:])
    return out


def _gelu(y):
    c0 = math.sqrt(2.0 / math.pi)
    t = jnp.tanh(c0 * (y + 0.044715 * y * y * y))
    return 0.5 * y * (1.0 + t), t


def s5_scan_fwd(u_p, a_l, apow_l, bm, cm, d_l, glu, fin, *, name, comm=()):
    L, C = u_p.shape
    NK, _, SW = bm.shape
    H = SW // 2
    RB = S5_STEPS * N_SEG
    NC = L // RB
    final_only = fin is None

    def scan_chunk(a_ref, buf, st):
        ar = jnp.broadcast_to(a_ref[:, :H], (N_SEG, H))
        ai = jnp.broadcast_to(a_ref[:, H:], (N_SEG, H))

        def step(j, carry):
            sr, si = carry
            rows = pl.ds(pl.multiple_of(j * N_SEG, N_SEG), N_SEG)
            sr, si = _cmul_add(ar, ai, sr, si, buf[rows, :H], buf[rows, H:])
            buf[rows, :H] = sr
            buf[rows, H:] = si
            return sr, si

        sr, si = lax.fori_loop(0, S5_STEPS, step, (st[:, :H], st[:, H:]), unroll=4)
        st[:, :H] = sr
        st[:, H:] = si

    if final_only:
        def body(u_ref, a_ref, bm_ref, fin_ref, buf, st):
            @pl.when(pl.program_id(1) == 0)
            def _():
                st[...] = jnp.zeros_like(st)

            buf[...] = jnp.dot(u_ref[...].astype(BF16), bm_ref[...], preferred_element_type=F32)
            scan_chunk(a_ref, buf, st)
            fin_ref[...] = st[...]

        return _pcall(
            body, name=name, grid=(NK, NC),
            in_specs=[_vspec((RB, LANES), lambda k, j: (j, k)), _vspec((None, 1, SW), lambda k, j: (k, 0, 0)),
                      _vspec((None, LANES, SW), lambda k, j: (k, 0, 0))],
            out_specs=_vspec((None, N_SEG, SW), lambda k, j: (k, 0, 0)),
            out_shape=jax.ShapeDtypeStruct((NK, N_SEG, SW), F32),
            scratch_shapes=[pltpu.VMEM((RB, SW), F32), pltpu.VMEM((N_SEG, SW), F32)],
            sem=("parallel", "arbitrary"), comm=comm,
        )(u_p, a_l, bm)

    def body(u_ref, a_ref, ap_ref, bm_ref, cm_ref, d_ref, glu_ref, fin_ref, o_ref, y_ref, s_ref, start_ref, buf, st):
        @pl.when(pl.program_id(1) == 0)
        def _():
            st[...] = _segment_starts(fin_ref[...], ap_ref[:, :H], ap_ref[:, H:], False)
            start_ref[...] = st[...]

        u = u_ref[...]
        buf[...] = jnp.dot(u.astype(BF16), bm_ref[...], preferred_element_type=F32)
        scan_chunk(a_ref, buf, st)
        states = buf[...]
        s_ref[...] = states
        y = jnp.dot(states.astype(BF16), cm_ref[...], preferred_element_type=F32) + d_ref[...] * u
        y_ref[...] = y
        yg, _ = _gelu(y)
        gate = jnp.dot(yg.astype(BF16), glu_ref[...], preferred_element_type=F32)
        o_ref[...] = (yg * jax.nn.sigmoid(gate)).astype(o_ref.dtype)

    blk = lambda k, j: (j, k)
    per_k = lambda k, j: (k, 0, 0)
    return _pcall(
        body, name=name, grid=(NK, NC),
        in_specs=[_vspec((RB, LANES), blk), _vspec((None, 1, SW), per_k), _vspec((None, 1, SW), per_k),
                  _vspec((None, LANES, SW), per_k), _vspec((None, SW, LANES), per_k), _vspec((1, LANES), lambda k, j: (0, k)),
                  _vspec((None, LANES, LANES), per_k), _vspec((None, N_SEG, SW), per_k)],
        out_specs=[_vspec((RB, LANES), blk), _vspec((RB, LANES), blk), _vspec((RB, SW), blk),
                   _vspec((None, N_SEG, SW), per_k)],
        out_shape=[jax.ShapeDtypeStruct((L, C), BF16), jax.ShapeDtypeStruct((L, C), F32),
                   jax.ShapeDtypeStruct((L, NK * SW), F32), jax.ShapeDtypeStruct((NK, N_SEG, SW), F32)],
        scratch_shapes=[pltpu.VMEM((RB, SW), F32), pltpu.VMEM((N_SEG, SW), F32)],
        sem=("parallel", "arbitrary"), comm=comm,
    )(u_p, a_l, apow_l, bm, cm, d_l, glu, fin)


def s5_out_bwd(dout_p, y_p, u_p, glu, *, name="s5_out_bwd", comm=()):
    L, C = y_p.shape
    NK = C // LANES
    tr = _tile8(L, 512)

    def body(do_ref, y_ref, u_ref, glu_ref, dy_ref, dglu_ref, dd_ref):
        @pl.when(pl.program_id(1) == 0)
        def _():
            dglu_ref[...] = jnp.zeros_like(dglu_ref)
            dd_ref[...] = jnp.zeros_like(dd_ref)

        y, do, w = y_ref[...], do_ref[...].astype(F32), glu_ref[...]
        yg, t = _gelu(y)
        sg = jax.nn.sigmoid(jnp.dot(yg.astype(BF16), w, preferred_element_type=F32))
        dgate = (do * yg * sg * (1.0 - sg)).astype(BF16)
        dyg = do * sg + lax.dot_general(dgate, w, (((1,), (1,)), ((), ())), preferred_element_type=F32)
        dglu_ref[...] += lax.dot_general(yg.astype(BF16), dgate, (((0,), (0,)), ((), ())), preferred_element_type=F32)
        c0 = math.sqrt(2.0 / math.pi)
        dgelu = 0.5 * (1.0 + t) + 0.5 * y * (1.0 - t * t) * c0 * (1.0 + 3.0 * 0.044715 * y * y)
        dy = dyg * dgelu
        dy_ref[...] = dy
        dd_ref[...] += jnp.sum(dy * u_ref[...], axis=0, keepdims=True)

    blk = lambda k, i: (i, k)
    return _pcall(
        body, name=name, grid=(NK, L // tr),
        in_specs=[_vspec((tr, LANES), blk)] * 3 + [_vspec((None, LANES, LANES), lambda k, i: (k, 0, 0))],
        out_specs=[_vspec((tr, LANES), blk), _vspec((None, LANES, LANES), lambda k, i: (k, 0, 0)),
                   _vspec((1, LANES), lambda k, i: (0, k))],
        out_shape=[jax.ShapeDtypeStruct((L, C), F32), jax.ShapeDtypeStruct((NK, LANES, LANES), F32),
                   jax.ShapeDtypeStruct((1, C), F32)],
        sem=("parallel", "arbitrary"), comm=comm,
    )(dout_p, y_p, u_p, glu)


def s5_scan_bwd(dy_p, a_l, apow_l, cmt, gfin, states=None, starts=None, u_p=None, bmt=None, d_l=None, *, name, comm=()):
    L, C = dy_p.shape
    NK, _, SW = cmt.shape
    H = SW // 2
    RB = S5_STEPS * N_SEG
    NC = L // RB
    final_only = gfin is None

    def scan_chunk(a_ref, buf, st):
        ar = jnp.broadcast_to(a_ref[:, :H], (N_SEG, H))
        ai = -jnp.broadcast_to(a_ref[:, H:], (N_SEG, H))

        def step(jj, carry):
            gr, gi = carry
            j = S5_STEPS - 1 - jj
            rows = pl.ds(pl.multiple_of(j * N_SEG, N_SEG), N_SEG)
            gr, gi = _cmul_add(ar, ai, gr, gi, buf[rows, :H], buf[rows, H:])
            buf[rows, :H] = gr
            buf[rows, H:] = gi
            return gr, gi

        gr, gi = lax.fori_loop(0, S5_STEPS, step, (st[:, :H], st[:, H:]), unroll=4)
        st[:, :H] = gr
        st[:, H:] = gi

    rblk = lambda k, j: (NC - 1 - j, k)
    per_k = lambda k, j: (k, 0, 0)

    if final_only:
        def body(dy_ref, a_ref, cmt_ref, fin_ref, buf, st):
            @pl.when(pl.program_id(1) == 0)
            def _():
                st[...] = jnp.zeros_like(st)

            buf[...] = jnp.dot(dy_ref[...].astype(BF16), cmt_ref[...], preferred_element_type=F32)
            scan_chunk(a_ref, buf, st)
            fin_ref[...] = st[...]

        return _pcall(
            body, name=name, grid=(NK, NC),
            in_specs=[_vspec((RB, LANES), rblk), _vspec((None, 1, SW), per_k), _vspec((None, LANES, SW), per_k)],
            out_specs=_vspec((None, N_SEG, SW), per_k),
            out_shape=jax.ShapeDtypeStruct((NK, N_SEG, SW), F32),
            scratch_shapes=[pltpu.VMEM((RB, SW), F32), pltpu.VMEM((N_SEG, SW), F32)],
            sem=("parallel", "arbitrary"), comm=comm,
        )(dy_p, a_l, cmt)

    def body(dy_ref, a_ref, ap_ref, cmt_ref, gfin_ref, s_ref, sprev_ref, start_ref, u_ref, bmt_ref, d_ref,
             du_ref, da_ref, dbm_ref, dcm_ref, buf, st):
        jc = pl.program_id(1)

        @pl.when(jc == 0)
        def _():
            st[...] = _segment_starts(gfin_ref[...], ap_ref[:, :H], -ap_ref[:, H:], True)
            da_ref[...] = jnp.zeros_like(da_ref)
            dbm_ref[...] = jnp.zeros_like(dbm_ref)
            dcm_ref[...] = jnp.zeros_like(dcm_ref)

        dy = dy_ref[...]
        dyb = dy.astype(BF16)
        buf[...] = jnp.dot(dyb, cmt_ref[...], preferred_element_type=F32)
        scan_chunk(a_ref, buf, st)
        g = buf[...]
        s = s_ref[...]
        first = jnp.where(jc == NC - 1, start_ref[...], sprev_ref[...])
        sp = jnp.concatenate([first, s[:RB - N_SEG, :]], axis=0)
        gr, gi, pr, pi = g[:, :H], g[:, H:], sp[:, :H], sp[:, H:]
        da_ref[...] += jnp.concatenate([jnp.sum(gr * pr + gi * pi, axis=0, keepdims=True),
                                        jnp.sum(gi * pr - gr * pi, axis=0, keepdims=True)], axis=1)
        gb = g.astype(BF16)
        u = u_ref[...]
        du_ref[...] = (jnp.dot(gb, bmt_ref[...], preferred_element_type=F32) + dy * d_ref[...]).astype(du_ref.dtype)
        dbm_ref[...] += lax.dot_general(u.astype(BF16), gb, (((0,), (0,)), ((), ())), preferred_element_type=F32)
        dcm_ref[...] += lax.dot_general(s.astype(BF16), dyb, (((0,), (0,)), ((), ())), preferred_element_type=F32)

    prev8 = lambda k, j: (jnp.maximum((NC - 1 - j) * S5_STEPS - 1, 0), k)
    return _pcall(
        body, name=name, grid=(NK, NC),
        in_specs=[_vspec((RB, LANES), rblk), _vspec((None, 1, SW), per_k), _vspec((None, 1, SW), per_k),
                  _vspec((None, LANES, SW), per_k), _vspec((None, N_SEG, SW), per_k), _vspec((RB, SW), rblk),
                  _vspec((N_SEG, SW), prev8), _vspec((None, N_SEG, SW), per_k), _vspec((RB, LANES), rblk),
                  _vspec((None, SW, LANES), per_k), _vspec((1, LANES), lambda k, j: (0, k))],
        out_specs=[_vspec((RB, LANES), rblk), _vspec((None, 1, SW), per_k), _vspec((None, LANES, SW), per_k),
                   _vspec((None, SW, LANES), per_k)],
        out_shape=[jax.ShapeDtypeStruct((L, C), BF16), jax.ShapeDtypeStruct((NK, 1, SW), F32),
                   jax.ShapeDtypeStruct((NK, LANES, SW), F32), jax.ShapeDtypeStruct((NK, SW, LANES), F32)],
        scratch_shapes=[pltpu.VMEM((RB, SW), F32), pltpu.VMEM((N_SEG, SW), F32)],
        sem=("parallel", "arbitrary"), comm=comm,
    )(dy_p, a_l, apow_l, cmt, gfin, states, states, starts, u_p, bmt, d_l)


def _s5_prep(a_re, a_im, log_dt, b_re, b_im, c_re, c_im, d, glu_w, seg_len):
    G, P = a_re.shape
    Hc = b_re.shape[-1]
    gl = LANES // Hc
    nk = G // gl
    dt = jnp.exp(log_dt)[:, None]
    er = jnp.exp(a_re * dt)
    ab_r, ab_i = er * jnp.cos(a_im * dt), er * jnp.sin(a_im * dt)
    den = a_re * a_re + a_im * a_im
    nr, ni = ab_r - 1.0, ab_i
    q_r, q_i = (nr * a_re + ni * a_im) / den, (ni * a_re - nr * a_im) / den
    bb_r = q_r[..., None] * b_re - q_i[..., None] * b_im
    bb_i = q_r[..., None] * b_im + q_i[..., None] * b_re
    ep = jnp.exp(a_re * dt * seg_len)
    ap_r, ap_i = ep * jnp.cos(a_im * dt * seg_len), ep * jnp.sin(a_im * dt * seg_len)
    eye = jnp.eye(gl, dtype=F32)

    def lanes(t):
        return t.reshape(nk, 1, gl * P)

    def b_mat(t):
        return jnp.einsum("kgph,gq->kghqp", t.reshape(nk, gl, P, Hc), eye).reshape(nk, gl * Hc, gl * P)

    def c_mat(t):
        return jnp.einsum("kghp,gq->kgpqh", t.reshape(nk, gl, Hc, P), eye).reshape(nk, gl * P, gl * Hc)

    a_l = jnp.concatenate([lanes(ab_r), lanes(ab_i)], axis=-1)
    apow_l = jnp.concatenate([lanes(ap_r), lanes(ap_i)], axis=-1)
    bm = jnp.concatenate([b_mat(bb_r), b_mat(bb_i)], axis=-1)
    cm = jnp.concatenate([c_mat(c_re), -c_mat(c_im)], axis=1)
    glu = jnp.einsum("kgho,gq->kghqo", glu_w.reshape(nk, gl, Hc, Hc), eye).reshape(nk, gl * Hc, gl * Hc)
    return a_l, apow_l, bm, cm, d.reshape(1, G * Hc), glu


def _to_segments(t):
    L, C = t.shape
    return t.reshape(N_SEG, L // N_SEG, C).transpose(1, 0, 2).reshape(L, C)


def _from_segments(t):
    L, C = t.shape
    return t.reshape(L // N_SEG, N_SEG, C).transpose(1, 0, 2).reshape(L, C)


def _swa_probs(q, kk, bias_t, sink):
    s = lax.dot_general(kk, q, (((1,), (1,)), ((), ())), preferred_element_type=F32) * (HEAD_DIM ** -0.5)
    s = s + bias_t
    m = jnp.maximum(jnp.max(s, axis=0, keepdims=True), sink)
    e = jnp.exp(s - m)
    es = jnp.exp(sink - m)
    inv = 1.0 / (jnp.sum(e, axis=0, keepdims=True) + es)
    return e * inv, es * inv


def _swa_blocks(q_ref, kp_ref, kc_ref, vp_ref, vc_ref, bias_ref, n2):
    W = WINDOW
    rows = Q_PER_KV * W
    k0, k1, v0, v1 = kc_ref[0:W, :], kc_ref[W:, :], vc_ref[0:W, :], vc_ref[W:, :]
    table = bias_ref[jnp.minimum(n2, 1)]
    return [(0, q_ref[:, 0:W, :].reshape(rows, HEAD_DIM), jnp.concatenate([kp_ref[...], k0], axis=0),
             jnp.concatenate([vp_ref[...], v0], axis=0), table),
            (1, q_ref[:, W:, :].reshape(rows, HEAD_DIM), jnp.concatenate([k0, k1], axis=0),
             jnp.concatenate([v0, v1], axis=0), bias_ref[1])]


def _swa_specs(nq):
    W = WINDOW
    qs = _vspec((Q_PER_KV, 2 * W, HEAD_DIM), lambda g, n: (g, n, 0))
    kprev = _vspec((None, W, HEAD_DIM), lambda g, n: (g, jnp.maximum(2 * n - 1, 0), 0))
    kcur = _vspec((None, 2 * W, HEAD_DIM), lambda g, n: (g, n, 0))
    bias = _vspec((2, None, 2 * W, Q_PER_KV * W), lambda g, n: (0, g, 0, 0))
    dbias = _vspec((None, 2 * W, Q_PER_KV * W), lambda g, n: (g, 0, 0))
    sink = _vspec((1, Q_PER_KV * W), lambda g, n: (0, g))
    outs = _vspec((None, 2, HEAD_DIM, Q_PER_KV * W), lambda g, n: (g, n, 0, 0))
    return qs, kprev, kcur, bias, dbias, sink, outs


def _from_head_lanes(t, L):
    nkv = t.shape[0]
    t = t.reshape(nkv, L // WINDOW, HEAD_DIM, Q_PER_KV, WINDOW)
    return t.transpose(1, 4, 0, 3, 2).reshape(L, nkv * Q_PER_KV * HEAD_DIM)


def _masked_bias(bias_tab):
    qi = np.arange(WINDOW)[:, None]
    kj = np.arange(2 * WINDOW)[None, :]
    valid = ((kj < WINDOW) & (kj > qi)) | ((kj >= WINDOW) & (kj - WINDOW <= qi))
    first = valid & (kj >= WINDOW)
    both = jnp.stack([jnp.where(first[None], bias_tab, NEG_INF), jnp.where(valid[None], bias_tab, NEG_INF)])
    nkv = bias_tab.shape[0] // Q_PER_KV
    both = both.reshape(2, nkv, Q_PER_KV, WINDOW, 2 * WINDOW).transpose(0, 1, 4, 2, 3)
    return both.reshape(2, nkv, 2 * WINDOW, Q_PER_KV * WINDOW)


def _unmasked_dbias(dbias_t):
    nkv = dbias_t.shape[0]
    t = dbias_t.reshape(nkv, 2 * WINDOW, Q_PER_KV, WINDOW).transpose(0, 2, 3, 1)
    return t.reshape(nkv * Q_PER_KV, WINDOW, 2 * WINDOW)


def swa_fwd(qT, kT, vT, bias2, sink_row, *, name="swa_fwd", comm=()):
    NQ, L, _ = qT.shape
    NKV = kT.shape[0]
    qs, kprev, kcur, bs, _, sk, outs = _swa_specs(NQ)

    def body(q_ref, kp_ref, kc_ref, vp_ref, vc_ref, bias_ref, sink_ref, o_ref):
        sink = sink_ref[...]
        for b, q, kk, vv, bias in _swa_blocks(q_ref, kp_ref, kc_ref, vp_ref, vc_ref, bias_ref, pl.program_id(1)):
            p, _ = _swa_probs(q, kk, bias, sink)
            o = lax.dot_general(vv, p.astype(BF16), (((0,), (0,)), ((), ())), preferred_element_type=F32)
            o_ref[b] = o.astype(o_ref.dtype)

    return _pcall(
        body, name=name, grid=(NKV, L // (2 * WINDOW)),
        in_specs=[qs, kprev, kcur, kprev, kcur, bs, sk], out_specs=outs,
        out_shape=jax.ShapeDtypeStruct((NKV, L // WINDOW, HEAD_DIM, Q_PER_KV * WINDOW), BF16),
        sem=("parallel", "arbitrary"), comm=comm,
    )(qT, kT, kT, vT, vT, bias2, sink_row)


def swa_bwd(qT, kT, vT, bias2, sink_row, doT, *, name="swa_bwd", comm=()):
    NQ, L, _ = qT.shape
    NKV = kT.shape[0]
    qs, kprev, kcur, bs, dbs, sk, outs = _swa_specs(NQ)
    W = WINDOW

    def body(q_ref, kp_ref, kc_ref, vp_ref, vc_ref, bias_ref, sink_ref, do_ref,
             dq_ref, dk_ref, dv_ref, dbias_ref, dsink_ref):
        n = pl.program_id(1)

        @pl.when(n == 0)
        def _():
            dk_ref[...] = jnp.zeros_like(dk_ref)
            dv_ref[...] = jnp.zeros_like(dv_ref)
            dbias_ref[...] = jnp.zeros_like(dbias_ref)
            dsink_ref[...] = jnp.zeros_like(dsink_ref)

        sink = sink_ref[...]
        scale = HEAD_DIM ** -0.5
        grads = []
        for b, q, kk, vv, bias in _swa_blocks(q_ref, kp_ref, kc_ref, vp_ref, vc_ref, bias_ref, n):
            p, ps = _swa_probs(q, kk, bias, sink)
            do = do_ref[:, b * W:(b + 1) * W, :].reshape(Q_PER_KV * W, HEAD_DIM)
            dp = lax.dot_general(vv, do, (((1,), (1,)), ((), ())), preferred_element_type=F32)
            delta = jnp.sum(p * dp, axis=0, keepdims=True)
            ds = p * (dp - delta)
            dsink_ref[...] += -ps * delta
            dbias_ref[...] += ds
            dsb = ds.astype(BF16)
            dq = lax.dot_general(kk, dsb, (((0,), (0,)), ((), ())), preferred_element_type=F32) * scale
            dq_ref[b] = dq.astype(dq_ref.dtype)
            grads.append((jnp.dot(dsb, q, preferred_element_type=F32) * scale,
                          jnp.dot(p.astype(BF16), do, preferred_element_type=F32)))
        (dk0, dv0), (dk1, dv1) = grads

        @pl.when(n == 0)
        def _():
            dk_ref[0:W, :] += dk0[W:, :]
            dv_ref[0:W, :] += dv0[W:, :]

        @pl.when(n > 0)
        def _():
            rows = pl.ds(pl.multiple_of((2 * n - 1) * W, W), 2 * W)
            dk_ref[rows, :] += dk0
            dv_ref[rows, :] += dv0

        rows = pl.ds(pl.multiple_of(2 * n * W, W), 2 * W)
        dk_ref[rows, :] += dk1
        dv_ref[rows, :] += dv1

    whole = _vspec((None, L, HEAD_DIM), lambda g, n: (g, 0, 0))
    return _pcall(
        body, name=name, grid=(NKV, L // (2 * W)),
        in_specs=[qs, kprev, kcur, kprev, kcur, bs, sk, qs],
        out_specs=[outs, whole, whole, dbs, sk],
        out_shape=[jax.ShapeDtypeStruct((NKV, L // W, HEAD_DIM, Q_PER_KV * W), BF16),
                   jax.ShapeDtypeStruct((NKV, L, HEAD_DIM), F32), jax.ShapeDtypeStruct((NKV, L, HEAD_DIM), F32),
                   jax.ShapeDtypeStruct((NKV, 2 * W, Q_PER_KV * W), F32), jax.ShapeDtypeStruct((1, NQ * W), F32)],
        sem=("parallel", "arbitrary"), comm=comm,
    )(qT, kT, kT, vT, vT, bias2, sink_row, doT)


def _bucket_table():
    qi = np.arange(WINDOW)[:, None]
    kj = np.arange(2 * WINDOW)[None, :]
    rel = qi + WINDOW - kj
    max_exact = N_BUCKETS // 2
    n = np.maximum(rel, 0)
    nf = np.maximum(n, max_exact).astype(np.float32)
    large = max_exact + (np.log(nf / max_exact) / math.log(MAX_DISTANCE / max_exact) * (N_BUCKETS - max_exact)).astype(np.int32)
    large = np.minimum(large, N_BUCKETS - 1)
    return np.where(n < max_exact, n, large).astype(np.int32).reshape(-1)


def _xa_probs(q, k):
    hd = q.shape[1]
    s = lax.dot_general(q, k, (((1,), (1,)), ((), ())), preferred_element_type=F32) * (hd ** -0.5)
    e = jnp.exp(s - jnp.max(s, axis=1, keepdims=True))
    return e / jnp.sum(e, axis=1, keepdims=True)


def xattn_fwd(q, kv, *, name="xattn_fwd"):
    L, D = q.shape
    Mm = kv.shape[0]
    hd = D // X_HEADS
    tq = _tile8(L, 512)

    def body(q_ref, kv_ref, o_ref):
        for h in range(X_HEADS):
            cols = slice(h * hd, (h + 1) * hd)
            p = _xa_probs(q_ref[:, cols], kv_ref[:, cols])
            o_ref[:, cols] = jnp.dot(p.astype(BF16), kv_ref[:, D + h * hd:D + (h + 1) * hd],
                                     preferred_element_type=F32).astype(o_ref.dtype)

    return pl.pallas_call(
        body, name=name, grid=(L // tq,),
        in_specs=[_vspec((tq, D), lambda i: (i, 0)), _vspec((Mm, 2 * D), lambda i: (0, 0))],
        out_specs=_vspec((tq, D), lambda i: (i, 0)),
        out_shape=jax.ShapeDtypeStruct((L, D), BF16),
        compiler_params=_params(("parallel",)),
    )(q, kv)


def xattn_bwd(q, kv, do, *, name="xattn_bwd"):
    L, D = q.shape
    Mm = kv.shape[0]
    hd = D // X_HEADS
    tq = _tile8(L, 512)

    def body(q_ref, kv_ref, do_ref, dq_ref, dkv_ref):
        @pl.when(pl.program_id(0) == 0)
        def _():
            dkv_ref[...] = jnp.zeros_like(dkv_ref)

        for h in range(X_HEADS):
            cols = slice(h * hd, (h + 1) * hd)
            vcols = slice(D + h * hd, D + (h + 1) * hd)
            qh, kh, vh, doh = q_ref[:, cols], kv_ref[:, cols], kv_ref[:, vcols], do_ref[:, cols]
            p = _xa_probs(qh, kh)
            dp = lax.dot_general(doh, vh, (((1,), (1,)), ((), ())), preferred_element_type=F32)
            ds = (p * (dp - jnp.sum(p * dp, axis=1, keepdims=True)) * (hd ** -0.5)).astype(BF16)
            dq_ref[:, cols] = jnp.dot(ds, kh, preferred_element_type=F32).astype(dq_ref.dtype)
            dkv_ref[:, cols] += lax.dot_general(ds, qh, (((0,), (0,)), ((), ())), preferred_element_type=F32)
            dkv_ref[:, vcols] += lax.dot_general(p.astype(BF16), doh, (((0,), (0,)), ((), ())), preferred_element_type=F32)

    return pl.pallas_call(
        body, name=name, grid=(L // tq,),
        in_specs=[_vspec((tq, D), lambda i: (i, 0)), _vspec((Mm, 2 * D), lambda i: (0, 0)), _vspec((tq, D), lambda i: (i, 0))],
        out_specs=[_vspec((tq, D), lambda i: (i, 0)), _vspec((Mm, 2 * D), lambda i: (0, 0))],
        out_shape=[jax.ShapeDtypeStruct((L, D), BF16), jax.ShapeDtypeStruct((Mm, 2 * D), F32)],
        compiler_params=_params(("arbitrary",)),
    )(q, kv, do)


def adamw(w, g, m, v, *, name="adamw"):
    R, C = w.shape
    tr = _tile8(R, max(SUBLANES, (512 * 1024) // C // SUBLANES * SUBLANES))

    def body(w_ref, g_ref, m_ref, v_ref, d_ref, nm_ref, nv_ref):
        g_ = g_ref[...]
        nm = ADAM_B1 * m_ref[...] + (1.0 - ADAM_B1) * g_
        nv = ADAM_B2 * v_ref[...] + (1.0 - ADAM_B2) * (g_ * g_)
        m_hat = nm / (1.0 - ADAM_B1 ** ADAM_STEP)
        v_hat = nv / (1.0 - ADAM_B2 ** ADAM_STEP)
        d_ref[...] = -ADAM_LR * (m_hat / (jnp.sqrt(v_hat) + ADAM_EPS) + ADAM_WD * w_ref[...])
        nm_ref[...] = nm
        nv_ref[...] = nv

    spec = _vspec((tr, C), lambda i: (i, 0))
    return pl.pallas_call(
        body, name=name, grid=(R // tr,), in_specs=[spec] * 4, out_specs=[spec] * 3,
        out_shape=[jax.ShapeDtypeStruct((R, C), F32)] * 3, compiler_params=_params(("parallel",)),
    )(w, g, m, v)


def _place():
    x, y, c = lax.axis_index("x"), lax.axis_index("y"), lax.axis_index("c")
    chips = [(1 - x, y), (x, 1 - y), (1 - x, 1 - y)]
    return x, y, c, chips


def _remote(src, dst, send, recv, k, to):
    return pltpu.make_async_remote_copy(src_ref=src, dst_ref=dst, send_sem=send.at[k], recv_sem=recv.at[k],
                                        device_id=to, device_id_type=MESH)


class _Job:
    result = None

    def start(self, ins, outs, send, recv, base):
        for cp in self.copies(ins, outs, send, recv, base)[0]:
            cp.start()


class GatherJob(_Job):
    nsem = 7

    def __init__(self, w):
        self.ops = [w]
        self.outs = [jax.ShapeDtypeStruct((N_CHIPS,) + w.shape, w.dtype)]
        self.cost = N_CHIPS * w.size

    def copies(self, ins, outs, send, recv, base, first_only=True):
        w, out = ins[0], outs[0]
        x, y, c, chips = _place()
        me, sib = 2 * x + y, (x, y, 1 - c)
        h = w.shape[0] // 2
        mine, theirs = pl.ds(c * h, h), pl.ds((1 - c) * h, h)
        first = [_remote(w.at[mine], out.at[me, mine], send, recv, base + j, (px, py, c))
                 for j, (px, py) in enumerate(chips)]
        first.append(_remote(w, out.at[me], send, recv, base + 6, sib))
        if first_only:
            return first,
        landed = [out.at[2 * px + py, mine] for px, py in chips]
        lands = [_remote(w.at[mine], landed[j], send, recv, base + j, (px, py, c)) for j, (px, py) in enumerate(chips)]
        passes = [_remote(landed[j], landed[j], send, recv, base + 3 + j, sib) for j in range(3)]
        arrives = [_remote(w.at[theirs], out.at[2 * px + py, theirs], send, recv, base + 3 + j, sib)
                   for j, (px, py) in enumerate(chips)]
        return first, lands, passes, arrives

    def finish(self, ins, outs, send, recv, base):
        first, lands, passes, arrives = self.copies(ins, outs, send, recv, base, first_only=False)
        for land, fwd in zip(lands, passes):
            land.wait_recv()
            fwd.start()
        first[3].wait_recv()
        for cp in arrives:
            cp.wait_recv()
        for cp in first + passes:
            cp.wait_send()


class ExchangeJob(_Job):
    nsem = 1

    def __init__(self, g):
        self.ops = [g]
        self.outs = [jax.ShapeDtypeStruct((g.shape[0], g.shape[1] // 2, g.shape[2]), g.dtype)]
        self.cost = 0.15 * g.size

    def copies(self, ins, outs, send, recv, base):
        x, y, c, _ = _place()
        r2 = ins[0].shape[1] // 2
        return [_remote(ins[0].at[:, pl.ds((1 - c) * r2, r2)], outs[0], send, recv, base, (x, y, 1 - c))],

    def finish(self, ins, outs, send, recv, base):
        self.copies(ins, outs, send, recv, base)[0][0].wait()


class ScatterJob(_Job):
    nsem = 3

    def __init__(self, p):
        self.ops = [p]
        self.outs = [jax.ShapeDtypeStruct((N_CHIPS - 1,) + p.shape[1:], p.dtype)]
        self.cost = 2 * p.size

    def copies(self, ins, outs, send, recv, base):
        x, y, c, chips = _place()
        return [_remote(ins[0].at[2 * px + py], outs[0].at[j], send, recv, base + j, (px, py, c))
                for j, (px, py) in enumerate(chips)],

    def finish(self, ins, outs, send, recv, base):
        for cp in self.copies(ins, outs, send, recv, base)[0]:
            cp.wait()


def run_comm(jobs, *, name):
    cops = [a for j in jobs for a in j.ops]
    couts = [s for j in jobs for s in j.outs]
    nsem = sum(j.nsem for j in jobs)

    def body(*refs):
        cin, cout = refs[:len(cops)], refs[len(cops):len(cops) + len(couts)]
        send, recv = refs[len(cops) + len(couts):]
        for phase in ("start", "finish"):
            ii = io = base = 0
            for j in jobs:
                getattr(j, phase)(cin[ii:ii + len(j.ops)], cout[io:io + len(j.outs)], send, recv, base)
                ii, io, base = ii + len(j.ops), io + len(j.outs), base + j.nsem

    res = pl.pallas_call(
        body, name=name, in_specs=[ANY] * len(cops), out_specs=[ANY] * len(couts), out_shape=couts,
        scratch_shapes=[pltpu.SemaphoreType.DMA((nsem,)), pltpu.SemaphoreType.DMA((nsem,))],
        compiler_params=pltpu.CompilerParams(has_side_effects=True),
    )(*cops)
    io = 0
    for j in jobs:
        j.result = list(res[io:io + len(j.outs)])
        io += len(j.outs)


def add_half(g, other, c_idx, *, name="rs_add_half"):
    S, R, C = g.shape
    r2 = R // 2
    tr = _tile8(r2, max(SUBLANES, (512 * 1024) // C // SUBLANES * SUBLANES))
    nb = r2 // tr

    def body(c_ref, g_ref, o_ref, out_ref):
        out_ref[...] = (g_ref[...].astype(F32) + o_ref[...].astype(F32)).astype(out_ref.dtype)

    return pl.pallas_call(
        body, name=name,
        grid_spec=pltpu.PrefetchScalarGridSpec(
            num_scalar_prefetch=1, grid=(S, nb),
            in_specs=[pl.BlockSpec((None, tr, C), lambda s, i, c_ref: (s, c_ref[0] * nb + i, 0)),
                      pl.BlockSpec((None, tr, C), lambda s, i, c_ref: (s, i, 0))],
            out_specs=pl.BlockSpec((None, tr, C), lambda s, i, c_ref: (s, i, 0))),
        out_shape=jax.ShapeDtypeStruct((S, r2, C), BF16),
        compiler_params=_params(("parallel", "parallel")),
    )(c_idx, g, other)


def add_partials(p, got, place_idx, gbuf, layer, *, name="rs_add_partials"):
    S, r2, C = p.shape
    tr = _tile8(r2, max(SUBLANES, (512 * 1024) // C // SUBLANES * SUBLANES))
    nb = r2 // tr

    def body(pi_ref, p_ref, g_ref, buf_ref, out_ref):
        out_ref[...] = ((p_ref[...].astype(F32) + g_ref[0].astype(F32)) + g_ref[1].astype(F32)) + g_ref[2].astype(F32)

    return pl.pallas_call(
        body, name=name,
        grid_spec=pltpu.PrefetchScalarGridSpec(
            num_scalar_prefetch=1, grid=(nb,),
            in_specs=[pl.BlockSpec((None, tr, C), lambda i, pi: (pi[0], i, 0)),
                      pl.BlockSpec((N_CHIPS - 1, tr, C), lambda i, pi: (0, i, 0)),
                      ANY],
            out_specs=pl.BlockSpec((None, tr, C), lambda i, pi: (layer, pi[1] * nb + i, 0))),
        out_shape=jax.ShapeDtypeStruct(gbuf.shape, F32),
        input_output_aliases={3: 0},
        compiler_params=_params(("parallel",)),
    )(place_idx, p, got, gbuf)


def join_halves(gbufs, *, name="rs_join_halves"):
    n = len(gbufs)

    def body(*refs):
        outs = refs[n:2 * n]
        send, recv = refs[2 * n:]
        x, y, c, _ = _place()
        sib = (x, y, 1 - c)
        cps = []
        for i in range(n):
            r2 = outs[i].shape[1] // 2
            mine = outs[i].at[:, pl.ds(c * r2, r2)]
            cp = _remote(mine, mine, send, recv, i, sib)
            cp.start()
            cps.append(cp)
        for i in range(n):
            r2 = outs[i].shape[1] // 2
            theirs = outs[i].at[:, pl.ds((1 - c) * r2, r2)]
            _remote(theirs, theirs, send, recv, i, sib).wait_recv()
        for cp in cps:
            cp.wait_send()

    return pl.pallas_call(
        body, name=name, in_specs=[ANY] * n, out_specs=[ANY] * n,
        out_shape=[jax.ShapeDtypeStruct(g.shape, F32) for g in gbufs],
        input_output_aliases={i: i for i in range(n)},
        scratch_shapes=[pltpu.SemaphoreType.DMA((n,)), pltpu.SemaphoreType.DMA((n,))],
        compiler_params=pltpu.CompilerParams(has_side_effects=True),
    )(*gbufs)


def all_gather_rows(v, *, name="all_gather_small"):
    m, ncol = v.shape

    def body(x_ref, out_ref, send, recv, lsem):
        x, y, c, chips = _place()
        me, sib = (x, y, c), (x, y, 1 - c)

        def rows(px, py, pc):
            return out_ref.at[pl.ds((4 * px + 2 * py + pc) * m, m), :]

        def copy(k, block, to, src=None):
            return _remote(rows(*block) if src is None else src, rows(*block), send, recv, k, to)

        mine = pltpu.make_async_copy(x_ref, rows(*me), lsem)
        mine.start()
        first = [copy(0, me, sib, src=x_ref)]
        first += [copy(1 + j, me, (*chip, c), src=x_ref) for j, chip in enumerate(chips)]
        for cp in first:
            cp.start()
        passed = [copy(4 + j, (*chip, c), sib) for j, chip in enumerate(chips)]
        for j, chip in enumerate(chips):
            copy(1 + j, (*chip, c), me).wait_recv()
            passed[j].start()
        copy(0, sib, me).wait_recv()
        for j, chip in enumerate(chips):
            copy(4 + j, (*chip, 1 - c), me).wait_recv()
        for cp in first + passed:
            cp.wait_send()
        mine.wait()

    return pl.pallas_call(
        body, name=name,
        in_specs=[pl.BlockSpec(memory_space=pltpu.VMEM)], out_specs=pl.BlockSpec(memory_space=pltpu.VMEM),
        out_shape=jax.ShapeDtypeStruct((8 * m, ncol), v.dtype),
        scratch_shapes=[pltpu.SemaphoreType.DMA((7,)), pltpu.SemaphoreType.DMA((7,)), pltpu.SemaphoreType.DMA],
        compiler_params=pltpu.CompilerParams(vmem_limit_bytes=VMEM_LIMIT_BYTES, has_side_effects=True),
    )(v)


def sum_blocks(g8, *, name="sum_blocks"):
    nb, m, ncol = g8.shape
    tr = _tile8(m, 512)

    def body(g_ref, o_ref):
        acc = g_ref[0]
        for k in range(1, nb):
            acc = acc + g_ref[k]
        o_ref[...] = acc

    return pl.pallas_call(
        body, name=name, grid=(m // tr,),
        in_specs=[_vspec((nb, tr, ncol), lambda i: (0, i, 0))], out_specs=_vspec((tr, ncol), lambda i: (i, 0)),
        out_shape=jax.ShapeDtypeStruct((m, ncol), F32), compiler_params=_params(("parallel",)),
    )(g8)


PACK_ROWS = 256


def _pack(arrs, mult):
    flat = jnp.concatenate([a.reshape(-1) for a in arrs])
    pad = (-flat.shape[0]) % (mult * LANES)
    return jnp.pad(flat, (0, pad)).reshape(-1, LANES)


def _unpack(packed, like):
    flat = packed.reshape(-1)
    out, off = [], 0
    for a in like:
        out.append(flat[off:off + a.size].reshape(a.shape))
        off += a.size
    return out


def _rows2d(a):
    return a.reshape(-1, a.shape[-1])


def kernel(x, mem, norm_mix, norm_xattn, norm_ffn, norm_final, norm_mem, rel_bias, ev_w_in, ev_conv_w, s5_a_re, s5_a_im, s5_log_dt, s5_b_re, s5_b_im, s5_c_re, s5_c_im, s5_d, s5_glu_w, ev_w_out, od_w_qkv, od_b_qkv, od_sinks, od_w_out, xa_w_q, xa_w_kv, xa_w_o, ff_w_gate, ff_w_up, ff_conv_w, ff_conv_b, ff_w_down, loss_target, m_norm_mix, m_norm_xattn, m_norm_ffn, m_norm_final, m_norm_mem, m_rel_bias, m_ev_w_in, m_ev_conv_w, m_s5_a_re, m_s5_a_im, m_s5_log_dt, m_s5_b_re, m_s5_b_im, m_s5_c_re, m_s5_c_im, m_s5_d, m_s5_glu_w, m_ev_w_out, m_od_w_qkv, m_od_b_qkv, m_od_sinks, m_od_w_out, m_xa_w_q, m_xa_w_kv, m_xa_w_o, m_ff_w_gate, m_ff_w_up, m_ff_conv_w, m_ff_conv_b, m_ff_w_down, v_norm_mix, v_norm_xattn, v_norm_ffn, v_norm_final, v_norm_mem, v_rel_bias, v_ev_w_in, v_ev_conv_w, v_s5_a_re, v_s5_a_im, v_s5_log_dt, v_s5_b_re, v_s5_b_im, v_s5_c_re, v_s5_c_im, v_s5_d, v_s5_glu_w, v_ev_w_out, v_od_w_qkv, v_od_b_qkv, v_od_sinks, v_od_w_out, v_xa_w_q, v_xa_w_kv, v_xa_w_o, v_ff_w_gate, v_ff_w_up, v_ff_conv_w, v_ff_conv_b, v_ff_w_down):
    names = ["norm_mix", "norm_xattn", "norm_ffn", "norm_final", "norm_mem", "rel_bias", "ev_w_in", "ev_conv_w",
             "s5_a_re", "s5_a_im", "s5_log_dt", "s5_b_re", "s5_b_im", "s5_c_re", "s5_c_im", "s5_d", "s5_glu_w",
             "ev_w_out", "od_w_qkv", "od_b_qkv", "od_sinks", "od_w_out", "xa_w_q", "xa_w_kv", "xa_w_o",
             "ff_w_gate", "ff_w_up", "ff_conv_w", "ff_conv_b", "ff_w_down"]
    env = dict(locals())
    W = {k: env[k] for k in names}
    Mo = {k: env["m_" + k] for k in names}
    Vo = {k: env["v_" + k] for k in names}

    h = x[0]
    target = loss_target[0]
    L, D = h.shape
    depth = norm_mix.shape[0]
    c_idx = lax.axis_index("c").astype(jnp.int32).reshape(1)
    me_idx = (2 * lax.axis_index("x") + lax.axis_index("y")).astype(jnp.int32).reshape(1)

    col_sharded = ["ev_w_in", "od_w_qkv", "xa_w_kv", "ff_w_gate", "ff_w_up"]
    row_sharded = ["ev_w_out", "od_w_out", "xa_w_q", "xa_w_o", "ff_w_down"]
    small_sharded = ["ev_conv_w", "od_b_qkv", "ff_conv_w"]
    big = col_sharded + row_sharded

    def layer_weights(l):
        mix = [("ev_w_in", l // 2), ("ev_w_out", l // 2)] if l % 2 == 0 else [("od_w_qkv", l // 2), ("od_w_out", l // 2)]
        return mix + [(k, l) for k in ("xa_w_q", "xa_w_kv", "xa_w_o", "ff_w_gate", "ff_w_up", "ff_w_down")]

    gjob = {kl: GatherJob(W[kl[0]][kl[1]].astype(BF16)) for l in range(depth) for kl in layer_weights(l)}
    small_jobs = [GatherJob(W[k]) for k in small_sharded]
    pending = [gjob[kl] for l in range(depth) for kl in layer_weights(l)]
    run_comm([pending.pop(0)] + small_jobs, name="gather_first")
    flushes = []

    def take(queue, host_cost):
        jobs, acc = [], 0.0
        while queue and acc + queue[0].cost <= 1.25 * host_cost:
            acc += queue[0].cost
            jobs.append(queue.pop(0))
        return jobs

    def fwd_host(fn, units, *args, **kw):
        return fn(*args, comm=take(pending, units), **kw)

    def weight(k, l):
        job = gjob[(k, l)]
        if job.result is None:
            n = pending.index(job) + 1
            run_comm(pending[:n], name="gather_flush_%d" % len(flushes))
            flushes.append(n)
            del pending[:n]
        return job.result[0]

    def wcol(k, l):
        return weight(k, l)

    def wrow(k, l):
        g = weight(k, l)
        return g.reshape(1, g.shape[0] * g.shape[1], g.shape[2])

    def fwd_mm(a, w3, **kw):
        return mm_nn(a, w3, comm=take(pending, w3.size * a.shape[0] / L), **kw)

    sg = [j.result[0] for j in small_jobs]
    ev_conv_w_f = sg[0].transpose(1, 2, 0, 3).reshape(ev_conv_w.shape[0], 3, -1)
    od_b_qkv_f = sg[1].transpose(1, 0, 2).reshape(od_b_qkv.shape[0], 1, -1)
    ff_conv_w_f = sg[2].transpose(1, 2, 0, 3).reshape(ff_conv_w.shape[0], 3, -1)

    buckets = _bucket_table()
    NQ = D // HEAD_DIM
    NKV = NQ // Q_PER_KV
    onehot = jnp.asarray((buckets[:, None] == np.arange(N_BUCKETS)[None, :]).astype(np.float32))
    bias_tab = jnp.dot(rel_bias.T, onehot.T, precision=lax.Precision.HIGHEST).reshape(NQ, WINDOW, 2 * WINDOW)
    bias2 = _masked_bias(bias_tab)

    mem_n = rms_fwd(mem[0], norm_mem.reshape(1, D), name="rms_fwd_mem")

    saved = []
    for l in range(depth):
        i = l // 2
        s = {"h0": h}
        hn = rms_fwd(h, norm_mix[l].reshape(1, D))
        s["hn"] = hn
        if l % 2 == 0:
            A = ev_conv_w_f.shape[-1]
            z = fwd_mm(hn, wcol("ev_w_in", i), name="mm_ev_in")
            ya = conv_mixer_fwd(z, ev_conv_w_f[i])
            prep = functools.partial(_s5_prep, seg_len=L // N_SEG)
            s5p = (s5_a_re[i], s5_a_im[i], s5_log_dt[i], s5_b_re[i], s5_b_im[i], s5_c_re[i], s5_c_im[i], s5_d[i], s5_glu_w[i])
            (a_l, apow_l, bm, cm, d_l, glu), prep_vjp = jax.vjp(prep, *s5p)
            bm16, cm16, glu16 = bm.astype(BF16), cm.astype(BF16), glu.astype(BF16)
            u_p = _to_segments(z[:, 3 * A:])
            fin = fwd_host(s5_scan_fwd, HOST_UNITS["s5_ends"] * u_p.size,
                           u_p, a_l, apow_l, bm16, cm16, d_l, glu16, None, name="s5_fwd_ends")
            ys_p, y_p, states, starts = fwd_host(s5_scan_fwd, HOST_UNITS["s5_fwd"] * u_p.size,
                                                 u_p, a_l, apow_l, bm16, cm16, d_l, glu16, fin, name="s5_fwd")
            ycat = jnp.concatenate([ya, _from_segments(ys_p)], axis=1)
            s.update(z=z, u_p=u_p, y_p=y_p, states=states, starts=starts, ycat=ycat, prep_vjp=prep_vjp,
                     s5ops=(a_l, apow_l, bm16, cm16, d_l, glu16))
            h = fwd_mm(ycat, wrow("ev_w_out", i), res=h, name="mm_ev_out")
        else:
            z = fwd_mm(hn, wcol("od_w_qkv", i), bias=od_b_qkv_f[i], out_dtype=BF16, name="mm_od_qkv")
            qT = z[:, :NQ * HEAD_DIM].reshape(L, NQ, HEAD_DIM).transpose(1, 0, 2)
            kT = z[:, NQ * HEAD_DIM:(NQ + NKV) * HEAD_DIM].reshape(L, NKV, HEAD_DIM).transpose(1, 0, 2)
            vT = z[:, (NQ + NKV) * HEAD_DIM:].reshape(L, NKV, HEAD_DIM).transpose(1, 0, 2)
            sink_row = jnp.repeat(od_sinks[i], WINDOW).reshape(1, NQ * WINDOW)
            o = _from_head_lanes(fwd_host(swa_fwd, HOST_UNITS["swa_fwd"] * qT.size, qT, kT, vT, bias2, sink_row), L)
            s.update(qT=qT, kT=kT, vT=vT, sink_row=sink_row, o=o)
            h = fwd_mm(o, wrow("od_w_out", i), res=h, name="mm_od_out")
        s["h1"] = h
        hn2 = rms_fwd(h, norm_xattn[l].reshape(1, D))
        q = fwd_mm(hn2, wrow("xa_w_q", l), out_dtype=BF16, name="mm_xa_q")
        kv = fwd_mm(mem_n, wcol("xa_w_kv", l), out_dtype=BF16, name="mm_xa_kv")
        ox = xattn_fwd(q, kv)
        s.update(hn2=hn2, q=q, kv=kv, ox=ox)
        h = fwd_mm(ox, wrow("xa_w_o", l), res=h, name="mm_xa_o")
        s["h2"] = h
        hn3 = rms_fwd(h, norm_ffn[l].reshape(1, D))
        gpre = fwd_mm(hn3, wcol("ff_w_gate", l), out_dtype=BF16, name="mm_ff_gate")
        up = fwd_mm(hn3, wcol("ff_w_up", l), out_dtype=BF16, name="mm_ff_up")
        act = fwd_host(ffn_act_fwd, HOST_UNITS["ffn_act_fwd"] * gpre.size,
                       gpre, up, ff_conv_w_f[l], ff_conv_b[l].reshape(1, -1))
        s.update(hn3=hn3, gpre=gpre, up=up, act=act)
        h = fwd_mm(act, wrow("ff_w_down", l), res=h, name="mm_ff_down")
        saved.append(s)

    loss11, dh, dg_final, dh16 = final_loss(h, norm_final.reshape(1, D), target)
    loss = lax.psum(loss11[0, 0], AXES)

    gs = {k: [None] * W[k].shape[0] for k in names if k not in big and W[k].ndim > 1 and k != "rel_bias"}
    dmem_n = None
    dbias_tab = jnp.zeros_like(bias_tab)
    place_idx = jnp.concatenate([me_idx, c_idx])
    gbufs, exchanges, scatters = {}, [], []

    def settle(jobs):
        for j in jobs:
            k, l = j.tag
            if isinstance(j, ExchangeJob):
                nxt = ScatterJob(add_half(j.ops[0], j.result[0], c_idx))
                nxt.tag = j.tag
                scatters.append(nxt)
            else:
                p = j.ops[0]
                if k not in gbufs:
                    gbufs[k] = jnp.zeros((W[k].shape[0], 2 * p.shape[1], p.shape[2]), F32)
                gbufs[k] = add_partials(p, j.result[0], place_idx, gbufs[k], l)

    def take_fit(queue, budget):
        jobs = []
        for j in list(queue):
            if j.cost <= budget:
                budget -= j.cost
                jobs.append(j)
                queue.remove(j)
        return jobs

    def bwd_host(fn, units, *args, **kw):
        jobs = take_fit(exchanges, 0.6 * units) + take_fit(scatters, 1.2 * units)
        out = fn(*args, comm=jobs, **kw)
        settle(jobs)
        return out

    def dx_mm(a, w3, **kw):
        return bwd_host(mm_nt, w3.size * a.shape[0] / L, a, w3, **kw)

    def dw_mm(k, l, xx, dy, S, **kw):
        g3 = bwd_host(mm_tn, xx.shape[1] * dy.shape[1] * xx.shape[0] / L, xx, dy, S, **kw)
        if S == 1:
            g3 = g3.reshape(N_CHIPS, g3.shape[1] // N_CHIPS, g3.shape[2])
        job = ExchangeJob(g3)
        job.tag = (k, l)
        exchanges.append(job)

    for l in reversed(range(depth)):
        i = l // 2
        s = saved[l]
        dact = dx_mm(dh16, wrow("ff_w_down", l), out_dtype=BF16, name="mm_ff_down_dx")
        dw_mm("ff_w_down", l, s["act"], dh16, 1, name="mm_ff_down_dw")
        dgpre, dup, dcw, dcb = bwd_host(ffn_act_bwd, HOST_UNITS["ffn_act_bwd"] * dact.size,
                                        dact, s["gpre"], s["up"], ff_conv_w_f[l], ff_conv_b[l].reshape(1, -1))
        gs["ff_conv_w"][l], gs["ff_conv_b"][l] = dcw, dcb[0]
        dhn3 = dx_mm(dgpre, wcol("ff_w_gate", l), name="mm_ff_gate_dx")
        dhn3 = dx_mm(dup, wcol("ff_w_up", l), res=dhn3, name="mm_ff_up_dx")
        dw_mm("ff_w_gate", l, s["hn3"], dgpre, N_CHIPS, name="mm_ff_gate_dw")
        dw_mm("ff_w_up", l, s["hn3"], dup, N_CHIPS, name="mm_ff_up_dw")
        dh, dg, dh16 = bwd_host(rms_bwd, HOST_UNITS["rms_bwd"] * dh.size, s["h2"], norm_ffn[l].reshape(1, D), dhn3, dh)
        gs["norm_ffn"][l] = dg[0]
        dox = dx_mm(dh16, wrow("xa_w_o", l), out_dtype=BF16, name="mm_xa_o_dx")
        dw_mm("xa_w_o", l, s["ox"], dh16, 1, name="mm_xa_o_dw")
        dq, dkv = xattn_bwd(s["q"], s["kv"], dox)
        dhn2 = dx_mm(dq, wrow("xa_w_q", l), name="mm_xa_q_dx")
        dw_mm("xa_w_q", l, s["hn2"], dq, 1, name="mm_xa_q_dw")
        dw_mm("xa_w_kv", l, mem_n, dkv, N_CHIPS, name="mm_xa_kv_dw")
        dmem_n = dx_mm(dkv, wcol("xa_w_kv", l), res=dmem_n, name="mm_xa_kv_dx")
        dh, dg, dh16 = bwd_host(rms_bwd, HOST_UNITS["rms_bwd"] * dh.size, s["h1"], norm_xattn[l].reshape(1, D), dhn2, dh)
        gs["norm_xattn"][l] = dg[0]
        if l % 2 == 0:
            A = ev_conv_w_f.shape[-1]
            dycat = dx_mm(dh16, wrow("ev_w_out", i), name="mm_ev_out_dx")
            dw_mm("ev_w_out", i, s["ycat"], dh16, 1, name="mm_ev_out_dw")
            dgb, dgc, dxa, dcw = conv_mixer_bwd(s["z"], ev_conv_w_f[i], dycat[:, :A])
            gs["ev_conv_w"][i] = dcw
            a_l, apow_l, bm16, cm16, d_l, glu16 = s["s5ops"]
            dys_p = _to_segments(dycat[:, A:])
            dy_p, dglu, dd = bwd_host(s5_out_bwd, HOST_UNITS["s5_out_bwd"] * dys_p.size, dys_p, s["y_p"], s["u_p"], glu16)
            cmt = cm16.transpose(0, 2, 1)
            bmt = bm16.transpose(0, 2, 1)
            gfin = bwd_host(s5_scan_bwd, HOST_UNITS["s5_ends"] * dy_p.size, dy_p, a_l, apow_l, cmt, None, name="s5_bwd_ends")
            du_p, da, dbm, dcm = bwd_host(s5_scan_bwd, HOST_UNITS["s5_bwd"] * dy_p.size, dy_p, a_l, apow_l, cmt, gfin,
                                          s["states"], s["starts"], s["u_p"], bmt, d_l, name="s5_bwd")
            dprm = s["prep_vjp"]((da, jnp.zeros_like(apow_l), dbm, dcm, dd, dglu))
            for k, g in zip(["s5_a_re", "s5_a_im", "s5_log_dt", "s5_b_re", "s5_b_im", "s5_c_re", "s5_c_im", "s5_d", "s5_glu_w"], dprm):
                gs[k][i] = g
            dz = jnp.concatenate([dgb, dgc, dxa, _from_segments(du_p)], axis=1)
            dhn = dx_mm(dz, wcol("ev_w_in", i), name="mm_ev_in_dx")
            dw_mm("ev_w_in", i, s["hn"], dz, N_CHIPS, name="mm_ev_in_dw")
        else:
            do = dx_mm(dh16, wrow("od_w_out", i), out_dtype=BF16, name="mm_od_out_dx")
            dw_mm("od_w_out", i, s["o"], dh16, 1, name="mm_od_out_dw")
            doT = do.reshape(L, NQ, HEAD_DIM).transpose(1, 0, 2)
            dqT, dkT, dvT, dbias, dsink = bwd_host(swa_bwd, HOST_UNITS["swa_bwd"] * doT.size,
                                                   s["qT"], s["kT"], s["vT"], bias2, s["sink_row"], doT)
            dbias_tab = dbias_tab + _unmasked_dbias(dbias)
            gs["od_sinks"][i] = jnp.sum(dsink.reshape(NQ, WINDOW), axis=1)
            dz = jnp.concatenate([_from_head_lanes(dqT, L),
                                  dkT.astype(BF16).transpose(1, 0, 2).reshape(L, NKV * HEAD_DIM),
                                  dvT.astype(BF16).transpose(1, 0, 2).reshape(L, NKV * HEAD_DIM)], axis=1)
            gs["od_b_qkv"][i] = col_sum(dz)[0]
            dhn = dx_mm(dz, wcol("od_w_qkv", i), name="mm_od_qkv_dx")
            dw_mm("od_w_qkv", i, s["hn"], dz, N_CHIPS, name="mm_od_qkv_dw")
        dh, dg, dh16 = bwd_host(rms_bwd, HOST_UNITS["rms_bwd"] * dh.size, s["h0"], norm_mix[l].reshape(1, D), dhn, dh)
        gs["norm_mix"][l] = dg[0]

    grad_x = dh[None]
    _, dg_mem, _ = rms_bwd(mem[0], norm_mem.reshape(1, D), dmem_n, jnp.zeros_like(dmem_n), name="rms_bwd_mem")
    d_rel_bias = jnp.dot(dbias_tab.reshape(NQ, -1), onehot, precision=lax.Precision.HIGHEST).T

    small = [k for k in names if k not in big]
    local_small = {k: (jnp.stack(gs[k]) if k in gs else None) for k in small}
    local_small["norm_final"] = dg_final[0]
    local_small["norm_mem"] = dg_mem[0]
    local_small["rel_bias"] = d_rel_bias
    full_shape = {k: W[k].shape for k in small}
    for k in small_sharded:
        full_shape[k] = local_small[k].shape
    lst = [local_small[k].reshape(full_shape[k]).astype(F32) for k in small]
    packed = _pack(lst, PACK_ROWS)
    m_rows = packed.shape[0]
    summed = sum_blocks(all_gather_rows(packed).reshape(8, m_rows, LANES))
    gsum = dict(zip(small, _unpack(summed, lst)))
    for k in small_sharded:
        n4 = W[k].shape[-1]
        gsum[k] = lax.dynamic_slice_in_dim(gsum[k], me_idx[0] * n4, n4, axis=gsum[k].ndim - 1)

    for queue, name in ((exchanges, "rs_exchange_rest"), (scatters, "rs_scatter_rest")):
        jobs = queue[:]
        del queue[:]
        if jobs:
            run_comm(jobs, name=name)
            settle(jobs)
    joined = join_halves([gbufs[k] for k in big])
    gbig = {k: g.reshape(W[k].shape) for k, g in zip(big, joined)}

    grads = {**gsum, **gbig}
    delta, new_m, new_v = {}, {}, {}
    for k in big:
        d_, m_, v_ = adamw(_rows2d(W[k]), _rows2d(grads[k]), _rows2d(Mo[k]), _rows2d(Vo[k]), name="adamw_" + k)
        delta[k], new_m[k], new_v[k] = d_.reshape(W[k].shape), m_.reshape(W[k].shape), v_.reshape(W[k].shape)
    sw = [W[k] for k in small]
    d_, m_, v_ = adamw(_pack(sw, PACK_ROWS), _pack([grads[k] for k in small], PACK_ROWS),
                       _pack([Mo[k] for k in small], PACK_ROWS), _pack([Vo[k] for k in small], PACK_ROWS), name="adamw_small")
    for k, a, b, c_ in zip(small, _unpack(d_, sw), _unpack(m_, sw), _unpack(v_, sw)):
        delta[k], new_m[k], new_v[k] = a, b, c_

    return (loss, grad_x, *[grads[k] for k in names], *[delta[k] for k in names],
            *[new_m[k] for k in names], *[new_v[k] for k in names])
```
